```python
import math
import jax
import jax.numpy as jnp
from jax import lax
import numpy as np

D_MODEL = 1024
BATCH = 4
SEQ = 4096
DEPTH = 1

GRID_W = 64
CTX_LEN = 256
EPS = 1e-6

GLA_HEADS = 4
GLA_DK = 128
GLA_DV = 256
GLA_KEY = GLA_HEADS * GLA_DK
GLA_VAL = GLA_HEADS * GLA_DV
GLA_RANK = 16
GLA_GATE_NORM = 16.0
GLA_CHUNK = 16

SSD_HEADS = 16
SSD_P = 64
SSD_INNER = SSD_HEADS * SSD_P
SSD_GROUPS = 2
SSD_HPG = SSD_HEADS // SSD_GROUPS
SSD_N = 128
SSD_CONV = 3
SSD_CHUNK = 64
SSD_XBC = SSD_INNER + 2 * SSD_GROUPS * SSD_N

MIX_WIDTH = GLA_VAL + SSD_INNER

OFF_K = GLA_KEY
OFF_V = 2 * GLA_KEY
OFF_R = OFF_V + GLA_VAL
OFF_G = OFF_R + GLA_VAL
OFF_Z = OFF_G + 2 * GLA_RANK
OFF_XBC = OFF_Z + SSD_INNER
OFF_DT = OFF_XBC + SSD_XBC
IN_WIDTH = OFF_DT + 2 * SSD_HEADS
IN_SPLIT_POINTS = (OFF_K, OFF_V, OFF_R, OFF_G, OFF_Z, OFF_XBC, OFF_DT)

N_GROUPS = 4
EXPERTS_PER_GROUP = 8
N_EXPERTS = N_GROUPS * EXPERTS_PER_GROUP
TOP_K = 2
D_FF = 512

kernel_name = 'hybrid_gla_ssd_hmoe_dit_block'


def rmsnorm(x, g):
    x32 = x.astype(jnp.float32)
    y = x32 * lax.rsqrt(jnp.mean(x32 * x32, axis=-1, keepdims=True) + EPS)
    return (y * g.astype(jnp.float32)).astype(x.dtype)


def modulate(h, shift, scale):
    return h * (1.0 + scale) + shift


def adaln(cond, w, b):
    return jnp.split((jax.nn.silu(cond) @ w + b)[:, None, :], 6, axis=-1)


def dwconv2d(u, w, b, rows, cols):
    bn, l, ch = u.shape
    img = u.reshape(bn, rows, cols, ch)
    out = lax.conv_general_dilated(img, w[:, :, None, :].astype(u.dtype), (1, 1), 'SAME',
                                   dimension_numbers=('NHWC', 'HWIO', 'NHWC'), feature_group_count=ch)
    return out.reshape(bn, l, ch) + b


def gla_chunk_scan(q, k, v, log_a, s0):
    bn, l, nh, _ = q.shape
    dv = v.shape[-1]
    n = l // GLA_CHUNK

    def chunks(t):
        return jnp.moveaxis(t.astype(jnp.float32).reshape(bn, n, GLA_CHUNK, nh, t.shape[-1]), 1, 0)

    mask = jnp.tril(jnp.ones((GLA_CHUNK, GLA_CHUNK), dtype=bool))[None, :, :, None, None]

    def step(s, inp):
        qc, kc, vc, ac = inp
        b = jnp.cumsum(ac, axis=1)
        decay = jnp.exp(jnp.where(mask, b[:, :, None] - b[:, None, :], -jnp.inf))
        scores = jnp.einsum('bihd,bjhd,bijhd->bhij', qc, kc, decay)
        o = (jnp.einsum('bhij,bjhv->bihv', scores, vc)
             + jnp.einsum('bihd,bhdv->bihv', qc * jnp.exp(b), s))
        b_last = b[:, -1]
        s = (jnp.exp(b_last)[..., None] * s
             + jnp.einsum('bjhd,bjhv->bhdv', kc * jnp.exp(b_last[:, None] - b), vc))
        return s, o

    s_fin, o = lax.scan(step, s0, (chunks(q), chunks(k), chunks(v), chunks(log_a)))
    o = jnp.moveaxis(o, 0, 1).reshape(bn, l, nh, dv)
    return o.astype(q.dtype), s_fin


def ssd_chunk_scan(xs, dt, a_neg, bm, cm, s0):
    bn, l, g, hg, p = xs.shape
    n = l // SSD_CHUNK
    f32 = jnp.float32

    def chunks(t):
        return t.astype(f32).reshape((bn, n, SSD_CHUNK) + t.shape[2:])

    xc, dtc, bc, cc = chunks(xs), chunks(dt), chunks(bm), chunks(cm)
    a_cum = jnp.cumsum(dtc * a_neg.astype(f32), axis=2)
    xdt = xc * dtc[..., None]
    mask = jnp.tril(jnp.ones((SSD_CHUNK, SSD_CHUNK), dtype=bool))[None, None, :, :, None, None]
    seg = jnp.exp(jnp.where(mask, a_cum[:, :, :, None] - a_cum[:, :, None, :], -jnp.inf))
    cb = jnp.einsum('bcign,bcjgn->bcijg', cc, bc)
    y_diag = jnp.einsum('bcijgh,bcjghp->bcighp', cb[..., None] * seg, xdt)
    decay_end = jnp.exp(a_cum[:, :, -1:] - a_cum)
    chunk_states = jnp.einsum('bcjgn,bcjghp->bcghpn', bc, xdt * decay_end[..., None])
    chunk_decay = jnp.exp(a_cum[:, :, -1])

    def step(s, inp):
        st, dec = inp
        return dec[..., None, None] * s + st, s

    s_fin, s_in = lax.scan(step, s0, (jnp.moveaxis(chunk_states, 1, 0), jnp.moveaxis(chunk_decay, 1, 0)))
    s_in = jnp.moveaxis(s_in, 0, 1)
    y_off = jnp.einsum('bcign,bcghpn->bcighp', cc, s_in) * jnp.exp(a_cum)[..., None]
    y = (y_diag + y_off).reshape(bn, l, g, hg, p)
    return y.astype(xs.dtype), s_fin


def token_mixer(h, rows, cols, states, w_in, gla_w_up, gla_b, gla_norm, conv_w, conv_b,
                dt_bias, a_log, d_skip, ssd_norm, with_output):
    bn, l, _ = h.shape
    proj = h @ w_in
    q, k, v, r, g_lr, z, xbc, dt_raw = jnp.split(proj, IN_SPLIT_POINTS, axis=-1)
    gs_f, gs_b, ss_f, ss_b = states
    flip = lambda t: jnp.flip(t, axis=1)

    q = q.reshape(bn, l, GLA_HEADS, GLA_DK) * (GLA_DK ** -0.5)
    k = k.reshape(bn, l, GLA_HEADS, GLA_DK)
    v = v.reshape(bn, l, GLA_HEADS, GLA_DV)
    g_f, g_b = jnp.split(g_lr, 2, axis=-1)
    la_f = (jax.nn.log_sigmoid(g_f @ gla_w_up[0] + gla_b[0]) / GLA_GATE_NORM).reshape(bn, l, GLA_HEADS, GLA_DK)
    la_b = (jax.nn.log_sigmoid(g_b @ gla_w_up[1] + gla_b[1]) / GLA_GATE_NORM).reshape(bn, l, GLA_HEADS, GLA_DK)
    o_f, gs_f = gla_chunk_scan(q, k, v, la_f, gs_f)
    o_b, gs_b = gla_chunk_scan(flip(q), flip(k), flip(v), flip(la_b), gs_b)

    xbc = jax.nn.silu(dwconv2d(xbc, conv_w, conv_b, rows, cols))
    xs, bm, cm = jnp.split(xbc, (SSD_INNER, SSD_INNER + SSD_GROUPS * SSD_N), axis=-1)
    xs = xs.reshape(bn, l, SSD_GROUPS, SSD_HPG, SSD_P)
    bm = bm.reshape(bn, l, SSD_GROUPS, SSD_N)
    cm = cm.reshape(bn, l, SSD_GROUPS, SSD_N)
    dt_f, dt_b = jnp.split(dt_raw, 2, axis=-1)
    dt_f = jax.nn.softplus(dt_f + dt_bias[0]).reshape(bn, l, SSD_GROUPS, SSD_HPG)
    dt_b = jax.nn.softplus(dt_b + dt_bias[1]).reshape(bn, l, SSD_GROUPS, SSD_HPG)
    a_f = -jnp.exp(a_log[0]).reshape(SSD_GROUPS, SSD_HPG)
    a_b = -jnp.exp(a_log[1]).reshape(SSD_GROUPS, SSD_HPG)
    y_f, ss_f = ssd_chunk_scan(xs, dt_f, a_f, bm, cm, ss_f)
    y_b, ss_b = ssd_chunk_scan(flip(xs), flip(dt_b), a_b, flip(bm), flip(cm), ss_b)
    new_states = (gs_f, gs_b, ss_f, ss_b)
    if not with_output:
        return None, new_states

    o = rmsnorm(o_f + flip(o_b), gla_norm).reshape(bn, l, GLA_VAL) * jax.nn.silu(r)
    y = y_f + flip(y_b) + xs * d_skip.reshape(SSD_GROUPS, SSD_HPG)[..., None]
    y = y.reshape(bn, l, SSD_INNER) * jax.nn.silu(z)
    y = rmsnorm(y.reshape(bn, l, SSD_GROUPS, SSD_INNER // SSD_GROUPS),
                ssd_norm.reshape(SSD_GROUPS, SSD_INNER // SSD_GROUPS)).reshape(bn, l, SSD_INNER)
    return jnp.concatenate([o, y], axis=-1), new_states


def hier_moe(h, wg, bg, we, be, w_gate, w_up, w_down):
    bn, l, d = h.shape
    t = h.reshape(bn * l, d)
    g_logits = (t @ wg + bg).astype(jnp.float32)
    g_sel = jnp.argmax(g_logits, axis=-1)
    p_group = jnp.take_along_axis(jax.nn.softmax(g_logits, axis=-1), g_sel[:, None], axis=-1)
    e_logits = (t @ we + be).astype(jnp.float32).reshape(-1, N_GROUPS, EXPERTS_PER_GROUP)
    e_in = jnp.take_along_axis(e_logits, g_sel[:, None, None], axis=1)[:, 0]
    top_v, top_i = lax.top_k(e_in, TOP_K)
    w = jax.nn.softmax(top_v, axis=-1) * p_group
    idx = g_sel[:, None] * EXPERTS_PER_GROUP + top_i
    combine = jnp.einsum('tk,tke->te', w, jax.nn.one_hot(idx, N_EXPERTS, dtype=jnp.float32))
    out = jnp.zeros((t.shape[0], d), jnp.float32)
    for e in range(N_EXPERTS):
        he = jax.nn.silu(t @ w_gate[e]) * (t @ w_up[e])
        out = out + combine[:, e:e + 1] * (he @ w_down[e]).astype(jnp.float32)
    return out.reshape(bn, l, d).astype(h.dtype)


def setup_inputs(seed: int = 0) -> dict:
    key = jax.random.key(seed)
    ks = jax.random.split(key, 32)
    f32 = jnp.float32

    def nrm(k, shape, s):
        return jax.random.normal(k, shape, f32) * s

    x = nrm(ks[0], (BATCH, SEQ, D_MODEL), 1.0)
    c = nrm(ks[1], (BATCH, D_MODEL), 1.0)
    ctx = nrm(ks[2], (BATCH, CTX_LEN, D_MODEL), 1.0)
    c_ctx = nrm(ks[3], (D_MODEL,), 1.0)
    w_ada = nrm(ks[4], (DEPTH, D_MODEL, 6 * D_MODEL), 0.5 * D_MODEL ** -0.5)
    b_ada = nrm(ks[5], (DEPTH, 6 * D_MODEL), 0.02)
    norm_mix = 1.0 + nrm(ks[6], (DEPTH, D_MODEL), 0.02)
    norm_ffn = 1.0 + nrm(ks[7], (DEPTH, D_MODEL), 0.02)
    w_in = nrm(ks[8], (DEPTH, D_MODEL, IN_WIDTH), D_MODEL ** -0.5)
    gla_w_up = nrm(ks[9], (DEPTH, 2, GLA_RANK, GLA_KEY), GLA_RANK ** -0.5)
    gla_b = nrm(ks[10], (DEPTH, 2, GLA_KEY), 0.1)
    gla_norm = 1.0 + nrm(ks[11], (DEPTH, GLA_DV), 0.02)
    ssd_conv_w = nrm(ks[12], (DEPTH, SSD_CONV, SSD_CONV, SSD_XBC), 1.0 / SSD_CONV)
    ssd_conv_b = nrm(ks[13], (DEPTH, SSD_XBC), 0.02)
    dt0 = jnp.exp(jax.random.uniform(ks[14], (DEPTH, 2, SSD_HEADS), f32, math.log(1e-3), math.log(1e-1)))
    ssd_dt_bias = dt0 + jnp.log(-jnp.expm1(-dt0))
    ssd_a_log = jnp.log(jax.random.uniform(ks[15], (DEPTH, 2, SSD_HEADS), f32, 1.0, 16.0))
    ssd_d = 1.0 + nrm(ks[16], (DEPTH, SSD_HEADS), 0.1)
    ssd_norm = 1.0 + nrm(ks[17], (DEPTH, SSD_INNER), 0.02)
    w_out = nrm(ks[18], (DEPTH, MIX_WIDTH, D_MODEL), MIX_WIDTH ** -0.5)
    router_group_w = nrm(ks[19], (DEPTH, D_MODEL, N_GROUPS), D_MODEL ** -0.5)
    router_group_b = nrm(ks[20], (DEPTH, N_GROUPS), 0.01)
    router_expert_w = nrm(ks[21], (DEPTH, D_MODEL, N_EXPERTS), D_MODEL ** -0.5)
    router_expert_b = nrm(ks[22], (DEPTH, N_EXPERTS), 0.01)
    expert_w_gate = nrm(ks[23], (DEPTH, N_EXPERTS, D_MODEL, D_FF), D_MODEL ** -0.5)
    expert_w_up = nrm(ks[24], (DEPTH, N_EXPERTS, D_MODEL, D_FF), D_MODEL ** -0.5)
    expert_w_down = nrm(ks[25], (DEPTH, N_EXPERTS, D_FF, D_MODEL), D_FF ** -0.5)
    final_norm = 1.0 + nrm(ks[26], (D_MODEL,), 0.02)
    return {'x': x, 'c': c, 'ctx': ctx, 'c_ctx': c_ctx, 'w_ada': w_ada, 'b_ada': b_ada,
            'norm_mix': norm_mix, 'norm_ffn': norm_ffn, 'w_in': w_in, 'gla_w_up': gla_w_up,
            'gla_b': gla_b, 'gla_norm': gla_norm, 'ssd_conv_w': ssd_conv_w, 'ssd_conv_b': ssd_conv_b,
            'ssd_dt_bias': ssd_dt_bias, 'ssd_a_log': ssd_a_log, 'ssd_d': ssd_d, 'ssd_norm': ssd_norm,
            'w_out': w_out, 'router_group_w': router_group_w, 'router_group_b': router_group_b,
            'router_expert_w': router_expert_w, 'router_expert_b': router_expert_b,
            'expert_w_gate': expert_w_gate, 'expert_w_up': expert_w_up, 'expert_w_down': expert_w_down,
            'final_norm': final_norm}


def reference(x, c, ctx, c_ctx, w_ada, b_ada, norm_mix, norm_ffn, w_in, gla_w_up, gla_b, gla_norm,
              ssd_conv_w, ssd_conv_b, ssd_dt_bias, ssd_a_log, ssd_d, ssd_norm, w_out,
              router_group_w, router_group_b, router_expert_w, router_expert_b,
              expert_w_gate, expert_w_up, expert_w_down, final_norm):
    bn, seq_len, _ = x.shape
    rows = seq_len // GRID_W
    ctx_len = ctx.shape[1]
    zero_states = (jnp.zeros((bn, GLA_HEADS, GLA_DK, GLA_DV), jnp.float32),
                   jnp.zeros((bn, GLA_HEADS, GLA_DK, GLA_DV), jnp.float32),
                   jnp.zeros((bn, SSD_GROUPS, SSD_HPG, SSD_P, SSD_N), jnp.float32),
                   jnp.zeros((bn, SSD_GROUPS, SSD_HPG, SSD_P, SSD_N), jnp.float32))
    h_lat, h_ctx = x, ctx
    for layer in range(DEPTH):
        last = layer == DEPTH - 1
        sh1, sc1, g1, sh2, sc2, g2 = adaln(c, w_ada[layer], b_ada[layer])
        csh1, csc1, cg1, csh2, csc2, cg2 = adaln(c_ctx[None, :], w_ada[layer], b_ada[layer])
        mix_w = (w_in[layer], gla_w_up[layer], gla_b[layer], gla_norm[layer], ssd_conv_w[layer],
                 ssd_conv_b[layer], ssd_dt_bias[layer], ssd_a_log[layer], ssd_d[layer], ssd_norm[layer])
        moe_w = (router_group_w[layer], router_group_b[layer], router_expert_w[layer], router_expert_b[layer],
                 expert_w_gate[layer], expert_w_up[layer], expert_w_down[layer])
        hc = modulate(rmsnorm(h_ctx, norm_mix[layer]), csh1, csc1)
        yc, ctx_states = token_mixer(hc, 1, ctx_len, zero_states, *mix_w, with_output=not last)
        hx = modulate(rmsnorm(h_lat, norm_mix[layer]), sh1, sc1)
        yx, _ = token_mixer(hx, rows, GRID_W, ctx_states, *mix_w, with_output=True)
        h_lat = h_lat + g1 * (yx @ w_out[layer])
        h_lat = h_lat + g2 * hier_moe(modulate(rmsnorm(h_lat, norm_ffn[layer]), sh2, sc2), *moe_w)
        if not last:
            h_ctx = h_ctx + cg1 * (yc @ w_out[layer])
            h_ctx = h_ctx + cg2 * hier_moe(modulate(rmsnorm(h_ctx, norm_ffn[layer]), csh2, csc2), *moe_w)
    return rmsnorm(h_lat, final_norm)
```

```python
import functools
import math

import numpy as np
import jax
import jax.numpy as jnp
from jax import lax
from jax.experimental import pallas as pl
from jax.experimental.pallas import tpu as pltpu

F32 = jnp.float32
BF16 = jnp.bfloat16

D_MODEL = 1024
GRID_W = 64
EPS = 1e-6

GLA_HEADS = 4
GLA_DK = 128
GLA_DV = 256
GLA_KEY = GLA_HEADS * GLA_DK
GLA_VAL = GLA_HEADS * GLA_DV
GLA_RANK = 16
GLA_GATE_NORM = 16.0

SSD_HEADS = 16
SSD_P = 64
SSD_INNER = SSD_HEADS * SSD_P
SSD_GROUPS = 2
SSD_HPG = SSD_HEADS // SSD_GROUPS
SSD_N = 128
SSD_XBC = SSD_INNER + 2 * SSD_GROUPS * SSD_N
SSD_GW = SSD_HPG * SSD_P

N_GROUPS = 4
EXPERTS_PER_GROUP = 8
N_EXPERTS = N_GROUPS * EXPERTS_PER_GROUP
D_FF = 512

LANES = 128
VMEM_LIMIT = 56 * 1024 * 1024

PC_Q = 0
PC_K = PC_Q + GLA_KEY
PC_V = PC_K + GLA_KEY
PC_R = PC_V + GLA_VAL
PC_Z = PC_R + GLA_VAL
PC_XBC = PC_Z + SSD_INNER
PROJ_W = PC_XBC + SSD_XBC
SMALL_W = LANES * (1 + 2 * SSD_GROUPS)

CHUNK = 128
EXPERT_LANE0 = 32


def _cparams(sem):
    return pltpu.CompilerParams(dimension_semantics=sem, vmem_limit_bytes=VMEM_LIMIT)


def _silu(x):
    return x / (1.0 + jnp.exp(-x))


def _softplus(x):
    return jnp.maximum(x, 0.0) + jnp.log1p(jnp.exp(-jnp.abs(x)))


def _bdot(a, b):
    return jnp.dot(a.astype(BF16), b.astype(BF16), preferred_element_type=F32)


def _bdot_nt(a, b):
    return lax.dot_general(a.astype(BF16), b.astype(BF16), (((1,), (1,)), ((), ())),
                           preferred_element_type=F32)


def _adaln_body(c_ref, w_ref, b_ref, o_ref):
    o_ref[...] = _bdot(_silu(c_ref[...]), w_ref[...]) + b_ref[...]


def _adaln(c8, w, b):
    n = w.shape[1]
    bn = 1024
    return pl.pallas_call(
        _adaln_body,
        grid=(n // bn,),
        in_specs=[pl.BlockSpec((8, D_MODEL), lambda j: (0, 0)),
                  pl.BlockSpec((D_MODEL, bn), lambda j: (0, j)),
                  pl.BlockSpec((1, bn), lambda j: (0, j))],
        out_specs=pl.BlockSpec((8, bn), lambda j: (0, j)),
        out_shape=jax.ShapeDtypeStruct((8, n), F32),
        compiler_params=_cparams(("arbitrary",)),
        name="adaln",
    )(c8, w, b)


INPROJ_TM = 256
INPROJ_NC = 512


def _inproj_body(ctx_ref, x_ref, mod_ref, nw_ref, w_ref, ws_ref, proj_ref, small_ref):
    is_ctx = pl.program_id(1) == 0
    xin = jnp.where(is_ctx, ctx_ref[0], x_ref[0])
    m = mod_ref[0]
    shift = jnp.where(is_ctx, m[0:1], m[2:3])
    scale = jnp.where(is_ctx, m[1:2], m[3:4])
    ms = jnp.mean(xin * xin, axis=-1, keepdims=True)
    h = xin * lax.rsqrt(ms + EPS) * nw_ref[...]
    hb = (h * (1.0 + scale) + shift).astype(BF16)
    for n in range(PROJ_W // INPROJ_NC):
        sl = slice(n * INPROJ_NC, (n + 1) * INPROJ_NC)
        proj_ref[0, :, sl] = jnp.dot(hb, w_ref[:, sl], preferred_element_type=F32).astype(BF16)
    small_ref[0] = jnp.dot(hb, ws_ref[...], preferred_element_type=F32)


def _inproj(ctx, x, mod1, norm_w, w_main, w_small):
    bn, seq, _ = x.shape
    ctx_len = ctx.shape[1]
    tm = INPROJ_TM
    assert ctx_len == tm and seq % tm == 0
    tt = ctx_len + seq
    nj = tt // tm
    return pl.pallas_call(
        _inproj_body,
        grid=(bn, nj),
        in_specs=[pl.BlockSpec((1, tm, D_MODEL), lambda b, j: (b, 0, 0)),
                  pl.BlockSpec((1, tm, D_MODEL), lambda b, j: (b, jnp.maximum(j - 1, 0), 0)),
                  pl.BlockSpec((1, 4, D_MODEL), lambda b, j: (b, 0, 0)),
                  pl.BlockSpec((1, D_MODEL), lambda b, j: (0, 0)),
                  pl.BlockSpec((D_MODEL, PROJ_W), lambda b, j: (0, 0)),
                  pl.BlockSpec((D_MODEL, SMALL_W), lambda b, j: (0, 0))],
        out_specs=[pl.BlockSpec((1, tm, PROJ_W), lambda b, j: (b, j, 0)),
                   pl.BlockSpec((1, tm, SMALL_W), lambda b, j: (b, j, 0))],
        out_shape=[jax.ShapeDtypeStruct((bn, tt, PROJ_W), BF16),
                   jax.ShapeDtypeStruct((bn, tt, SMALL_W), F32)],
        compiler_params=_cparams(("arbitrary", "arbitrary")),
        name="inproj",
    )(ctx, x, mod1, norm_w, w_main, w_small)


CONV_CB = 256
CONV_TT = 256


def _conv_body(ctx_len, seq, p_ref, w_ref, b_ref, o_ref, s_ref):
    lat0 = ctx_len + GRID_W
    tt = CONV_TT
    n_tiles = seq // tt
    zeros = jnp.zeros((GRID_W, CONV_CB), F32)
    s_ref[0:ctx_len, :] = p_ref[0, 0:ctx_len, :].astype(F32)
    s_ref[ctx_len:lat0, :] = zeros
    s_ref[lat0 + seq:lat0 + seq + GRID_W, :] = zeros

    def copy_tile(t, c):
        src = pl.multiple_of(ctx_len + t * tt, tt)
        dst = pl.multiple_of(lat0 + t * tt, GRID_W)
        s_ref[pl.ds(dst, tt), :] = p_ref[0, pl.ds(src, tt), :].astype(F32)
        return c

    lax.fori_loop(0, n_tiles, copy_tile, 0)

    w = w_ref[...]
    bias = b_ref[...]
    pos = lax.broadcasted_iota(jnp.int32, (tt, 1), 0)

    def taps(xr, kh, first, last, acc):
        xm = jnp.where(first, 0.0, pltpu.roll(xr, 1, 0))
        xp = jnp.where(last, 0.0, pltpu.roll(xr, tt - 1, 0))
        return (acc + xm * w[3 * kh:3 * kh + 1] + xr * w[3 * kh + 1:3 * kh + 2]
                + xp * w[3 * kh + 2:3 * kh + 3])

    acc = taps(s_ref[0:ctx_len, :], 1, pos == 0, pos == ctx_len - 1,
               jnp.broadcast_to(bias, (tt, CONV_CB)))
    o_ref[0, 0:ctx_len, :] = _silu(acc).astype(BF16)

    col = pos % GRID_W
    first = col == 0
    last = col == GRID_W - 1

    def tile(t, c):
        base = lat0 + t * tt
        acc = jnp.broadcast_to(bias, (tt, CONV_CB))
        for kh in range(3):
            start = pl.multiple_of(base + (kh - 1) * GRID_W, GRID_W)
            acc = taps(s_ref[pl.ds(start, tt), :], kh, first, last, acc)
        dst = pl.multiple_of(ctx_len + t * tt, tt)
        o_ref[0, pl.ds(dst, tt), :] = _silu(acc).astype(BF16)
        return c

    lax.fori_loop(0, n_tiles, tile, 0)


def _conv(proj, conv_w9, conv_b, ctx_len, seq):
    bn, tt, _ = proj.shape
    assert ctx_len == CONV_TT and seq % CONV_TT == 0 and CONV_TT % GRID_W == 0
    ncb = SSD_XBC // CONV_CB
    cb0 = PC_XBC // CONV_CB
    return pl.pallas_call(
        functools.partial(_conv_body, ctx_len, seq),
        grid=(bn, ncb),
        in_specs=[pl.BlockSpec((1, tt, CONV_CB), lambda b, j: (b, 0, cb0 + j)),
                  pl.BlockSpec((9, CONV_CB), lambda b, j: (0, j)),
                  pl.BlockSpec((1, CONV_CB), lambda b, j: (0, j))],
        out_specs=pl.BlockSpec((1, tt, CONV_CB), lambda b, j: (b, 0, j)),
        out_shape=jax.ShapeDtypeStruct((bn, tt, SSD_XBC), BF16),
        scratch_shapes=[pltpu.VMEM((tt + 2 * GRID_W, CONV_CB), F32)],
        compiler_params=_cparams(("arbitrary", "arbitrary")),
        name="conv",
    )(proj, conv_w9, conv_b)


def _chunk_index(n_ctx, n_steps):
    def idx(d, s):
        bwd = jnp.where(s < n_ctx, n_ctx - 1 - s, n_steps + n_ctx - 1 - s)
        return jnp.where(d == 0, s, bwd)
    return idx


def _scan_sum(a, rev):
    n = a.shape[0]
    row = lax.broadcasted_iota(jnp.int32, (n, 1), 0)
    sh = 1
    while sh < n:
        if rev:
            a = a + jnp.where(row < n - sh, pltpu.roll(a, n - sh, 0), 0.0)
        else:
            a = a + jnp.where(row >= sh, pltpu.roll(a, sh, 0), 0.0)
        sh *= 2
    return a


def _level_matrix(n, rev):
    i = np.arange(n)[:, None]
    j = np.arange(n)[None, :]
    x = i ^ j
    lv = np.where(x > 0, np.floor(np.log2(np.maximum(x, 1))).astype(np.int64) + 1, 0)
    earlier = (j > i) if rev else (j < i)
    return np.where(earlier, lv, 0).astype(np.int32)


def _gla_chunk(rev, n_ctx, q_ref, k_ref, v_ref, sm_ref, wup_ref, gb_ref, lv_ref, o_ref, s_ref):
    n = CHUNK
    step = pl.program_id(3)
    q = q_ref[0].astype(F32) * (GLA_DK ** -0.5)
    k = k_ref[0].astype(F32)
    v = v_ref[0]
    x = jnp.dot(sm_ref[0].astype(BF16), wup_ref[0], preferred_element_type=F32) + gb_ref[0]
    la = (jnp.minimum(x, 0.0) - jnp.log1p(jnp.exp(-jnp.abs(x)))) / GLA_GATE_NORM
    b = _scan_sum(la, rev)
    row = lax.broadcasted_iota(jnp.int32, (n, 1), 0)
    b_end = b[0:1] if rev else b[n - 1:n]

    kh = (k * jnp.exp(b_end - b)).T.astype(BF16)
    dec_col = jnp.exp(jnp.broadcast_to(b_end, (n, GLA_DK))).T[:, 0:1]
    s_old = s_ref[...]

    @pl.when(step >= n_ctx)
    def _():
        lv = lv_ref[0]
        att = jnp.zeros((n, n), F32)
        e_h = b
        h = 1
        level = 1
        while h < n:
            upper = (row & h) != 0
            if rev:
                e = jnp.where(upper, e_h - b, b - pltpu.roll(e_h, n - h, 0))
            else:
                e = jnp.where(upper, b - pltpu.roll(e_h, h, 0), e_h - b)
            w = jnp.exp(e)
            a_l = _bdot_nt(q * w, k * w)
            att = att + jnp.where(lv == level, a_l, 0.0)
            if 2 * h < n:
                if rev:
                    e_h = jnp.where(upper, pltpu.roll(e_h, h, 0), e_h)
                else:
                    e_h = jnp.where(upper, e_h, pltpu.roll(e_h, n - h, 0))
            h *= 2
            level += 1
        o = jnp.dot(att.astype(BF16), v, preferred_element_type=F32)
        o = o + jnp.sum(q * k, axis=-1, keepdims=True) * v.astype(F32)
        o = o + _bdot(q * jnp.exp(b), s_old)
        c = jnp.where(pl.program_id(2) == 0, step,
                      jnp.where(step < n_ctx, n_ctx - 1 - step,
                                pl.num_programs(3) + n_ctx - 1 - step))
        dst = pl.multiple_of((c - n_ctx) * n, n)
        if rev:
            o_ref[0, pl.ds(dst, n), :] += o
        else:
            o_ref[0, pl.ds(dst, n), :] = o

    s_ref[...] = dec_col * s_old + jnp.dot(kh, v, preferred_element_type=F32)


def _gla_body(n_ctx, *refs):
    s_ref = refs[-1]

    @pl.when(pl.program_id(3) == 0)
    def _():
        s_ref[...] = jnp.zeros_like(s_ref)

    @pl.when(pl.program_id(2) == 0)
    def _():
        _gla_chunk(False, n_ctx, *refs)

    @pl.when(pl.program_id(2) == 1)
    def _():
        _gla_chunk(True, n_ctx, *refs)


def _gla(proj, small, wup_pad, gla_b3, lvm, ctx_len, seq):
    bn, tt, _ = proj.shape
    n = CHUNK
    assert ctx_len % n == 0 and seq % n == 0
    n_ctx = ctx_len // n
    n_steps = tt // n
    cidx = _chunk_index(n_ctx, n_steps)
    kb = PC_K // GLA_DK
    vb = PC_V // GLA_DV
    return pl.pallas_call(
        functools.partial(_gla_body, n_ctx),
        grid=(bn, GLA_HEADS, 2, n_steps),
        in_specs=[pl.BlockSpec((1, n, GLA_DK), lambda b, h, d, s: (b, cidx(d, s), h)),
                  pl.BlockSpec((1, n, GLA_DK), lambda b, h, d, s: (b, cidx(d, s), kb + h)),
                  pl.BlockSpec((1, n, GLA_DV), lambda b, h, d, s: (b, cidx(d, s), vb + h)),
                  pl.BlockSpec((1, n, LANES), lambda b, h, d, s: (b, cidx(d, s), 0)),
                  pl.BlockSpec((1, LANES, GLA_DK), lambda b, h, d, s: (d, 0, h)),
                  pl.BlockSpec((1, 1, GLA_DK), lambda b, h, d, s: (d, 0, h)),
                  pl.BlockSpec((1, n, n), lambda b, h, d, s: (d, 0, 0))],
        out_specs=pl.BlockSpec((1, seq, GLA_DV), lambda b, h, d, s: (b, 0, h)),
        out_shape=jax.ShapeDtypeStruct((bn, seq, GLA_VAL), F32),
        scratch_shapes=[pltpu.VMEM((GLA_DK, GLA_DV), F32)],
        compiler_params=_cparams(("arbitrary",) * 4),
        name="gla",
    )(proj, proj, proj, small, wup_pad, gla_b3, lvm)


def _ssd_chunk(rev, n_ctx, x_ref, bm_ref, cm_ref, sm_ref, dtb_ref, alog_ref, dsk_ref, y_ref, s_ref, xs_ref):
    n = CHUNK
    step = pl.program_id(3)
    lane = lax.broadcasted_iota(jnp.int32, (1, LANES), 1)
    head_lane = lane < SSD_HPG
    dt = _softplus(sm_ref[0] + dtb_ref[0, 0])
    a_neg = jnp.where(head_lane, -jnp.exp(alog_ref[0, 0]), 0.0)
    acum = _scan_sum(dt * a_neg, rev)
    acum_t = acum.T
    a_end = acum[0:1] if rev else acum[n - 1:n]
    x = x_ref[0].astype(F32)
    cm = cm_ref[0]
    bm = bm_ref[0]
    s_old = s_ref[...]
    emit = step >= n_ctx

    @pl.when(emit)
    def _():
        ii = lax.broadcasted_iota(jnp.int32, (n, n), 0)
        jj = lax.broadcasted_iota(jnp.int32, (n, n), 1)
        causal = (jj >= ii) if rev else (jj <= ii)
        cb = _bdot_nt(cm, bm)
        y_off = jnp.dot(cm, s_old.astype(BF16), preferred_element_type=F32)
        c = jnp.where(pl.program_id(2) == 0, step,
                      jnp.where(step < n_ctx, n_ctx - 1 - step,
                                pl.num_programs(3) + n_ctx - 1 - step))
        dst = pl.multiple_of((c - n_ctx) * n, n)
        for hh in range(SSD_HPG):
            hs = slice(hh * SSD_P, (hh + 1) * SSD_P)
            col = acum[:, hh:hh + 1]
            seg = jnp.exp(jnp.where(causal, col - acum_t[hh:hh + 1, :], -jnp.inf))
            xdt = x[:, hs] * dt[:, hh:hh + 1]
            yh = _bdot(cb * seg, xdt) + y_off[:, hs] * jnp.exp(col)
            if rev:
                y_ref[0, pl.ds(dst, n), hs] += yh
            else:
                y_ref[0, pl.ds(dst, n), hs] = yh + x[:, hs] * dsk_ref[:, hs]

    for hh in range(SSD_HPG):
        hs = slice(hh * SSD_P, (hh + 1) * SSD_P)
        col = acum[:, hh:hh + 1]
        xs_ref[:, hs] = x[:, hs] * (dt[:, hh:hh + 1] * jnp.exp(a_end[:, hh:hh + 1] - col))
    new = jnp.dot(bm.astype(F32).T.astype(BF16), xs_ref[...].astype(BF16), preferred_element_type=F32)
    for hh in range(SSD_HPG):
        hs = slice(hh * SSD_P, (hh + 1) * SSD_P)
        s_ref[:, hs] = s_old[:, hs] * jnp.exp(a_end[:, hh:hh + 1]) + new[:, hs]


def _ssd_body(n_ctx, *refs):
    s_ref = refs[-2]

    @pl.when(pl.program_id(3) == 0)
    def _():
        s_ref[...] = jnp.zeros_like(s_ref)

    @pl.when(pl.program_id(2) == 0)
    def _():
        _ssd_chunk(False, n_ctx, *refs)

    @pl.when(pl.program_id(2) == 1)
    def _():
        _ssd_chunk(True, n_ctx, *refs)


def _ssd(xact, small, dtb4, alog4, dskip_row, ctx_len, seq):
    bn, tt, _ = xact.shape
    n = CHUNK
    n_ctx = ctx_len // n
    n_steps = tt // n
    cidx = _chunk_index(n_ctx, n_steps)
    bb = SSD_INNER // SSD_N
    return pl.pallas_call(
        functools.partial(_ssd_body, n_ctx),
        grid=(bn, SSD_GROUPS, 2, n_steps),
        in_specs=[pl.BlockSpec((1, n, SSD_GW), lambda b, g, d, s: (b, cidx(d, s), g)),
                  pl.BlockSpec((1, n, SSD_N), lambda b, g, d, s: (b, cidx(d, s), bb + g)),
                  pl.BlockSpec((1, n, SSD_N), lambda b, g, d, s: (b, cidx(d, s), bb + SSD_GROUPS + g)),
                  pl.BlockSpec((1, n, LANES), lambda b, g, d, s: (b, cidx(d, s), 1 + SSD_GROUPS * d + g)),
                  pl.BlockSpec((1, 1, 1, LANES), lambda b, g, d, s: (d, g, 0, 0)),
                  pl.BlockSpec((1, 1, 1, LANES), lambda b, g, d, s: (d, g, 0, 0)),
                  pl.BlockSpec((1, SSD_GW), lambda b, g, d, s: (0, g))],
        out_specs=pl.BlockSpec((1, seq, SSD_GW), lambda b, g, d, s: (b, 0, g)),
        out_shape=jax.ShapeDtypeStruct((bn, seq, SSD_INNER), F32),
        scratch_shapes=[pltpu.VMEM((SSD_N, SSD_GW), F32), pltpu.VMEM((n, SSD_GW), F32)],
        compiler_params=_cparams(("arbitrary",) * 4),
        name="ssd",
    )(xact, xact, xact, small, dtb4, alog4, dskip_row)


OUT_TM = 256


def _rms(x, w):
    return x * lax.rsqrt(jnp.mean(x * x, axis=-1, keepdims=True) + EPS) * w


def _outproj_body(o_ref, r_ref, y_ref, z_ref, x_ref, mod_ref, gn_ref, sn_ref, fn_ref, wo_ref, wr_ref, br_ref,
                  hlat_ref, h2_ref, comb_ref):
    tm = OUT_TM
    parts = []
    for h in range(GLA_HEADS):
        hs = slice(h * GLA_DV, (h + 1) * GLA_DV)
        parts.append((_rms(o_ref[0, :, hs], gn_ref[...]) * _silu(r_ref[0, :, hs].astype(F32))).astype(BF16))
    for g in range(SSD_GROUPS):
        gs = slice(g * SSD_GW, (g + 1) * SSD_GW)
        yg = y_ref[0, :, gs] * _silu(z_ref[0, :, gs].astype(F32))
        parts.append(_rms(yg, sn_ref[:, gs]).astype(BF16))
    mix = jnp.concatenate(parts, axis=-1)
    m = mod_ref[0]
    hlat = x_ref[0] + m[0:1] * jnp.dot(mix, wo_ref[...], preferred_element_type=F32)
    hlat_ref[0] = hlat
    h2 = (_rms(hlat, fn_ref[...]) * (1.0 + m[2:3]) + m[1:2]).astype(BF16)
    h2_ref[0] = h2

    lg = jnp.dot(h2, wr_ref[...], preferred_element_type=F32) + br_ref[...]
    lanef = lax.broadcasted_iota(jnp.int32, (tm, LANES), 1).astype(F32)
    ninf = -jnp.inf
    is_g = lanef < float(N_GROUPS)
    gl = jnp.where(is_g, lg, ninf)
    gmax = jnp.max(gl, axis=-1, keepdims=True)
    gsel = jnp.min(jnp.where(gl == gmax, lanef, float(LANES)), axis=-1, keepdims=True)
    pg = 1.0 / jnp.sum(jnp.where(is_g, jnp.exp(lg - gmax), 0.0), axis=-1, keepdims=True)
    lo = float(EXPERT_LANE0) + float(EXPERTS_PER_GROUP) * gsel
    in_grp = (lanef >= lo) & (lanef < lo + float(EXPERTS_PER_GROUP))
    el = jnp.where(in_grp, lg, ninf)
    v1 = jnp.max(el, axis=-1, keepdims=True)
    i1 = jnp.min(jnp.where(el == v1, lanef, float(LANES)), axis=-1, keepdims=True)
    el2 = jnp.where(lanef == i1, ninf, el)
    v2 = jnp.max(el2, axis=-1, keepdims=True)
    i2 = jnp.min(jnp.where(el2 == v2, lanef, float(LANES)), axis=-1, keepdims=True)
    t = jnp.exp(v2 - v1)
    w1 = pg / (1.0 + t)
    w2 = pg * t / (1.0 + t)
    comb_ref[0] = jnp.where(lanef == i1, w1, 0.0) + jnp.where(lanef == i2, w2, 0.0)


def _outproj(o_gla, proj, y_ssd, x, mod2, gn_row, sn_row, fn_row, w_out, w_router, b_router, ctx_len):
    bn, seq, _ = x.shape
    tm = OUT_TM
    assert ctx_len % tm == 0 and seq % tm == 0
    j0 = ctx_len // tm
    rb = PC_R // GLA_VAL
    zb = PC_Z // SSD_INNER
    tok = lambda b, j: (b, j, 0)
    const = lambda b, j: (0, 0)
    return pl.pallas_call(
        _outproj_body,
        grid=(bn, seq // tm),
        in_specs=[pl.BlockSpec((1, tm, GLA_VAL), tok),
                  pl.BlockSpec((1, tm, GLA_VAL), lambda b, j: (b, j0 + j, rb)),
                  pl.BlockSpec((1, tm, SSD_INNER), tok),
                  pl.BlockSpec((1, tm, SSD_INNER), lambda b, j: (b, j0 + j, zb)),
                  pl.BlockSpec((1, tm, D_MODEL), tok),
                  pl.BlockSpec((1, 4, D_MODEL), lambda b, j: (b, 0, 0)),
                  pl.BlockSpec((1, GLA_DV), const),
                  pl.BlockSpec((1, SSD_INNER), const),
                  pl.BlockSpec((1, D_MODEL), const),
                  pl.BlockSpec((GLA_VAL + SSD_INNER, D_MODEL), const),
                  pl.BlockSpec((D_MODEL, LANES), const),
                  pl.BlockSpec((1, LANES), const)],
        out_specs=[pl.BlockSpec((1, tm, D_MODEL), tok),
                   pl.BlockSpec((1, tm, D_MODEL), tok),
                   pl.BlockSpec((1, tm, LANES), tok)],
        out_shape=[jax.ShapeDtypeStruct((bn, seq, D_MODEL), F32),
                   jax.ShapeDtypeStruct((bn, seq, D_MODEL), BF16),
                   jax.ShapeDtypeStruct((bn, seq, LANES), F32)],
        compiler_params=_cparams(("arbitrary", "arbitrary")),
        name="outproj",
    )(o_gla, proj, y_ssd, proj, x, mod2, gn_row, sn_row, fn_row, w_out, w_router, b_router)


MOE_TM = 1024


def _moe_body(h2_ref, comb_ref, hlat_ref, g2_ref, fn_ref, wg_ref, wu_ref, wd_ref, out_ref, acc_ref):
    e = pl.program_id(2)

    @pl.when(e == 0)
    def _():
        acc_ref[...] = jnp.zeros_like(acc_ref)

    h2 = h2_ref[0]
    lane = lax.broadcasted_iota(jnp.int32, (1, LANES), 1)
    cw = jnp.sum(jnp.where(lane == EXPERT_LANE0 + e, comb_ref[0], 0.0), axis=-1, keepdims=True)
    gate = jnp.dot(h2, wg_ref[0].astype(BF16), preferred_element_type=F32)
    up = jnp.dot(h2, wu_ref[0].astype(BF16), preferred_element_type=F32)
    he = (_silu(gate) * up).astype(BF16)
    acc_ref[...] += cw * jnp.dot(he, wd_ref[0].astype(BF16), preferred_element_type=F32)

    @pl.when(e == pl.num_programs(2) - 1)
    def _():
        out_ref[0] = _rms(hlat_ref[0] + g2_ref[0] * acc_ref[...], fn_ref[...])


def _moe(h2, comb, hlat, g2, fn_row, w_gate, w_up, w_down):
    bn, seq, _ = h2.shape
    tm = MOE_TM
    tok = lambda b, j, e: (b, j, 0)
    return pl.pallas_call(
        _moe_body,
        grid=(bn, seq // tm, N_EXPERTS),
        in_specs=[pl.BlockSpec((1, tm, D_MODEL), tok),
                  pl.BlockSpec((1, tm, LANES), tok),
                  pl.BlockSpec((1, tm, D_MODEL), tok),
                  pl.BlockSpec((1, 1, D_MODEL), lambda b, j, e: (b, 0, 0)),
                  pl.BlockSpec((1, D_MODEL), lambda b, j, e: (0, 0)),
                  pl.BlockSpec((1, D_MODEL, D_FF), lambda b, j, e: (e, 0, 0)),
                  pl.BlockSpec((1, D_MODEL, D_FF), lambda b, j, e: (e, 0, 0)),
                  pl.BlockSpec((1, D_FF, D_MODEL), lambda b, j, e: (e, 0, 0))],
        out_specs=pl.BlockSpec((1, tm, D_MODEL), tok),
        out_shape=jax.ShapeDtypeStruct((bn, seq, D_MODEL), F32),
        scratch_shapes=[pltpu.VMEM((tm, D_MODEL), F32)],
        compiler_params=_cparams(("arbitrary",) * 3),
        name="moe",
    )(h2, comb, hlat, g2, fn_row, w_gate, w_up, w_down)


def _prep_weights(w_in, gla_w_up, gla_b, ssd_dt_bias, ssd_a_log, ssd_d, router_group_w, router_group_b,
                  router_expert_w, router_expert_b):
    off_k = GLA_KEY
    off_v = 2 * GLA_KEY
    off_r = off_v + GLA_VAL
    off_g = off_r + GLA_VAL
    off_z = off_g + 2 * GLA_RANK
    off_xbc = off_z + SSD_INNER
    off_dt = off_xbc + SSD_XBC
    w_main = jnp.concatenate([w_in[:, :off_g], w_in[:, off_z:off_dt]], axis=1).astype(BF16)
    blocks = [jnp.pad(w_in[:, off_g:off_z], ((0, 0), (0, LANES - 2 * GLA_RANK)))]
    for d in range(2):
        for g in range(SSD_GROUPS):
            c0 = off_dt + d * SSD_HEADS + g * SSD_HPG
            blocks.append(jnp.pad(w_in[:, c0:c0 + SSD_HPG], ((0, 0), (0, LANES - SSD_HPG))))
    w_small = jnp.concatenate(blocks, axis=1).astype(BF16)
    wup = jnp.zeros((2, LANES, GLA_KEY), F32)
    wup = wup.at[0, 0:GLA_RANK].set(gla_w_up[0]).at[1, GLA_RANK:2 * GLA_RANK].set(gla_w_up[1]).astype(BF16)
    pad_h = lambda t: jnp.pad(t.reshape(2, SSD_GROUPS, 1, SSD_HPG), ((0, 0), (0, 0), (0, 0), (0, LANES - SSD_HPG)))
    w_router = jnp.zeros((D_MODEL, LANES), F32)
    w_router = w_router.at[:, 0:N_GROUPS].set(router_group_w)
    w_router = w_router.at[:, EXPERT_LANE0:EXPERT_LANE0 + N_EXPERTS].set(router_expert_w).astype(BF16)
    b_router = jnp.zeros((1, LANES), F32).at[0, 0:N_GROUPS].set(router_group_b)
    b_router = b_router.at[0, EXPERT_LANE0:EXPERT_LANE0 + N_EXPERTS].set(router_expert_b)
    return (w_main, w_small, wup, gla_b.reshape(2, 1, GLA_KEY), pad_h(ssd_dt_bias), pad_h(ssd_a_log),
            jnp.repeat(ssd_d, SSD_P)[None, :], w_router, b_router)


def kernel(x, c, ctx, c_ctx, w_ada, b_ada, norm_mix, norm_ffn, w_in, gla_w_up, gla_b, gla_norm, ssd_conv_w,
           ssd_conv_b, ssd_dt_bias, ssd_a_log, ssd_d, ssd_norm, w_out, router_group_w, router_group_b,
           router_expert_w, router_expert_b, expert_w_gate, expert_w_up, expert_w_down, final_norm):
    assert w_ada.shape[0] == 1, "single-layer block"
    bn, seq, d = x.shape
    ctx_len = ctx.shape[1]
    assert seq // GRID_W == GRID_W and d == D_MODEL

    c8 = jnp.concatenate([c, c_ctx[None, :], jnp.zeros((8 - bn - 1, d), F32)], axis=0)
    mods = _adaln(c8, w_ada[0], b_ada[0][None, :])
    sh1, sc1, g1, sh2, sc2, g2 = [mods[:, i * d:(i + 1) * d] for i in range(6)]
    ctx_row = lambda t: jnp.broadcast_to(t[bn:bn + 1], (bn, d))
    mod1 = jnp.stack([ctx_row(sh1), ctx_row(sc1), sh1[:bn], sc1[:bn]], axis=1)
    mod2 = jnp.stack([g1[:bn], sh2[:bn], sc2[:bn], g2[:bn]], axis=1)

    (w_main, w_small, wup, gla_b3, dtb4, alog4, dskip_row, w_router, b_router) = _prep_weights(
        w_in[0], gla_w_up[0], gla_b[0], ssd_dt_bias[0], ssd_a_log[0], ssd_d[0], router_group_w[0],
        router_group_b[0], router_expert_w[0], router_expert_b[0])

    proj, small = _inproj(ctx, x, mod1, norm_mix, w_main, w_small)
    xact = _conv(proj, ssd_conv_w[0].reshape(9, SSD_XBC), ssd_conv_b, ctx_len, seq)
    lvm = jnp.asarray(np.stack([_level_matrix(CHUNK, False), _level_matrix(CHUNK, True)]))
    o_gla = _gla(proj, small, wup, gla_b3, lvm, ctx_len, seq)
    y_ssd = _ssd(xact, small, dtb4, alog4, dskip_row, ctx_len, seq)
    hlat, h2, comb = _outproj(o_gla, proj, y_ssd, x, mod2, gla_norm, ssd_norm, norm_ffn,
                              w_out[0].astype(BF16), w_router, b_router, ctx_len)
    return _moe(h2, comb, hlat, mod2[:, 3:4, :], final_norm[None, :], expert_w_gate[0], expert_w_up[0],
                expert_w_down[0])
```

```python
import functools
import math

import numpy as np
import jax
import jax.numpy as jnp
from jax import lax
from jax.experimental import pallas as pl
from jax.experimental.pallas import tpu as pltpu

F32 = jnp.float32
BF16 = jnp.bfloat16

D_MODEL = 1024
GRID_W = 64
EPS = 1e-6

GLA_HEADS = 4
GLA_DK = 128
GLA_DV = 256
GLA_KEY = GLA_HEADS * GLA_DK
GLA_VAL = GLA_HEADS * GLA_DV
GLA_RANK = 16
GLA_GATE_NORM = 16.0

SSD_HEADS = 16
SSD_P = 64
SSD_INNER = SSD_HEADS * SSD_P
SSD_GROUPS = 2
SSD_HPG = SSD_HEADS // SSD_GROUPS
SSD_N = 128
SSD_XBC = SSD_INNER + 2 * SSD_GROUPS * SSD_N
SSD_GW = SSD_HPG * SSD_P

N_GROUPS = 4
EXPERTS_PER_GROUP = 8
N_EXPERTS = N_GROUPS * EXPERTS_PER_GROUP
D_FF = 512

LANES = 128
VMEM_LIMIT = 56 * 1024 * 1024

PC_Q = 0
PC_K = PC_Q + GLA_KEY
PC_V = PC_K + GLA_KEY
PC_R = PC_V + GLA_VAL
PC_Z = PC_R + GLA_VAL
PC_XBC = PC_Z + SSD_INNER
PROJ_W = PC_XBC + SSD_XBC
SMALL_W = LANES * (1 + 2 * SSD_GROUPS)

CHUNK = 128
EXPERT_LANE0 = 32


def _cparams(sem):
    return pltpu.CompilerParams(dimension_semantics=sem, vmem_limit_bytes=VMEM_LIMIT)


def _silu(x):
    return x / (1.0 + jnp.exp(-x))


def _softplus(x):
    return jnp.maximum(x, 0.0) + jnp.log1p(jnp.exp(-jnp.abs(x)))


def _bdot(a, b):
    return jnp.dot(a.astype(BF16), b.astype(BF16), preferred_element_type=F32)


def _bdot_nt(a, b):
    return lax.dot_general(a.astype(BF16), b.astype(BF16), (((1,), (1,)), ((), ())),
                           preferred_element_type=F32)


def _adaln_body(c_ref, w_ref, b_ref, o_ref):
    o_ref[...] = _bdot(_silu(c_ref[...]), w_ref[...]) + b_ref[...]


def _adaln(c8, w, b):
    n = w.shape[1]
    bn = 1024
    return pl.pallas_call(
        _adaln_body,
        grid=(n // bn,),
        in_specs=[pl.BlockSpec((8, D_MODEL), lambda j: (0, 0)),
                  pl.BlockSpec((D_MODEL, bn), lambda j: (0, j)),
                  pl.BlockSpec((1, bn), lambda j: (0, j))],
        out_specs=pl.BlockSpec((8, bn), lambda j: (0, j)),
        out_shape=jax.ShapeDtypeStruct((8, n), F32),
        compiler_params=_cparams(("arbitrary",)),
        name="adaln",
    )(c8, w, b)


INPROJ_TM = 256
INPROJ_NC = 512


def _inproj_body(ctx_ref, x_ref, mod_ref, nw_ref, w_ref, ws_ref, proj_ref, small_ref):
    is_ctx = pl.program_id(1) == 0
    xin = jnp.where(is_ctx, ctx_ref[0], x_ref[0])
    m = mod_ref[0]
    shift = jnp.where(is_ctx, m[0:1], m[2:3])
    scale = jnp.where(is_ctx, m[1:2], m[3:4])
    ms = jnp.mean(xin * xin, axis=-1, keepdims=True)
    h = xin * lax.rsqrt(ms + EPS) * nw_ref[...]
    hb = (h * (1.0 + scale) + shift).astype(BF16)
    for n in range(PROJ_W // INPROJ_NC):
        sl = slice(n * INPROJ_NC, (n + 1) * INPROJ_NC)
        proj_ref[0, :, sl] = jnp.dot(hb, w_ref[:, sl], preferred_element_type=F32).astype(BF16)
    small_ref[0] = jnp.dot(hb, ws_ref[...], preferred_element_type=F32)


def _inproj(ctx, x, mod1, norm_w, w_main, w_small):
    bn, seq, _ = x.shape
    ctx_len = ctx.shape[1]
    tm = INPROJ_TM
    assert ctx_len == tm and seq % tm == 0
    tt = ctx_len + seq
    nj = tt // tm
    return pl.pallas_call(
        _inproj_body,
        grid=(bn, nj),
        in_specs=[pl.BlockSpec((1, tm, D_MODEL), lambda b, j: (b, 0, 0)),
                  pl.BlockSpec((1, tm, D_MODEL), lambda b, j: (b, jnp.maximum(j - 1, 0), 0)),
                  pl.BlockSpec((1, 4, D_MODEL), lambda b, j: (b, 0, 0)),
                  pl.BlockSpec((1, D_MODEL), lambda b, j: (0, 0)),
                  pl.BlockSpec((D_MODEL, PROJ_W), lambda b, j: (0, 0)),
                  pl.BlockSpec((D_MODEL, SMALL_W), lambda b, j: (0, 0))],
        out_specs=[pl.BlockSpec((1, tm, PROJ_W), lambda b, j: (b, j, 0)),
                   pl.BlockSpec((1, tm, SMALL_W), lambda b, j: (b, j, 0))],
        out_shape=[jax.ShapeDtypeStruct((bn, tt, PROJ_W), BF16),
                   jax.ShapeDtypeStruct((bn, tt, SMALL_W), F32)],
        compiler_params=_cparams(("arbitrary", "arbitrary")),
        name="inproj",
    )(ctx, x, mod1, norm_w, w_main, w_small)


CONV_CB = 256
CONV_TT = 256


def _conv_body(ctx_len, seq, p_ref, w_ref, b_ref, o_ref, s_ref):
    lat0 = ctx_len + GRID_W
    tt = CONV_TT
    n_tiles = seq // tt
    zeros = jnp.zeros((GRID_W, CONV_CB), F32)
    s_ref[0:ctx_len, :] = p_ref[0, 0:ctx_len, :].astype(F32)
    s_ref[ctx_len:lat0, :] = zeros
    s_ref[lat0 + seq:lat0 + seq + GRID_W, :] = zeros

    def copy_tile(t, c):
        src = pl.multiple_of(ctx_len + t * tt, tt)
        dst = pl.multiple_of(lat0 + t * tt, GRID_W)
        s_ref[pl.ds(dst, tt), :] = p_ref[0, pl.ds(src, tt), :].astype(F32)
        return c

    lax.fori_loop(0, n_tiles, copy_tile, 0)

    w = w_ref[...]
    bias = b_ref[...]
    pos = lax.broadcasted_iota(jnp.int32, (tt, 1), 0)

    def taps(xr, kh, first, last, acc):
        xm = jnp.where(first, 0.0, pltpu.roll(xr, 1, 0))
        xp = jnp.where(last, 0.0, pltpu.roll(xr, tt - 1, 0))
        return (acc + xm * w[3 * kh:3 * kh + 1] + xr * w[3 * kh + 1:3 * kh + 2]
                + xp * w[3 * kh + 2:3 * kh + 3])

    acc = taps(s_ref[0:ctx_len, :], 1, pos == 0, pos == ctx_len - 1,
               jnp.broadcast_to(bias, (tt, CONV_CB)))
    o_ref[0, 0:ctx_len, :] = _silu(acc).astype(BF16)

    col = pos % GRID_W
    first = col == 0
    last = col == GRID_W - 1

    def tile(t, c):
        base = lat0 + t * tt
        acc = jnp.broadcast_to(bias, (tt, CONV_CB))
        for kh in range(3):
            start = pl.multiple_of(base + (kh - 1) * GRID_W, GRID_W)
            acc = taps(s_ref[pl.ds(start, tt), :], kh, first, last, acc)
        dst = pl.multiple_of(ctx_len + t * tt, tt)
        o_ref[0, pl.ds(dst, tt), :] = _silu(acc).astype(BF16)
        return c

    lax.fori_loop(0, n_tiles, tile, 0)


def _conv(proj, conv_w9, conv_b, ctx_len, seq):
    bn, tt, _ = proj.shape
    assert ctx_len == CONV_TT and seq % CONV_TT == 0 and CONV_TT % GRID_W == 0
    ncb = SSD_XBC // CONV_CB
    cb0 = PC_XBC // CONV_CB
    return pl.pallas_call(
        functools.partial(_conv_body, ctx_len, seq),
        grid=(bn, ncb),
        in_specs=[pl.BlockSpec((1, tt, CONV_CB), lambda b, j: (b, 0, cb0 + j)),
                  pl.BlockSpec((9, CONV_CB), lambda b, j: (0, j)),
                  pl.BlockSpec((1, CONV_CB), lambda b, j: (0, j))],
        out_specs=pl.BlockSpec((1, tt, CONV_CB), lambda b, j: (b, 0, j)),
        out_shape=jax.ShapeDtypeStruct((bn, tt, SSD_XBC), BF16),
        scratch_shapes=[pltpu.VMEM((tt + 2 * GRID_W, CONV_CB), F32)],
        compiler_params=_cparams(("arbitrary", "arbitrary")),
        name="conv",
    )(proj, conv_w9, conv_b)


def _chunk_index(n_ctx, n_steps):
    def idx(d, s):
        bwd = jnp.where(s < n_ctx, n_ctx - 1 - s, n_steps + n_ctx - 1 - s)
        return jnp.where(d == 0, s, bwd)
    return idx


def _scan_sum(a, rev):
    n = a.shape[0]
    row = lax.broadcasted_iota(jnp.int32, (n, 1), 0)
    sh = 1
    while sh < n:
        if rev:
            a = a + jnp.where(row < n - sh, pltpu.roll(a, n - sh, 0), 0.0)
        else:
            a = a + jnp.where(row >= sh, pltpu.roll(a, sh, 0), 0.0)
        sh *= 2
    return a


def _level_matrix(n, rev):
    i = np.arange(n)[:, None]
    j = np.arange(n)[None, :]
    x = i ^ j
    lv = np.where(x > 0, np.floor(np.log2(np.maximum(x, 1))).astype(np.int64) + 1, 0)
    earlier = (j > i) if rev else (j < i)
    return np.where(earlier, lv, 0).astype(np.int32)


def _gla_chunk(rev, n_ctx, q_ref, k_ref, v_ref, sm_ref, wup_ref, gb_ref, lv_ref, o_ref, s_ref):
    n = CHUNK
    step = pl.program_id(3)
    q = q_ref[0].astype(F32) * (GLA_DK ** -0.5)
    k = k_ref[0].astype(F32)
    v = v_ref[0]
    x = jnp.dot(sm_ref[0].astype(BF16), wup_ref[0], preferred_element_type=F32) + gb_ref[0]
    la = (jnp.minimum(x, 0.0) - jnp.log1p(jnp.exp(-jnp.abs(x)))) / GLA_GATE_NORM
    b = _scan_sum(la, rev)
    row = lax.broadcasted_iota(jnp.int32, (n, 1), 0)
    b_end = b[0:1] if rev else b[n - 1:n]

    kh = (k * jnp.exp(b_end - b)).T.astype(BF16)
    dec_col = jnp.exp(jnp.broadcast_to(b_end, (n, GLA_DK))).T[:, 0:1]
    s_old = s_ref[...]

    @pl.when(step >= n_ctx)
    def _():
        lv = lv_ref[0]
        att = jnp.zeros((n, n), F32)
        e_h = b
        h = 1
        level = 1
        while h < n:
            upper = (row & h) != 0
            if rev:
                e = jnp.where(upper, e_h - b, b - pltpu.roll(e_h, n - h, 0))
            else:
                e = jnp.where(upper, b - pltpu.roll(e_h, h, 0), e_h - b)
            w = jnp.exp(e)
            a_l = _bdot_nt(q * w, k * w)
            att = att + jnp.where(lv == level, a_l, 0.0)
            if 2 * h < n:
                if rev:
                    e_h = jnp.where(upper, pltpu.roll(e_h, h, 0), e_h)
                else:
                    e_h = jnp.where(upper, e_h, pltpu.roll(e_h, n - h, 0))
            h *= 2
            level += 1
        o = jnp.dot(att.astype(BF16), v, preferred_element_type=F32)
        o = o + jnp.sum(q * k, axis=-1, keepdims=True) * v.astype(F32)
        o = o + _bdot(q * jnp.exp(b), s_old)
        c = jnp.where(pl.program_id(2) == 0, step,
                      jnp.where(step < n_ctx, n_ctx - 1 - step,
                                pl.num_programs(3) + n_ctx - 1 - step))
        dst = pl.multiple_of((c - n_ctx) * n, n)
        if rev:
            o_ref[0, pl.ds(dst, n), :] += o
        else:
            o_ref[0, pl.ds(dst, n), :] = o

    s_ref[...] = dec_col * s_old + jnp.dot(kh, v, preferred_element_type=F32)


def _gla_body(n_ctx, *refs):
    s_ref = refs[-1]

    @pl.when(pl.program_id(3) == 0)
    def _():
        s_ref[...] = jnp.zeros_like(s_ref)

    @pl.when(pl.program_id(2) == 0)
    def _():
        _gla_chunk(False, n_ctx, *refs)

    @pl.when(pl.program_id(2) == 1)
    def _():
        _gla_chunk(True, n_ctx, *refs)


def _gla(proj, small, wup_pad, gla_b3, lvm, ctx_len, seq):
    bn, tt, _ = proj.shape
    n = CHUNK
    assert ctx_len % n == 0 and seq % n == 0
    n_ctx = ctx_len // n
    n_steps = tt // n
    cidx = _chunk_index(n_ctx, n_steps)
    kb = PC_K // GLA_DK
    vb = PC_V // GLA_DV
    return pl.pallas_call(
        functools.partial(_gla_body, n_ctx),
        grid=(bn, GLA_HEADS, 2, n_steps),
        in_specs=[pl.BlockSpec((1, n, GLA_DK), lambda b, h, d, s: (b, cidx(d, s), h)),
                  pl.BlockSpec((1, n, GLA_DK), lambda b, h, d, s: (b, cidx(d, s), kb + h)),
                  pl.BlockSpec((1, n, GLA_DV), lambda b, h, d, s: (b, cidx(d, s), vb + h)),
                  pl.BlockSpec((1, n, LANES), lambda b, h, d, s: (b, cidx(d, s), 0)),
                  pl.BlockSpec((1, LANES, GLA_DK), lambda b, h, d, s: (d, 0, h)),
                  pl.BlockSpec((1, 1, GLA_DK), lambda b, h, d, s: (d, 0, h)),
                  pl.BlockSpec((1, n, n), lambda b, h, d, s: (d, 0, 0))],
        out_specs=pl.BlockSpec((1, seq, GLA_DV), lambda b, h, d, s: (b, 0, h)),
        out_shape=jax.ShapeDtypeStruct((bn, seq, GLA_VAL), F32),
        scratch_shapes=[pltpu.VMEM((GLA_DK, GLA_DV), F32)],
        compiler_params=_cparams(("arbitrary",) * 4),
        name="gla",
    )(proj, proj, proj, small, wup_pad, gla_b3, lvm)


def _ssd_chunk(rev, n_ctx, x_ref, bm_ref, cm_ref, sm_ref, dtb_ref, alog_ref, dsk_ref, y_ref, s_ref, xs_ref):
    n = CHUNK
    step = pl.program_id(3)
    lane = lax.broadcasted_iota(jnp.int32, (1, LANES), 1)
    head_lane = lane < SSD_HPG
    dt = _softplus(sm_ref[0] + dtb_ref[0, 0])
    a_neg = jnp.where(head_lane, -jnp.exp(alog_ref[0, 0]), 0.0)
    acum = _scan_sum(dt * a_neg, rev)
    acum_t = acum.T
    a_end = acum[0:1] if rev else acum[n - 1:n]
    x = x_ref[0].astype(F32)
    cm = cm_ref[0]
    bm = bm_ref[0]
    s_old = s_ref[...]
    emit = step >= n_ctx

    @pl.when(emit)
    def _():
        ii = lax.broadcasted_iota(jnp.int32, (n, n), 0)
        jj = lax.broadcasted_iota(jnp.int32, (n, n), 1)
        causal = (jj >= ii) if rev else (jj <= ii)
        cb = _bdot_nt(cm, bm)
        y_off = jnp.dot(cm, s_old.astype(BF16), preferred_element_type=F32)
        c = jnp.where(pl.program_id(2) == 0, step,
                      jnp.where(step < n_ctx, n_ctx - 1 - step,
                                pl.num_programs(3) + n_ctx - 1 - step))
        dst = pl.multiple_of((c - n_ctx) * n, n)
        for hh in range(SSD_HPG):
            hs = slice(hh * SSD_P, (hh + 1) * SSD_P)
            col = acum[:, hh:hh + 1]
            seg = jnp.exp(jnp.where(causal, col - acum_t[hh:hh + 1, :], -jnp.inf))
            xdt = x[:, hs] * dt[:, hh:hh + 1]
            yh = _bdot(cb * seg, xdt) + y_off[:, hs] * jnp.exp(col)
            if rev:
                y_ref[0, pl.ds(dst, n), hs] += yh
            else:
                y_ref[0, pl.ds(dst, n), hs] = yh + x[:, hs] * dsk_ref[:, hs]

    for hh in range(SSD_HPG):
        hs = slice(hh * SSD_P, (hh + 1) * SSD_P)
        col = acum[:, hh:hh + 1]
        xs_ref[:, hs] = x[:, hs] * (dt[:, hh:hh + 1] * jnp.exp(a_end[:, hh:hh + 1] - col))
    new = jnp.dot(bm.astype(F32).T.astype(BF16), xs_ref[...].astype(BF16), preferred_element_type=F32)
    for hh in range(SSD_HPG):
        hs = slice(hh * SSD_P, (hh + 1) * SSD_P)
        s_ref[:, hs] = s_old[:, hs] * jnp.exp(a_end[:, hh:hh + 1]) + new[:, hs]


def _ssd_body(n_ctx, *refs):
    s_ref = refs[-2]

    @pl.when(pl.program_id(3) == 0)
    def _():
        s_ref[...] = jnp.zeros_like(s_ref)

    @pl.when(pl.program_id(2) == 0)
    def _():
        _ssd_chunk(False, n_ctx, *refs)

    @pl.when(pl.program_id(2) == 1)
    def _():
        _ssd_chunk(True, n_ctx, *refs)


def _ssd(xact, small, dtb4, alog4, dskip_row, ctx_len, seq):
    bn, tt, _ = xact.shape
    n = CHUNK
    n_ctx = ctx_len // n
    n_steps = tt // n
    cidx = _chunk_index(n_ctx, n_steps)
    bb = SSD_INNER // SSD_N
    return pl.pallas_call(
        functools.partial(_ssd_body, n_ctx),
        grid=(bn, SSD_GROUPS, 2, n_steps),
        in_specs=[pl.BlockSpec((1, n, SSD_GW), lambda b, g, d, s: (b, cidx(d, s), g)),
                  pl.BlockSpec((1, n, SSD_N), lambda b, g, d, s: (b, cidx(d, s), bb + g)),
                  pl.BlockSpec((1, n, SSD_N), lambda b, g, d, s: (b, cidx(d, s), bb + SSD_GROUPS + g)),
                  pl.BlockSpec((1, n, LANES), lambda b, g, d, s: (b, cidx(d, s), 1 + SSD_GROUPS * d + g)),
                  pl.BlockSpec((1, 1, 1, LANES), lambda b, g, d, s: (d, g, 0, 0)),
                  pl.BlockSpec((1, 1, 1, LANES), lambda b, g, d, s: (d, g, 0, 0)),
                  pl.BlockSpec((1, SSD_GW), lambda b, g, d, s: (0, g))],
        out_specs=pl.BlockSpec((1, seq, SSD_GW), lambda b, g, d, s: (b, 0, g)),
        out_shape=jax.ShapeDtypeStruct((bn, seq, SSD_INNER), F32),
        scratch_shapes=[pltpu.VMEM((SSD_N, SSD_GW), F32), pltpu.VMEM((n, SSD_GW), F32)],
        compiler_params=_cparams(("arbitrary",) * 4),
        name="ssd",
    )(xact, xact, xact, small, dtb4, alog4, dskip_row)


OUT_TM = 256
RT_E1, RT_E2, RT_RANK1, RT_RANK2, RT_W1, RT_W2 = range(6)


def _rms(x, w):
    return x * lax.rsqrt(jnp.mean(x * x, axis=-1, keepdims=True) + EPS) * w


def _outproj_body(o_ref, r_ref, y_ref, z_ref, x_ref, mod_ref, gn_ref, sn_ref, fn_ref, wo_ref, wr_ref, br_ref,
                  hlat_ref, h2_ref, route_ref, cnt_ref, carry_ref):
    tm = OUT_TM

    @pl.when((pl.program_id(0) == 0) & (pl.program_id(1) == 0))
    def _():
        carry_ref[...] = jnp.zeros_like(carry_ref)

    parts = []
    for h in range(GLA_HEADS):
        hs = slice(h * GLA_DV, (h + 1) * GLA_DV)
        parts.append((_rms(o_ref[0, :, hs], gn_ref[...]) * _silu(r_ref[0, :, hs].astype(F32))).astype(BF16))
    for g in range(SSD_GROUPS):
        gs = slice(g * SSD_GW, (g + 1) * SSD_GW)
        yg = y_ref[0, :, gs] * _silu(z_ref[0, :, gs].astype(F32))
        parts.append(_rms(yg, sn_ref[:, gs]).astype(BF16))
    mix = jnp.concatenate(parts, axis=-1)
    m = mod_ref[0]
    hlat = x_ref[0] + m[0:1] * jnp.dot(mix, wo_ref[...], preferred_element_type=F32)
    hlat_ref[0] = hlat
    h2f = _rms(hlat, fn_ref[...]) * (1.0 + m[2:3]) + m[1:2]
    h2_ref[0] = h2f

    lg = jnp.dot(h2f.astype(BF16), wr_ref[...], preferred_element_type=F32) + br_ref[...]
    lanef = lax.broadcasted_iota(jnp.int32, (tm, LANES), 1).astype(F32)
    ninf = -jnp.inf
    is_g = lanef < float(N_GROUPS)
    gl = jnp.where(is_g, lg, ninf)
    gmax = jnp.max(gl, axis=-1, keepdims=True)
    gsel = jnp.min(jnp.where(gl == gmax, lanef, float(LANES)), axis=-1, keepdims=True)
    pg = 1.0 / jnp.sum(jnp.where(is_g, jnp.exp(lg - gmax), 0.0), axis=-1, keepdims=True)
    lo = float(EXPERT_LANE0) + float(EXPERTS_PER_GROUP) * gsel
    in_grp = (lanef >= lo) & (lanef < lo + float(EXPERTS_PER_GROUP))
    el = jnp.where(in_grp, lg, ninf)
    v1 = jnp.max(el, axis=-1, keepdims=True)
    i1 = jnp.min(jnp.where(el == v1, lanef, float(LANES)), axis=-1, keepdims=True)
    el2 = jnp.where(lanef == i1, ninf, el)
    v2 = jnp.max(el2, axis=-1, keepdims=True)
    i2 = jnp.min(jnp.where(el2 == v2, lanef, float(LANES)), axis=-1, keepdims=True)
    t = jnp.exp(v2 - v1)
    w1 = pg / (1.0 + t)
    w2 = pg * t / (1.0 + t)

    sel1 = lanef == i1
    sel2 = lanef == i2
    member = jnp.where(sel1 | sel2, 1.0, 0.0)
    ii = lax.broadcasted_iota(jnp.int32, (tm, tm), 0)
    jj = lax.broadcasted_iota(jnp.int32, (tm, tm), 1)
    before = jnp.where(jj < ii, 1.0, 0.0).astype(BF16)
    ranks = jnp.dot(before, member.astype(BF16), preferred_element_type=F32) + carry_ref[...]
    rank1 = jnp.sum(jnp.where(sel1, ranks, 0.0), axis=-1, keepdims=True)
    rank2 = jnp.sum(jnp.where(sel2, ranks, 0.0), axis=-1, keepdims=True)
    carry = carry_ref[...] + jnp.sum(member, axis=0, keepdims=True)
    carry_ref[...] = carry
    cnt_ref[...] = carry
    rec = jnp.zeros((tm, LANES), F32)
    for lane_id, val in ((RT_E1, i1 - float(EXPERT_LANE0)), (RT_E2, i2 - float(EXPERT_LANE0)),
                         (RT_RANK1, rank1), (RT_RANK2, rank2), (RT_W1, w1), (RT_W2, w2)):
        rec = jnp.where(lanef == float(lane_id), val, rec)
    route_ref[0] = rec


def _outproj(o_gla, proj, y_ssd, x, mod2, gn_row, sn_row, fn_row, w_out, w_router, b_router, ctx_len):
    bn, seq, _ = x.shape
    tm = OUT_TM
    assert ctx_len % tm == 0 and seq % tm == 0
    j0 = ctx_len // tm
    rb = PC_R // GLA_VAL
    zb = PC_Z // SSD_INNER
    tok = lambda b, j: (b, j, 0)
    const = lambda b, j: (0, 0)
    return pl.pallas_call(
        _outproj_body,
        grid=(bn, seq // tm),
        in_specs=[pl.BlockSpec((1, tm, GLA_VAL), tok),
                  pl.BlockSpec((1, tm, GLA_VAL), lambda b, j: (b, j0 + j, rb)),
                  pl.BlockSpec((1, tm, SSD_INNER), tok),
                  pl.BlockSpec((1, tm, SSD_INNER), lambda b, j: (b, j0 + j, zb)),
                  pl.BlockSpec((1, tm, D_MODEL), tok),
                  pl.BlockSpec((1, 4, D_MODEL), lambda b, j: (b, 0, 0)),
                  pl.BlockSpec((1, GLA_DV), const),
                  pl.BlockSpec((1, SSD_INNER), const),
                  pl.BlockSpec((1, D_MODEL), const),
                  pl.BlockSpec((GLA_VAL + SSD_INNER, D_MODEL), const),
                  pl.BlockSpec((D_MODEL, LANES), const),
                  pl.BlockSpec((1, LANES), const)],
        out_specs=[pl.BlockSpec((1, tm, D_MODEL), tok),
                   pl.BlockSpec((1, tm, D_MODEL), tok),
                   pl.BlockSpec((1, tm, LANES), tok),
                   pl.BlockSpec((1, LANES), const)],
        out_shape=[jax.ShapeDtypeStruct((bn, seq, D_MODEL), F32),
                   jax.ShapeDtypeStruct((bn, seq, D_MODEL), F32),
                   jax.ShapeDtypeStruct((bn, seq, LANES), F32),
                   jax.ShapeDtypeStruct((1, LANES), F32)],
        scratch_shapes=[pltpu.VMEM((1, LANES), F32)],
        compiler_params=_cparams(("arbitrary", "arbitrary")),
        name="outproj",
    )(o_gla, proj, y_ssd, proj, x, mod2, gn_row, sn_row, fn_row, w_out, w_router, b_router)


MOE_TILE = 256
PLAN_TM = 1024
DISPATCH_TM = 512
COMBINE_TM = 256


def _moe_tiles(n_tokens):
    return (2 * n_tokens) // MOE_TILE + N_EXPERTS


def _plan_body(nt_pad, route_ref, cnt_ref, pos_ref, te_ref, nt_ref):
    lane = lax.broadcasted_iota(jnp.int32, (1, LANES), 1).astype(F32)
    is_e = (lane >= float(EXPERT_LANE0)) & (lane < float(EXPERT_LANE0 + N_EXPERTS))
    tiles = jnp.where(is_e, jnp.floor((cnt_ref[...] + float(MOE_TILE - 1)) / float(MOE_TILE)), 0.0)
    ii = lax.broadcasted_iota(jnp.int32, (LANES, LANES), 0)
    jj = lax.broadcasted_iota(jnp.int32, (LANES, LANES), 1)
    lower = jnp.where(ii < jj, 1.0, 0.0).astype(BF16)
    first_tile = jnp.dot(jnp.broadcast_to(tiles, (8, LANES)).astype(BF16), lower,
                         preferred_element_type=F32)[0:1]
    off_row = first_tile * float(MOE_TILE)
    for sb in range(PLAN_TM // LANES):
        r = route_ref[sb * LANES:(sb + 1) * LANES, :]
        o1 = jnp.sum(jnp.where(lane == r[:, RT_E1:RT_E1 + 1] + float(EXPERT_LANE0), off_row, 0.0),
                     axis=-1, keepdims=True)
        o2 = jnp.sum(jnp.where(lane == r[:, RT_E2:RT_E2 + 1] + float(EXPERT_LANE0), off_row, 0.0),
                     axis=-1, keepdims=True)
        p = jnp.where(lane == 0.0, o1 + r[:, RT_RANK1:RT_RANK1 + 1],
                      jnp.where(lane == 1.0, o2 + r[:, RT_RANK2:RT_RANK2 + 1], 0.0))
        pos_ref[sb] = p.T[0:8, :].astype(jnp.int32)

    @pl.when(pl.program_id(0) == 0)
    def _():
        ti = lax.broadcasted_iota(jnp.int32, (nt_pad, 1), 0).astype(F32)
        done = jnp.where(is_e & (first_tile + tiles <= ti), 1.0, 0.0)
        te = jnp.minimum(jnp.sum(done, axis=-1, keepdims=True), float(N_EXPERTS - 1))
        te_ref[...] = te.astype(jnp.int32)
        nt_ref[...] = jnp.sum(tiles, axis=-1, keepdims=True).astype(jnp.int32)


def _plan(route, cnt):
    n_tok = route.shape[0]
    nt_pad = _moe_tiles(n_tok)
    nsb = PLAN_TM // LANES
    return pl.pallas_call(
        functools.partial(_plan_body, nt_pad),
        grid=(n_tok // PLAN_TM,),
        in_specs=[pl.BlockSpec((PLAN_TM, LANES), lambda i: (i, 0)),
                  pl.BlockSpec((1, LANES), lambda i: (0, 0))],
        out_specs=[pl.BlockSpec((nsb, 8, LANES), lambda i: (i, 0, 0)),
                   pl.BlockSpec((nt_pad, 1), lambda i: (0, 0)),
                   pl.BlockSpec((1, 1), lambda i: (0, 0))],
        out_shape=[jax.ShapeDtypeStruct((n_tok // LANES, 8, LANES), jnp.int32),
                   jax.ShapeDtypeStruct((nt_pad, 1), jnp.int32),
                   jax.ShapeDtypeStruct((1, 1), jnp.int32)],
        compiler_params=_cparams(("arbitrary",)),
        name="moe_plan",
    )(route, cnt)


def _row_copy(src, s_row, dst, d_row, sem):
    return pltpu.make_async_copy(src.at[pl.ds(s_row, 1)], dst.at[pl.ds(d_row, 1)], sem)


def _dispatch_body(pos_ref, h2_ref, xs_in_ref, xs_ref, sem):
    del xs_in_ref
    base = pl.program_id(0) * DISPATCH_TM

    def issue(t, c):
        blk = t // LANES
        l = t % LANES
        _row_copy(h2_ref, base + t, xs_ref, pos_ref[blk, 0, l], sem).start()
        _row_copy(h2_ref, base + t, xs_ref, pos_ref[blk, 1, l], sem).start()
        return c

    lax.fori_loop(0, DISPATCH_TM, issue, 0, unroll=8)

    def drain(t, c):
        _row_copy(h2_ref, 0, xs_ref, 0, sem).wait()
        return c

    lax.fori_loop(0, 2 * DISPATCH_TM, drain, 0, unroll=8)


def _dispatch(pos, h2p, xs_zero):
    n_tok = h2p.shape[0]
    nb = DISPATCH_TM // LANES
    return pl.pallas_call(
        _dispatch_body,
        grid=(n_tok // DISPATCH_TM,),
        in_specs=[pl.BlockSpec((nb, 8, LANES), lambda i: (i, 0, 0), memory_space=pltpu.SMEM),
                  pl.BlockSpec(memory_space=pl.ANY),
                  pl.BlockSpec(memory_space=pl.ANY)],
        out_specs=pl.BlockSpec(memory_space=pl.ANY),
        out_shape=jax.ShapeDtypeStruct(xs_zero.shape, xs_zero.dtype),
        scratch_shapes=[pltpu.SemaphoreType.DMA(())],
        input_output_aliases={2: 0},
        compiler_params=_cparams(("arbitrary",)),
        name="moe_dispatch",
    )(pos, h2p, xs_zero)


def _expert_body(te_ref, nt_ref, xs_ref, wg_ref, wu_ref, wd_ref, ys_ref, wgb_ref, wub_ref, wdb_ref):
    i = pl.program_id(0)
    new_expert = (i == 0) | (te_ref[i] != te_ref[jnp.maximum(i - 1, 0)])

    @pl.when(new_expert)
    def _():
        wgb_ref[...] = wg_ref[0].astype(BF16)
        wub_ref[...] = wu_ref[0].astype(BF16)
        wdb_ref[...] = wd_ref[0].astype(BF16)

    @pl.when(i < nt_ref[0])
    def _():
        x = xs_ref[...].astype(BF16)
        gate = jnp.dot(x, wgb_ref[...], preferred_element_type=F32)
        up = jnp.dot(x, wub_ref[...], preferred_element_type=F32)
        ys_ref[...] = jnp.dot((_silu(gate) * up).astype(BF16), wdb_ref[...], preferred_element_type=F32)

    @pl.when(i >= nt_ref[0])
    def _():
        ys_ref[...] = jnp.zeros_like(ys_ref)


def _experts(te, nt, xs, w_gate, w_up, w_down):
    rows, width = xs.shape
    n_tiles = rows // MOE_TILE
    grid_spec = pltpu.PrefetchScalarGridSpec(
        num_scalar_prefetch=2,
        grid=(n_tiles,),
        in_specs=[pl.BlockSpec((MOE_TILE, width), lambda i, te, nt: (i, 0)),
                  pl.BlockSpec((1, D_MODEL, D_FF), lambda i, te, nt: (te[i], 0, 0)),
                  pl.BlockSpec((1, D_MODEL, D_FF), lambda i, te, nt: (te[i], 0, 0)),
                  pl.BlockSpec((1, D_FF, D_MODEL), lambda i, te, nt: (te[i], 0, 0))],
        out_specs=pl.BlockSpec((MOE_TILE, width), lambda i, te, nt: (i, 0)),
        scratch_shapes=[pltpu.VMEM((D_MODEL, D_FF), BF16), pltpu.VMEM((D_MODEL, D_FF), BF16),
                        pltpu.VMEM((D_FF, D_MODEL), BF16)])
    return pl.pallas_call(
        _expert_body,
        grid_spec=grid_spec,
        out_shape=jax.ShapeDtypeStruct((rows, width), F32),
        compiler_params=_cparams(("arbitrary",)),
        name="moe_experts",
    )(te, nt, xs, w_gate, w_up, w_down)


def _combine_body(pos_ref, route_ref, hlat_ref, g2_ref, fn_ref, ys_ref, out_ref, buf_ref, sem):
    tm = COMBINE_TM

    def issue(t, c):
        blk = t // LANES
        l = t % LANES
        _row_copy(ys_ref, pos_ref[blk, 0, l], buf_ref.at[0], t, sem).start()
        _row_copy(ys_ref, pos_ref[blk, 1, l], buf_ref.at[1], t, sem).start()
        return c

    lax.fori_loop(0, tm, issue, 0, unroll=8)

    def drain(t, c):
        _row_copy(ys_ref, 0, buf_ref.at[0], 0, sem).wait()
        return c

    lax.fori_loop(0, 2 * tm, drain, 0, unroll=8)

    r = route_ref[0]
    moe = r[:, RT_W1:RT_W1 + 1] * buf_ref[0] + r[:, RT_W2:RT_W2 + 1] * buf_ref[1]
    out_ref[0] = _rms(hlat_ref[0] + g2_ref[0] * moe, fn_ref[...])


def _combine(pos, route, hlat, g2, fn_row, ys):
    bn, seq, _ = hlat.shape
    tm = COMBINE_TM
    nj = seq // tm
    nb = tm // LANES
    width = ys.shape[1]
    tok = lambda b, j: (b, j, 0)
    return pl.pallas_call(
        _combine_body,
        grid=(bn, nj),
        in_specs=[pl.BlockSpec((nb, 8, LANES), lambda b, j: (b * nj + j, 0, 0), memory_space=pltpu.SMEM),
                  pl.BlockSpec((1, tm, LANES), tok),
                  pl.BlockSpec((1, tm, D_MODEL), tok),
                  pl.BlockSpec((1, 1, D_MODEL), lambda b, j: (b, 0, 0)),
                  pl.BlockSpec((1, D_MODEL), lambda b, j: (0, 0)),
                  pl.BlockSpec(memory_space=pl.ANY)],
        out_specs=pl.BlockSpec((1, tm, D_MODEL), tok),
        out_shape=jax.ShapeDtypeStruct((bn, seq, D_MODEL), F32),
        scratch_shapes=[pltpu.VMEM((2, tm, width), F32), pltpu.SemaphoreType.DMA(())],
        compiler_params=_cparams(("arbitrary", "arbitrary")),
        name="moe_combine",
    )(pos, route, hlat, g2, fn_row, ys)


def _prep_weights(w_in, gla_w_up, gla_b, ssd_dt_bias, ssd_a_log, ssd_d, router_group_w, router_group_b,
                  router_expert_w, router_expert_b):
    off_k = GLA_KEY
    off_v = 2 * GLA_KEY
    off_r = off_v + GLA_VAL
    off_g = off_r + GLA_VAL
    off_z = off_g + 2 * GLA_RANK
    off_xbc = off_z + SSD_INNER
    off_dt = off_xbc + SSD_XBC
    w_main = jnp.concatenate([w_in[:, :off_g], w_in[:, off_z:off_dt]], axis=1).astype(BF16)
    blocks = [jnp.pad(w_in[:, off_g:off_z], ((0, 0), (0, LANES - 2 * GLA_RANK)))]
    for d in range(2):
        for g in range(SSD_GROUPS):
            c0 = off_dt + d * SSD_HEADS + g * SSD_HPG
            blocks.append(jnp.pad(w_in[:, c0:c0 + SSD_HPG], ((0, 0), (0, LANES - SSD_HPG))))
    w_small = jnp.concatenate(blocks, axis=1).astype(BF16)
    wup = jnp.zeros((2, LANES, GLA_KEY), F32)
    wup = wup.at[0, 0:GLA_RANK].set(gla_w_up[0]).at[1, GLA_RANK:2 * GLA_RANK].set(gla_w_up[1]).astype(BF16)
    pad_h = lambda t: jnp.pad(t.reshape(2, SSD_GROUPS, 1, SSD_HPG), ((0, 0), (0, 0), (0, 0), (0, LANES - SSD_HPG)))
    w_router = jnp.zeros((D_MODEL, LANES), F32)
    w_router = w_router.at[:, 0:N_GROUPS].set(router_group_w)
    w_router = w_router.at[:, EXPERT_LANE0:EXPERT_LANE0 + N_EXPERTS].set(router_expert_w).astype(BF16)
    b_router = jnp.zeros((1, LANES), F32).at[0, 0:N_GROUPS].set(router_group_b)
    b_router = b_router.at[0, EXPERT_LANE0:EXPERT_LANE0 + N_EXPERTS].set(router_expert_b)
    return (w_main, w_small, wup, gla_b.reshape(2, 1, GLA_KEY), pad_h(ssd_dt_bias), pad_h(ssd_a_log),
            jnp.repeat(ssd_d, SSD_P)[None, :], w_router, b_router)


def kernel(x, c, ctx, c_ctx, w_ada, b_ada, norm_mix, norm_ffn, w_in, gla_w_up, gla_b, gla_norm, ssd_conv_w,
           ssd_conv_b, ssd_dt_bias, ssd_a_log, ssd_d, ssd_norm, w_out, router_group_w, router_group_b,
           router_expert_w, router_expert_b, expert_w_gate, expert_w_up, expert_w_down, final_norm):
    assert w_ada.shape[0] == 1, "single-layer block"
    bn, seq, d = x.shape
    ctx_len = ctx.shape[1]
    assert seq // GRID_W == GRID_W and d == D_MODEL

    c8 = jnp.concatenate([c, c_ctx[None, :], jnp.zeros((8 - bn - 1, d), F32)], axis=0)
    mods = _adaln(c8, w_ada[0], b_ada[0][None, :])
    sh1, sc1, g1, sh2, sc2, g2 = [mods[:, i * d:(i + 1) * d] for i in range(6)]
    ctx_row = lambda t: jnp.broadcast_to(t[bn:bn + 1], (bn, d))
    mod1 = jnp.stack([ctx_row(sh1), ctx_row(sc1), sh1[:bn], sc1[:bn]], axis=1)
    mod2 = jnp.stack([g1[:bn], sh2[:bn], sc2[:bn], g2[:bn]], axis=1)

    (w_main, w_small, wup, gla_b3, dtb4, alog4, dskip_row, w_router, b_router) = _prep_weights(
        w_in[0], gla_w_up[0], gla_b[0], ssd_dt_bias[0], ssd_a_log[0], ssd_d[0], router_group_w[0],
        router_group_b[0], router_expert_w[0], router_expert_b[0])

    proj, small = _inproj(ctx, x, mod1, norm_mix, w_main, w_small)
    xact = _conv(proj, ssd_conv_w[0].reshape(9, SSD_XBC), ssd_conv_b, ctx_len, seq)
    lvm = jnp.asarray(np.stack([_level_matrix(CHUNK, False), _level_matrix(CHUNK, True)]))
    o_gla = _gla(proj, small, wup, gla_b3, lvm, ctx_len, seq)
    y_ssd = _ssd(xact, small, dtb4, alog4, dskip_row, ctx_len, seq)
    hlat, h2p, route, cnt = _outproj(o_gla, proj, y_ssd, x, mod2, gla_norm, ssd_norm, norm_ffn,
                                     w_out[0].astype(BF16), w_router, b_router, ctx_len)
    n_tok = bn * seq
    pos, te, nt = _plan(route.reshape(n_tok, LANES), cnt)
    xs = _dispatch(pos, h2p.reshape(n_tok, d), jnp.zeros((_moe_tiles(n_tok) * MOE_TILE, d), F32))
    ys = _experts(te.reshape(-1), nt.reshape(-1), xs, expert_w_gate[0], expert_w_up[0], expert_w_down[0])
    return _combine(pos, route, hlat, mod2[:, 3:4, :], final_norm[None, :], ys)
```

```python
import functools
import math

import numpy as np
import jax
import jax.numpy as jnp
from jax import lax
from jax.experimental import pallas as pl
from jax.experimental.pallas import tpu as pltpu

F32 = jnp.float32
BF16 = jnp.bfloat16

D_MODEL = 1024
GRID_W = 64
EPS = 1e-6

GLA_HEADS = 4
GLA_DK = 128
GLA_DV = 256
GLA_KEY = GLA_HEADS * GLA_DK
GLA_VAL = GLA_HEADS * GLA_DV
GLA_RANK = 16
GLA_GATE_NORM = 16.0

SSD_HEADS = 16
SSD_P = 64
SSD_INNER = SSD_HEADS * SSD_P
SSD_GROUPS = 2
SSD_HPG = SSD_HEADS // SSD_GROUPS
SSD_N = 128
SSD_XBC = SSD_INNER + 2 * SSD_GROUPS * SSD_N
SSD_GW = SSD_HPG * SSD_P

N_GROUPS = 4
EXPERTS_PER_GROUP = 8
N_EXPERTS = N_GROUPS * EXPERTS_PER_GROUP
D_FF = 512

LANES = 128
VMEM_LIMIT = 56 * 1024 * 1024

PC_Q = 0
PC_K = PC_Q + GLA_KEY
PC_V = PC_K + GLA_KEY
PC_R = PC_V + GLA_VAL
PC_Z = PC_R + GLA_VAL
PC_XBC = PC_Z + SSD_INNER
PROJ_W = PC_XBC + SSD_XBC
SMALL_W = LANES * (1 + 2 * SSD_GROUPS)

CHUNK = 128
EXPERT_LANE0 = 32


def _cparams(sem):
    return pltpu.CompilerParams(dimension_semantics=sem, vmem_limit_bytes=VMEM_LIMIT)


def _silu(x):
    return x / (1.0 + jnp.exp(-x))


def _softplus(x):
    return jnp.maximum(x, 0.0) + jnp.log1p(jnp.exp(-jnp.abs(x)))


def _bdot(a, b):
    return jnp.dot(a.astype(BF16), b.astype(BF16), preferred_element_type=F32)


def _bdot_nt(a, b):
    return lax.dot_general(a.astype(BF16), b.astype(BF16), (((1,), (1,)), ((), ())),
                           preferred_element_type=F32)


def _adaln_body(c_ref, w_ref, b_ref, o_ref):
    o_ref[...] = _bdot(_silu(c_ref[...]), w_ref[...]) + b_ref[...]


def _adaln(c8, w, b):
    n = w.shape[1]
    bn = 1024
    return pl.pallas_call(
        _adaln_body,
        grid=(n // bn,),
        in_specs=[pl.BlockSpec((8, D_MODEL), lambda j: (0, 0)),
                  pl.BlockSpec((D_MODEL, bn), lambda j: (0, j)),
                  pl.BlockSpec((1, bn), lambda j: (0, j))],
        out_specs=pl.BlockSpec((8, bn), lambda j: (0, j)),
        out_shape=jax.ShapeDtypeStruct((8, n), F32),
        compiler_params=_cparams(("arbitrary",)),
        name="adaln",
    )(c8, w, b)


INPROJ_TM = 256
INPROJ_NC = 512


def _inproj_body(ctx_ref, x_ref, mod_ref, nw_ref, w_ref, ws_ref, proj_ref, small_ref):
    is_ctx = pl.program_id(1) == 0
    xin = jnp.where(is_ctx, ctx_ref[0], x_ref[0])
    m = mod_ref[0]
    shift = jnp.where(is_ctx, m[0:1], m[2:3])
    scale = jnp.where(is_ctx, m[1:2], m[3:4])
    ms = jnp.mean(xin * xin, axis=-1, keepdims=True)
    h = xin * lax.rsqrt(ms + EPS) * nw_ref[...]
    hb = (h * (1.0 + scale) + shift).astype(BF16)
    for n in range(PROJ_W // INPROJ_NC):
        sl = slice(n * INPROJ_NC, (n + 1) * INPROJ_NC)
        proj_ref[0, :, sl] = jnp.dot(hb, w_ref[:, sl], preferred_element_type=F32).astype(BF16)
    small_ref[0] = jnp.dot(hb, ws_ref[...], preferred_element_type=F32)


def _inproj(ctx, x, mod1, norm_w, w_main, w_small):
    bn, seq, _ = x.shape
    ctx_len = ctx.shape[1]
    tm = INPROJ_TM
    assert ctx_len == tm and seq % tm == 0
    tt = ctx_len + seq
    nj = tt // tm
    return pl.pallas_call(
        _inproj_body,
        grid=(bn, nj),
        in_specs=[pl.BlockSpec((1, tm, D_MODEL), lambda b, j: (b, 0, 0)),
                  pl.BlockSpec((1, tm, D_MODEL), lambda b, j: (b, jnp.maximum(j - 1, 0), 0)),
                  pl.BlockSpec((1, 4, D_MODEL), lambda b, j: (b, 0, 0)),
                  pl.BlockSpec((1, D_MODEL), lambda b, j: (0, 0)),
                  pl.BlockSpec((D_MODEL, PROJ_W), lambda b, j: (0, 0)),
                  pl.BlockSpec((D_MODEL, SMALL_W), lambda b, j: (0, 0))],
        out_specs=[pl.BlockSpec((1, tm, PROJ_W), lambda b, j: (b, j, 0)),
                   pl.BlockSpec((1, tm, SMALL_W), lambda b, j: (b, j, 0))],
        out_shape=[jax.ShapeDtypeStruct((bn, tt, PROJ_W), BF16),
                   jax.ShapeDtypeStruct((bn, tt, SMALL_W), F32)],
        compiler_params=_cparams(("arbitrary", "arbitrary")),
        name="inproj",
    )(ctx, x, mod1, norm_w, w_main, w_small)


CONV_CB = 256
CONV_TT = 256


def _conv_body(ctx_len, seq, p_ref, w_ref, b_ref, o_ref, s_ref):
    lat0 = ctx_len + GRID_W
    tt = CONV_TT
    n_tiles = seq // tt
    zeros = jnp.zeros((GRID_W, CONV_CB), F32)
    s_ref[0:ctx_len, :] = p_ref[0, 0:ctx_len, :].astype(F32)
    s_ref[ctx_len:lat0, :] = zeros
    s_ref[lat0 + seq:lat0 + seq + GRID_W, :] = zeros

    def copy_tile(t, c):
        src = pl.multiple_of(ctx_len + t * tt, tt)
        dst = pl.multiple_of(lat0 + t * tt, GRID_W)
        s_ref[pl.ds(dst, tt), :] = p_ref[0, pl.ds(src, tt), :].astype(F32)
        return c

    lax.fori_loop(0, n_tiles, copy_tile, 0)

    w = w_ref[...]
    bias = b_ref[...]
    pos = lax.broadcasted_iota(jnp.int32, (tt, 1), 0)

    def taps(xr, kh, first, last, acc):
        xm = jnp.where(first, 0.0, pltpu.roll(xr, 1, 0))
        xp = jnp.where(last, 0.0, pltpu.roll(xr, tt - 1, 0))
        return (acc + xm * w[3 * kh:3 * kh + 1] + xr * w[3 * kh + 1:3 * kh + 2]
                + xp * w[3 * kh + 2:3 * kh + 3])

    acc = taps(s_ref[0:ctx_len, :], 1, pos == 0, pos == ctx_len - 1,
               jnp.broadcast_to(bias, (tt, CONV_CB)))
    o_ref[0, 0:ctx_len, :] = _silu(acc).astype(BF16)

    col = pos % GRID_W
    first = col == 0
    last = col == GRID_W - 1

    def tile(t, c):
        base = lat0 + t * tt
        acc = jnp.broadcast_to(bias, (tt, CONV_CB))
        for kh in range(3):
            start = pl.multiple_of(base + (kh - 1) * GRID_W, GRID_W)
            acc = taps(s_ref[pl.ds(start, tt), :], kh, first, last, acc)
        dst = pl.multiple_of(ctx_len + t * tt, tt)
        o_ref[0, pl.ds(dst, tt), :] = _silu(acc).astype(BF16)
        return c

    lax.fori_loop(0, n_tiles, tile, 0)


def _conv(proj, conv_w9, conv_b, ctx_len, seq):
    bn, tt, _ = proj.shape
    assert ctx_len == CONV_TT and seq % CONV_TT == 0 and CONV_TT % GRID_W == 0
    ncb = SSD_XBC // CONV_CB
    cb0 = PC_XBC // CONV_CB
    return pl.pallas_call(
        functools.partial(_conv_body, ctx_len, seq),
        grid=(bn, ncb),
        in_specs=[pl.BlockSpec((1, tt, CONV_CB), lambda b, j: (b, 0, cb0 + j)),
                  pl.BlockSpec((9, CONV_CB), lambda b, j: (0, j)),
                  pl.BlockSpec((1, CONV_CB), lambda b, j: (0, j))],
        out_specs=pl.BlockSpec((1, tt, CONV_CB), lambda b, j: (b, 0, j)),
        out_shape=jax.ShapeDtypeStruct((bn, tt, SSD_XBC), BF16),
        scratch_shapes=[pltpu.VMEM((tt + 2 * GRID_W, CONV_CB), F32)],
        compiler_params=_cparams(("arbitrary", "arbitrary")),
        name="conv",
    )(proj, conv_w9, conv_b)


def _chunk_index(n_ctx, n_steps):
    def idx(d, s):
        bwd = jnp.where(s < n_ctx, n_ctx - 1 - s, n_steps + n_ctx - 1 - s)
        return jnp.where(d == 0, s, bwd)
    return idx


def _scan_sum(a, rev):
    n = a.shape[0]
    row = lax.broadcasted_iota(jnp.int32, (n, 1), 0)
    sh = 1
    while sh < n:
        if rev:
            a = a + jnp.where(row < n - sh, pltpu.roll(a, n - sh, 0), 0.0)
        else:
            a = a + jnp.where(row >= sh, pltpu.roll(a, sh, 0), 0.0)
        sh *= 2
    return a


def _level_matrix(n, rev):
    i = np.arange(n)[:, None]
    j = np.arange(n)[None, :]
    x = i ^ j
    lv = np.where(x > 0, np.floor(np.log2(np.maximum(x, 1))).astype(np.int64) + 1, 0)
    earlier = (j > i) if rev else (j < i)
    return np.where(earlier, lv, 0).astype(np.int32)


def _gla_chunk(rev, n_ctx, q_ref, k_ref, v_ref, sm_ref, wup_ref, gb_ref, lv_ref, o_ref, s_ref):
    n = CHUNK
    step = pl.program_id(3)
    q = q_ref[0].astype(F32) * (GLA_DK ** -0.5)
    k = k_ref[0].astype(F32)
    v = v_ref[0]
    x = jnp.dot(sm_ref[0].astype(BF16), wup_ref[0], preferred_element_type=F32) + gb_ref[0]
    la = (jnp.minimum(x, 0.0) - jnp.log1p(jnp.exp(-jnp.abs(x)))) / GLA_GATE_NORM
    b = _scan_sum(la, rev)
    row = lax.broadcasted_iota(jnp.int32, (n, 1), 0)
    b_end = b[0:1] if rev else b[n - 1:n]

    kh = (k * jnp.exp(b_end - b)).T.astype(BF16)
    dec_col = jnp.exp(jnp.broadcast_to(b_end, (n, GLA_DK))).T[:, 0:1]
    s_old = s_ref[...]

    @pl.when(step >= n_ctx)
    def _():
        lv = lv_ref[0]
        att = jnp.zeros((n, n), F32)
        e_h = b
        h = 1
        level = 1
        while h < n:
            upper = (row & h) != 0
            if rev:
                e = jnp.where(upper, e_h - b, b - pltpu.roll(e_h, n - h, 0))
            else:
                e = jnp.where(upper, b - pltpu.roll(e_h, h, 0), e_h - b)
            w = jnp.exp(e)
            a_l = _bdot_nt(q * w, k * w)
            att = att + jnp.where(lv == level, a_l, 0.0)
            if 2 * h < n:
                if rev:
                    e_h = jnp.where(upper, pltpu.roll(e_h, h, 0), e_h)
                else:
                    e_h = jnp.where(upper, e_h, pltpu.roll(e_h, n - h, 0))
            h *= 2
            level += 1
        o = jnp.dot(att.astype(BF16), v, preferred_element_type=F32)
        o = o + jnp.sum(q * k, axis=-1, keepdims=True) * v.astype(F32)
        o = o + _bdot(q * jnp.exp(b), s_old)
        c = jnp.where(pl.program_id(2) == 0, step,
                      jnp.where(step < n_ctx, n_ctx - 1 - step,
                                pl.num_programs(3) + n_ctx - 1 - step))
        dst = pl.multiple_of((c - n_ctx) * n, n)
        if rev:
            o_ref[0, pl.ds(dst, n), :] += o
        else:
            o_ref[0, pl.ds(dst, n), :] = o

    s_ref[...] = dec_col * s_old + jnp.dot(kh, v, preferred_element_type=F32)


def _gla_body(n_ctx, *refs):
    s_ref = refs[-1]

    @pl.when(pl.program_id(3) == 0)
    def _():
        s_ref[...] = jnp.zeros_like(s_ref)

    @pl.when(pl.program_id(2) == 0)
    def _():
        _gla_chunk(False, n_ctx, *refs)

    @pl.when(pl.program_id(2) == 1)
    def _():
        _gla_chunk(True, n_ctx, *refs)


def _gla(proj, small, wup_pad, gla_b3, lvm, ctx_len, seq):
    bn, tt, _ = proj.shape
    n = CHUNK
    assert ctx_len % n == 0 and seq % n == 0
    n_ctx = ctx_len // n
    n_steps = tt // n
    cidx = _chunk_index(n_ctx, n_steps)
    kb = PC_K // GLA_DK
    vb = PC_V // GLA_DV
    return pl.pallas_call(
        functools.partial(_gla_body, n_ctx),
        grid=(bn, GLA_HEADS, 2, n_steps),
        in_specs=[pl.BlockSpec((1, n, GLA_DK), lambda b, h, d, s: (b, cidx(d, s), h)),
                  pl.BlockSpec((1, n, GLA_DK), lambda b, h, d, s: (b, cidx(d, s), kb + h)),
                  pl.BlockSpec((1, n, GLA_DV), lambda b, h, d, s: (b, cidx(d, s), vb + h)),
                  pl.BlockSpec((1, n, LANES), lambda b, h, d, s: (b, cidx(d, s), 0)),
                  pl.BlockSpec((1, LANES, GLA_DK), lambda b, h, d, s: (d, 0, h)),
                  pl.BlockSpec((1, 1, GLA_DK), lambda b, h, d, s: (d, 0, h)),
                  pl.BlockSpec((1, n, n), lambda b, h, d, s: (d, 0, 0))],
        out_specs=pl.BlockSpec((1, seq, GLA_DV), lambda b, h, d, s: (b, 0, h)),
        out_shape=jax.ShapeDtypeStruct((bn, seq, GLA_VAL), F32),
        scratch_shapes=[pltpu.VMEM((GLA_DK, GLA_DV), F32)],
        compiler_params=_cparams(("arbitrary",) * 4),
        name="gla",
    )(proj, proj, proj, small, wup_pad, gla_b3, lvm)


def _ssd_chunk(rev, n_ctx, x_ref, bm_ref, cm_ref, sm_ref, dtb_ref, alog_ref, dsk_ref, y_ref, s_ref, xs_ref):
    n = CHUNK
    step = pl.program_id(3)
    lane = lax.broadcasted_iota(jnp.int32, (1, LANES), 1)
    head_lane = lane < SSD_HPG
    dt = _softplus(sm_ref[0] + dtb_ref[0, 0])
    a_neg = jnp.where(head_lane, -jnp.exp(alog_ref[0, 0]), 0.0)
    acum = _scan_sum(dt * a_neg, rev)
    acum_t = acum.T
    a_end = acum[0:1] if rev else acum[n - 1:n]
    x = x_ref[0].astype(F32)
    cm = cm_ref[0]
    bm = bm_ref[0]
    s_old = s_ref[...]
    emit = step >= n_ctx

    @pl.when(emit)
    def _():
        ii = lax.broadcasted_iota(jnp.int32, (n, n), 0)
        jj = lax.broadcasted_iota(jnp.int32, (n, n), 1)
        causal = (jj >= ii) if rev else (jj <= ii)
        cb = _bdot_nt(cm, bm)
        y_off = jnp.dot(cm, s_old.astype(BF16), preferred_element_type=F32)
        c = jnp.where(pl.program_id(2) == 0, step,
                      jnp.where(step < n_ctx, n_ctx - 1 - step,
                                pl.num_programs(3) + n_ctx - 1 - step))
        dst = pl.multiple_of((c - n_ctx) * n, n)
        for hh in range(SSD_HPG):
            hs = slice(hh * SSD_P, (hh + 1) * SSD_P)
            col = acum[:, hh:hh + 1]
            seg = jnp.exp(jnp.where(causal, col - acum_t[hh:hh + 1, :], -jnp.inf))
            xdt = x[:, hs] * dt[:, hh:hh + 1]
            yh = _bdot(cb * seg, xdt) + y_off[:, hs] * jnp.exp(col)
            if rev:
                y_ref[0, pl.ds(dst, n), hs] += yh
            else:
                y_ref[0, pl.ds(dst, n), hs] = yh + x[:, hs] * dsk_ref[:, hs]

    for hh in range(SSD_HPG):
        hs = slice(hh * SSD_P, (hh + 1) * SSD_P)
        col = acum[:, hh:hh + 1]
        xs_ref[:, hs] = x[:, hs] * (dt[:, hh:hh + 1] * jnp.exp(a_end[:, hh:hh + 1] - col))
    new = jnp.dot(bm.astype(F32).T.astype(BF16), xs_ref[...].astype(BF16), preferred_element_type=F32)
    for hh in range(SSD_HPG):
        hs = slice(hh * SSD_P, (hh + 1) * SSD_P)
        s_ref[:, hs] = s_old[:, hs] * jnp.exp(a_end[:, hh:hh + 1]) + new[:, hs]


def _ssd_body(n_ctx, *refs):
    s_ref = refs[-2]

    @pl.when(pl.program_id(3) == 0)
    def _():
        s_ref[...] = jnp.zeros_like(s_ref)

    @pl.when(pl.program_id(2) == 0)
    def _():
        _ssd_chunk(False, n_ctx, *refs)

    @pl.when(pl.program_id(2) == 1)
    def _():
        _ssd_chunk(True, n_ctx, *refs)


def _ssd(xact, small, dtb4, alog4, dskip_row, ctx_len, seq):
    bn, tt, _ = xact.shape
    n = CHUNK
    n_ctx = ctx_len // n
    n_steps = tt // n
    cidx = _chunk_index(n_ctx, n_steps)
    bb = SSD_INNER // SSD_N
    return pl.pallas_call(
        functools.partial(_ssd_body, n_ctx),
        grid=(bn, SSD_GROUPS, 2, n_steps),
        in_specs=[pl.BlockSpec((1, n, SSD_GW), lambda b, g, d, s: (b, cidx(d, s), g)),
                  pl.BlockSpec((1, n, SSD_N), lambda b, g, d, s: (b, cidx(d, s), bb + g)),
                  pl.BlockSpec((1, n, SSD_N), lambda b, g, d, s: (b, cidx(d, s), bb + SSD_GROUPS + g)),
                  pl.BlockSpec((1, n, LANES), lambda b, g, d, s: (b, cidx(d, s), 1 + SSD_GROUPS * d + g)),
                  pl.BlockSpec((1, 1, 1, LANES), lambda b, g, d, s: (d, g, 0, 0)),
                  pl.BlockSpec((1, 1, 1, LANES), lambda b, g, d, s: (d, g, 0, 0)),
                  pl.BlockSpec((1, SSD_GW), lambda b, g, d, s: (0, g))],
        out_specs=pl.BlockSpec((1, seq, SSD_GW), lambda b, g, d, s: (b, 0, g)),
        out_shape=jax.ShapeDtypeStruct((bn, seq, SSD_INNER), F32),
        scratch_shapes=[pltpu.VMEM((SSD_N, SSD_GW), F32), pltpu.VMEM((n, SSD_GW), F32)],
        compiler_params=_cparams(("arbitrary",) * 4),
        name="ssd",
    )(xact, xact, xact, small, dtb4, alog4, dskip_row)


OUT_TM = 256
RT_E1, RT_E2, RT_RANK1, RT_RANK2, RT_W1, RT_W2 = range(6)


def _rms(x, w):
    return x * lax.rsqrt(jnp.mean(x * x, axis=-1, keepdims=True) + EPS) * w


ROW_TILE = (D_MODEL // LANES, LANES)


def _store_row_tiles(ref, x):
    for c in range(ROW_TILE[0]):
        ref[:, c, :] = x[:, c * LANES:(c + 1) * LANES]


def _load_row_tiles(ref):
    return jnp.concatenate([ref[:, c, :] for c in range(ROW_TILE[0])], axis=-1)


def _outproj_body(o_ref, r_ref, y_ref, z_ref, x_ref, mod_ref, gn_ref, sn_ref, fn_ref, wo_ref, wr_ref, br_ref,
                  hlat_ref, h2_ref, route_ref, cnt_ref, carry_ref):
    tm = OUT_TM

    @pl.when((pl.program_id(0) == 0) & (pl.program_id(1) == 0))
    def _():
        carry_ref[...] = jnp.zeros_like(carry_ref)

    parts = []
    for h in range(GLA_HEADS):
        hs = slice(h * GLA_DV, (h + 1) * GLA_DV)
        parts.append((_rms(o_ref[0, :, hs], gn_ref[...]) * _silu(r_ref[0, :, hs].astype(F32))).astype(BF16))
    for g in range(SSD_GROUPS):
        gs = slice(g * SSD_GW, (g + 1) * SSD_GW)
        yg = y_ref[0, :, gs] * _silu(z_ref[0, :, gs].astype(F32))
        parts.append(_rms(yg, sn_ref[:, gs]).astype(BF16))
    mix = jnp.concatenate(parts, axis=-1)
    m = mod_ref[0]
    hlat = x_ref[0] + m[0:1] * jnp.dot(mix, wo_ref[...], preferred_element_type=F32)
    hlat_ref[0] = hlat
    h2f = _rms(hlat, fn_ref[...]) * (1.0 + m[2:3]) + m[1:2]
    _store_row_tiles(h2_ref.at[0], h2f)

    lg = jnp.dot(h2f.astype(BF16), wr_ref[...], preferred_element_type=F32) + br_ref[...]
    lanef = lax.broadcasted_iota(jnp.int32, (tm, LANES), 1).astype(F32)
    ninf = -jnp.inf
    is_g = lanef < float(N_GROUPS)
    gl = jnp.where(is_g, lg, ninf)
    gmax = jnp.max(gl, axis=-1, keepdims=True)
    gsel = jnp.min(jnp.where(gl == gmax, lanef, float(LANES)), axis=-1, keepdims=True)
    pg = 1.0 / jnp.sum(jnp.where(is_g, jnp.exp(lg - gmax), 0.0), axis=-1, keepdims=True)
    lo = float(EXPERT_LANE0) + float(EXPERTS_PER_GROUP) * gsel
    in_grp = (lanef >= lo) & (lanef < lo + float(EXPERTS_PER_GROUP))
    el = jnp.where(in_grp, lg, ninf)
    v1 = jnp.max(el, axis=-1, keepdims=True)
    i1 = jnp.min(jnp.where(el == v1, lanef, float(LANES)), axis=-1, keepdims=True)
    el2 = jnp.where(lanef == i1, ninf, el)
    v2 = jnp.max(el2, axis=-1, keepdims=True)
    i2 = jnp.min(jnp.where(el2 == v2, lanef, float(LANES)), axis=-1, keepdims=True)
    t = jnp.exp(v2 - v1)
    w1 = pg / (1.0 + t)
    w2 = pg * t / (1.0 + t)

    sel1 = lanef == i1
    sel2 = lanef == i2
    member = jnp.where(sel1 | sel2, 1.0, 0.0)
    ii = lax.broadcasted_iota(jnp.int32, (tm, tm), 0)
    jj = lax.broadcasted_iota(jnp.int32, (tm, tm), 1)
    before = jnp.where(jj < ii, 1.0, 0.0).astype(BF16)
    ranks = jnp.dot(before, member.astype(BF16), preferred_element_type=F32) + carry_ref[...]
    rank1 = jnp.sum(jnp.where(sel1, ranks, 0.0), axis=-1, keepdims=True)
    rank2 = jnp.sum(jnp.where(sel2, ranks, 0.0), axis=-1, keepdims=True)
    carry = carry_ref[...] + jnp.sum(member, axis=0, keepdims=True)
    carry_ref[...] = carry
    cnt_ref[...] = carry
    rec = jnp.zeros((tm, LANES), F32)
    for lane_id, val in ((RT_E1, i1 - float(EXPERT_LANE0)), (RT_E2, i2 - float(EXPERT_LANE0)),
                         (RT_RANK1, rank1), (RT_RANK2, rank2), (RT_W1, w1), (RT_W2, w2)):
        rec = jnp.where(lanef == float(lane_id), val, rec)
    route_ref[0] = rec


def _outproj(o_gla, proj, y_ssd, x, mod2, gn_row, sn_row, fn_row, w_out, w_router, b_router, ctx_len):
    bn, seq, _ = x.shape
    tm = OUT_TM
    assert ctx_len % tm == 0 and seq % tm == 0
    j0 = ctx_len // tm
    rb = PC_R // GLA_VAL
    zb = PC_Z // SSD_INNER
    tok = lambda b, j: (b, j, 0)
    const = lambda b, j: (0, 0)
    return pl.pallas_call(
        _outproj_body,
        grid=(bn, seq // tm),
        in_specs=[pl.BlockSpec((1, tm, GLA_VAL), tok),
                  pl.BlockSpec((1, tm, GLA_VAL), lambda b, j: (b, j0 + j, rb)),
                  pl.BlockSpec((1, tm, SSD_INNER), tok),
                  pl.BlockSpec((1, tm, SSD_INNER), lambda b, j: (b, j0 + j, zb)),
                  pl.BlockSpec((1, tm, D_MODEL), tok),
                  pl.BlockSpec((1, 4, D_MODEL), lambda b, j: (b, 0, 0)),
                  pl.BlockSpec((1, GLA_DV), const),
                  pl.BlockSpec((1, SSD_INNER), const),
                  pl.BlockSpec((1, D_MODEL), const),
                  pl.BlockSpec((GLA_VAL + SSD_INNER, D_MODEL), const),
                  pl.BlockSpec((D_MODEL, LANES), const),
                  pl.BlockSpec((1, LANES), const)],
        out_specs=[pl.BlockSpec((1, tm, D_MODEL), tok),
                   pl.BlockSpec((1, tm) + ROW_TILE, lambda b, j: (b, j, 0, 0)),
                   pl.BlockSpec((1, tm, LANES), tok),
                   pl.BlockSpec((1, LANES), const)],
        out_shape=[jax.ShapeDtypeStruct((bn, seq, D_MODEL), F32),
                   jax.ShapeDtypeStruct((bn, seq) + ROW_TILE, F32),
                   jax.ShapeDtypeStruct((bn, seq, LANES), F32),
                   jax.ShapeDtypeStruct((1, LANES), F32)],
        scratch_shapes=[pltpu.VMEM((1, LANES), F32)],
        compiler_params=_cparams(("arbitrary", "arbitrary")),
        name="outproj",
    )(o_gla, proj, y_ssd, proj, x, mod2, gn_row, sn_row, fn_row, w_out, w_router, b_router)


MOE_TILE = 256
PLAN_TM = 1024
DISPATCH_TM = 512
COMBINE_TM = 256


def _moe_tiles(n_tokens):
    return (2 * n_tokens) // MOE_TILE + N_EXPERTS


def _plan_body(nt_pad, route_ref, cnt_ref, pos_ref, te_ref, nt_ref):
    lane = lax.broadcasted_iota(jnp.int32, (1, LANES), 1).astype(F32)
    is_e = (lane >= float(EXPERT_LANE0)) & (lane < float(EXPERT_LANE0 + N_EXPERTS))
    tiles = jnp.where(is_e, jnp.floor((cnt_ref[...] + float(MOE_TILE - 1)) / float(MOE_TILE)), 0.0)
    ii = lax.broadcasted_iota(jnp.int32, (LANES, LANES), 0)
    jj = lax.broadcasted_iota(jnp.int32, (LANES, LANES), 1)
    lower = jnp.where(ii < jj, 1.0, 0.0).astype(BF16)
    first_tile = jnp.dot(jnp.broadcast_to(tiles, (8, LANES)).astype(BF16), lower,
                         preferred_element_type=F32)[0:1]
    off_row = first_tile * float(MOE_TILE)
    for sb in range(PLAN_TM // LANES):
        r = route_ref[sb * LANES:(sb + 1) * LANES, :]
        o1 = jnp.sum(jnp.where(lane == r[:, RT_E1:RT_E1 + 1] + float(EXPERT_LANE0), off_row, 0.0),
                     axis=-1, keepdims=True)
        o2 = jnp.sum(jnp.where(lane == r[:, RT_E2:RT_E2 + 1] + float(EXPERT_LANE0), off_row, 0.0),
                     axis=-1, keepdims=True)
        p = jnp.where(lane == 0.0, o1 + r[:, RT_RANK1:RT_RANK1 + 1],
                      jnp.where(lane == 1.0, o2 + r[:, RT_RANK2:RT_RANK2 + 1], 0.0))
        pos_ref[sb] = p.T[0:8, :].astype(jnp.int32)

    @pl.when(pl.program_id(0) == 0)
    def _():
        ti = lax.broadcasted_iota(jnp.int32, (nt_pad, 1), 0).astype(F32)
        done = jnp.where(is_e & (first_tile + tiles <= ti), 1.0, 0.0)
        te = jnp.minimum(jnp.sum(done, axis=-1, keepdims=True), float(N_EXPERTS - 1))
        te_ref[...] = te.astype(jnp.int32)
        nt_ref[...] = jnp.sum(tiles, axis=-1, keepdims=True).astype(jnp.int32)


def _plan(route, cnt):
    n_tok = route.shape[0]
    nt_pad = _moe_tiles(n_tok)
    nsb = PLAN_TM // LANES
    return pl.pallas_call(
        functools.partial(_plan_body, nt_pad),
        grid=(n_tok // PLAN_TM,),
        in_specs=[pl.BlockSpec((PLAN_TM, LANES), lambda i: (i, 0)),
                  pl.BlockSpec((1, LANES), lambda i: (0, 0))],
        out_specs=[pl.BlockSpec((nsb, 8, LANES), lambda i: (i, 0, 0)),
                   pl.BlockSpec((nt_pad, 1), lambda i: (0, 0)),
                   pl.BlockSpec((1, 1), lambda i: (0, 0))],
        out_shape=[jax.ShapeDtypeStruct((n_tok // LANES, 8, LANES), jnp.int32),
                   jax.ShapeDtypeStruct((nt_pad, 1), jnp.int32),
                   jax.ShapeDtypeStruct((1, 1), jnp.int32)],
        compiler_params=_cparams(("arbitrary",)),
        name="moe_plan",
    )(route, cnt)


def _row_copy(src, s_row, dst, d_row, sem):
    return pltpu.make_async_copy(src.at[s_row], dst.at[d_row], sem)


def _issue_rows(n_rows, pos_ref, start_pair):
    for blk in range(n_rows // LANES):
        def issue(l, c, blk=blk):
            start_pair(blk * LANES + l, pos_ref[blk, 0, l], pos_ref[blk, 1, l])
            return c
        lax.fori_loop(0, LANES, issue, 0, unroll=8)


def _drain_rows(n_copies, copy):
    def drain(t, c):
        copy.wait()
        return c
    lax.fori_loop(0, n_copies, drain, 0, unroll=8)


def _dispatch_body(pos_ref, h2_ref, xs_in_ref, xs_ref, sem):
    del xs_in_ref

    def start_pair(t, p1, p2):
        _row_copy(h2_ref, t, xs_ref, p1, sem).start()
        _row_copy(h2_ref, t, xs_ref, p2, sem).start()

    _issue_rows(DISPATCH_TM, pos_ref, start_pair)
    _drain_rows(2 * DISPATCH_TM, _row_copy(h2_ref, 0, xs_ref, 0, sem))


def _dispatch(pos, h2t, xs_zero):
    n_tok = h2t.shape[0]
    nb = DISPATCH_TM // LANES
    return pl.pallas_call(
        _dispatch_body,
        grid=(n_tok // DISPATCH_TM,),
        in_specs=[pl.BlockSpec((nb, 8, LANES), lambda i: (i, 0, 0), memory_space=pltpu.SMEM),
                  pl.BlockSpec((DISPATCH_TM,) + ROW_TILE, lambda i: (i, 0, 0)),
                  pl.BlockSpec(memory_space=pl.ANY)],
        out_specs=pl.BlockSpec(memory_space=pl.ANY),
        out_shape=jax.ShapeDtypeStruct(xs_zero.shape, xs_zero.dtype),
        scratch_shapes=[pltpu.SemaphoreType.DMA(())],
        input_output_aliases={2: 0},
        compiler_params=_cparams(("arbitrary",)),
        name="moe_dispatch",
    )(pos, h2t, xs_zero)


def _expert_body(te_ref, nt_ref, xs_ref, wg_ref, wu_ref, wd_ref, ys_ref, wgb_ref, wub_ref, wdb_ref):
    i = pl.program_id(0)
    new_expert = (i == 0) | (te_ref[i] != te_ref[jnp.maximum(i - 1, 0)])

    @pl.when(new_expert)
    def _():
        wgb_ref[...] = wg_ref[0].astype(BF16)
        wub_ref[...] = wu_ref[0].astype(BF16)
        wdb_ref[...] = wd_ref[0].astype(BF16)

    @pl.when(i < nt_ref[0])
    def _():
        x = _load_row_tiles(xs_ref).astype(BF16)
        gate = jnp.dot(x, wgb_ref[...], preferred_element_type=F32)
        up = jnp.dot(x, wub_ref[...], preferred_element_type=F32)
        y = jnp.dot((_silu(gate) * up).astype(BF16), wdb_ref[...], preferred_element_type=F32)
        _store_row_tiles(ys_ref, y)

    @pl.when(i >= nt_ref[0])
    def _():
        ys_ref[...] = jnp.zeros_like(ys_ref)


def _experts(te, nt, xs, w_gate, w_up, w_down):
    rows = xs.shape[0]
    n_tiles = rows // MOE_TILE
    row_block = pl.BlockSpec((MOE_TILE,) + ROW_TILE, lambda i, te, nt: (i, 0, 0))
    grid_spec = pltpu.PrefetchScalarGridSpec(
        num_scalar_prefetch=2,
        grid=(n_tiles,),
        in_specs=[row_block,
                  pl.BlockSpec((1, D_MODEL, D_FF), lambda i, te, nt: (te[i], 0, 0)),
                  pl.BlockSpec((1, D_MODEL, D_FF), lambda i, te, nt: (te[i], 0, 0)),
                  pl.BlockSpec((1, D_FF, D_MODEL), lambda i, te, nt: (te[i], 0, 0))],
        out_specs=row_block,
        scratch_shapes=[pltpu.VMEM((D_MODEL, D_FF), BF16), pltpu.VMEM((D_MODEL, D_FF), BF16),
                        pltpu.VMEM((D_FF, D_MODEL), BF16)])
    return pl.pallas_call(
        _expert_body,
        grid_spec=grid_spec,
        out_shape=jax.ShapeDtypeStruct(xs.shape, F32),
        compiler_params=_cparams(("arbitrary",)),
        name="moe_experts",
    )(te, nt, xs, w_gate, w_up, w_down)


def _combine_body(pos_ref, route_ref, hlat_ref, g2_ref, fn_ref, ys_ref, out_ref, buf_ref, sem):
    tm = COMBINE_TM

    def start_pair(t, p1, p2):
        _row_copy(ys_ref, p1, buf_ref.at[0], t, sem).start()
        _row_copy(ys_ref, p2, buf_ref.at[1], t, sem).start()

    _issue_rows(tm, pos_ref, start_pair)
    _drain_rows(2 * tm, _row_copy(ys_ref, 0, buf_ref.at[0], 0, sem))

    r = route_ref[0]
    moe = (r[:, RT_W1:RT_W1 + 1] * _load_row_tiles(buf_ref.at[0])
           + r[:, RT_W2:RT_W2 + 1] * _load_row_tiles(buf_ref.at[1]))
    out_ref[0] = _rms(hlat_ref[0] + g2_ref[0] * moe, fn_ref[...])


def _combine(pos, route, hlat, g2, fn_row, ys):
    bn, seq, _ = hlat.shape
    tm = COMBINE_TM
    nj = seq // tm
    nb = tm // LANES
    tok = lambda b, j: (b, j, 0)
    return pl.pallas_call(
        _combine_body,
        grid=(bn, nj),
        in_specs=[pl.BlockSpec((nb, 8, LANES), lambda b, j: (b * nj + j, 0, 0), memory_space=pltpu.SMEM),
                  pl.BlockSpec((1, tm, LANES), tok),
                  pl.BlockSpec((1, tm, D_MODEL), tok),
                  pl.BlockSpec((1, 1, D_MODEL), lambda b, j: (b, 0, 0)),
                  pl.BlockSpec((1, D_MODEL), lambda b, j: (0, 0)),
                  pl.BlockSpec(memory_space=pl.ANY)],
        out_specs=pl.BlockSpec((1, tm, D_MODEL), tok),
        out_shape=jax.ShapeDtypeStruct((bn, seq, D_MODEL), F32),
        scratch_shapes=[pltpu.VMEM((2, tm) + ROW_TILE, F32), pltpu.SemaphoreType.DMA(())],
        compiler_params=_cparams(("arbitrary", "arbitrary")),
        name="moe_combine",
    )(pos, route, hlat, g2, fn_row, ys)


def _prep_weights(w_in, gla_w_up, gla_b, ssd_dt_bias, ssd_a_log, ssd_d, router_group_w, router_group_b,
                  router_expert_w, router_expert_b):
    off_k = GLA_KEY
    off_v = 2 * GLA_KEY
    off_r = off_v + GLA_VAL
    off_g = off_r + GLA_VAL
    off_z = off_g + 2 * GLA_RANK
    off_xbc = off_z + SSD_INNER
    off_dt = off_xbc + SSD_XBC
    w_main = jnp.concatenate([w_in[:, :off_g], w_in[:, off_z:off_dt]], axis=1).astype(BF16)
    blocks = [jnp.pad(w_in[:, off_g:off_z], ((0, 0), (0, LANES - 2 * GLA_RANK)))]
    for d in range(2):
        for g in range(SSD_GROUPS):
            c0 = off_dt + d * SSD_HEADS + g * SSD_HPG
            blocks.append(jnp.pad(w_in[:, c0:c0 + SSD_HPG], ((0, 0), (0, LANES - SSD_HPG))))
    w_small = jnp.concatenate(blocks, axis=1).astype(BF16)
    wup = jnp.zeros((2, LANES, GLA_KEY), F32)
    wup = wup.at[0, 0:GLA_RANK].set(gla_w_up[0]).at[1, GLA_RANK:2 * GLA_RANK].set(gla_w_up[1]).astype(BF16)
    pad_h = lambda t: jnp.pad(t.reshape(2, SSD_GROUPS, 1, SSD_HPG), ((0, 0), (0, 0), (0, 0), (0, LANES - SSD_HPG)))
    w_router = jnp.zeros((D_MODEL, LANES), F32)
    w_router = w_router.at[:, 0:N_GROUPS].set(router_group_w)
    w_router = w_router.at[:, EXPERT_LANE0:EXPERT_LANE0 + N_EXPERTS].set(router_expert_w).astype(BF16)
    b_router = jnp.zeros((1, LANES), F32).at[0, 0:N_GROUPS].set(router_group_b)
    b_router = b_router.at[0, EXPERT_LANE0:EXPERT_LANE0 + N_EXPERTS].set(router_expert_b)
    return (w_main, w_small, wup, gla_b.reshape(2, 1, GLA_KEY), pad_h(ssd_dt_bias), pad_h(ssd_a_log),
            jnp.repeat(ssd_d, SSD_P)[None, :], w_router, b_router)


def kernel(x, c, ctx, c_ctx, w_ada, b_ada, norm_mix, norm_ffn, w_in, gla_w_up, gla_b, gla_norm, ssd_conv_w,
           ssd_conv_b, ssd_dt_bias, ssd_a_log, ssd_d, ssd_norm, w_out, router_group_w, router_group_b,
           router_expert_w, router_expert_b, expert_w_gate, expert_w_up, expert_w_down, final_norm):
    assert w_ada.shape[0] == 1, "single-layer block"
    bn, seq, d = x.shape
    ctx_len = ctx.shape[1]
    assert seq // GRID_W == GRID_W and d == D_MODEL

    c8 = jnp.concatenate([c, c_ctx[None, :], jnp.zeros((8 - bn - 1, d), F32)], axis=0)
    mods = _adaln(c8, w_ada[0], b_ada[0][None, :])
    sh1, sc1, g1, sh2, sc2, g2 = [mods[:, i * d:(i + 1) * d] for i in range(6)]
    ctx_row = lambda t: jnp.broadcast_to(t[bn:bn + 1], (bn, d))
    mod1 = jnp.stack([ctx_row(sh1), ctx_row(sc1), sh1[:bn], sc1[:bn]], axis=1)
    mod2 = jnp.stack([g1[:bn], sh2[:bn], sc2[:bn], g2[:bn]], axis=1)

    (w_main, w_small, wup, gla_b3, dtb4, alog4, dskip_row, w_router, b_router) = _prep_weights(
        w_in[0], gla_w_up[0], gla_b[0], ssd_dt_bias[0], ssd_a_log[0], ssd_d[0], router_group_w[0],
        router_group_b[0], router_expert_w[0], router_expert_b[0])

    proj, small = _inproj(ctx, x, mod1, norm_mix, w_main, w_small)
    xact = _conv(proj, ssd_conv_w[0].reshape(9, SSD_XBC), ssd_conv_b, ctx_len, seq)
    lvm = jnp.asarray(np.stack([_level_matrix(CHUNK, False), _level_matrix(CHUNK, True)]))
    o_gla = _gla(proj, small, wup, gla_b3, lvm, ctx_len, seq)
    y_ssd = _ssd(xact, small, dtb4, alog4, dskip_row, ctx_len, seq)
    hlat, h2p, route, cnt = _outproj(o_gla, proj, y_ssd, x, mod2, gla_norm, ssd_norm, norm_ffn,
                                     w_out[0].astype(BF16), w_router, b_router, ctx_len)
    n_tok = bn * seq
    pos, te, nt = _plan(route.reshape(n_tok, LANES), cnt)
    xs = _dispatch(pos, h2p.reshape((n_tok,) + ROW_TILE),
                   jnp.zeros((_moe_tiles(n_tok) * MOE_TILE,) + ROW_TILE, F32))
    ys = _experts(te.reshape(-1), nt.reshape(-1), xs, expert_w_gate[0], expert_w_up[0], expert_w_down[0])
    return _combine(pos, route, hlat, mod2[:, 3:4, :], final_norm[None, :], ys)
```

```python
import functools
import math

import numpy as np
import jax
import jax.numpy as jnp
from jax import lax
from jax.experimental import pallas as pl
from jax.experimental.pallas import tpu as pltpu

F32 = jnp.float32
BF16 = jnp.bfloat16

D_MODEL = 1024
GRID_W = 64
EPS = 1e-6

GLA_HEADS = 4
GLA_DK = 128
GLA_DV = 256
GLA_KEY = GLA_HEADS * GLA_DK
GLA_VAL = GLA_HEADS * GLA_DV
GLA_RANK = 16
GLA_GATE_NORM = 16.0

SSD_HEADS = 16
SSD_P = 64
SSD_INNER = SSD_HEADS * SSD_P
SSD_GROUPS = 2
SSD_HPG = SSD_HEADS // SSD_GROUPS
SSD_N = 128
SSD_XBC = SSD_INNER + 2 * SSD_GROUPS * SSD_N
SSD_GW = SSD_HPG * SSD_P

N_GROUPS = 4
EXPERTS_PER_GROUP = 8
N_EXPERTS = N_GROUPS * EXPERTS_PER_GROUP
D_FF = 512

LANES = 128
VMEM_LIMIT = 56 * 1024 * 1024

PC_Q = 0
PC_K = PC_Q + GLA_KEY
PC_V = PC_K + GLA_KEY
PC_R = PC_V + GLA_VAL
PC_Z = PC_R + GLA_VAL
PC_XBC = PC_Z + SSD_INNER
PROJ_W = PC_XBC + SSD_XBC
SMALL_W = LANES * (1 + 2 * SSD_GROUPS)

CHUNK = 128
EXPERT_LANE0 = 32


def _cparams(sem):
    return pltpu.CompilerParams(dimension_semantics=sem, vmem_limit_bytes=VMEM_LIMIT)


def _silu(x):
    return x / (1.0 + jnp.exp(-x))


def _softplus(x):
    return jnp.maximum(x, 0.0) + jnp.log1p(jnp.exp(-jnp.abs(x)))


def _bdot(a, b):
    return jnp.dot(a.astype(BF16), b.astype(BF16), preferred_element_type=F32)


def _bdot_nt(a, b):
    return lax.dot_general(a.astype(BF16), b.astype(BF16), (((1,), (1,)), ((), ())),
                           preferred_element_type=F32)


def _adaln_body(c_ref, w_ref, b_ref, o_ref):
    o_ref[...] = _bdot(_silu(c_ref[...]), w_ref[...]) + b_ref[...]


def _adaln(c8, w, b):
    n = w.shape[1]
    bn = 1024
    return pl.pallas_call(
        _adaln_body,
        grid=(n // bn,),
        in_specs=[pl.BlockSpec((8, D_MODEL), lambda j: (0, 0)),
                  pl.BlockSpec((D_MODEL, bn), lambda j: (0, j)),
                  pl.BlockSpec((1, bn), lambda j: (0, j))],
        out_specs=pl.BlockSpec((8, bn), lambda j: (0, j)),
        out_shape=jax.ShapeDtypeStruct((8, n), F32),
        compiler_params=_cparams(("arbitrary",)),
        name="adaln",
    )(c8, w, b)


INPROJ_TM = 256
INPROJ_NC = 512


def _inproj_body(ctx_ref, x_ref, mod_ref, nw_ref, w_ref, ws_ref, proj_ref, small_ref):
    is_ctx = pl.program_id(1) == 0
    xin = jnp.where(is_ctx, ctx_ref[0], x_ref[0])
    m = mod_ref[0]
    shift = jnp.where(is_ctx, m[0:1], m[2:3])
    scale = jnp.where(is_ctx, m[1:2], m[3:4])
    ms = jnp.mean(xin * xin, axis=-1, keepdims=True)
    h = xin * lax.rsqrt(ms + EPS) * nw_ref[...]
    hb = (h * (1.0 + scale) + shift).astype(BF16)
    for n in range(PROJ_W // INPROJ_NC):
        sl = slice(n * INPROJ_NC, (n + 1) * INPROJ_NC)
        proj_ref[0, :, sl] = jnp.dot(hb, w_ref[:, sl], preferred_element_type=F32).astype(BF16)
    small_ref[0] = jnp.dot(hb, ws_ref[...], preferred_element_type=F32)


def _inproj(ctx, x, mod1, norm_w, w_main, w_small):
    bn, seq, _ = x.shape
    ctx_len = ctx.shape[1]
    tm = INPROJ_TM
    assert ctx_len == tm and seq % tm == 0
    tt = ctx_len + seq
    nj = tt // tm
    return pl.pallas_call(
        _inproj_body,
        grid=(bn, nj),
        in_specs=[pl.BlockSpec((1, tm, D_MODEL), lambda b, j: (b, 0, 0)),
                  pl.BlockSpec((1, tm, D_MODEL), lambda b, j: (b, jnp.maximum(j - 1, 0), 0)),
                  pl.BlockSpec((1, 4, D_MODEL), lambda b, j: (b, 0, 0)),
                  pl.BlockSpec((1, D_MODEL), lambda b, j: (0, 0)),
                  pl.BlockSpec((D_MODEL, PROJ_W), lambda b, j: (0, 0)),
                  pl.BlockSpec((D_MODEL, SMALL_W), lambda b, j: (0, 0))],
        out_specs=[pl.BlockSpec((1, tm, PROJ_W), lambda b, j: (b, j, 0)),
                   pl.BlockSpec((1, tm, SMALL_W), lambda b, j: (b, j, 0))],
        out_shape=[jax.ShapeDtypeStruct((bn, tt, PROJ_W), BF16),
                   jax.ShapeDtypeStruct((bn, tt, SMALL_W), F32)],
        compiler_params=_cparams(("arbitrary", "arbitrary")),
        name="inproj",
    )(ctx, x, mod1, norm_w, w_main, w_small)


CONV_CB = 256
CONV_TT = 256


def _conv_body(ctx_len, seq, p_ref, w_ref, b_ref, o_ref, s_ref):
    lat0 = ctx_len + GRID_W
    tt = CONV_TT
    n_tiles = seq // tt
    zeros = jnp.zeros((GRID_W, CONV_CB), F32)
    s_ref[0:ctx_len, :] = p_ref[0, 0:ctx_len, :].astype(F32)
    s_ref[ctx_len:lat0, :] = zeros
    s_ref[lat0 + seq:lat0 + seq + GRID_W, :] = zeros

    def copy_tile(t, c):
        src = pl.multiple_of(ctx_len + t * tt, tt)
        dst = pl.multiple_of(lat0 + t * tt, GRID_W)
        s_ref[pl.ds(dst, tt), :] = p_ref[0, pl.ds(src, tt), :].astype(F32)
        return c

    lax.fori_loop(0, n_tiles, copy_tile, 0)

    w = w_ref[...]
    bias = b_ref[...]
    pos = lax.broadcasted_iota(jnp.int32, (tt, 1), 0)

    def taps(xr, kh, first, last, acc):
        xm = jnp.where(first, 0.0, pltpu.roll(xr, 1, 0))
        xp = jnp.where(last, 0.0, pltpu.roll(xr, tt - 1, 0))
        return (acc + xm * w[3 * kh:3 * kh + 1] + xr * w[3 * kh + 1:3 * kh + 2]
                + xp * w[3 * kh + 2:3 * kh + 3])

    acc = taps(s_ref[0:ctx_len, :], 1, pos == 0, pos == ctx_len - 1,
               jnp.broadcast_to(bias, (tt, CONV_CB)))
    o_ref[0, 0:ctx_len, :] = _silu(acc).astype(BF16)

    col = pos % GRID_W
    first = col == 0
    last = col == GRID_W - 1

    def tile(t, c):
        base = lat0 + t * tt
        acc = jnp.broadcast_to(bias, (tt, CONV_CB))
        for kh in range(3):
            start = pl.multiple_of(base + (kh - 1) * GRID_W, GRID_W)
            acc = taps(s_ref[pl.ds(start, tt), :], kh, first, last, acc)
        dst = pl.multiple_of(ctx_len + t * tt, tt)
        o_ref[0, pl.ds(dst, tt), :] = _silu(acc).astype(BF16)
        return c

    lax.fori_loop(0, n_tiles, tile, 0)


def _conv(proj, conv_w9, conv_b, ctx_len, seq):
    bn, tt, _ = proj.shape
    assert ctx_len == CONV_TT and seq % CONV_TT == 0 and CONV_TT % GRID_W == 0
    ncb = SSD_XBC // CONV_CB
    cb0 = PC_XBC // CONV_CB
    return pl.pallas_call(
        functools.partial(_conv_body, ctx_len, seq),
        grid=(bn, ncb),
        in_specs=[pl.BlockSpec((1, tt, CONV_CB), lambda b, j: (b, 0, cb0 + j)),
                  pl.BlockSpec((9, CONV_CB), lambda b, j: (0, j)),
                  pl.BlockSpec((1, CONV_CB), lambda b, j: (0, j))],
        out_specs=pl.BlockSpec((1, tt, CONV_CB), lambda b, j: (b, 0, j)),
        out_shape=jax.ShapeDtypeStruct((bn, tt, SSD_XBC), BF16),
        scratch_shapes=[pltpu.VMEM((tt + 2 * GRID_W, CONV_CB), F32)],
        compiler_params=_cparams(("arbitrary", "arbitrary")),
        name="conv",
    )(proj, conv_w9, conv_b)


def _bwd_chunk(s, n_ctx, n_steps):
    return jnp.where(s < n_ctx, n_ctx - 1 - s, n_steps + n_ctx - 1 - s)


def _out_row(c, n_ctx, n, seq):
    return pl.multiple_of(jnp.where(c >= n_ctx, (c - n_ctx) * n, seq), n)


def _scan_sum(a, rev):
    n = a.shape[0]
    row = lax.broadcasted_iota(jnp.int32, (n, 1), 0)
    sh = 1
    while sh < n:
        if rev:
            a = a + jnp.where(row < n - sh, pltpu.roll(a, n - sh, 0), 0.0)
        else:
            a = a + jnp.where(row >= sh, pltpu.roll(a, sh, 0), 0.0)
        sh *= 2
    return a


def _level_matrix(n, rev):
    i = np.arange(n)[:, None]
    j = np.arange(n)[None, :]
    x = i ^ j
    lv = np.where(x > 0, np.floor(np.log2(np.maximum(x, 1))).astype(np.int64) + 1, 0)
    earlier = (j > i) if rev else (j < i)
    return np.where(earlier, lv, 0).astype(np.int32)


GLA_HPS = 2


def _gla_dir(rev, dst, q_ref, k_ref, v_ref, sm_ref, wup, gb, lv, o_ref, s_ref):
    n = CHUNK
    q = q_ref[0].astype(F32) * (GLA_DK ** -0.5)
    k = k_ref[0].astype(F32)
    x = jnp.dot(sm_ref[0].astype(BF16), wup, preferred_element_type=F32) + gb
    la = (jnp.minimum(x, 0.0) - jnp.log1p(jnp.exp(-jnp.abs(x)))) / GLA_GATE_NORM
    b = _scan_sum(la, rev)
    row = lax.broadcasted_iota(jnp.int32, (n, 1), 0)
    b_end = b[0:1] if rev else b[n - 1:n]
    k_end = k * jnp.exp(b_end - b)
    q_in = q * jnp.exp(b)
    dec_end = jnp.exp(b_end)

    att = [jnp.zeros((n, n), F32) for _ in range(GLA_HPS)]
    e_h = b
    h = 1
    level = 1
    while h < n:
        upper = (row & h) != 0
        if rev:
            e = jnp.where(upper, e_h - b, b - pltpu.roll(e_h, n - h, 0))
        else:
            e = jnp.where(upper, b - pltpu.roll(e_h, h, 0), e_h - b)
        w = jnp.exp(e)
        qw = (q * w).astype(BF16)
        kw = (k * w).astype(BF16)
        mask = lv == level
        for hh in range(GLA_HPS):
            ks = slice(hh * GLA_DK, (hh + 1) * GLA_DK)
            att[hh] = att[hh] + jnp.where(mask, _bdot_nt(qw[:, ks], kw[:, ks]), 0.0)
        if 2 * h < n:
            if rev:
                e_h = jnp.where(upper, pltpu.roll(e_h, h, 0), e_h)
            else:
                e_h = jnp.where(upper, e_h, pltpu.roll(e_h, n - h, 0))
        h *= 2
        level += 1

    for hh in range(GLA_HPS):
        ks = slice(hh * GLA_DK, (hh + 1) * GLA_DK)
        vs = slice(hh * GLA_DV, (hh + 1) * GLA_DV)
        v = v_ref[0, :, vs]
        s_old = s_ref[hh]
        o = jnp.dot(att[hh].astype(BF16), v, preferred_element_type=F32)
        o = o + jnp.sum(q[:, ks] * k[:, ks], axis=-1, keepdims=True) * v.astype(F32)
        o = o + _bdot(q_in[:, ks], s_old)
        o_ref[0, pl.ds(dst, n), vs] += o
        dec_col = jnp.broadcast_to(dec_end[:, ks], (n, GLA_DK)).T[:, 0:1]
        s_ref[hh] = dec_col * s_old + jnp.dot(k_end[:, ks].T.astype(BF16), v, preferred_element_type=F32)


def _gla_body(n_ctx, seq, qf, kf, vf, smf, qb, kb, vb, smb, wup_ref, gb_ref, lv_ref, o_ref, s_ref):
    step = pl.program_id(2)
    n = CHUNK

    @pl.when(step == 0)
    def _():
        s_ref[...] = jnp.zeros_like(s_ref)
        o_ref[...] = jnp.zeros_like(o_ref)

    cb = _bwd_chunk(step, n_ctx, pl.num_programs(2))
    _gla_dir(False, _out_row(step, n_ctx, n, seq), qf, kf, vf, smf, wup_ref[0], gb_ref[0], lv_ref[0],
             o_ref, s_ref.at[0])
    _gla_dir(True, _out_row(cb, n_ctx, n, seq), qb, kb, vb, smb, wup_ref[1], gb_ref[1], lv_ref[1],
             o_ref, s_ref.at[1])


def _gla(proj, small, wup_pad, gla_b3, lvm, ctx_len, seq):
    bn, tt, _ = proj.shape
    n = CHUNK
    assert ctx_len % n == 0 and seq % n == 0
    n_ctx = ctx_len // n
    n_steps = tt // n
    hw = GLA_HPS * GLA_DK
    vw = GLA_HPS * GLA_DV
    kb = PC_K // hw
    vb = PC_V // vw
    fwd = lambda s: s
    bwd = lambda s: _bwd_chunk(s, n_ctx, n_steps)

    def chunk_specs(c):
        return [pl.BlockSpec((1, n, hw), lambda b, h, s: (b, c(s), h)),
                pl.BlockSpec((1, n, hw), lambda b, h, s: (b, c(s), kb + h)),
                pl.BlockSpec((1, n, vw), lambda b, h, s: (b, c(s), vb + h)),
                pl.BlockSpec((1, n, LANES), lambda b, h, s: (b, c(s), 0))]

    return pl.pallas_call(
        functools.partial(_gla_body, n_ctx, seq),
        grid=(bn, GLA_HEADS // GLA_HPS, n_steps),
        in_specs=chunk_specs(fwd) + chunk_specs(bwd) + [
            pl.BlockSpec((2, LANES, hw), lambda b, h, s: (0, 0, h)),
            pl.BlockSpec((2, 1, hw), lambda b, h, s: (0, 0, h)),
            pl.BlockSpec((2, n, n), lambda b, h, s: (0, 0, 0))],
        out_specs=pl.BlockSpec((1, seq + n, vw), lambda b, h, s: (b, 0, h)),
        out_shape=jax.ShapeDtypeStruct((bn, seq + n, GLA_VAL), F32),
        scratch_shapes=[pltpu.VMEM((2, GLA_HPS, GLA_DK, GLA_DV), F32)],
        compiler_params=_cparams(("arbitrary",) * 3),
        name="gla",
    )(proj, proj, proj, small, proj, proj, proj, small, wup_pad, gla_b3, lvm)


def _ssd_dir(rev, first, dst, x_ref, bm_ref, cm_ref, sm_ref, dtb, alog, dsk_ref, y_ref, s_ref):
    n = CHUNK
    lane = lax.broadcasted_iota(jnp.int32, (1, LANES), 1)
    dt = _softplus(sm_ref[0] + dtb)
    a_neg = jnp.where(lane < SSD_HPG, -jnp.exp(alog), 0.0)
    acum = _scan_sum(dt * a_neg, rev)
    acum_t = acum.T
    x = x_ref[0].astype(F32)
    cm = cm_ref[0]
    bm = bm_ref[0]
    s_old = s_ref[...]
    ii = lax.broadcasted_iota(jnp.int32, (n, n), 0)
    jj = lax.broadcasted_iota(jnp.int32, (n, n), 1)
    causal = (jj >= ii) if rev else (jj <= ii)
    cb = _bdot_nt(cm, bm)
    y_off = jnp.dot(cm, s_old.astype(BF16), preferred_element_type=F32)
    low = lane < SSD_P
    xs_parts = []
    send_parts = []
    for p in range(SSD_HPG // 2):
        slab = slice(p * LANES, (p + 1) * LANES)
        heads = []
        for hh in (2 * p, 2 * p + 1):
            colb = jnp.broadcast_to(acum[:, hh:hh + 1], (n, LANES))
            dtb_h = jnp.broadcast_to(dt[:, hh:hh + 1], (n, LANES))
            seg = jnp.exp(jnp.where(causal, colb - acum_t[hh:hh + 1, :], -jnp.inf))
            end = colb[0:1] if rev else colb[n - 1:n]
            heads.append(((cb * seg).astype(BF16), dtb_h, jnp.exp(colb), jnp.exp(end - colb), jnp.exp(end)))
        (m0, dt0, ec0, dc0, se0), (m1, dt1, ec1, dc1, se1) = heads
        xdt = x[:, slab] * jnp.where(low, dt0, dt1)
        rhs = jnp.concatenate([jnp.where(low, xdt, 0.0), jnp.where(low, 0.0, xdt)], axis=0).astype(BF16)
        y = jnp.dot(jnp.concatenate([m0, m1], axis=1), rhs, preferred_element_type=F32)
        y = y + y_off[:, slab] * jnp.where(low, ec0, ec1)
        if first:
            y = y + x[:, slab] * dsk_ref[:, slab]
        y_ref[0, pl.ds(dst, n), slab] += y
        xs_parts.append((xdt * jnp.where(low, dc0, dc1)).astype(BF16))
        send_parts.append(jnp.where(low, se0, se1))
    new = jnp.dot(bm.astype(F32).T.astype(BF16), jnp.concatenate(xs_parts, axis=1),
                  preferred_element_type=F32)
    s_ref[...] = s_old * jnp.concatenate(send_parts, axis=1) + new


def _ssd_body(n_ctx, seq, xf, bf, cf, smf, xb, bb, cbk, smb, dtb_ref, alog_ref, dsk_ref, y_ref, s_ref):
    step = pl.program_id(2)
    n = CHUNK

    @pl.when(step == 0)
    def _():
        s_ref[...] = jnp.zeros_like(s_ref)
        y_ref[...] = jnp.zeros_like(y_ref)

    cbw = _bwd_chunk(step, n_ctx, pl.num_programs(2))
    _ssd_dir(False, True, _out_row(step, n_ctx, n, seq), xf, bf, cf, smf, dtb_ref[0, 0], alog_ref[0, 0],
             dsk_ref, y_ref, s_ref.at[0])
    _ssd_dir(True, False, _out_row(cbw, n_ctx, n, seq), xb, bb, cbk, smb, dtb_ref[1, 0], alog_ref[1, 0],
             dsk_ref, y_ref, s_ref.at[1])


def _ssd(xact, small, dtb4, alog4, dskip_row, ctx_len, seq):
    bn, tt, _ = xact.shape
    n = CHUNK
    n_ctx = ctx_len // n
    n_steps = tt // n
    bblk = SSD_INNER // SSD_N
    fwd = lambda s: s
    bwd = lambda s: _bwd_chunk(s, n_ctx, n_steps)

    def chunk_specs(c, d):
        return [pl.BlockSpec((1, n, SSD_GW), lambda b, g, s: (b, c(s), g)),
                pl.BlockSpec((1, n, SSD_N), lambda b, g, s: (b, c(s), bblk + g)),
                pl.BlockSpec((1, n, SSD_N), lambda b, g, s: (b, c(s), bblk + SSD_GROUPS + g)),
                pl.BlockSpec((1, n, LANES), lambda b, g, s: (b, c(s), 1 + SSD_GROUPS * d + g))]

    return pl.pallas_call(
        functools.partial(_ssd_body, n_ctx, seq),
        grid=(bn, SSD_GROUPS, n_steps),
        in_specs=chunk_specs(fwd, 0) + chunk_specs(bwd, 1) + [
            pl.BlockSpec((2, 1, 1, LANES), lambda b, g, s: (0, g, 0, 0)),
            pl.BlockSpec((2, 1, 1, LANES), lambda b, g, s: (0, g, 0, 0)),
            pl.BlockSpec((1, SSD_GW), lambda b, g, s: (0, g))],
        out_specs=pl.BlockSpec((1, seq + n, SSD_GW), lambda b, g, s: (b, 0, g)),
        out_shape=jax.ShapeDtypeStruct((bn, seq + n, SSD_INNER), F32),
        scratch_shapes=[pltpu.VMEM((2, SSD_N, SSD_GW), F32)],
        compiler_params=_cparams(("arbitrary",) * 3),
        name="ssd",
    )(xact, xact, xact, small, xact, xact, xact, small, dtb4, alog4, dskip_row)


OUT_TM = 256
RT_E1, RT_E2, RT_RANK1, RT_RANK2, RT_W1, RT_W2 = range(6)


def _rms(x, w):
    return x * lax.rsqrt(jnp.mean(x * x, axis=-1, keepdims=True) + EPS) * w


ROW_TILE = (D_MODEL // LANES, LANES)


def _store_row_tiles(ref, x):
    for c in range(ROW_TILE[0]):
        ref[:, c, :] = x[:, c * LANES:(c + 1) * LANES]


def _load_row_tiles(ref):
    return jnp.concatenate([ref[:, c, :] for c in range(ROW_TILE[0])], axis=-1)


def _outproj_body(o_ref, r_ref, y_ref, z_ref, x_ref, mod_ref, gn_ref, sn_ref, fn_ref, wo_ref, wr_ref, br_ref,
                  hlat_ref, h2_ref, route_ref, cnt_ref, carry_ref):
    tm = OUT_TM

    @pl.when((pl.program_id(0) == 0) & (pl.program_id(1) == 0))
    def _():
        carry_ref[...] = jnp.zeros_like(carry_ref)

    parts = []
    for h in range(GLA_HEADS):
        hs = slice(h * GLA_DV, (h + 1) * GLA_DV)
        parts.append((_rms(o_ref[0, :, hs], gn_ref[...]) * _silu(r_ref[0, :, hs].astype(F32))).astype(BF16))
    for g in range(SSD_GROUPS):
        gs = slice(g * SSD_GW, (g + 1) * SSD_GW)
        yg = y_ref[0, :, gs] * _silu(z_ref[0, :, gs].astype(F32))
        parts.append(_rms(yg, sn_ref[:, gs]).astype(BF16))
    mix = jnp.concatenate(parts, axis=-1)
    m = mod_ref[0]
    hlat = x_ref[0] + m[0:1] * jnp.dot(mix, wo_ref[...], preferred_element_type=F32)
    hlat_ref[0] = hlat
    h2f = _rms(hlat, fn_ref[...]) * (1.0 + m[2:3]) + m[1:2]
    _store_row_tiles(h2_ref.at[0], h2f)

    lg = jnp.dot(h2f.astype(BF16), wr_ref[...], preferred_element_type=F32) + br_ref[...]
    lanef = lax.broadcasted_iota(jnp.int32, (tm, LANES), 1).astype(F32)
    ninf = -jnp.inf
    is_g = lanef < float(N_GROUPS)
    gl = jnp.where(is_g, lg, ninf)
    gmax = jnp.max(gl, axis=-1, keepdims=True)
    gsel = jnp.min(jnp.where(gl == gmax, lanef, float(LANES)), axis=-1, keepdims=True)
    pg = 1.0 / jnp.sum(jnp.where(is_g, jnp.exp(lg - gmax), 0.0), axis=-1, keepdims=True)
    lo = float(EXPERT_LANE0) + float(EXPERTS_PER_GROUP) * gsel
    in_grp = (lanef >= lo) & (lanef < lo + float(EXPERTS_PER_GROUP))
    el = jnp.where(in_grp, lg, ninf)
    v1 = jnp.max(el, axis=-1, keepdims=True)
    i1 = jnp.min(jnp.where(el == v1, lanef, float(LANES)), axis=-1, keepdims=True)
    el2 = jnp.where(lanef == i1, ninf, el)
    v2 = jnp.max(el2, axis=-1, keepdims=True)
    i2 = jnp.min(jnp.where(el2 == v2, lanef, float(LANES)), axis=-1, keepdims=True)
    t = jnp.exp(v2 - v1)
    w1 = pg / (1.0 + t)
    w2 = pg * t / (1.0 + t)

    sel1 = lanef == i1
    sel2 = lanef == i2
    member = jnp.where(sel1 | sel2, 1.0, 0.0)
    ii = lax.broadcasted_iota(jnp.int32, (tm, tm), 0)
    jj = lax.broadcasted_iota(jnp.int32, (tm, tm), 1)
    before = jnp.where(jj < ii, 1.0, 0.0).astype(BF16)
    ranks = jnp.dot(before, member.astype(BF16), preferred_element_type=F32) + carry_ref[...]
    rank1 = jnp.sum(jnp.where(sel1, ranks, 0.0), axis=-1, keepdims=True)
    rank2 = jnp.sum(jnp.where(sel2, ranks, 0.0), axis=-1, keepdims=True)
    carry = carry_ref[...] + jnp.sum(member, axis=0, keepdims=True)
    carry_ref[...] = carry
    cnt_ref[...] = carry
    rec = jnp.zeros((tm, LANES), F32)
    for lane_id, val in ((RT_E1, i1 - float(EXPERT_LANE0)), (RT_E2, i2 - float(EXPERT_LANE0)),
                         (RT_RANK1, rank1), (RT_RANK2, rank2), (RT_W1, w1), (RT_W2, w2)):
        rec = jnp.where(lanef == float(lane_id), val, rec)
    route_ref[0] = rec


def _outproj(o_gla, proj, y_ssd, x, mod2, gn_row, sn_row, fn_row, w_out, w_router, b_router, ctx_len):
    bn, seq, _ = x.shape
    tm = OUT_TM
    assert ctx_len % tm == 0 and seq % tm == 0
    j0 = ctx_len // tm
    rb = PC_R // GLA_VAL
    zb = PC_Z // SSD_INNER
    tok = lambda b, j: (b, j, 0)
    const = lambda b, j: (0, 0)
    return pl.pallas_call(
        _outproj_body,
        grid=(bn, seq // tm),
        in_specs=[pl.BlockSpec((1, tm, GLA_VAL), tok),
                  pl.BlockSpec((1, tm, GLA_VAL), lambda b, j: (b, j0 + j, rb)),
                  pl.BlockSpec((1, tm, SSD_INNER), tok),
                  pl.BlockSpec((1, tm, SSD_INNER), lambda b, j: (b, j0 + j, zb)),
                  pl.BlockSpec((1, tm, D_MODEL), tok),
                  pl.BlockSpec((1, 4, D_MODEL), lambda b, j: (b, 0, 0)),
                  pl.BlockSpec((1, GLA_DV), const),
                  pl.BlockSpec((1, SSD_INNER), const),
                  pl.BlockSpec((1, D_MODEL), const),
                  pl.BlockSpec((GLA_VAL + SSD_INNER, D_MODEL), const),
                  pl.BlockSpec((D_MODEL, LANES), const),
                  pl.BlockSpec((1, LANES), const)],
        out_specs=[pl.BlockSpec((1, tm, D_MODEL), tok),
                   pl.BlockSpec((1, tm) + ROW_TILE, lambda b, j: (b, j, 0, 0)),
                   pl.BlockSpec((1, tm, LANES), tok),
                   pl.BlockSpec((1, LANES), const)],
        out_shape=[jax.ShapeDtypeStruct((bn, seq, D_MODEL), F32),
                   jax.ShapeDtypeStruct((bn, seq) + ROW_TILE, F32),
                   jax.ShapeDtypeStruct((bn, seq, LANES), F32),
                   jax.ShapeDtypeStruct((1, LANES), F32)],
        scratch_shapes=[pltpu.VMEM((1, LANES), F32)],
        compiler_params=_cparams(("arbitrary", "arbitrary")),
        name="outproj",
    )(o_gla, proj, y_ssd, proj, x, mod2, gn_row, sn_row, fn_row, w_out, w_router, b_router)


MOE_TILE = 256
PLAN_TM = 1024
DISPATCH_TM = 512
COMBINE_TM = 256


def _moe_tiles(n_tokens):
    return (2 * n_tokens) // MOE_TILE + N_EXPERTS


def _plan_body(nt_pad, route_ref, cnt_ref, pos_ref, te_ref, nt_ref):
    lane = lax.broadcasted_iota(jnp.int32, (1, LANES), 1).astype(F32)
    is_e = (lane >= float(EXPERT_LANE0)) & (lane < float(EXPERT_LANE0 + N_EXPERTS))
    tiles = jnp.where(is_e, jnp.floor((cnt_ref[...] + float(MOE_TILE - 1)) / float(MOE_TILE)), 0.0)
    ii = lax.broadcasted_iota(jnp.int32, (LANES, LANES), 0)
    jj = lax.broadcasted_iota(jnp.int32, (LANES, LANES), 1)
    lower = jnp.where(ii < jj, 1.0, 0.0).astype(BF16)
    first_tile = jnp.dot(jnp.broadcast_to(tiles, (8, LANES)).astype(BF16), lower,
                         preferred_element_type=F32)[0:1]
    off_row = first_tile * float(MOE_TILE)
    for sb in range(PLAN_TM // LANES):
        r = route_ref[sb * LANES:(sb + 1) * LANES, :]
        o1 = jnp.sum(jnp.where(lane == r[:, RT_E1:RT_E1 + 1] + float(EXPERT_LANE0), off_row, 0.0),
                     axis=-1, keepdims=True)
        o2 = jnp.sum(jnp.where(lane == r[:, RT_E2:RT_E2 + 1] + float(EXPERT_LANE0), off_row, 0.0),
                     axis=-1, keepdims=True)
        p = jnp.where(lane == 0.0, o1 + r[:, RT_RANK1:RT_RANK1 + 1],
                      jnp.where(lane == 1.0, o2 + r[:, RT_RANK2:RT_RANK2 + 1], 0.0))
        pos_ref[sb] = p.T[0:8, :].astype(jnp.int32)

    @pl.when(pl.program_id(0) == 0)
    def _():
        ti = lax.broadcasted_iota(jnp.int32, (nt_pad, 1), 0).astype(F32)
        done = jnp.where(is_e & (first_tile + tiles <= ti), 1.0, 0.0)
        te = jnp.minimum(jnp.sum(done, axis=-1, keepdims=True), float(N_EXPERTS - 1))
        te_ref[...] = te.astype(jnp.int32)
        nt_ref[...] = jnp.sum(tiles, axis=-1, keepdims=True).astype(jnp.int32)


def _plan(route, cnt):
    n_tok = route.shape[0]
    nt_pad = _moe_tiles(n_tok)
    nsb = PLAN_TM // LANES
    return pl.pallas_call(
        functools.partial(_plan_body, nt_pad),
        grid=(n_tok // PLAN_TM,),
        in_specs=[pl.BlockSpec((PLAN_TM, LANES), lambda i: (i, 0)),
                  pl.BlockSpec((1, LANES), lambda i: (0, 0))],
        out_specs=[pl.BlockSpec((nsb, 8, LANES), lambda i: (i, 0, 0)),
                   pl.BlockSpec((nt_pad, 1), lambda i: (0, 0)),
                   pl.BlockSpec((1, 1), lambda i: (0, 0))],
        out_shape=[jax.ShapeDtypeStruct((n_tok // LANES, 8, LANES), jnp.int32),
                   jax.ShapeDtypeStruct((nt_pad, 1), jnp.int32),
                   jax.ShapeDtypeStruct((1, 1), jnp.int32)],
        compiler_params=_cparams(("arbitrary",)),
        name="moe_plan",
    )(route, cnt)


def _row_copy(src, s_row, dst, d_row, sem):
    return pltpu.make_async_copy(src.at[s_row], dst.at[d_row], sem)


def _issue_rows(n_rows, pos_ref, start_pair):
    for blk in range(n_rows // LANES):
        def issue(l, c, blk=blk):
            start_pair(blk * LANES + l, pos_ref[blk, 0, l], pos_ref[blk, 1, l])
            return c
        lax.fori_loop(0, LANES, issue, 0, unroll=8)


def _drain_rows(n_copies, copy):
    def drain(t, c):
        copy.wait()
        return c
    lax.fori_loop(0, n_copies, drain, 0, unroll=8)


def _dispatch_body(pos_ref, h2_ref, xs_in_ref, xs_ref, sem):
    del xs_in_ref

    def start_pair(t, p1, p2):
        _row_copy(h2_ref, t, xs_ref, p1, sem).start()
        _row_copy(h2_ref, t, xs_ref, p2, sem).start()

    _issue_rows(DISPATCH_TM, pos_ref, start_pair)
    _drain_rows(2 * DISPATCH_TM, _row_copy(h2_ref, 0, xs_ref, 0, sem))


def _dispatch(pos, h2t, xs_zero):
    n_tok = h2t.shape[0]
    nb = DISPATCH_TM // LANES
    return pl.pallas_call(
        _dispatch_body,
        grid=(n_tok // DISPATCH_TM,),
        in_specs=[pl.BlockSpec((nb, 8, LANES), lambda i: (i, 0, 0), memory_space=pltpu.SMEM),
                  pl.BlockSpec((DISPATCH_TM,) + ROW_TILE, lambda i: (i, 0, 0)),
                  pl.BlockSpec(memory_space=pl.ANY)],
        out_specs=pl.BlockSpec(memory_space=pl.ANY),
        out_shape=jax.ShapeDtypeStruct(xs_zero.shape, xs_zero.dtype),
        scratch_shapes=[pltpu.SemaphoreType.DMA(())],
        input_output_aliases={2: 0},
        compiler_params=_cparams(("arbitrary",)),
        name="moe_dispatch",
    )(pos, h2t, xs_zero)


def _expert_body(te_ref, nt_ref, xs_ref, wg_ref, wu_ref, wd_ref, ys_ref, wgb_ref, wub_ref, wdb_ref):
    i = pl.program_id(0)
    new_expert = (i == 0) | (te_ref[i] != te_ref[jnp.maximum(i - 1, 0)])

    @pl.when(new_expert)
    def _():
        wgb_ref[...] = wg_ref[0].astype(BF16)
        wub_ref[...] = wu_ref[0].astype(BF16)
        wdb_ref[...] = wd_ref[0].astype(BF16)

    @pl.when(i < nt_ref[0])
    def _():
        x = _load_row_tiles(xs_ref).astype(BF16)
        gate = jnp.dot(x, wgb_ref[...], preferred_element_type=F32)
        up = jnp.dot(x, wub_ref[...], preferred_element_type=F32)
        y = jnp.dot((_silu(gate) * up).astype(BF16), wdb_ref[...], preferred_element_type=F32)
        _store_row_tiles(ys_ref, y)

    @pl.when(i >= nt_ref[0])
    def _():
        ys_ref[...] = jnp.zeros_like(ys_ref)


def _experts(te, nt, xs, w_gate, w_up, w_down):
    rows = xs.shape[0]
    n_tiles = rows // MOE_TILE
    row_block = pl.BlockSpec((MOE_TILE,) + ROW_TILE, lambda i, te, nt: (i, 0, 0))
    grid_spec = pltpu.PrefetchScalarGridSpec(
        num_scalar_prefetch=2,
        grid=(n_tiles,),
        in_specs=[row_block,
                  pl.BlockSpec((1, D_MODEL, D_FF), lambda i, te, nt: (te[i], 0, 0)),
                  pl.BlockSpec((1, D_MODEL, D_FF), lambda i, te, nt: (te[i], 0, 0)),
                  pl.BlockSpec((1, D_FF, D_MODEL), lambda i, te, nt: (te[i], 0, 0))],
        out_specs=row_block,
        scratch_shapes=[pltpu.VMEM((D_MODEL, D_FF), BF16), pltpu.VMEM((D_MODEL, D_FF), BF16),
                        pltpu.VMEM((D_FF, D_MODEL), BF16)])
    return pl.pallas_call(
        _expert_body,
        grid_spec=grid_spec,
        out_shape=jax.ShapeDtypeStruct(xs.shape, F32),
        compiler_params=_cparams(("arbitrary",)),
        name="moe_experts",
    )(te, nt, xs, w_gate, w_up, w_down)


def _combine_body(pos_ref, route_ref, hlat_ref, g2_ref, fn_ref, ys_ref, out_ref, buf_ref, sem):
    tm = COMBINE_TM

    def start_pair(t, p1, p2):
        _row_copy(ys_ref, p1, buf_ref.at[0], t, sem).start()
        _row_copy(ys_ref, p2, buf_ref.at[1], t, sem).start()

    _issue_rows(tm, pos_ref, start_pair)
    _drain_rows(2 * tm, _row_copy(ys_ref, 0, buf_ref.at[0], 0, sem))

    r = route_ref[0]
    moe = (r[:, RT_W1:RT_W1 + 1] * _load_row_tiles(buf_ref.at[0])
           + r[:, RT_W2:RT_W2 + 1] * _load_row_tiles(buf_ref.at[1]))
    out_ref[0] = _rms(hlat_ref[0] + g2_ref[0] * moe, fn_ref[...])


def _combine(pos, route, hlat, g2, fn_row, ys):
    bn, seq, _ = hlat.shape
    tm = COMBINE_TM
    nj = seq // tm
    nb = tm // LANES
    tok = lambda b, j: (b, j, 0)
    return pl.pallas_call(
        _combine_body,
        grid=(bn, nj),
        in_specs=[pl.BlockSpec((nb, 8, LANES), lambda b, j: (b * nj + j, 0, 0), memory_space=pltpu.SMEM),
                  pl.BlockSpec((1, tm, LANES), tok),
                  pl.BlockSpec((1, tm, D_MODEL), tok),
                  pl.BlockSpec((1, 1, D_MODEL), lambda b, j: (b, 0, 0)),
                  pl.BlockSpec((1, D_MODEL), lambda b, j: (0, 0)),
                  pl.BlockSpec(memory_space=pl.ANY)],
        out_specs=pl.BlockSpec((1, tm, D_MODEL), tok),
        out_shape=jax.ShapeDtypeStruct((bn, seq, D_MODEL), F32),
        scratch_shapes=[pltpu.VMEM((2, tm) + ROW_TILE, F32), pltpu.SemaphoreType.DMA(())],
        compiler_params=_cparams(("arbitrary", "arbitrary")),
        name="moe_combine",
    )(pos, route, hlat, g2, fn_row, ys)


def _prep_weights(w_in, gla_w_up, gla_b, ssd_dt_bias, ssd_a_log, ssd_d, router_group_w, router_group_b,
                  router_expert_w, router_expert_b):
    off_k = GLA_KEY
    off_v = 2 * GLA_KEY
    off_r = off_v + GLA_VAL
    off_g = off_r + GLA_VAL
    off_z = off_g + 2 * GLA_RANK
    off_xbc = off_z + SSD_INNER
    off_dt = off_xbc + SSD_XBC
    w_main = jnp.concatenate([w_in[:, :off_g], w_in[:, off_z:off_dt]], axis=1).astype(BF16)
    blocks = [jnp.pad(w_in[:, off_g:off_z], ((0, 0), (0, LANES - 2 * GLA_RANK)))]
    for d in range(2):
        for g in range(SSD_GROUPS):
            c0 = off_dt + d * SSD_HEADS + g * SSD_HPG
            blocks.append(jnp.pad(w_in[:, c0:c0 + SSD_HPG], ((0, 0), (0, LANES - SSD_HPG))))
    w_small = jnp.concatenate(blocks, axis=1).astype(BF16)
    wup = jnp.zeros((2, LANES, GLA_KEY), F32)
    wup = wup.at[0, 0:GLA_RANK].set(gla_w_up[0]).at[1, GLA_RANK:2 * GLA_RANK].set(gla_w_up[1]).astype(BF16)
    pad_h = lambda t: jnp.pad(t.reshape(2, SSD_GROUPS, 1, SSD_HPG), ((0, 0), (0, 0), (0, 0), (0, LANES - SSD_HPG)))
    w_router = jnp.zeros((D_MODEL, LANES), F32)
    w_router = w_router.at[:, 0:N_GROUPS].set(router_group_w)
    w_router = w_router.at[:, EXPERT_LANE0:EXPERT_LANE0 + N_EXPERTS].set(router_expert_w).astype(BF16)
    b_router = jnp.zeros((1, LANES), F32).at[0, 0:N_GROUPS].set(router_group_b)
    b_router = b_router.at[0, EXPERT_LANE0:EXPERT_LANE0 + N_EXPERTS].set(router_expert_b)
    return (w_main, w_small, wup, gla_b.reshape(2, 1, GLA_KEY), pad_h(ssd_dt_bias), pad_h(ssd_a_log),
            jnp.repeat(ssd_d, SSD_P)[None, :], w_router, b_router)


def kernel(x, c, ctx, c_ctx, w_ada, b_ada, norm_mix, norm_ffn, w_in, gla_w_up, gla_b, gla_norm, ssd_conv_w,
           ssd_conv_b, ssd_dt_bias, ssd_a_log, ssd_d, ssd_norm, w_out, router_group_w, router_group_b,
           router_expert_w, router_expert_b, expert_w_gate, expert_w_up, expert_w_down, final_norm):
    assert w_ada.shape[0] == 1, "single-layer block"
    bn, seq, d = x.shape
    ctx_len = ctx.shape[1]
    assert seq // GRID_W == GRID_W and d == D_MODEL

    c8 = jnp.concatenate([c, c_ctx[None, :], jnp.zeros((8 - bn - 1, d), F32)], axis=0)
    mods = _adaln(c8, w_ada[0], b_ada[0][None, :])
    sh1, sc1, g1, sh2, sc2, g2 = [mods[:, i * d:(i + 1) * d] for i in range(6)]
    ctx_row = lambda t: jnp.broadcast_to(t[bn:bn + 1], (bn, d))
    mod1 = jnp.stack([ctx_row(sh1), ctx_row(sc1), sh1[:bn], sc1[:bn]], axis=1)
    mod2 = jnp.stack([g1[:bn], sh2[:bn], sc2[:bn], g2[:bn]], axis=1)

    (w_main, w_small, wup, gla_b3, dtb4, alog4, dskip_row, w_router, b_router) = _prep_weights(
        w_in[0], gla_w_up[0], gla_b[0], ssd_dt_bias[0], ssd_a_log[0], ssd_d[0], router_group_w[0],
        router_group_b[0], router_expert_w[0], router_expert_b[0])

    proj, small = _inproj(ctx, x, mod1, norm_mix, w_main, w_small)
    xact = _conv(proj, ssd_conv_w[0].reshape(9, SSD_XBC), ssd_conv_b, ctx_len, seq)
    lvm = jnp.asarray(np.stack([_level_matrix(CHUNK, False), _level_matrix(CHUNK, True)]))
    o_gla = _gla(proj, small, wup, gla_b3, lvm, ctx_len, seq)
    y_ssd = _ssd(xact, small, dtb4, alog4, dskip_row, ctx_len, seq)
    hlat, h2p, route, cnt = _outproj(o_gla, proj, y_ssd, x, mod2, gla_norm, ssd_norm, norm_ffn,
                                     w_out[0].astype(BF16), w_router, b_router, ctx_len)
    n_tok = bn * seq
    pos, te, nt = _plan(route.reshape(n_tok, LANES), cnt)
    xs = _dispatch(pos, h2p.reshape((n_tok,) + ROW_TILE),
                   jnp.zeros((_moe_tiles(n_tok) * MOE_TILE,) + ROW_TILE, F32))
    ys = _experts(te.reshape(-1), nt.reshape(-1), xs, expert_w_gate[0], expert_w_up[0], expert_w_down[0])
    return _combine(pos, route, hlat, mod2[:, 3:4, :], final_norm[None, :], ys)
```

```python
import functools
import math

import numpy as np
import jax
import jax.numpy as jnp
from jax import lax
from jax.experimental import pallas as pl
from jax.experimental.pallas import tpu as pltpu

F32 = jnp.float32
BF16 = jnp.bfloat16

D_MODEL = 1024
GRID_W = 64
EPS = 1e-6

GLA_HEADS = 4
GLA_DK = 128
GLA_DV = 256
GLA_KEY = GLA_HEADS * GLA_DK
GLA_VAL = GLA_HEADS * GLA_DV
GLA_RANK = 16
GLA_GATE_NORM = 16.0

SSD_HEADS = 16
SSD_P = 64
SSD_INNER = SSD_HEADS * SSD_P
SSD_GROUPS = 2
SSD_HPG = SSD_HEADS // SSD_GROUPS
SSD_N = 128
SSD_XBC = SSD_INNER + 2 * SSD_GROUPS * SSD_N
SSD_GW = SSD_HPG * SSD_P

N_GROUPS = 4
EXPERTS_PER_GROUP = 8
N_EXPERTS = N_GROUPS * EXPERTS_PER_GROUP
D_FF = 512

LANES = 128
VMEM_LIMIT = 56 * 1024 * 1024

PC_Q = 0
PC_K = PC_Q + GLA_KEY
PC_V = PC_K + GLA_KEY
PC_R = PC_V + GLA_VAL
PC_Z = PC_R + GLA_VAL
PC_XBC = PC_Z + SSD_INNER
PROJ_W = PC_XBC + SSD_XBC
SMALL_DT0 = 2
SMALL_W = LANES * (SMALL_DT0 + 2 * SSD_GROUPS)

CHUNK = 128
EXPERT_LANE0 = 32


def _cparams(sem):
    return pltpu.CompilerParams(dimension_semantics=sem, vmem_limit_bytes=VMEM_LIMIT)


def _silu(x):
    return x / (1.0 + jnp.exp(-x))


def _softplus(x):
    return jnp.maximum(x, 0.0) + jnp.log(1.0 + jnp.exp(-jnp.abs(x)))


def _log_sigmoid(x):
    return jnp.minimum(x, 0.0) - jnp.log(1.0 + jnp.exp(-jnp.abs(x)))


def _bdot(a, b):
    return jnp.dot(a.astype(BF16), b.astype(BF16), preferred_element_type=F32)


def _bdot_nt(a, b):
    return lax.dot_general(a.astype(BF16), b.astype(BF16), (((1,), (1,)), ((), ())),
                           preferred_element_type=F32)


def _adaln_body(c_ref, w_ref, b_ref, o_ref):
    o_ref[...] = _bdot(_silu(c_ref[...]), w_ref[...]) + b_ref[...]


def _adaln(c8, w, b):
    n = w.shape[1]
    bn = 1024
    return pl.pallas_call(
        _adaln_body,
        grid=(n // bn,),
        in_specs=[pl.BlockSpec((8, D_MODEL), lambda j: (0, 0)),
                  pl.BlockSpec((D_MODEL, bn), lambda j: (0, j)),
                  pl.BlockSpec((1, bn), lambda j: (0, j))],
        out_specs=pl.BlockSpec((8, bn), lambda j: (0, j)),
        out_shape=jax.ShapeDtypeStruct((8, n), F32),
        compiler_params=_cparams(("arbitrary",)),
        name="adaln",
    )(c8, w, b)


INPROJ_TM = 256
INPROJ_NC = 512


def _inproj_body(ctx_ref, x_ref, mod_ref, nw_ref, w_ref, ws_ref, proj_ref, small_ref):
    is_ctx = pl.program_id(1) == 0
    xin = jnp.where(is_ctx, ctx_ref[0], x_ref[0])
    m = mod_ref[0]
    shift = jnp.where(is_ctx, m[0:1], m[2:3])
    scale = jnp.where(is_ctx, m[1:2], m[3:4])
    ms = jnp.mean(xin * xin, axis=-1, keepdims=True)
    h = xin * lax.rsqrt(ms + EPS) * nw_ref[...]
    hb = (h * (1.0 + scale) + shift).astype(BF16)
    for n in range(PROJ_W // INPROJ_NC):
        sl = slice(n * INPROJ_NC, (n + 1) * INPROJ_NC)
        proj_ref[0, :, sl] = jnp.dot(hb, w_ref[:, sl], preferred_element_type=F32).astype(BF16)
    small_ref[0] = jnp.dot(hb, ws_ref[...], preferred_element_type=F32)


def _inproj(ctx, x, mod1, norm_w, w_main, w_small):
    bn, seq, _ = x.shape
    ctx_len = ctx.shape[1]
    tm = INPROJ_TM
    assert ctx_len == tm and seq % tm == 0
    tt = ctx_len + seq
    nj = tt // tm
    return pl.pallas_call(
        _inproj_body,
        grid=(bn, nj),
        in_specs=[pl.BlockSpec((1, tm, D_MODEL), lambda b, j: (b, 0, 0)),
                  pl.BlockSpec((1, tm, D_MODEL), lambda b, j: (b, jnp.maximum(j - 1, 0), 0)),
                  pl.BlockSpec((1, 4, D_MODEL), lambda b, j: (b, 0, 0)),
                  pl.BlockSpec((1, D_MODEL), lambda b, j: (0, 0)),
                  pl.BlockSpec((D_MODEL, PROJ_W), lambda b, j: (0, 0)),
                  pl.BlockSpec((D_MODEL, SMALL_W), lambda b, j: (0, 0))],
        out_specs=[pl.BlockSpec((1, tm, PROJ_W), lambda b, j: (b, j, 0)),
                   pl.BlockSpec((1, tm, SMALL_W), lambda b, j: (b, j, 0))],
        out_shape=[jax.ShapeDtypeStruct((bn, tt, PROJ_W), BF16),
                   jax.ShapeDtypeStruct((bn, tt, SMALL_W), F32)],
        compiler_params=_cparams(("arbitrary", "arbitrary")),
        name="inproj",
    )(ctx, x, mod1, norm_w, w_main, w_small)


CONV_CB = 256
CONV_TT = 256


def _conv_body(ctx_len, seq, p_ref, w_ref, b_ref, o_ref, s_ref):
    lat0 = ctx_len + GRID_W
    tt = CONV_TT
    n_tiles = seq // tt
    zeros = jnp.zeros((GRID_W, CONV_CB), F32)
    s_ref[0:ctx_len, :] = p_ref[0, 0:ctx_len, :].astype(F32)
    s_ref[ctx_len:lat0, :] = zeros
    s_ref[lat0 + seq:lat0 + seq + GRID_W, :] = zeros

    def copy_tile(t, c):
        src = pl.multiple_of(ctx_len + t * tt, tt)
        dst = pl.multiple_of(lat0 + t * tt, GRID_W)
        s_ref[pl.ds(dst, tt), :] = p_ref[0, pl.ds(src, tt), :].astype(F32)
        return c

    lax.fori_loop(0, n_tiles, copy_tile, 0)

    w = w_ref[...]
    bias = b_ref[...]
    pos = lax.broadcasted_iota(jnp.int32, (tt, 1), 0)

    def conv_rows(rows, first, last):
        left, mid, right = [sum(xr * w[3 * kh + kw:3 * kh + kw + 1] for xr, kh in rows) for kw in range(3)]
        return (bias + mid + jnp.where(first, 0.0, pltpu.roll(left, 1, 0))
                + jnp.where(last, 0.0, pltpu.roll(right, tt - 1, 0)))

    acc = conv_rows([(s_ref[0:ctx_len, :], 1)], pos == 0, pos == ctx_len - 1)
    o_ref[0, 0:ctx_len, :] = _silu(acc).astype(BF16)

    col = pos % GRID_W
    first = col == 0
    last = col == GRID_W - 1

    def tile(t, c):
        base = lat0 + t * tt
        rows = [(s_ref[pl.ds(pl.multiple_of(base + (kh - 1) * GRID_W, GRID_W), tt), :], kh) for kh in range(3)]
        dst = pl.multiple_of(ctx_len + t * tt, tt)
        o_ref[0, pl.ds(dst, tt), :] = _silu(conv_rows(rows, first, last)).astype(BF16)
        return c

    lax.fori_loop(0, n_tiles, tile, 0)


def _conv(proj, conv_w9, conv_b, ctx_len, seq):
    bn, tt, _ = proj.shape
    assert ctx_len == CONV_TT and seq % CONV_TT == 0 and CONV_TT % GRID_W == 0
    ncb = SSD_XBC // CONV_CB
    cb0 = PC_XBC // CONV_CB
    return pl.pallas_call(
        functools.partial(_conv_body, ctx_len, seq),
        grid=(bn, ncb),
        in_specs=[pl.BlockSpec((1, tt, CONV_CB), lambda b, j: (b, 0, cb0 + j)),
                  pl.BlockSpec((9, CONV_CB), lambda b, j: (0, j)),
                  pl.BlockSpec((1, CONV_CB), lambda b, j: (0, j))],
        out_specs=pl.BlockSpec((1, tt, CONV_CB), lambda b, j: (b, 0, j)),
        out_shape=jax.ShapeDtypeStruct((bn, tt, SSD_XBC), BF16),
        scratch_shapes=[pltpu.VMEM((tt + 2 * GRID_W, CONV_CB), F32)],
        compiler_params=_cparams(("arbitrary", "arbitrary")),
        name="conv",
    )(proj, conv_w9, conv_b)


def _bwd_chunk(s, n_ctx, n_steps):
    return jnp.where(s < n_ctx, n_ctx - 1 - s, n_steps + n_ctx - 1 - s)


def _out_row(c, n_ctx, n, seq):
    return pl.multiple_of(jnp.where(c >= n_ctx, (c - n_ctx) * n, seq), n)


def _scan_sum(a, rev):
    n = a.shape[0]
    row = lax.broadcasted_iota(jnp.int32, (n, 1), 0)
    sh = 1
    while sh < n:
        if rev:
            a = a + jnp.where(row < n - sh, pltpu.roll(a, n - sh, 0), 0.0)
        else:
            a = a + jnp.where(row >= sh, pltpu.roll(a, sh, 0), 0.0)
        sh *= 2
    return a


def _level_matrix(n, rev):
    i = np.arange(n)[:, None]
    j = np.arange(n)[None, :]
    x = i ^ j
    lv = np.where(x > 0, np.floor(np.log2(np.maximum(x, 1))).astype(np.int64) + 1, 0)
    earlier = (j > i) if rev else (j < i)
    return np.where(earlier, lv, 0).astype(np.int32)


GLA_HPS = 4


def _gla_dir(rev, dst, q_ref, k_ref, v_ref, sm_ref, wup, gb, lv, o_ref, s_ref):
    n = CHUNK
    q = q_ref[0].astype(F32) * (GLA_DK ** -0.5)
    k = k_ref[0].astype(F32)
    x = jnp.dot(sm_ref[0].astype(BF16), wup, preferred_element_type=F32) + gb
    b = _scan_sum(_log_sigmoid(x) / GLA_GATE_NORM, rev)
    row = lax.broadcasted_iota(jnp.int32, (n, 1), 0)
    b_end = b[0:1] if rev else b[n - 1:n]
    k_end = k * jnp.exp(b_end - b)
    q_in = q * jnp.exp(b)
    dec_end = jnp.exp(b_end)

    att = [jnp.zeros((n, n), F32) for _ in range(GLA_HPS)]
    e_h = b
    h = 1
    level = 1
    while h < n:
        upper = (row & h) != 0
        if rev:
            e = jnp.where(upper, e_h - b, b - pltpu.roll(e_h, n - h, 0))
        else:
            e = jnp.where(upper, b - pltpu.roll(e_h, h, 0), e_h - b)
        w = jnp.exp(e)
        qw = (q * w).astype(BF16)
        kw = (k * w).astype(BF16)
        mask = lv == level
        for hh in range(GLA_HPS):
            ks = slice(hh * GLA_DK, (hh + 1) * GLA_DK)
            att[hh] = att[hh] + jnp.where(mask, _bdot_nt(qw[:, ks], kw[:, ks]), 0.0)
        if 2 * h < n:
            if rev:
                e_h = jnp.where(upper, pltpu.roll(e_h, h, 0), e_h)
            else:
                e_h = jnp.where(upper, e_h, pltpu.roll(e_h, n - h, 0))
        h *= 2
        level += 1

    for hh in range(GLA_HPS):
        ks = slice(hh * GLA_DK, (hh + 1) * GLA_DK)
        vs = slice(hh * GLA_DV, (hh + 1) * GLA_DV)
        v = v_ref[0, :, vs]
        s_old = s_ref[hh]
        o = jnp.dot(att[hh].astype(BF16), v, preferred_element_type=F32)
        o = o + jnp.sum(q[:, ks] * k[:, ks], axis=-1, keepdims=True) * v.astype(F32)
        o = o + _bdot(q_in[:, ks], s_old)
        o_ref[0, pl.ds(dst, n), vs] += o
        dec_col = jnp.broadcast_to(dec_end[:, ks], (n, GLA_DK)).T[:, 0:1]
        s_ref[hh] = dec_col * s_old + jnp.dot(k_end[:, ks].T.astype(BF16), v, preferred_element_type=F32)


def _gla_body(n_ctx, seq, qf, kf, vf, smf, qb, kb, vb, smb, wup_ref, gb_ref, lv_ref, o_ref, s_ref):
    step = pl.program_id(2)
    n = CHUNK

    @pl.when(step == 0)
    def _():
        s_ref[...] = jnp.zeros_like(s_ref)
        o_ref[...] = jnp.zeros_like(o_ref)

    cb = _bwd_chunk(step, n_ctx, pl.num_programs(2))
    _gla_dir(False, _out_row(step, n_ctx, n, seq), qf, kf, vf, smf, wup_ref[0], gb_ref[0], lv_ref[0],
             o_ref, s_ref.at[0])
    _gla_dir(True, _out_row(cb, n_ctx, n, seq), qb, kb, vb, smb, wup_ref[1], gb_ref[1], lv_ref[1],
             o_ref, s_ref.at[1])


def _gla(proj, small, wup_pad, gla_b3, lvm, ctx_len, seq):
    bn, tt, _ = proj.shape
    n = CHUNK
    assert ctx_len % n == 0 and seq % n == 0
    n_ctx = ctx_len // n
    n_steps = tt // n
    hw = GLA_HPS * GLA_DK
    vw = GLA_HPS * GLA_DV
    kb = PC_K // hw
    vb = PC_V // vw
    fwd = lambda s: s
    bwd = lambda s: _bwd_chunk(s, n_ctx, n_steps)

    def chunk_specs(c):
        return [pl.BlockSpec((1, n, hw), lambda b, h, s: (b, c(s), h)),
                pl.BlockSpec((1, n, hw), lambda b, h, s: (b, c(s), kb + h)),
                pl.BlockSpec((1, n, vw), lambda b, h, s: (b, c(s), vb + h)),
                pl.BlockSpec((1, n, LANES), lambda b, h, s: (b, c(s), 0))]

    return pl.pallas_call(
        functools.partial(_gla_body, n_ctx, seq),
        grid=(bn, GLA_HEADS // GLA_HPS, n_steps),
        in_specs=chunk_specs(fwd) + chunk_specs(bwd) + [
            pl.BlockSpec((2, LANES, hw), lambda b, h, s: (0, 0, h)),
            pl.BlockSpec((2, 1, hw), lambda b, h, s: (0, 0, h)),
            pl.BlockSpec((2, n, n), lambda b, h, s: (0, 0, 0))],
        out_specs=pl.BlockSpec((1, seq + n, vw), lambda b, h, s: (b, 0, h)),
        out_shape=jax.ShapeDtypeStruct((bn, seq + n, GLA_VAL), F32),
        scratch_shapes=[pltpu.VMEM((2, GLA_HPS, GLA_DK, GLA_DV), F32)],
        compiler_params=_cparams(("arbitrary",) * 3),
        name="gla",
    )(proj, proj, proj, small, proj, proj, proj, small, wup_pad, gla_b3, lvm)


def _ssd_dir(rev, first, g, dst, x_ref, bm_ref, cm_ref, sm_ref, dtb, alog, dsk_ref, y_ref, s_ref):
    n = CHUNK
    gl = slice(g * LANES, (g + 1) * LANES)
    g0 = g * SSD_GW
    lane = lax.broadcasted_iota(jnp.int32, (1, LANES), 1)
    dt = _softplus(sm_ref[0, :, gl] + dtb)
    a_neg = jnp.where(lane < SSD_HPG, -jnp.exp(alog), 0.0)
    acum = _scan_sum(dt * a_neg, rev)
    acum_t = acum.T
    x = x_ref[0, :, g0:g0 + SSD_GW].astype(F32)
    cm = cm_ref[0, :, gl]
    bm = bm_ref[0, :, gl]
    s_old = s_ref[...]
    ii = lax.broadcasted_iota(jnp.int32, (n, n), 0)
    jj = lax.broadcasted_iota(jnp.int32, (n, n), 1)
    causal = (jj >= ii) if rev else (jj <= ii)
    cb = _bdot_nt(cm, bm)
    y_off = jnp.dot(cm, s_old.astype(BF16), preferred_element_type=F32)
    low = lane < SSD_P
    xs_parts = []
    send_parts = []
    for p in range(SSD_HPG // 2):
        slab = slice(p * LANES, (p + 1) * LANES)
        heads = []
        for hh in (2 * p, 2 * p + 1):
            colb = jnp.broadcast_to(acum[:, hh:hh + 1], (n, LANES))
            dtb_h = jnp.broadcast_to(dt[:, hh:hh + 1], (n, LANES))
            seg = jnp.exp(jnp.where(causal, colb - acum_t[hh:hh + 1, :], -jnp.inf))
            end = colb[0:1] if rev else colb[n - 1:n]
            heads.append(((cb * seg).astype(BF16), dtb_h, jnp.exp(colb), jnp.exp(end - colb), jnp.exp(end)))
        (m0, dt0, ec0, dc0, se0), (m1, dt1, ec1, dc1, se1) = heads
        xdt = x[:, slab] * jnp.where(low, dt0, dt1)
        rhs = jnp.concatenate([jnp.where(low, xdt, 0.0), jnp.where(low, 0.0, xdt)], axis=0).astype(BF16)
        y = jnp.dot(jnp.concatenate([m0, m1], axis=1), rhs, preferred_element_type=F32)
        y = y + y_off[:, slab] * jnp.where(low, ec0, ec1)
        out = slice(g0 + p * LANES, g0 + (p + 1) * LANES)
        if first:
            y = y + x[:, slab] * dsk_ref[:, out]
        y_ref[0, pl.ds(dst, n), out] += y
        xs_parts.append((xdt * jnp.where(low, dc0, dc1)).astype(BF16))
        send_parts.append(jnp.where(low, se0, se1))
    new = jnp.dot(bm.astype(F32).T.astype(BF16), jnp.concatenate(xs_parts, axis=1),
                  preferred_element_type=F32)
    s_ref[...] = s_old * jnp.concatenate(send_parts, axis=1) + new


def _ssd_body(n_ctx, seq, xf, bf, cf, smf, xb, bb, cbk, smb, dtb_ref, alog_ref, dsk_ref, y_ref, s_ref):
    step = pl.program_id(1)
    n = CHUNK

    @pl.when(step == 0)
    def _():
        s_ref[...] = jnp.zeros_like(s_ref)
        y_ref[...] = jnp.zeros_like(y_ref)

    dst_f = _out_row(step, n_ctx, n, seq)
    dst_b = _out_row(_bwd_chunk(step, n_ctx, pl.num_programs(1)), n_ctx, n, seq)
    for g in range(SSD_GROUPS):
        _ssd_dir(False, True, g, dst_f, xf, bf, cf, smf, dtb_ref[0, g], alog_ref[0, g], dsk_ref, y_ref,
                 s_ref.at[0, g])
        _ssd_dir(True, False, g, dst_b, xb, bb, cbk, smb, dtb_ref[1, g], alog_ref[1, g], dsk_ref, y_ref,
                 s_ref.at[1, g])


def _ssd(xact, small, dtb4, alog4, dskip_row, ctx_len, seq):
    bn, tt, _ = xact.shape
    n = CHUNK
    n_ctx = ctx_len // n
    n_steps = tt // n
    gn = SSD_GROUPS * SSD_N
    assert SSD_INNER % gn == 0 and (SMALL_DT0 * LANES) % (SSD_GROUPS * LANES) == 0
    bblk = SSD_INNER // gn
    sblk = SMALL_DT0 // SSD_GROUPS
    fwd = lambda s: s
    bwd = lambda s: _bwd_chunk(s, n_ctx, n_steps)

    def chunk_specs(c, d):
        return [pl.BlockSpec((1, n, SSD_INNER), lambda b, s: (b, c(s), 0)),
                pl.BlockSpec((1, n, gn), lambda b, s: (b, c(s), bblk)),
                pl.BlockSpec((1, n, gn), lambda b, s: (b, c(s), bblk + 1)),
                pl.BlockSpec((1, n, SSD_GROUPS * LANES), lambda b, s: (b, c(s), sblk + d))]

    whole4 = pl.BlockSpec((2, SSD_GROUPS, 1, LANES), lambda b, s: (0, 0, 0, 0))
    return pl.pallas_call(
        functools.partial(_ssd_body, n_ctx, seq),
        grid=(bn, n_steps),
        in_specs=chunk_specs(fwd, 0) + chunk_specs(bwd, 1) + [
            whole4, whole4, pl.BlockSpec((1, SSD_INNER), lambda b, s: (0, 0))],
        out_specs=pl.BlockSpec((1, seq + n, SSD_INNER), lambda b, s: (b, 0, 0)),
        out_shape=jax.ShapeDtypeStruct((bn, seq + n, SSD_INNER), F32),
        scratch_shapes=[pltpu.VMEM((2, SSD_GROUPS, SSD_N, SSD_GW), F32)],
        compiler_params=_cparams(("arbitrary",) * 2),
        name="ssd",
    )(xact, xact, xact, small, xact, xact, xact, small, dtb4, alog4, dskip_row)


OUT_TM = 256
RT_E1, RT_E2, RT_RANK1, RT_RANK2, RT_W1, RT_W2 = range(6)


def _rms(x, w):
    return x * lax.rsqrt(jnp.mean(x * x, axis=-1, keepdims=True) + EPS) * w


ROW_TILE = (D_MODEL // LANES, LANES)


def _store_row_tiles(ref, x):
    ref[...] = x.reshape((x.shape[0],) + ROW_TILE)


def _load_row_tiles(ref):
    return ref[...].reshape(ref.shape[0], D_MODEL)


def _outproj_body(o_ref, r_ref, y_ref, z_ref, x_ref, mod_ref, gn_ref, sn_ref, fn_ref, wo_ref, wr_ref, br_ref,
                  hlat_ref, h2_ref, route_ref, cnt_ref, carry_ref):
    tm = OUT_TM

    @pl.when((pl.program_id(0) == 0) & (pl.program_id(1) == 0))
    def _():
        carry_ref[...] = jnp.zeros_like(carry_ref)

    parts = []
    for h in range(GLA_HEADS):
        hs = slice(h * GLA_DV, (h + 1) * GLA_DV)
        parts.append((_rms(o_ref[0, :, hs], gn_ref[...]) * _silu(r_ref[0, :, hs].astype(F32))).astype(BF16))
    for g in range(SSD_GROUPS):
        gs = slice(g * SSD_GW, (g + 1) * SSD_GW)
        yg = y_ref[0, :, gs] * _silu(z_ref[0, :, gs].astype(F32))
        parts.append(_rms(yg, sn_ref[:, gs]).astype(BF16))
    mix = jnp.concatenate(parts, axis=-1)
    m = mod_ref[0]
    hlat = x_ref[0] + m[0:1] * jnp.dot(mix, wo_ref[...], preferred_element_type=F32)
    hlat_ref[0] = hlat
    h2f = _rms(hlat, fn_ref[...]) * (1.0 + m[2:3]) + m[1:2]
    _store_row_tiles(h2_ref.at[0], h2f)

    lg = jnp.dot(h2f.astype(BF16), wr_ref[...], preferred_element_type=F32) + br_ref[...]
    lanef = lax.broadcasted_iota(jnp.int32, (tm, LANES), 1).astype(F32)
    ninf = -jnp.inf
    is_g = lanef < float(N_GROUPS)
    gl = jnp.where(is_g, lg, ninf)
    gmax = jnp.max(gl, axis=-1, keepdims=True)
    gsel = jnp.min(jnp.where(gl == gmax, lanef, float(LANES)), axis=-1, keepdims=True)
    pg = 1.0 / jnp.sum(jnp.where(is_g, jnp.exp(lg - gmax), 0.0), axis=-1, keepdims=True)
    lo = float(EXPERT_LANE0) + float(EXPERTS_PER_GROUP) * gsel
    in_grp = (lanef >= lo) & (lanef < lo + float(EXPERTS_PER_GROUP))
    el = jnp.where(in_grp, lg, ninf)
    v1 = jnp.max(el, axis=-1, keepdims=True)
    i1 = jnp.min(jnp.where(el == v1, lanef, float(LANES)), axis=-1, keepdims=True)
    el2 = jnp.where(lanef == i1, ninf, el)
    v2 = jnp.max(el2, axis=-1, keepdims=True)
    i2 = jnp.min(jnp.where(el2 == v2, lanef, float(LANES)), axis=-1, keepdims=True)
    t = jnp.exp(v2 - v1)
    w1 = pg / (1.0 + t)
    w2 = pg * t / (1.0 + t)

    sel1 = lanef == i1
    sel2 = lanef == i2
    member = jnp.where(sel1 | sel2, 1.0, 0.0)
    ii = lax.broadcasted_iota(jnp.int32, (tm, tm), 0)
    jj = lax.broadcasted_iota(jnp.int32, (tm, tm), 1)
    before = jnp.where(jj < ii, 1.0, 0.0).astype(BF16)
    ranks = jnp.dot(before, member.astype(BF16), preferred_element_type=F32) + carry_ref[...]
    rank1 = jnp.sum(jnp.where(sel1, ranks, 0.0), axis=-1, keepdims=True)
    rank2 = jnp.sum(jnp.where(sel2, ranks, 0.0), axis=-1, keepdims=True)
    carry = carry_ref[...] + jnp.sum(member, axis=0, keepdims=True)
    carry_ref[...] = carry
    cnt_ref[...] = carry
    rec = jnp.zeros((tm, LANES), F32)
    for lane_id, val in ((RT_E1, i1 - float(EXPERT_LANE0)), (RT_E2, i2 - float(EXPERT_LANE0)),
                         (RT_RANK1, rank1), (RT_RANK2, rank2), (RT_W1, w1), (RT_W2, w2)):
        rec = jnp.where(lanef == float(lane_id), val, rec)
    route_ref[0] = rec


def _outproj(o_gla, proj, y_ssd, x, mod2, gn_row, sn_row, fn_row, w_out, w_router, b_router, ctx_len):
    bn, seq, _ = x.shape
    tm = OUT_TM
    assert ctx_len % tm == 0 and seq % tm == 0
    j0 = ctx_len // tm
    rb = PC_R // GLA_VAL
    zb = PC_Z // SSD_INNER
    tok = lambda b, j: (b, j, 0)
    const = lambda b, j: (0, 0)
    return pl.pallas_call(
        _outproj_body,
        grid=(bn, seq // tm),
        in_specs=[pl.BlockSpec((1, tm, GLA_VAL), tok),
                  pl.BlockSpec((1, tm, GLA_VAL), lambda b, j: (b, j0 + j, rb)),
                  pl.BlockSpec((1, tm, SSD_INNER), tok),
                  pl.BlockSpec((1, tm, SSD_INNER), lambda b, j: (b, j0 + j, zb)),
                  pl.BlockSpec((1, tm, D_MODEL), tok),
                  pl.BlockSpec((1, 4, D_MODEL), lambda b, j: (b, 0, 0)),
                  pl.BlockSpec((1, GLA_DV), const),
                  pl.BlockSpec((1, SSD_INNER), const),
                  pl.BlockSpec((1, D_MODEL), const),
                  pl.BlockSpec((GLA_VAL + SSD_INNER, D_MODEL), const),
                  pl.BlockSpec((D_MODEL, LANES), const),
                  pl.BlockSpec((1, LANES), const)],
        out_specs=[pl.BlockSpec((1, tm, D_MODEL), tok),
                   pl.BlockSpec((1, tm) + ROW_TILE, lambda b, j: (b, j, 0, 0)),
                   pl.BlockSpec((1, tm, LANES), tok),
                   pl.BlockSpec((1, LANES), const)],
        out_shape=[jax.ShapeDtypeStruct((bn, seq, D_MODEL), F32),
                   jax.ShapeDtypeStruct((bn, seq) + ROW_TILE, F32),
                   jax.ShapeDtypeStruct((bn, seq, LANES), F32),
                   jax.ShapeDtypeStruct((1, LANES), F32)],
        scratch_shapes=[pltpu.VMEM((1, LANES), F32)],
        compiler_params=_cparams(("arbitrary", "arbitrary")),
        name="outproj",
    )(o_gla, proj, y_ssd, proj, x, mod2, gn_row, sn_row, fn_row, w_out, w_router, b_router)


MOE_TILE = 256
PLAN_TM = 1024
DISPATCH_TM = 512
COMBINE_TM = 256


def _moe_tiles(n_tokens):
    return (2 * n_tokens) // MOE_TILE + N_EXPERTS


def _plan_body(nt_pad, route_ref, cnt_ref, pos_ref, te_ref, nt_ref):
    lane = lax.broadcasted_iota(jnp.int32, (1, LANES), 1).astype(F32)
    is_e = (lane >= float(EXPERT_LANE0)) & (lane < float(EXPERT_LANE0 + N_EXPERTS))
    tiles = jnp.where(is_e, jnp.floor((cnt_ref[...] + float(MOE_TILE - 1)) / float(MOE_TILE)), 0.0)
    ii = lax.broadcasted_iota(jnp.int32, (LANES, LANES), 0)
    jj = lax.broadcasted_iota(jnp.int32, (LANES, LANES), 1)
    lower = jnp.where(ii < jj, 1.0, 0.0).astype(BF16)
    first_tile = jnp.dot(jnp.broadcast_to(tiles, (8, LANES)).astype(BF16), lower,
                         preferred_element_type=F32)[0:1]
    off_row = first_tile * float(MOE_TILE)
    for sb in range(PLAN_TM // LANES):
        r = route_ref[sb * LANES:(sb + 1) * LANES, :]
        o1 = jnp.sum(jnp.where(lane == r[:, RT_E1:RT_E1 + 1] + float(EXPERT_LANE0), off_row, 0.0),
                     axis=-1, keepdims=True)
        o2 = jnp.sum(jnp.where(lane == r[:, RT_E2:RT_E2 + 1] + float(EXPERT_LANE0), off_row, 0.0),
                     axis=-1, keepdims=True)
        p = jnp.where(lane == 0.0, o1 + r[:, RT_RANK1:RT_RANK1 + 1],
                      jnp.where(lane == 1.0, o2 + r[:, RT_RANK2:RT_RANK2 + 1], 0.0))
        pos_ref[sb] = p.T[0:8, :].astype(jnp.int32)

    @pl.when(pl.program_id(0) == 0)
    def _():
        ti = lax.broadcasted_iota(jnp.int32, (nt_pad, 1), 0).astype(F32)
        done = jnp.where(is_e & (first_tile + tiles <= ti), 1.0, 0.0)
        te = jnp.minimum(jnp.sum(done, axis=-1, keepdims=True), float(N_EXPERTS - 1))
        te_ref[...] = te.astype(jnp.int32)
        nt_ref[...] = jnp.sum(tiles, axis=-1, keepdims=True).astype(jnp.int32)


def _plan(route, cnt):
    n_tok = route.shape[0]
    nt_pad = _moe_tiles(n_tok)
    nsb = PLAN_TM // LANES
    return pl.pallas_call(
        functools.partial(_plan_body, nt_pad),
        grid=(n_tok // PLAN_TM,),
        in_specs=[pl.BlockSpec((PLAN_TM, LANES), lambda i: (i, 0)),
                  pl.BlockSpec((1, LANES), lambda i: (0, 0))],
        out_specs=[pl.BlockSpec((nsb, 8, LANES), lambda i: (i, 0, 0)),
                   pl.BlockSpec((nt_pad, 1), lambda i: (0, 0)),
                   pl.BlockSpec((1, 1), lambda i: (0, 0))],
        out_shape=[jax.ShapeDtypeStruct((n_tok // LANES, 8, LANES), jnp.int32),
                   jax.ShapeDtypeStruct((nt_pad, 1), jnp.int32),
                   jax.ShapeDtypeStruct((1, 1), jnp.int32)],
        compiler_params=_cparams(("arbitrary",)),
        name="moe_plan",
    )(route, cnt)


def _row_copy(src, s_row, dst, d_row, sem):
    return pltpu.make_async_copy(src.at[s_row], dst.at[d_row], sem)


def _issue_rows(n_rows, pos_ref, start_pair):
    for blk in range(n_rows // LANES):
        def issue(l, c, blk=blk):
            start_pair(blk * LANES + l, pos_ref[blk, 0, l], pos_ref[blk, 1, l])
            return c
        lax.fori_loop(0, LANES, issue, 0, unroll=8)


def _drain_rows(n_copies, copy):
    def drain(t, c):
        copy.wait()
        return c
    lax.fori_loop(0, n_copies, drain, 0, unroll=8)


def _dispatch_body(pos_ref, h2_ref, xs_in_ref, xs_ref, sem):
    del xs_in_ref

    def start_pair(t, p1, p2):
        _row_copy(h2_ref, t, xs_ref, p1, sem).start()
        _row_copy(h2_ref, t, xs_ref, p2, sem).start()

    _issue_rows(DISPATCH_TM, pos_ref, start_pair)
    _drain_rows(2 * DISPATCH_TM, _row_copy(h2_ref, 0, xs_ref, 0, sem))


def _dispatch(pos, h2t, xs_zero):
    n_tok = h2t.shape[0]
    nb = DISPATCH_TM // LANES
    return pl.pallas_call(
        _dispatch_body,
        grid=(n_tok // DISPATCH_TM,),
        in_specs=[pl.BlockSpec((nb, 8, LANES), lambda i: (i, 0, 0), memory_space=pltpu.SMEM),
                  pl.BlockSpec((DISPATCH_TM,) + ROW_TILE, lambda i: (i, 0, 0)),
                  pl.BlockSpec(memory_space=pl.ANY)],
        out_specs=pl.BlockSpec(memory_space=pl.ANY),
        out_shape=jax.ShapeDtypeStruct(xs_zero.shape, xs_zero.dtype),
        scratch_shapes=[pltpu.SemaphoreType.DMA(())],
        input_output_aliases={2: 0},
        compiler_params=_cparams(("arbitrary",)),
        name="moe_dispatch",
    )(pos, h2t, xs_zero)


def _expert_body(te_ref, nt_ref, xs_ref, wg_ref, wu_ref, wd_ref, ys_ref, wgb_ref, wub_ref, wdb_ref):
    i = pl.program_id(0)
    new_expert = (i == 0) | (te_ref[i] != te_ref[jnp.maximum(i - 1, 0)])

    @pl.when(new_expert)
    def _():
        wgb_ref[...] = wg_ref[0].astype(BF16)
        wub_ref[...] = wu_ref[0].astype(BF16)
        wdb_ref[...] = wd_ref[0].astype(BF16)

    @pl.when(i < nt_ref[0])
    def _():
        x = _load_row_tiles(xs_ref).astype(BF16)
        gate = jnp.dot(x, wgb_ref[...], preferred_element_type=F32)
        up = jnp.dot(x, wub_ref[...], preferred_element_type=F32)
        y = jnp.dot((_silu(gate) * up).astype(BF16), wdb_ref[...], preferred_element_type=F32)
        _store_row_tiles(ys_ref, y)

    @pl.when(i >= nt_ref[0])
    def _():
        ys_ref[...] = jnp.zeros_like(ys_ref)


def _experts(te, nt, xs, w_gate, w_up, w_down):
    rows = xs.shape[0]
    n_tiles = rows // MOE_TILE
    row_block = pl.BlockSpec((MOE_TILE,) + ROW_TILE, lambda i, te, nt: (i, 0, 0))
    grid_spec = pltpu.PrefetchScalarGridSpec(
        num_scalar_prefetch=2,
        grid=(n_tiles,),
        in_specs=[row_block,
                  pl.BlockSpec((1, D_MODEL, D_FF), lambda i, te, nt: (te[i], 0, 0)),
                  pl.BlockSpec((1, D_MODEL, D_FF), lambda i, te, nt: (te[i], 0, 0)),
                  pl.BlockSpec((1, D_FF, D_MODEL), lambda i, te, nt: (te[i], 0, 0))],
        out_specs=row_block,
        scratch_shapes=[pltpu.VMEM((D_MODEL, D_FF), BF16), pltpu.VMEM((D_MODEL, D_FF), BF16),
                        pltpu.VMEM((D_FF, D_MODEL), BF16)])
    return pl.pallas_call(
        _expert_body,
        grid_spec=grid_spec,
        out_shape=jax.ShapeDtypeStruct(xs.shape, F32),
        compiler_params=_cparams(("arbitrary",)),
        name="moe_experts",
    )(te, nt, xs, w_gate, w_up, w_down)


def _combine_body(pos_ref, route_ref, hlat_ref, g2_ref, fn_ref, ys_ref, out_ref, buf_ref, sem):
    tm = COMBINE_TM

    def start_pair(t, p1, p2):
        _row_copy(ys_ref, p1, buf_ref.at[0], t, sem).start()
        _row_copy(ys_ref, p2, buf_ref.at[1], t, sem).start()

    _issue_rows(tm, pos_ref, start_pair)
    _drain_rows(2 * tm, _row_copy(ys_ref, 0, buf_ref.at[0], 0, sem))

    r = route_ref[0]
    moe = (r[:, RT_W1:RT_W1 + 1] * _load_row_tiles(buf_ref.at[0])
           + r[:, RT_W2:RT_W2 + 1] * _load_row_tiles(buf_ref.at[1]))
    out_ref[0] = _rms(hlat_ref[0] + g2_ref[0] * moe, fn_ref[...])


def _combine(pos, route, hlat, g2, fn_row, ys):
    bn, seq, _ = hlat.shape
    tm = COMBINE_TM
    nj = seq // tm
    nb = tm // LANES
    tok = lambda b, j: (b, j, 0)
    return pl.pallas_call(
        _combine_body,
        grid=(bn, nj),
        in_specs=[pl.BlockSpec((nb, 8, LANES), lambda b, j: (b * nj + j, 0, 0), memory_space=pltpu.SMEM),
                  pl.BlockSpec((1, tm, LANES), tok),
                  pl.BlockSpec((1, tm, D_MODEL), tok),
                  pl.BlockSpec((1, 1, D_MODEL), lambda b, j: (b, 0, 0)),
                  pl.BlockSpec((1, D_MODEL), lambda b, j: (0, 0)),
                  pl.BlockSpec(memory_space=pl.ANY)],
        out_specs=pl.BlockSpec((1, tm, D_MODEL), tok),
        out_shape=jax.ShapeDtypeStruct((bn, seq, D_MODEL), F32),
        scratch_shapes=[pltpu.VMEM((2, tm) + ROW_TILE, F32), pltpu.SemaphoreType.DMA(())],
        compiler_params=_cparams(("arbitrary", "arbitrary")),
        name="moe_combine",
    )(pos, route, hlat, g2, fn_row, ys)


def _prep_weights(w_in, gla_w_up, gla_b, ssd_dt_bias, ssd_a_log, ssd_d, router_group_w, router_group_b,
                  router_expert_w, router_expert_b):
    off_k = GLA_KEY
    off_v = 2 * GLA_KEY
    off_r = off_v + GLA_VAL
    off_g = off_r + GLA_VAL
    off_z = off_g + 2 * GLA_RANK
    off_xbc = off_z + SSD_INNER
    off_dt = off_xbc + SSD_XBC
    w_main = jnp.concatenate([w_in[:, :off_g], w_in[:, off_z:off_dt]], axis=1).astype(BF16)
    blocks = [jnp.pad(w_in[:, off_g:off_z], ((0, 0), (0, SMALL_DT0 * LANES - 2 * GLA_RANK)))]
    for d in range(2):
        for g in range(SSD_GROUPS):
            c0 = off_dt + d * SSD_HEADS + g * SSD_HPG
            blocks.append(jnp.pad(w_in[:, c0:c0 + SSD_HPG], ((0, 0), (0, LANES - SSD_HPG))))
    w_small = jnp.concatenate(blocks, axis=1).astype(BF16)
    wup = jnp.zeros((2, LANES, GLA_KEY), F32)
    wup = wup.at[0, 0:GLA_RANK].set(gla_w_up[0]).at[1, GLA_RANK:2 * GLA_RANK].set(gla_w_up[1]).astype(BF16)
    pad_h = lambda t: jnp.pad(t.reshape(2, SSD_GROUPS, 1, SSD_HPG), ((0, 0), (0, 0), (0, 0), (0, LANES - SSD_HPG)))
    w_router = jnp.zeros((D_MODEL, LANES), F32)
    w_router = w_router.at[:, 0:N_GROUPS].set(router_group_w)
    w_router = w_router.at[:, EXPERT_LANE0:EXPERT_LANE0 + N_EXPERTS].set(router_expert_w).astype(BF16)
    b_router = jnp.zeros((1, LANES), F32).at[0, 0:N_GROUPS].set(router_group_b)
    b_router = b_router.at[0, EXPERT_LANE0:EXPERT_LANE0 + N_EXPERTS].set(router_expert_b)
    return (w_main, w_small, wup, gla_b.reshape(2, 1, GLA_KEY), pad_h(ssd_dt_bias), pad_h(ssd_a_log),
            jnp.repeat(ssd_d, SSD_P)[None, :], w_router, b_router)


def kernel(x, c, ctx, c_ctx, w_ada, b_ada, norm_mix, norm_ffn, w_in, gla_w_up, gla_b, gla_norm, ssd_conv_w,
           ssd_conv_b, ssd_dt_bias, ssd_a_log, ssd_d, ssd_norm, w_out, router_group_w, router_group_b,
           router_expert_w, router_expert_b, expert_w_gate, expert_w_up, expert_w_down, final_norm):
    assert w_ada.shape[0] == 1, "single-layer block"
    bn, seq, d = x.shape
    ctx_len = ctx.shape[1]
    assert seq // GRID_W == GRID_W and d == D_MODEL

    c8 = jnp.concatenate([c, c_ctx[None, :], jnp.zeros((8 - bn - 1, d), F32)], axis=0)
    mods = _adaln(c8, w_ada[0], b_ada[0][None, :])
    sh1, sc1, g1, sh2, sc2, g2 = [mods[:, i * d:(i + 1) * d] for i in range(6)]
    ctx_row = lambda t: jnp.broadcast_to(t[bn:bn + 1], (bn, d))
    mod1 = jnp.stack([ctx_row(sh1), ctx_row(sc1), sh1[:bn], sc1[:bn]], axis=1)
    mod2 = jnp.stack([g1[:bn], sh2[:bn], sc2[:bn], g2[:bn]], axis=1)

    (w_main, w_small, wup, gla_b3, dtb4, alog4, dskip_row, w_router, b_router) = _prep_weights(
        w_in[0], gla_w_up[0], gla_b[0], ssd_dt_bias[0], ssd_a_log[0], ssd_d[0], router_group_w[0],
        router_group_b[0], router_expert_w[0], router_expert_b[0])

    proj, small = _inproj(ctx, x, mod1, norm_mix, w_main, w_small)
    xact = _conv(proj, ssd_conv_w[0].reshape(9, SSD_XBC), ssd_conv_b, ctx_len, seq)
    lvm = jnp.asarray(np.stack([_level_matrix(CHUNK, False), _level_matrix(CHUNK, True)]))
    o_gla = _gla(proj, small, wup, gla_b3, lvm, ctx_len, seq)
    y_ssd = _ssd(xact, small, dtb4, alog4, dskip_row, ctx_len, seq)
    hlat, h2p, route, cnt = _outproj(o_gla, proj, y_ssd, x, mod2, gla_norm, ssd_norm, norm_ffn,
                                     w_out[0].astype(BF16), w_router, b_router, ctx_len)
    n_tok = bn * seq
    pos, te, nt = _plan(route.reshape(n_tok, LANES), cnt)
    xs = _dispatch(pos, h2p.reshape((n_tok,) + ROW_TILE),
                   jnp.zeros((_moe_tiles(n_tok) * MOE_TILE,) + ROW_TILE, F32))
    ys = _experts(te.reshape(-1), nt.reshape(-1), xs, expert_w_gate[0], expert_w_up[0], expert_w_down[0])
    return _combine(pos, route, hlat, mod2[:, 3:4, :], final_norm[None, :], ys)
```

```python
import functools
import math

import numpy as np
import jax
import jax.numpy as jnp
from jax import lax
from jax.experimental import pallas as pl
from jax.experimental.pallas import tpu as pltpu

F32 = jnp.float32
BF16 = jnp.bfloat16

D_MODEL = 1024
GRID_W = 64
EPS = 1e-6

GLA_HEADS = 4
GLA_DK = 128
GLA_DV = 256
GLA_KEY = GLA_HEADS * GLA_DK
GLA_VAL = GLA_HEADS * GLA_DV
GLA_RANK = 16
GLA_GATE_NORM = 16.0

SSD_HEADS = 16
SSD_P = 64
SSD_INNER = SSD_HEADS * SSD_P
SSD_GROUPS = 2
SSD_HPG = SSD_HEADS // SSD_GROUPS
SSD_N = 128
SSD_XBC = SSD_INNER + 2 * SSD_GROUPS * SSD_N
SSD_GW = SSD_HPG * SSD_P

N_GROUPS = 4
EXPERTS_PER_GROUP = 8
N_EXPERTS = N_GROUPS * EXPERTS_PER_GROUP
D_FF = 512

LANES = 128
VMEM_LIMIT = 56 * 1024 * 1024

PC_Q = 0
PC_K = PC_Q + GLA_KEY
PC_V = PC_K + GLA_KEY
PC_R = PC_V + GLA_VAL
PC_Z = PC_R + GLA_VAL
PC_XBC = PC_Z + SSD_INNER
PROJ_W = PC_XBC + SSD_XBC
SMALL_DT0 = 2 * GLA_RANK
SMALL_W = LANES

CHUNK = 128
EXPERT_LANE0 = 32


def _cparams(sem):
    return pltpu.CompilerParams(dimension_semantics=sem, vmem_limit_bytes=VMEM_LIMIT)


def _silu(x):
    return x / (1.0 + jnp.exp(-x))


def _softplus(x):
    return jnp.maximum(x, 0.0) + jnp.log(1.0 + jnp.exp(-jnp.abs(x)))


def _log_sigmoid(x):
    return jnp.minimum(x, 0.0) - jnp.log(1.0 + jnp.exp(-jnp.abs(x)))


def _bdot(a, b):
    return jnp.dot(a.astype(BF16), b.astype(BF16), preferred_element_type=F32)


def _bdot_nt(a, b):
    return lax.dot_general(a.astype(BF16), b.astype(BF16), (((1,), (1,)), ((), ())),
                           preferred_element_type=F32)


def _adaln_body(c_ref, w_ref, b_ref, o_ref):
    o_ref[...] = _bdot(_silu(c_ref[...]), w_ref[...]) + b_ref[...]


def _adaln(c8, w, b):
    n = w.shape[1]
    bn = 1024
    return pl.pallas_call(
        _adaln_body,
        grid=(n // bn,),
        in_specs=[pl.BlockSpec((8, D_MODEL), lambda j: (0, 0)),
                  pl.BlockSpec((D_MODEL, bn), lambda j: (0, j)),
                  pl.BlockSpec((1, bn), lambda j: (0, j))],
        out_specs=pl.BlockSpec((8, bn), lambda j: (0, j)),
        out_shape=jax.ShapeDtypeStruct((8, n), F32),
        compiler_params=_cparams(("arbitrary",)),
        name="adaln",
    )(c8, w, b)


INPROJ_TM = 256
INPROJ_NC = 512


def _inproj_body(ctx_ref, x_ref, mod_ref, nw_ref, w_ref, ws_ref, proj_ref, small_ref):
    is_ctx = pl.program_id(1) == 0
    xin = jnp.where(is_ctx, ctx_ref[0], x_ref[0])
    m = mod_ref[0]
    shift = jnp.where(is_ctx, m[0:1], m[2:3])
    scale = jnp.where(is_ctx, m[1:2], m[3:4])
    ms = jnp.mean(xin * xin, axis=-1, keepdims=True)
    h = xin * lax.rsqrt(ms + EPS) * nw_ref[...]
    hb = (h * (1.0 + scale) + shift).astype(BF16)
    for n in range(PROJ_W // INPROJ_NC):
        sl = slice(n * INPROJ_NC, (n + 1) * INPROJ_NC)
        proj_ref[0, :, sl] = jnp.dot(hb, w_ref[:, sl], preferred_element_type=F32).astype(BF16)
    small_ref[0] = jnp.dot(hb, ws_ref[...], preferred_element_type=F32)


def _inproj(ctx, x, mod1, norm_w, w_main, w_small):
    bn, seq, _ = x.shape
    ctx_len = ctx.shape[1]
    tm = INPROJ_TM
    assert ctx_len == tm and seq % tm == 0
    tt = ctx_len + seq
    nj = tt // tm
    return pl.pallas_call(
        _inproj_body,
        grid=(bn, nj),
        in_specs=[pl.BlockSpec((1, tm, D_MODEL), lambda b, j: (b, 0, 0)),
                  pl.BlockSpec((1, tm, D_MODEL), lambda b, j: (b, jnp.maximum(j - 1, 0), 0)),
                  pl.BlockSpec((1, 4, D_MODEL), lambda b, j: (b, 0, 0)),
                  pl.BlockSpec((1, D_MODEL), lambda b, j: (0, 0)),
                  pl.BlockSpec((D_MODEL, PROJ_W), lambda b, j: (0, 0)),
                  pl.BlockSpec((D_MODEL, SMALL_W), lambda b, j: (0, 0))],
        out_specs=[pl.BlockSpec((1, tm, PROJ_W), lambda b, j: (b, j, 0)),
                   pl.BlockSpec((1, tm, SMALL_W), lambda b, j: (b, j, 0))],
        out_shape=[jax.ShapeDtypeStruct((bn, tt, PROJ_W), BF16),
                   jax.ShapeDtypeStruct((bn, tt, SMALL_W), F32)],
        compiler_params=_cparams(("arbitrary", "arbitrary")),
        name="inproj",
    )(ctx, x, mod1, norm_w, w_main, w_small)


CONV_CB = 256
CONV_TT = 256


def _conv_body(ctx_len, seq, p_ref, w_ref, b_ref, o_ref, s_ref):
    lat0 = ctx_len + GRID_W
    tt = CONV_TT
    n_tiles = seq // tt
    zeros = jnp.zeros((GRID_W, CONV_CB), F32)
    s_ref[0:ctx_len, :] = p_ref[0, 0:ctx_len, :].astype(F32)
    s_ref[ctx_len:lat0, :] = zeros
    s_ref[lat0 + seq:lat0 + seq + GRID_W, :] = zeros

    def copy_tile(t, c):
        src = pl.multiple_of(ctx_len + t * tt, tt)
        dst = pl.multiple_of(lat0 + t * tt, GRID_W)
        s_ref[pl.ds(dst, tt), :] = p_ref[0, pl.ds(src, tt), :].astype(F32)
        return c

    lax.fori_loop(0, n_tiles, copy_tile, 0)

    w = w_ref[...]
    bias = b_ref[...]
    pos = lax.broadcasted_iota(jnp.int32, (tt, 1), 0)

    def conv_rows(rows, first, last):
        left, mid, right = [sum(xr * w[3 * kh + kw:3 * kh + kw + 1] for xr, kh in rows) for kw in range(3)]
        return (bias + mid + jnp.where(first, 0.0, pltpu.roll(left, 1, 0))
                + jnp.where(last, 0.0, pltpu.roll(right, tt - 1, 0)))

    acc = conv_rows([(s_ref[0:ctx_len, :], 1)], pos == 0, pos == ctx_len - 1)
    o_ref[0, 0:ctx_len, :] = _silu(acc).astype(BF16)

    col = pos % GRID_W
    first = col == 0
    last = col == GRID_W - 1

    def tile(t, c):
        base = lat0 + t * tt
        rows = [(s_ref[pl.ds(pl.multiple_of(base + (kh - 1) * GRID_W, GRID_W), tt), :], kh) for kh in range(3)]
        dst = pl.multiple_of(ctx_len + t * tt, tt)
        o_ref[0, pl.ds(dst, tt), :] = _silu(conv_rows(rows, first, last)).astype(BF16)
        return c

    lax.fori_loop(0, n_tiles, tile, 0)


def _conv(proj, conv_w9, conv_b, ctx_len, seq):
    bn, tt, _ = proj.shape
    assert ctx_len == CONV_TT and seq % CONV_TT == 0 and CONV_TT % GRID_W == 0
    ncb = SSD_XBC // CONV_CB
    cb0 = PC_XBC // CONV_CB
    return pl.pallas_call(
        functools.partial(_conv_body, ctx_len, seq),
        grid=(bn, ncb),
        in_specs=[pl.BlockSpec((1, tt, CONV_CB), lambda b, j: (b, 0, cb0 + j)),
                  pl.BlockSpec((9, CONV_CB), lambda b, j: (0, j)),
                  pl.BlockSpec((1, CONV_CB), lambda b, j: (0, j))],
        out_specs=pl.BlockSpec((1, tt, CONV_CB), lambda b, j: (b, 0, j)),
        out_shape=jax.ShapeDtypeStruct((bn, tt, SSD_XBC), BF16),
        scratch_shapes=[pltpu.VMEM((tt + 2 * GRID_W, CONV_CB), F32)],
        compiler_params=_cparams(("arbitrary", "arbitrary")),
        name="conv",
    )(proj, conv_w9, conv_b)


def _bwd_chunk(s, n_ctx, n_steps):
    return jnp.where(s < n_ctx, n_ctx - 1 - s, n_steps + n_ctx - 1 - s)


def _out_row(c, n_ctx, n, seq):
    return pl.multiple_of(jnp.where(c >= n_ctx, (c - n_ctx) * n, seq), n)


def _scan_sum(a, rev):
    n = a.shape[0]
    row = lax.broadcasted_iota(jnp.int32, (n, 1), 0)
    sh = 1
    while sh < n:
        if rev:
            a = a + jnp.where(row < n - sh, pltpu.roll(a, n - sh, 0), 0.0)
        else:
            a = a + jnp.where(row >= sh, pltpu.roll(a, sh, 0), 0.0)
        sh *= 2
    return a


def _level_matrix(n, rev):
    i = np.arange(n)[:, None]
    j = np.arange(n)[None, :]
    x = i ^ j
    lv = np.where(x > 0, np.floor(np.log2(np.maximum(x, 1))).astype(np.int64) + 1, 0)
    earlier = (j > i) if rev else (j < i)
    return np.where(earlier, lv, 0).astype(np.int32)


GLA_HPS = 4


def _gla_dir(rev, dst, q_ref, k_ref, v_ref, sm_ref, wup, gb, lv, o_ref, s_ref):
    n = CHUNK
    q = q_ref[0].astype(F32) * (GLA_DK ** -0.5)
    k = k_ref[0].astype(F32)
    x = jnp.dot(sm_ref[0].astype(BF16), wup, preferred_element_type=F32) + gb
    b = _scan_sum(_log_sigmoid(x) / GLA_GATE_NORM, rev)
    row = lax.broadcasted_iota(jnp.int32, (n, 1), 0)
    b_end = b[0:1] if rev else b[n - 1:n]
    k_end = k * jnp.exp(b_end - b)
    q_in = q * jnp.exp(b)
    dec_end = jnp.exp(b_end)

    att = [jnp.zeros((n, n), F32) for _ in range(GLA_HPS)]
    e_h = b
    h = 1
    level = 1
    while h < n:
        upper = (row & h) != 0
        if rev:
            e = jnp.where(upper, e_h - b, b - pltpu.roll(e_h, n - h, 0))
        else:
            e = jnp.where(upper, b - pltpu.roll(e_h, h, 0), e_h - b)
        w = jnp.exp(e)
        qw = (q * w).astype(BF16)
        kw = (k * w).astype(BF16)
        mask = lv == level
        for hh in range(GLA_HPS):
            ks = slice(hh * GLA_DK, (hh + 1) * GLA_DK)
            att[hh] = att[hh] + jnp.where(mask, _bdot_nt(qw[:, ks], kw[:, ks]), 0.0)
        if 2 * h < n:
            if rev:
                e_h = jnp.where(upper, pltpu.roll(e_h, h, 0), e_h)
            else:
                e_h = jnp.where(upper, e_h, pltpu.roll(e_h, n - h, 0))
        h *= 2
        level += 1

    for hh in range(GLA_HPS):
        ks = slice(hh * GLA_DK, (hh + 1) * GLA_DK)
        vs = slice(hh * GLA_DV, (hh + 1) * GLA_DV)
        v = v_ref[0, :, vs]
        s_old = s_ref[hh]
        o = jnp.dot(att[hh].astype(BF16), v, preferred_element_type=F32)
        o = o + jnp.sum(q[:, ks] * k[:, ks], axis=-1, keepdims=True) * v.astype(F32)
        o = o + _bdot(q_in[:, ks], s_old)
        o_ref[0, pl.ds(dst, n), vs] += o
        dec_col = jnp.broadcast_to(dec_end[:, ks], (n, GLA_DK)).T[:, 0:1]
        s_ref[hh] = dec_col * s_old + jnp.dot(k_end[:, ks].T.astype(BF16), v, preferred_element_type=F32)


def _gla_body(n_ctx, seq, qf, kf, vf, smf, qb, kb, vb, smb, wup_ref, gb_ref, lv_ref, o_ref, s_ref):
    step = pl.program_id(2)
    n = CHUNK

    @pl.when(step == 0)
    def _():
        s_ref[...] = jnp.zeros_like(s_ref)
        o_ref[...] = jnp.zeros_like(o_ref)

    cb = _bwd_chunk(step, n_ctx, pl.num_programs(2))
    _gla_dir(False, _out_row(step, n_ctx, n, seq), qf, kf, vf, smf, wup_ref[0], gb_ref[0], lv_ref[0],
             o_ref, s_ref.at[0])
    _gla_dir(True, _out_row(cb, n_ctx, n, seq), qb, kb, vb, smb, wup_ref[1], gb_ref[1], lv_ref[1],
             o_ref, s_ref.at[1])


def _gla(proj, small, wup_pad, gla_b3, lvm, ctx_len, seq):
    bn, tt, _ = proj.shape
    n = CHUNK
    assert ctx_len % n == 0 and seq % n == 0
    n_ctx = ctx_len // n
    n_steps = tt // n
    hw = GLA_HPS * GLA_DK
    vw = GLA_HPS * GLA_DV
    kb = PC_K // hw
    vb = PC_V // vw
    fwd = lambda s: s
    bwd = lambda s: _bwd_chunk(s, n_ctx, n_steps)

    def chunk_specs(c):
        return [pl.BlockSpec((1, n, hw), lambda b, h, s: (b, c(s), h)),
                pl.BlockSpec((1, n, hw), lambda b, h, s: (b, c(s), kb + h)),
                pl.BlockSpec((1, n, vw), lambda b, h, s: (b, c(s), vb + h)),
                pl.BlockSpec((1, n, LANES), lambda b, h, s: (b, c(s), 0))]

    return pl.pallas_call(
        functools.partial(_gla_body, n_ctx, seq),
        grid=(bn, GLA_HEADS // GLA_HPS, n_steps),
        in_specs=chunk_specs(fwd) + chunk_specs(bwd) + [
            pl.BlockSpec((2, LANES, hw), lambda b, h, s: (0, 0, h)),
            pl.BlockSpec((2, 1, hw), lambda b, h, s: (0, 0, h)),
            pl.BlockSpec((2, n, n), lambda b, h, s: (0, 0, 0))],
        out_specs=pl.BlockSpec((1, seq + n, vw), lambda b, h, s: (b, 0, h)),
        out_shape=jax.ShapeDtypeStruct((bn, seq + n, GLA_VAL), F32),
        scratch_shapes=[pltpu.VMEM((2, GLA_HPS, GLA_DK, GLA_DV), F32)],
        compiler_params=_cparams(("arbitrary",) * 3),
        name="gla",
    )(proj, proj, proj, small, proj, proj, proj, small, wup_pad, gla_b3, lvm)


def _ssd_dir(rev, first, g, dst, x_ref, bm_ref, cm_ref, sm_ref, dtb, alog, dsk_ref, y_ref, s_ref):
    n = CHUNK
    gl = slice(g * LANES, (g + 1) * LANES)
    g0 = g * SSD_GW
    lane = lax.broadcasted_iota(jnp.int32, (1, LANES), 1)
    dt_lane0 = SMALL_DT0 + (SSD_HEADS if rev else 0) + g * SSD_HPG
    dt = _softplus(pltpu.roll(sm_ref[0], LANES - dt_lane0, 1) + dtb)
    a_neg = jnp.where(lane < SSD_HPG, -jnp.exp(alog), 0.0)
    acum = _scan_sum(dt * a_neg, rev)
    acum_t = acum.T
    x = x_ref[0, :, g0:g0 + SSD_GW].astype(F32)
    cm = cm_ref[0, :, gl]
    bm = bm_ref[0, :, gl]
    s_old = s_ref[...]
    ii = lax.broadcasted_iota(jnp.int32, (n, n), 0)
    jj = lax.broadcasted_iota(jnp.int32, (n, n), 1)
    causal = (jj >= ii) if rev else (jj <= ii)
    cb = _bdot_nt(cm, bm)
    y_off = jnp.dot(cm, s_old.astype(BF16), preferred_element_type=F32)
    low = lane < SSD_P
    xs_parts = []
    send_parts = []
    for p in range(SSD_HPG // 2):
        slab = slice(p * LANES, (p + 1) * LANES)
        heads = []
        for hh in (2 * p, 2 * p + 1):
            colb = jnp.broadcast_to(acum[:, hh:hh + 1], (n, LANES))
            dtb_h = jnp.broadcast_to(dt[:, hh:hh + 1], (n, LANES))
            seg = jnp.exp(jnp.where(causal, colb - acum_t[hh:hh + 1, :], -jnp.inf))
            end = colb[0:1] if rev else colb[n - 1:n]
            heads.append(((cb * seg).astype(BF16), dtb_h, jnp.exp(colb), jnp.exp(end - colb), jnp.exp(end)))
        (m0, dt0, ec0, dc0, se0), (m1, dt1, ec1, dc1, se1) = heads
        xdt = x[:, slab] * jnp.where(low, dt0, dt1)
        rhs = jnp.concatenate([jnp.where(low, xdt, 0.0), jnp.where(low, 0.0, xdt)], axis=0).astype(BF16)
        y = jnp.dot(jnp.concatenate([m0, m1], axis=1), rhs, preferred_element_type=F32)
        y = y + y_off[:, slab] * jnp.where(low, ec0, ec1)
        out = slice(g0 + p * LANES, g0 + (p + 1) * LANES)
        if first:
            y = y + x[:, slab] * dsk_ref[:, out]
        y_ref[0, pl.ds(dst, n), out] += y
        xs_parts.append((xdt * jnp.where(low, dc0, dc1)).astype(BF16))
        send_parts.append(jnp.where(low, se0, se1))
    new = jnp.dot(bm.astype(F32).T.astype(BF16), jnp.concatenate(xs_parts, axis=1),
                  preferred_element_type=F32)
    s_ref[...] = s_old * jnp.concatenate(send_parts, axis=1) + new


def _ssd_body(n_ctx, seq, xf, bf, cf, smf, xb, bb, cbk, smb, dtb_ref, alog_ref, dsk_ref, y_ref, s_ref):
    step = pl.program_id(1)
    n = CHUNK

    @pl.when(step == 0)
    def _():
        s_ref[...] = jnp.zeros_like(s_ref)
        y_ref[...] = jnp.zeros_like(y_ref)

    dst_f = _out_row(step, n_ctx, n, seq)
    dst_b = _out_row(_bwd_chunk(step, n_ctx, pl.num_programs(1)), n_ctx, n, seq)
    for g in range(SSD_GROUPS):
        _ssd_dir(False, True, g, dst_f, xf, bf, cf, smf, dtb_ref[0, g], alog_ref[0, g], dsk_ref, y_ref,
                 s_ref.at[0, g])
        _ssd_dir(True, False, g, dst_b, xb, bb, cbk, smb, dtb_ref[1, g], alog_ref[1, g], dsk_ref, y_ref,
                 s_ref.at[1, g])


def _ssd(xact, small, dtb4, alog4, dskip_row, ctx_len, seq):
    bn, tt, _ = xact.shape
    n = CHUNK
    n_ctx = ctx_len // n
    n_steps = tt // n
    gn = SSD_GROUPS * SSD_N
    assert SSD_INNER % gn == 0
    bblk = SSD_INNER // gn
    fwd = lambda s: s
    bwd = lambda s: _bwd_chunk(s, n_ctx, n_steps)

    def chunk_specs(c):
        return [pl.BlockSpec((1, n, SSD_INNER), lambda b, s: (b, c(s), 0)),
                pl.BlockSpec((1, n, gn), lambda b, s: (b, c(s), bblk)),
                pl.BlockSpec((1, n, gn), lambda b, s: (b, c(s), bblk + 1)),
                pl.BlockSpec((1, n, SMALL_W), lambda b, s: (b, c(s), 0))]

    whole4 = pl.BlockSpec((2, SSD_GROUPS, 1, LANES), lambda b, s: (0, 0, 0, 0))
    return pl.pallas_call(
        functools.partial(_ssd_body, n_ctx, seq),
        grid=(bn, n_steps),
        in_specs=chunk_specs(fwd) + chunk_specs(bwd) + [
            whole4, whole4, pl.BlockSpec((1, SSD_INNER), lambda b, s: (0, 0))],
        out_specs=pl.BlockSpec((1, seq + n, SSD_INNER), lambda b, s: (b, 0, 0)),
        out_shape=jax.ShapeDtypeStruct((bn, seq + n, SSD_INNER), F32),
        scratch_shapes=[pltpu.VMEM((2, SSD_GROUPS, SSD_N, SSD_GW), F32)],
        compiler_params=_cparams(("arbitrary",) * 2),
        name="ssd",
    )(xact, xact, xact, small, xact, xact, xact, small, dtb4, alog4, dskip_row)


OUT_TM = 256
RT_E1, RT_E2, RT_RANK1, RT_RANK2, RT_W1, RT_W2 = range(6)


def _rms(x, w):
    return x * lax.rsqrt(jnp.mean(x * x, axis=-1, keepdims=True) + EPS) * w


ROW_TILE = (D_MODEL // LANES, LANES)


def _store_row_tiles(ref, x):
    ref[...] = x.reshape((x.shape[0],) + ROW_TILE)


def _load_row_tiles(ref):
    return ref[...].reshape(ref.shape[0], D_MODEL)


def _outproj_body(o_ref, r_ref, y_ref, z_ref, x_ref, mod_ref, gn_ref, sn_ref, fn_ref, wo_ref, wr_ref, br_ref,
                  hlat_ref, h2_ref, route_ref, cnt_ref, carry_ref):
    tm = OUT_TM

    @pl.when((pl.program_id(0) == 0) & (pl.program_id(1) == 0))
    def _():
        carry_ref[...] = jnp.zeros_like(carry_ref)

    parts = []
    for h in range(GLA_HEADS):
        hs = slice(h * GLA_DV, (h + 1) * GLA_DV)
        parts.append((_rms(o_ref[0, :, hs], gn_ref[...]) * _silu(r_ref[0, :, hs].astype(F32))).astype(BF16))
    for g in range(SSD_GROUPS):
        gs = slice(g * SSD_GW, (g + 1) * SSD_GW)
        yg = y_ref[0, :, gs] * _silu(z_ref[0, :, gs].astype(F32))
        parts.append(_rms(yg, sn_ref[:, gs]).astype(BF16))
    mix = jnp.concatenate(parts, axis=-1)
    m = mod_ref[0]
    hlat = x_ref[0] + m[0:1] * jnp.dot(mix, wo_ref[...], preferred_element_type=F32)
    hlat_ref[0] = hlat
    h2f = _rms(hlat, fn_ref[...]) * (1.0 + m[2:3]) + m[1:2]
    _store_row_tiles(h2_ref.at[0], h2f)

    lg = jnp.dot(h2f.astype(BF16), wr_ref[...], preferred_element_type=F32) + br_ref[...]
    lanef = lax.broadcasted_iota(jnp.int32, (tm, LANES), 1).astype(F32)
    ninf = -jnp.inf
    is_g = lanef < float(N_GROUPS)
    gl = jnp.where(is_g, lg, ninf)
    gmax = jnp.max(gl, axis=-1, keepdims=True)
    gsel = jnp.min(jnp.where(gl == gmax, lanef, float(LANES)), axis=-1, keepdims=True)
    pg = 1.0 / jnp.sum(jnp.where(is_g, jnp.exp(lg - gmax), 0.0), axis=-1, keepdims=True)
    lo = float(EXPERT_LANE0) + float(EXPERTS_PER_GROUP) * gsel
    in_grp = (lanef >= lo) & (lanef < lo + float(EXPERTS_PER_GROUP))
    el = jnp.where(in_grp, lg, ninf)
    v1 = jnp.max(el, axis=-1, keepdims=True)
    i1 = jnp.min(jnp.where(el == v1, lanef, float(LANES)), axis=-1, keepdims=True)
    el2 = jnp.where(lanef == i1, ninf, el)
    v2 = jnp.max(el2, axis=-1, keepdims=True)
    i2 = jnp.min(jnp.where(el2 == v2, lanef, float(LANES)), axis=-1, keepdims=True)
    t = jnp.exp(v2 - v1)
    w1 = pg / (1.0 + t)
    w2 = pg * t / (1.0 + t)

    sel1 = lanef == i1
    sel2 = lanef == i2
    member = jnp.where(sel1 | sel2, 1.0, 0.0)
    ii = lax.broadcasted_iota(jnp.int32, (tm, tm), 0)
    jj = lax.broadcasted_iota(jnp.int32, (tm, tm), 1)
    before = jnp.where(jj < ii, 1.0, 0.0).astype(BF16)
    ranks = jnp.dot(before, member.astype(BF16), preferred_element_type=F32) + carry_ref[...]
    rank1 = jnp.sum(jnp.where(sel1, ranks, 0.0), axis=-1, keepdims=True)
    rank2 = jnp.sum(jnp.where(sel2, ranks, 0.0), axis=-1, keepdims=True)
    carry = carry_ref[...] + jnp.sum(member, axis=0, keepdims=True)
    carry_ref[...] = carry
    cnt_ref[...] = carry
    rec = jnp.zeros((tm, LANES), F32)
    for lane_id, val in ((RT_E1, i1 - float(EXPERT_LANE0)), (RT_E2, i2 - float(EXPERT_LANE0)),
                         (RT_RANK1, rank1), (RT_RANK2, rank2), (RT_W1, w1), (RT_W2, w2)):
        rec = jnp.where(lanef == float(lane_id), val, rec)
    route_ref[0] = rec


def _outproj(o_gla, proj, y_ssd, x, mod2, gn_row, sn_row, fn_row, w_out, w_router, b_router, ctx_len):
    bn, seq, _ = x.shape
    tm = OUT_TM
    assert ctx_len % tm == 0 and seq % tm == 0
    j0 = ctx_len // tm
    rb = PC_R // GLA_VAL
    zb = PC_Z // SSD_INNER
    tok = lambda b, j: (b, j, 0)
    const = lambda b, j: (0, 0)
    return pl.pallas_call(
        _outproj_body,
        grid=(bn, seq // tm),
        in_specs=[pl.BlockSpec((1, tm, GLA_VAL), tok),
                  pl.BlockSpec((1, tm, GLA_VAL), lambda b, j: (b, j0 + j, rb)),
                  pl.BlockSpec((1, tm, SSD_INNER), tok),
                  pl.BlockSpec((1, tm, SSD_INNER), lambda b, j: (b, j0 + j, zb)),
                  pl.BlockSpec((1, tm, D_MODEL), tok),
                  pl.BlockSpec((1, 4, D_MODEL), lambda b, j: (b, 0, 0)),
                  pl.BlockSpec((1, GLA_DV), const),
                  pl.BlockSpec((1, SSD_INNER), const),
                  pl.BlockSpec((1, D_MODEL), const),
                  pl.BlockSpec((GLA_VAL + SSD_INNER, D_MODEL), const),
                  pl.BlockSpec((D_MODEL, LANES), const),
                  pl.BlockSpec((1, LANES), const)],
        out_specs=[pl.BlockSpec((1, tm, D_MODEL), tok),
                   pl.BlockSpec((1, tm) + ROW_TILE, lambda b, j: (b, j, 0, 0)),
                   pl.BlockSpec((1, tm, LANES), tok),
                   pl.BlockSpec((1, LANES), const)],
        out_shape=[jax.ShapeDtypeStruct((bn, seq, D_MODEL), F32),
                   jax.ShapeDtypeStruct((bn, seq) + ROW_TILE, F32),
                   jax.ShapeDtypeStruct((bn, seq, LANES), F32),
                   jax.ShapeDtypeStruct((1, LANES), F32)],
        scratch_shapes=[pltpu.VMEM((1, LANES), F32)],
        compiler_params=_cparams(("arbitrary", "arbitrary")),
        name="outproj",
    )(o_gla, proj, y_ssd, proj, x, mod2, gn_row, sn_row, fn_row, w_out, w_router, b_router)


MOE_TILE = 256
PLAN_TM = 1024
DISPATCH_TM = 1024
COMBINE_TM = 256


def _moe_tiles(n_tokens):
    return (2 * n_tokens) // MOE_TILE + N_EXPERTS


def _plan_body(nt_pad, route_ref, cnt_ref, pos_ref, te_ref, nt_ref, pad0_ref, padn_ref):
    lane = lax.broadcasted_iota(jnp.int32, (1, LANES), 1).astype(F32)
    is_e = (lane >= float(EXPERT_LANE0)) & (lane < float(EXPERT_LANE0 + N_EXPERTS))
    tiles = jnp.where(is_e, jnp.floor((cnt_ref[...] + float(MOE_TILE - 1)) / float(MOE_TILE)), 0.0)
    ii = lax.broadcasted_iota(jnp.int32, (LANES, LANES), 0)
    jj = lax.broadcasted_iota(jnp.int32, (LANES, LANES), 1)
    lower = jnp.where(ii < jj, 1.0, 0.0).astype(BF16)
    first_tile = jnp.dot(jnp.broadcast_to(tiles, (8, LANES)).astype(BF16), lower,
                         preferred_element_type=F32)[0:1]
    off_row = first_tile * float(MOE_TILE)
    for sb in range(PLAN_TM // LANES):
        r = route_ref[sb * LANES:(sb + 1) * LANES, :]
        o1 = jnp.sum(jnp.where(lane == r[:, RT_E1:RT_E1 + 1] + float(EXPERT_LANE0), off_row, 0.0),
                     axis=-1, keepdims=True)
        o2 = jnp.sum(jnp.where(lane == r[:, RT_E2:RT_E2 + 1] + float(EXPERT_LANE0), off_row, 0.0),
                     axis=-1, keepdims=True)
        p = jnp.where(lane == 0.0, o1 + r[:, RT_RANK1:RT_RANK1 + 1],
                      jnp.where(lane == 1.0, o2 + r[:, RT_RANK2:RT_RANK2 + 1], 0.0))
        pos_ref[sb] = p.T[0:8, :].astype(jnp.int32)

    @pl.when(pl.program_id(0) == 0)
    def _():
        ti = lax.broadcasted_iota(jnp.int32, (nt_pad, 1), 0).astype(F32)
        done = jnp.where(is_e & (first_tile + tiles <= ti), 1.0, 0.0)
        te = jnp.minimum(jnp.sum(done, axis=-1, keepdims=True), float(N_EXPERTS - 1))
        te_ref[...] = te.astype(jnp.int32)
        nt_ref[...] = jnp.sum(tiles, axis=-1, keepdims=True).astype(jnp.int32)
        used = jnp.where(is_e, cnt_ref[...], 0.0)
        pad0_ref[...] = (off_row + used).astype(jnp.int32)
        padn_ref[...] = (tiles * float(MOE_TILE) - used).astype(jnp.int32)


def _plan(route, cnt):
    n_tok = route.shape[0]
    nt_pad = _moe_tiles(n_tok)
    nsb = PLAN_TM // LANES
    return pl.pallas_call(
        functools.partial(_plan_body, nt_pad),
        grid=(n_tok // PLAN_TM,),
        in_specs=[pl.BlockSpec((PLAN_TM, LANES), lambda i: (i, 0)),
                  pl.BlockSpec((1, LANES), lambda i: (0, 0))],
        out_specs=[pl.BlockSpec((nsb, 8, LANES), lambda i: (i, 0, 0)),
                   pl.BlockSpec((nt_pad, 1), lambda i: (0, 0)),
                   pl.BlockSpec((1, 1), lambda i: (0, 0)),
                   pl.BlockSpec((1, LANES), lambda i: (0, 0)),
                   pl.BlockSpec((1, LANES), lambda i: (0, 0))],
        out_shape=[jax.ShapeDtypeStruct((n_tok // LANES, 8, LANES), jnp.int32),
                   jax.ShapeDtypeStruct((nt_pad, 1), jnp.int32),
                   jax.ShapeDtypeStruct((1, 1), jnp.int32),
                   jax.ShapeDtypeStruct((1, LANES), jnp.int32),
                   jax.ShapeDtypeStruct((1, LANES), jnp.int32)],
        compiler_params=_cparams(("arbitrary",)),
        name="moe_plan",
    )(route, cnt)


def _row_copy(src, s_row, dst, d_row, sem):
    return pltpu.make_async_copy(src.at[s_row], dst.at[d_row], sem)


def _issue_rows(n_rows, pos_ref, start_pair):
    for blk in range(n_rows // LANES):
        def issue(l, c, blk=blk):
            start_pair(blk * LANES + l, pos_ref[blk, 0, l], pos_ref[blk, 1, l])
            return c
        lax.fori_loop(0, LANES, issue, 0, unroll=8)


def _drain_rows(n_copies, copy):
    def drain(t, c):
        copy.wait()
        return c
    lax.fori_loop(0, n_copies, drain, 0, unroll=8)


def _dispatch_body(pad0_ref, padn_ref, nt_ref, pos_ref, h2_ref, xs_ref, zero_ref, sem):
    def start_pair(t, p1, p2):
        _row_copy(h2_ref, t, xs_ref, p1, sem).start()
        _row_copy(h2_ref, t, xs_ref, p2, sem).start()

    _issue_rows(DISPATCH_TM, pos_ref, start_pair)
    _drain_rows(2 * DISPATCH_TM, _row_copy(h2_ref, 0, xs_ref, 0, sem))

    @pl.when(pl.program_id(0) == 0)
    def _():
        zero_ref[...] = jnp.zeros_like(zero_ref)
        n_tiles = xs_ref.shape[0] // MOE_TILE

        def tile_copy(i):
            return pltpu.make_async_copy(zero_ref, xs_ref.at[pl.ds(i * MOE_TILE, MOE_TILE)], sem)

        def fill_tile(i, c):
            tile_copy(i).start()
            return c

        def drain_tile(i, c):
            tile_copy(i).wait()
            return c

        lax.fori_loop(nt_ref[0], n_tiles, fill_tile, 0)
        lax.fori_loop(nt_ref[0], n_tiles, drain_tile, 0)
        for e in range(N_EXPERTS):
            row0 = pad0_ref[EXPERT_LANE0 + e]
            n_pad = padn_ref[EXPERT_LANE0 + e]

            def fill(r, c, row0=row0):
                _row_copy(zero_ref, 0, xs_ref, row0 + r, sem).start()
                return c

            def drain(r, c):
                _row_copy(zero_ref, 0, xs_ref, 0, sem).wait()
                return c

            lax.fori_loop(0, n_pad, fill, 0)
            lax.fori_loop(0, n_pad, drain, 0)


def _dispatch(pad0, padn, nt, pos, h2t, n_rows):
    n_tok = h2t.shape[0]
    nb = DISPATCH_TM // LANES
    grid_spec = pltpu.PrefetchScalarGridSpec(
        num_scalar_prefetch=3,
        grid=(n_tok // DISPATCH_TM,),
        in_specs=[pl.BlockSpec((nb, 8, LANES), lambda i, *_: (i, 0, 0), memory_space=pltpu.SMEM),
                  pl.BlockSpec((DISPATCH_TM,) + ROW_TILE, lambda i, *_: (i, 0, 0))],
        out_specs=pl.BlockSpec(memory_space=pl.ANY),
        scratch_shapes=[pltpu.VMEM((MOE_TILE,) + ROW_TILE, F32), pltpu.SemaphoreType.DMA(())])
    return pl.pallas_call(
        _dispatch_body,
        grid_spec=grid_spec,
        out_shape=jax.ShapeDtypeStruct((n_rows,) + ROW_TILE, F32),
        compiler_params=_cparams(("arbitrary",)),
        name="moe_dispatch",
    )(pad0, padn, nt, pos, h2t)


def _expert_body(te_ref, nt_ref, xs_ref, wg_ref, wu_ref, wd_ref, ys_ref, wgb_ref, wub_ref, wdb_ref):
    i = pl.program_id(0)
    new_expert = (i == 0) | (te_ref[i] != te_ref[jnp.maximum(i - 1, 0)])

    @pl.when(new_expert)
    def _():
        wgb_ref[...] = wg_ref[0].astype(BF16)
        wub_ref[...] = wu_ref[0].astype(BF16)
        wdb_ref[...] = wd_ref[0].astype(BF16)

    @pl.when(i < nt_ref[0])
    def _():
        x = _load_row_tiles(xs_ref).astype(BF16)
        gate = jnp.dot(x, wgb_ref[...], preferred_element_type=F32)
        up = jnp.dot(x, wub_ref[...], preferred_element_type=F32)
        y = jnp.dot((_silu(gate) * up).astype(BF16), wdb_ref[...], preferred_element_type=F32)
        _store_row_tiles(ys_ref, y)

    @pl.when(i >= nt_ref[0])
    def _():
        ys_ref[...] = jnp.zeros_like(ys_ref)


def _experts(te, nt, xs, w_gate, w_up, w_down):
    rows = xs.shape[0]
    n_tiles = rows // MOE_TILE
    row_block = pl.BlockSpec((MOE_TILE,) + ROW_TILE, lambda i, te, nt: (i, 0, 0))
    grid_spec = pltpu.PrefetchScalarGridSpec(
        num_scalar_prefetch=2,
        grid=(n_tiles,),
        in_specs=[row_block,
                  pl.BlockSpec((1, D_MODEL, D_FF), lambda i, te, nt: (te[i], 0, 0)),
                  pl.BlockSpec((1, D_MODEL, D_FF), lambda i, te, nt: (te[i], 0, 0)),
                  pl.BlockSpec((1, D_FF, D_MODEL), lambda i, te, nt: (te[i], 0, 0))],
        out_specs=row_block,
        scratch_shapes=[pltpu.VMEM((D_MODEL, D_FF), BF16), pltpu.VMEM((D_MODEL, D_FF), BF16),
                        pltpu.VMEM((D_FF, D_MODEL), BF16)])
    return pl.pallas_call(
        _expert_body,
        grid_spec=grid_spec,
        out_shape=jax.ShapeDtypeStruct(xs.shape, F32),
        compiler_params=_cparams(("arbitrary",)),
        name="moe_experts",
    )(te, nt, xs, w_gate, w_up, w_down)


def _combine_body(pos_ref, nxt_ref, route_ref, hlat_ref, g2_ref, fn_ref, ys_ref, out_ref, buf_ref, sem):
    tm = COMBINE_TM
    i = pl.program_id(0)
    slot = i % 2

    def fetch(p_ref, s):
        def start_pair(t, p1, p2):
            _row_copy(ys_ref, p1, buf_ref.at[s, 0], t, sem.at[s]).start()
            _row_copy(ys_ref, p2, buf_ref.at[s, 1], t, sem.at[s]).start()
        _issue_rows(tm, p_ref, start_pair)

    @pl.when(i == 0)
    def _():
        fetch(pos_ref, 0)

    @pl.when(i + 1 < pl.num_programs(0))
    def _():
        fetch(nxt_ref, 1 - slot)

    _drain_rows(2 * tm, _row_copy(ys_ref, 0, buf_ref.at[slot, 0], 0, sem.at[slot]))

    r = route_ref[0]
    moe = (r[:, RT_W1:RT_W1 + 1] * _load_row_tiles(buf_ref.at[slot, 0])
           + r[:, RT_W2:RT_W2 + 1] * _load_row_tiles(buf_ref.at[slot, 1]))
    out_ref[0] = _rms(hlat_ref[0] + g2_ref[0] * moe, fn_ref[...])


def _combine(pos, route, hlat, g2, fn_row, ys):
    bn, seq, _ = hlat.shape
    tm = COMBINE_TM
    nj = seq // tm
    nb = tm // LANES
    last = bn * nj - 1
    tok = lambda i: (i // nj, i % nj, 0)
    return pl.pallas_call(
        _combine_body,
        grid=(bn * nj,),
        in_specs=[pl.BlockSpec((nb, 8, LANES), lambda i: (i, 0, 0), memory_space=pltpu.SMEM),
                  pl.BlockSpec((nb, 8, LANES), lambda i: (jnp.minimum(i + 1, last), 0, 0),
                               memory_space=pltpu.SMEM),
                  pl.BlockSpec((1, tm, LANES), tok),
                  pl.BlockSpec((1, tm, D_MODEL), tok),
                  pl.BlockSpec((1, 1, D_MODEL), lambda i: (i // nj, 0, 0)),
                  pl.BlockSpec((1, D_MODEL), lambda i: (0, 0)),
                  pl.BlockSpec(memory_space=pl.ANY)],
        out_specs=pl.BlockSpec((1, tm, D_MODEL), tok),
        out_shape=jax.ShapeDtypeStruct((bn, seq, D_MODEL), F32),
        scratch_shapes=[pltpu.VMEM((2, 2, tm) + ROW_TILE, F32), pltpu.SemaphoreType.DMA((2,))],
        compiler_params=_cparams(("arbitrary",)),
        name="moe_combine",
    )(pos, pos, route, hlat, g2, fn_row, ys)


def _prep_weights(w_in, gla_w_up, gla_b, ssd_dt_bias, ssd_a_log, ssd_d, router_group_w, router_group_b,
                  router_expert_w, router_expert_b):
    off_k = GLA_KEY
    off_v = 2 * GLA_KEY
    off_r = off_v + GLA_VAL
    off_g = off_r + GLA_VAL
    off_z = off_g + 2 * GLA_RANK
    off_xbc = off_z + SSD_INNER
    off_dt = off_xbc + SSD_XBC
    w_main = jnp.concatenate([w_in[:, :off_g], w_in[:, off_z:off_dt]], axis=1).astype(BF16)
    w_small = jnp.concatenate([w_in[:, off_g:off_z], w_in[:, off_dt:off_dt + 2 * SSD_HEADS]], axis=1)
    w_small = jnp.pad(w_small, ((0, 0), (0, SMALL_W - w_small.shape[1]))).astype(BF16)
    wup = jnp.zeros((2, LANES, GLA_KEY), F32)
    wup = wup.at[0, 0:GLA_RANK].set(gla_w_up[0]).at[1, GLA_RANK:2 * GLA_RANK].set(gla_w_up[1]).astype(BF16)
    pad_h = lambda t: jnp.pad(t.reshape(2, SSD_GROUPS, 1, SSD_HPG), ((0, 0), (0, 0), (0, 0), (0, LANES - SSD_HPG)))
    w_router = jnp.zeros((D_MODEL, LANES), F32)
    w_router = w_router.at[:, 0:N_GROUPS].set(router_group_w)
    w_router = w_router.at[:, EXPERT_LANE0:EXPERT_LANE0 + N_EXPERTS].set(router_expert_w).astype(BF16)
    b_router = jnp.zeros((1, LANES), F32).at[0, 0:N_GROUPS].set(router_group_b)
    b_router = b_router.at[0, EXPERT_LANE0:EXPERT_LANE0 + N_EXPERTS].set(router_expert_b)
    return (w_main, w_small, wup, gla_b.reshape(2, 1, GLA_KEY), pad_h(ssd_dt_bias), pad_h(ssd_a_log),
            jnp.repeat(ssd_d, SSD_P)[None, :], w_router, b_router)


def kernel(x, c, ctx, c_ctx, w_ada, b_ada, norm_mix, norm_ffn, w_in, gla_w_up, gla_b, gla_norm, ssd_conv_w,
           ssd_conv_b, ssd_dt_bias, ssd_a_log, ssd_d, ssd_norm, w_out, router_group_w, router_group_b,
           router_expert_w, router_expert_b, expert_w_gate, expert_w_up, expert_w_down, final_norm):
    assert w_ada.shape[0] == 1, "single-layer block"
    bn, seq, d = x.shape
    ctx_len = ctx.shape[1]
    assert seq // GRID_W == GRID_W and d == D_MODEL

    c8 = jnp.concatenate([c, c_ctx[None, :], jnp.zeros((8 - bn - 1, d), F32)], axis=0)
    mods = _adaln(c8, w_ada[0], b_ada[0][None, :])
    sh1, sc1, g1, sh2, sc2, g2 = [mods[:, i * d:(i + 1) * d] for i in range(6)]
    ctx_row = lambda t: jnp.broadcast_to(t[bn:bn + 1], (bn, d))
    mod1 = jnp.stack([ctx_row(sh1), ctx_row(sc1), sh1[:bn], sc1[:bn]], axis=1)
    mod2 = jnp.stack([g1[:bn], sh2[:bn], sc2[:bn], g2[:bn]], axis=1)

    (w_main, w_small, wup, gla_b3, dtb4, alog4, dskip_row, w_router, b_router) = _prep_weights(
        w_in[0], gla_w_up[0], gla_b[0], ssd_dt_bias[0], ssd_a_log[0], ssd_d[0], router_group_w[0],
        router_group_b[0], router_expert_w[0], router_expert_b[0])

    proj, small = _inproj(ctx, x, mod1, norm_mix, w_main, w_small)
    xact = _conv(proj, ssd_conv_w[0].reshape(9, SSD_XBC), ssd_conv_b, ctx_len, seq)
    lvm = jnp.asarray(np.stack([_level_matrix(CHUNK, False), _level_matrix(CHUNK, True)]))
    o_gla = _gla(proj, small, wup, gla_b3, lvm, ctx_len, seq)
    y_ssd = _ssd(xact, small, dtb4, alog4, dskip_row, ctx_len, seq)
    hlat, h2p, route, cnt = _outproj(o_gla, proj, y_ssd, x, mod2, gla_norm, ssd_norm, norm_ffn,
                                     w_out[0].astype(BF16), w_router, b_router, ctx_len)
    n_tok = bn * seq
    pos, te, nt, pad0, padn = _plan(route.reshape(n_tok, LANES), cnt)
    te, nt = te.reshape(-1), nt.reshape(-1)
    xs = _dispatch(pad0.reshape(-1), padn.reshape(-1), nt, pos, h2p.reshape((n_tok,) + ROW_TILE),
                   _moe_tiles(n_tok) * MOE_TILE)
    ys = _experts(te, nt, xs, expert_w_gate[0], expert_w_up[0], expert_w_down[0])
    return _combine(pos, route, hlat, mod2[:, 3:4, :], final_norm[None, :], ys)
```

```python
import functools
import math

import numpy as np
import jax
import jax.numpy as jnp
from jax import lax
from jax.experimental import pallas as pl
from jax.experimental.pallas import tpu as pltpu

F32 = jnp.float32
BF16 = jnp.bfloat16

D_MODEL = 1024
GRID_W = 64
EPS = 1e-6

GLA_HEADS = 4
GLA_DK = 128
GLA_DV = 256
GLA_KEY = GLA_HEADS * GLA_DK
GLA_VAL = GLA_HEADS * GLA_DV
GLA_RANK = 16
GLA_GATE_NORM = 16.0

SSD_HEADS = 16
SSD_P = 64
SSD_INNER = SSD_HEADS * SSD_P
SSD_GROUPS = 2
SSD_HPG = SSD_HEADS // SSD_GROUPS
SSD_N = 128
SSD_XBC = SSD_INNER + 2 * SSD_GROUPS * SSD_N
SSD_GW = SSD_HPG * SSD_P

N_GROUPS = 4
EXPERTS_PER_GROUP = 8
N_EXPERTS = N_GROUPS * EXPERTS_PER_GROUP
D_FF = 512

LANES = 128
VMEM_LIMIT = 56 * 1024 * 1024

PC_Q = 0
PC_K = PC_Q + GLA_KEY
PC_V = PC_K + GLA_KEY
PC_R = PC_V + GLA_VAL
PC_Z = PC_R + GLA_VAL
PC_XBC = PC_Z + SSD_INNER
PROJ_W = PC_XBC + SSD_XBC
SMALL_DT0 = 2 * GLA_RANK
SMALL_ROT0 = 2
SMALL_W = LANES * (SMALL_ROT0 + 2 * SSD_GROUPS)

CHUNK = 128
EXPERT_LANE0 = 32


def _cparams(sem):
    return pltpu.CompilerParams(dimension_semantics=sem, vmem_limit_bytes=VMEM_LIMIT)


def _silu(x):
    return x / (1.0 + jnp.exp(-x))


def _softplus(x):
    return jnp.maximum(x, 0.0) + jnp.log(1.0 + jnp.exp(-jnp.abs(x)))


def _log_sigmoid(x):
    return jnp.minimum(x, 0.0) - jnp.log(1.0 + jnp.exp(-jnp.abs(x)))


def _bdot(a, b):
    return jnp.dot(a.astype(BF16), b.astype(BF16), preferred_element_type=F32)


def _bdot_nt(a, b):
    return lax.dot_general(a.astype(BF16), b.astype(BF16), (((1,), (1,)), ((), ())),
                           preferred_element_type=F32)


def _adaln_body(c_ref, w_ref, b_ref, o_ref):
    o_ref[...] = _bdot(_silu(c_ref[...]), w_ref[...]) + b_ref[...]


def _adaln(c8, w, b):
    n = w.shape[1]
    bn = 1024
    return pl.pallas_call(
        _adaln_body,
        grid=(n // bn,),
        in_specs=[pl.BlockSpec((8, D_MODEL), lambda j: (0, 0)),
                  pl.BlockSpec((D_MODEL, bn), lambda j: (0, j)),
                  pl.BlockSpec((1, bn), lambda j: (0, j))],
        out_specs=pl.BlockSpec((8, bn), lambda j: (0, j)),
        out_shape=jax.ShapeDtypeStruct((8, n), F32),
        compiler_params=_cparams(("arbitrary",)),
        name="adaln",
    )(c8, w, b)


INPROJ_TM = 256
INPROJ_NC = 512


def _inproj_body(ctx_ref, x_ref, mod_ref, nw_ref, w_ref, ws_ref, proj_ref, small_ref):
    is_ctx = pl.program_id(1) == 0
    xin = jnp.where(is_ctx, ctx_ref[0], x_ref[0])
    m = mod_ref[0]
    shift = jnp.where(is_ctx, m[0:1], m[2:3])
    scale = jnp.where(is_ctx, m[1:2], m[3:4])
    ms = jnp.mean(xin * xin, axis=-1, keepdims=True)
    h = xin * lax.rsqrt(ms + EPS) * nw_ref[...]
    hb = (h * (1.0 + scale) + shift).astype(BF16)
    for n in range(PROJ_W // INPROJ_NC):
        sl = slice(n * INPROJ_NC, (n + 1) * INPROJ_NC)
        proj_ref[0, :, sl] = jnp.dot(hb, w_ref[:, sl], preferred_element_type=F32).astype(BF16)
    sm = jnp.dot(hb, ws_ref[...], preferred_element_type=F32)
    small_ref[0, :, 0:LANES] = sm
    small_ref[0, :, LANES:SMALL_ROT0 * LANES] = jnp.zeros((sm.shape[0], (SMALL_ROT0 - 1) * LANES), F32)
    for k in range(2 * SSD_GROUPS):
        lane0 = SMALL_DT0 + k * SSD_HPG
        small_ref[0, :, (SMALL_ROT0 + k) * LANES:(SMALL_ROT0 + k + 1) * LANES] = pltpu.roll(sm, LANES - lane0, 1)


def _inproj(ctx, x, mod1, norm_w, w_main, w_small):
    bn, seq, _ = x.shape
    ctx_len = ctx.shape[1]
    tm = INPROJ_TM
    assert ctx_len == tm and seq % tm == 0
    tt = ctx_len + seq
    nj = tt // tm
    return pl.pallas_call(
        _inproj_body,
        grid=(bn, nj),
        in_specs=[pl.BlockSpec((1, tm, D_MODEL), lambda b, j: (b, 0, 0)),
                  pl.BlockSpec((1, tm, D_MODEL), lambda b, j: (b, jnp.maximum(j - 1, 0), 0)),
                  pl.BlockSpec((1, 4, D_MODEL), lambda b, j: (b, 0, 0)),
                  pl.BlockSpec((1, D_MODEL), lambda b, j: (0, 0)),
                  pl.BlockSpec((D_MODEL, PROJ_W), lambda b, j: (0, 0)),
                  pl.BlockSpec((D_MODEL, LANES), lambda b, j: (0, 0))],
        out_specs=[pl.BlockSpec((1, tm, PROJ_W), lambda b, j: (b, j, 0)),
                   pl.BlockSpec((1, tm, SMALL_W), lambda b, j: (b, j, 0))],
        out_shape=[jax.ShapeDtypeStruct((bn, tt, PROJ_W), BF16),
                   jax.ShapeDtypeStruct((bn, tt, SMALL_W), F32)],
        compiler_params=_cparams(("arbitrary", "arbitrary")),
        name="inproj",
    )(ctx, x, mod1, norm_w, w_main, w_small)


CONV_CB = 256
CONV_TT = 256


def _conv_body(ctx_len, seq, p_ref, w_ref, b_ref, o_ref, s_ref):
    lat0 = ctx_len + GRID_W
    tt = CONV_TT
    n_tiles = seq // tt
    zeros = jnp.zeros((GRID_W, CONV_CB), F32)
    s_ref[0:ctx_len, :] = p_ref[0, 0:ctx_len, :].astype(F32)
    s_ref[ctx_len:lat0, :] = zeros
    s_ref[lat0 + seq:lat0 + seq + GRID_W, :] = zeros

    def copy_tile(t, c):
        src = pl.multiple_of(ctx_len + t * tt, tt)
        dst = pl.multiple_of(lat0 + t * tt, GRID_W)
        s_ref[pl.ds(dst, tt), :] = p_ref[0, pl.ds(src, tt), :].astype(F32)
        return c

    lax.fori_loop(0, n_tiles, copy_tile, 0)

    w = w_ref[...]
    bias = b_ref[...]
    pos = lax.broadcasted_iota(jnp.int32, (tt, 1), 0)

    def conv_rows(rows, first, last):
        left, mid, right = [sum(xr * w[3 * kh + kw:3 * kh + kw + 1] for xr, kh in rows) for kw in range(3)]
        return (bias + mid + jnp.where(first, 0.0, pltpu.roll(left, 1, 0))
                + jnp.where(last, 0.0, pltpu.roll(right, tt - 1, 0)))

    acc = conv_rows([(s_ref[0:ctx_len, :], 1)], pos == 0, pos == ctx_len - 1)
    o_ref[0, 0:ctx_len, :] = _silu(acc).astype(BF16)

    col = pos % GRID_W
    first = col == 0
    last = col == GRID_W - 1

    def tile(t, c):
        base = lat0 + t * tt
        rows = [(s_ref[pl.ds(pl.multiple_of(base + (kh - 1) * GRID_W, GRID_W), tt), :], kh) for kh in range(3)]
        dst = pl.multiple_of(ctx_len + t * tt, tt)
        o_ref[0, pl.ds(dst, tt), :] = _silu(conv_rows(rows, first, last)).astype(BF16)
        return c

    lax.fori_loop(0, n_tiles, tile, 0)


def _conv(proj, conv_w9, conv_b, ctx_len, seq):
    bn, tt, _ = proj.shape
    assert ctx_len == CONV_TT and seq % CONV_TT == 0 and CONV_TT % GRID_W == 0
    ncb = SSD_XBC // CONV_CB
    cb0 = PC_XBC // CONV_CB
    return pl.pallas_call(
        functools.partial(_conv_body, ctx_len, seq),
        grid=(bn, ncb),
        in_specs=[pl.BlockSpec((1, tt, CONV_CB), lambda b, j: (b, 0, cb0 + j)),
                  pl.BlockSpec((9, CONV_CB), lambda b, j: (0, j)),
                  pl.BlockSpec((1, CONV_CB), lambda b, j: (0, j))],
        out_specs=pl.BlockSpec((1, tt, CONV_CB), lambda b, j: (b, 0, j)),
        out_shape=jax.ShapeDtypeStruct((bn, tt, SSD_XBC), BF16),
        scratch_shapes=[pltpu.VMEM((tt + 2 * GRID_W, CONV_CB), F32)],
        compiler_params=_cparams(("arbitrary", "arbitrary")),
        name="conv",
    )(proj, conv_w9, conv_b)


def _bwd_chunk(s, n_ctx, n_steps):
    return jnp.where(s < n_ctx, n_ctx - 1 - s, n_steps + n_ctx - 1 - s)


def _out_row(c, n_ctx, n, seq):
    return pl.multiple_of(jnp.where(c >= n_ctx, (c - n_ctx) * n, seq), n)


def _scan_sum(a, rev):
    n = a.shape[0]
    row = lax.broadcasted_iota(jnp.int32, (n, 1), 0)
    sh = 1
    while sh < n:
        if rev:
            a = a + jnp.where(row < n - sh, pltpu.roll(a, n - sh, 0), 0.0)
        else:
            a = a + jnp.where(row >= sh, pltpu.roll(a, sh, 0), 0.0)
        sh *= 2
    return a


def _level_matrix(n, rev):
    i = np.arange(n)[:, None]
    j = np.arange(n)[None, :]
    x = i ^ j
    lv = np.where(x > 0, np.floor(np.log2(np.maximum(x, 1))).astype(np.int64) + 1, 0)
    earlier = (j > i) if rev else (j < i)
    return np.where(earlier, lv, 0).astype(np.int32)


GLA_HPS = 4


def _gla_dir(rev, dst, q_ref, k_ref, v_ref, sm_ref, wup, gb, lv, o_ref, s_ref):
    n = CHUNK
    q = q_ref[0].astype(F32) * (GLA_DK ** -0.5)
    k = k_ref[0].astype(F32)
    x = jnp.dot(sm_ref[0].astype(BF16), wup, preferred_element_type=F32) + gb
    b = _scan_sum(_log_sigmoid(x) / GLA_GATE_NORM, rev)
    row = lax.broadcasted_iota(jnp.int32, (n, 1), 0)
    b_end = b[0:1] if rev else b[n - 1:n]
    k_end = k * jnp.exp(b_end - b)
    q_in = q * jnp.exp(b)
    dec_end = jnp.exp(b_end)

    att = [jnp.zeros((n, n), F32) for _ in range(GLA_HPS)]
    e_h = b
    h = 1
    level = 1
    while h < n:
        upper = (row & h) != 0
        if rev:
            e = jnp.where(upper, e_h - b, b - pltpu.roll(e_h, n - h, 0))
        else:
            e = jnp.where(upper, b - pltpu.roll(e_h, h, 0), e_h - b)
        w = jnp.exp(e)
        qw = (q * w).astype(BF16)
        kw = (k * w).astype(BF16)
        mask = lv == level
        for hh in range(GLA_HPS):
            ks = slice(hh * GLA_DK, (hh + 1) * GLA_DK)
            att[hh] = att[hh] + jnp.where(mask, _bdot_nt(qw[:, ks], kw[:, ks]), 0.0)
        if 2 * h < n:
            if rev:
                e_h = jnp.where(upper, pltpu.roll(e_h, h, 0), e_h)
            else:
                e_h = jnp.where(upper, e_h, pltpu.roll(e_h, n - h, 0))
        h *= 2
        level += 1

    for hh in range(GLA_HPS):
        ks = slice(hh * GLA_DK, (hh + 1) * GLA_DK)
        vs = slice(hh * GLA_DV, (hh + 1) * GLA_DV)
        v = v_ref[0, :, vs]
        s_old = s_ref[hh]
        o = jnp.dot(att[hh].astype(BF16), v, preferred_element_type=F32)
        o = o + jnp.sum(q[:, ks] * k[:, ks], axis=-1, keepdims=True) * v.astype(F32)
        o = o + _bdot(q_in[:, ks], s_old)
        o_ref[0, pl.ds(dst, n), vs] += o
        dec_col = jnp.broadcast_to(dec_end[:, ks], (n, GLA_DK)).T[:, 0:1]
        s_ref[hh] = dec_col * s_old + jnp.dot(k_end[:, ks].T.astype(BF16), v, preferred_element_type=F32)


def _gla_body(n_ctx, seq, qf, kf, vf, smf, qb, kb, vb, smb, wup_ref, gb_ref, lv_ref, o_ref, s_ref):
    step = pl.program_id(2)
    n = CHUNK

    @pl.when(step == 0)
    def _():
        s_ref[...] = jnp.zeros_like(s_ref)
        o_ref[...] = jnp.zeros_like(o_ref)

    cb = _bwd_chunk(step, n_ctx, pl.num_programs(2))
    _gla_dir(False, _out_row(step, n_ctx, n, seq), qf, kf, vf, smf, wup_ref[0], gb_ref[0], lv_ref[0],
             o_ref, s_ref.at[0])
    _gla_dir(True, _out_row(cb, n_ctx, n, seq), qb, kb, vb, smb, wup_ref[1], gb_ref[1], lv_ref[1],
             o_ref, s_ref.at[1])


def _gla(proj, small, wup_pad, gla_b3, lvm, ctx_len, seq):
    bn, tt, _ = proj.shape
    n = CHUNK
    assert ctx_len % n == 0 and seq % n == 0
    n_ctx = ctx_len // n
    n_steps = tt // n
    hw = GLA_HPS * GLA_DK
    vw = GLA_HPS * GLA_DV
    kb = PC_K // hw
    vb = PC_V // vw
    fwd = lambda s: s
    bwd = lambda s: _bwd_chunk(s, n_ctx, n_steps)

    def chunk_specs(c):
        return [pl.BlockSpec((1, n, hw), lambda b, h, s: (b, c(s), h)),
                pl.BlockSpec((1, n, hw), lambda b, h, s: (b, c(s), kb + h)),
                pl.BlockSpec((1, n, vw), lambda b, h, s: (b, c(s), vb + h)),
                pl.BlockSpec((1, n, LANES), lambda b, h, s: (b, c(s), 0))]

    return pl.pallas_call(
        functools.partial(_gla_body, n_ctx, seq),
        grid=(bn, GLA_HEADS // GLA_HPS, n_steps),
        in_specs=chunk_specs(fwd) + chunk_specs(bwd) + [
            pl.BlockSpec((2, LANES, hw), lambda b, h, s: (0, 0, h)),
            pl.BlockSpec((2, 1, hw), lambda b, h, s: (0, 0, h)),
            pl.BlockSpec((2, n, n), lambda b, h, s: (0, 0, 0))],
        out_specs=pl.BlockSpec((1, seq + n, vw), lambda b, h, s: (b, 0, h)),
        out_shape=jax.ShapeDtypeStruct((bn, seq + n, GLA_VAL), F32),
        scratch_shapes=[pltpu.VMEM((2, GLA_HPS, GLA_DK, GLA_DV), F32)],
        compiler_params=_cparams(("arbitrary",) * 3),
        name="gla",
    )(proj, proj, proj, small, proj, proj, proj, small, wup_pad, gla_b3, lvm)


def _ssd_dir(rev, first, g, dst, x_ref, bm_ref, cm_ref, sm_ref, dtb, alog, dsk_ref, y_ref, s_ref):
    n = CHUNK
    gl = slice(g * LANES, (g + 1) * LANES)
    g0 = g * SSD_GW
    lane = lax.broadcasted_iota(jnp.int32, (1, LANES), 1)
    dt = _softplus(sm_ref[0, :, gl] + dtb)
    a_neg = jnp.where(lane < SSD_HPG, -jnp.exp(alog), 0.0)
    acum = _scan_sum(dt * a_neg, rev)
    acum_t = acum.T
    x = x_ref[0, :, g0:g0 + SSD_GW].astype(F32)
    cm = cm_ref[0, :, gl]
    bm = bm_ref[0, :, gl]
    s_old = s_ref[...]
    ii = lax.broadcasted_iota(jnp.int32, (n, n), 0)
    jj = lax.broadcasted_iota(jnp.int32, (n, n), 1)
    causal = (jj >= ii) if rev else (jj <= ii)
    cb = _bdot_nt(cm, bm)
    y_off = jnp.dot(cm, s_old.astype(BF16), preferred_element_type=F32)
    low = lane < SSD_P
    xs_parts = []
    send_parts = []
    for p in range(SSD_HPG // 2):
        slab = slice(p * LANES, (p + 1) * LANES)
        heads = []
        for hh in (2 * p, 2 * p + 1):
            colb = jnp.broadcast_to(acum[:, hh:hh + 1], (n, LANES))
            dtb_h = jnp.broadcast_to(dt[:, hh:hh + 1], (n, LANES))
            seg = jnp.exp(jnp.where(causal, colb - acum_t[hh:hh + 1, :], -jnp.inf))
            end = colb[0:1] if rev else colb[n - 1:n]
            heads.append(((cb * seg).astype(BF16), dtb_h, jnp.exp(colb), jnp.exp(end - colb), jnp.exp(end)))
        (m0, dt0, ec0, dc0, se0), (m1, dt1, ec1, dc1, se1) = heads
        xdt = x[:, slab] * jnp.where(low, dt0, dt1)
        rhs = jnp.concatenate([jnp.where(low, xdt, 0.0), jnp.where(low, 0.0, xdt)], axis=0).astype(BF16)
        y = jnp.dot(jnp.concatenate([m0, m1], axis=1), rhs, preferred_element_type=F32)
        y = y + y_off[:, slab] * jnp.where(low, ec0, ec1)
        out = slice(g0 + p * LANES, g0 + (p + 1) * LANES)
        if first:
            y = y + x[:, slab] * dsk_ref[:, out]
        y_ref[0, pl.ds(dst, n), out] += y
        xs_parts.append((xdt * jnp.where(low, dc0, dc1)).astype(BF16))
        send_parts.append(jnp.where(low, se0, se1))
    new = jnp.dot(bm.astype(F32).T.astype(BF16), jnp.concatenate(xs_parts, axis=1),
                  preferred_element_type=F32)
    s_ref[...] = s_old * jnp.concatenate(send_parts, axis=1) + new


def _ssd_body(n_ctx, seq, xf, bf, cf, smf, xb, bb, cbk, smb, dtb_ref, alog_ref, dsk_ref, y_ref, s_ref):
    step = pl.program_id(1)
    n = CHUNK

    @pl.when(step == 0)
    def _():
        s_ref[...] = jnp.zeros_like(s_ref)
        y_ref[...] = jnp.zeros_like(y_ref)

    dst_f = _out_row(step, n_ctx, n, seq)
    dst_b = _out_row(_bwd_chunk(step, n_ctx, pl.num_programs(1)), n_ctx, n, seq)
    for g in range(SSD_GROUPS):
        _ssd_dir(False, True, g, dst_f, xf, bf, cf, smf, dtb_ref[0, g], alog_ref[0, g], dsk_ref, y_ref,
                 s_ref.at[0, g])
        _ssd_dir(True, False, g, dst_b, xb, bb, cbk, smb, dtb_ref[1, g], alog_ref[1, g], dsk_ref, y_ref,
                 s_ref.at[1, g])


def _ssd(xact, small, dtb4, alog4, dskip_row, ctx_len, seq):
    bn, tt, _ = xact.shape
    n = CHUNK
    n_ctx = ctx_len // n
    n_steps = tt // n
    gn = SSD_GROUPS * SSD_N
    assert SSD_INNER % gn == 0
    bblk = SSD_INNER // gn
    fwd = lambda s: s
    bwd = lambda s: _bwd_chunk(s, n_ctx, n_steps)

    assert SMALL_ROT0 % SSD_GROUPS == 0
    sblk = SMALL_ROT0 // SSD_GROUPS

    def chunk_specs(c, d):
        return [pl.BlockSpec((1, n, SSD_INNER), lambda b, s: (b, c(s), 0)),
                pl.BlockSpec((1, n, gn), lambda b, s: (b, c(s), bblk)),
                pl.BlockSpec((1, n, gn), lambda b, s: (b, c(s), bblk + 1)),
                pl.BlockSpec((1, n, SSD_GROUPS * LANES), lambda b, s: (b, c(s), sblk + d))]

    whole4 = pl.BlockSpec((2, SSD_GROUPS, 1, LANES), lambda b, s: (0, 0, 0, 0))
    return pl.pallas_call(
        functools.partial(_ssd_body, n_ctx, seq),
        grid=(bn, n_steps),
        in_specs=chunk_specs(fwd, 0) + chunk_specs(bwd, 1) + [
            whole4, whole4, pl.BlockSpec((1, SSD_INNER), lambda b, s: (0, 0))],
        out_specs=pl.BlockSpec((1, seq + n, SSD_INNER), lambda b, s: (b, 0, 0)),
        out_shape=jax.ShapeDtypeStruct((bn, seq + n, SSD_INNER), F32),
        scratch_shapes=[pltpu.VMEM((2, SSD_GROUPS, SSD_N, SSD_GW), F32)],
        compiler_params=_cparams(("arbitrary",) * 2),
        name="ssd",
    )(xact, xact, xact, small, xact, xact, xact, small, dtb4, alog4, dskip_row)


OUT_TM = 256
RT_E1, RT_E2, RT_RANK1, RT_RANK2, RT_W1, RT_W2 = range(6)


def _rms(x, w):
    return x * lax.rsqrt(jnp.mean(x * x, axis=-1, keepdims=True) + EPS) * w


ROW_TILE = (D_MODEL // LANES, LANES)


def _store_row_tiles(ref, x):
    ref[...] = x.reshape((x.shape[0],) + ROW_TILE)


def _load_row_tiles(ref):
    return ref[...].reshape(ref.shape[0], D_MODEL)


def _outproj_body(o_ref, r_ref, y_ref, z_ref, x_ref, mod_ref, gn_ref, sn_ref, fn_ref, wo_ref, wr_ref, br_ref,
                  hlat_ref, h2_ref, route_ref, cnt_ref, carry_ref):
    tm = OUT_TM

    @pl.when((pl.program_id(0) == 0) & (pl.program_id(1) == 0))
    def _():
        carry_ref[...] = jnp.zeros_like(carry_ref)

    parts = []
    for h in range(GLA_HEADS):
        hs = slice(h * GLA_DV, (h + 1) * GLA_DV)
        parts.append((_rms(o_ref[0, :, hs], gn_ref[...]) * _silu(r_ref[0, :, hs].astype(F32))).astype(BF16))
    for g in range(SSD_GROUPS):
        gs = slice(g * SSD_GW, (g + 1) * SSD_GW)
        yg = y_ref[0, :, gs] * _silu(z_ref[0, :, gs].astype(F32))
        parts.append(_rms(yg, sn_ref[:, gs]).astype(BF16))
    mix = jnp.concatenate(parts, axis=-1)
    m = mod_ref[0]
    hlat = x_ref[0] + m[0:1] * jnp.dot(mix, wo_ref[...], preferred_element_type=F32)
    hlat_ref[0] = hlat
    h2f = _rms(hlat, fn_ref[...]) * (1.0 + m[2:3]) + m[1:2]
    _store_row_tiles(h2_ref.at[0], h2f)

    lg = jnp.dot(h2f.astype(BF16), wr_ref[...], preferred_element_type=F32) + br_ref[...]
    lanef = lax.broadcasted_iota(jnp.int32, (tm, LANES), 1).astype(F32)
    ninf = -jnp.inf
    is_g = lanef < float(N_GROUPS)
    gl = jnp.where(is_g, lg, ninf)
    gmax = jnp.max(gl, axis=-1, keepdims=True)
    gsel = jnp.min(jnp.where(gl == gmax, lanef, float(LANES)), axis=-1, keepdims=True)
    pg = 1.0 / jnp.sum(jnp.where(is_g, jnp.exp(lg - gmax), 0.0), axis=-1, keepdims=True)
    lo = float(EXPERT_LANE0) + float(EXPERTS_PER_GROUP) * gsel
    in_grp = (lanef >= lo) & (lanef < lo + float(EXPERTS_PER_GROUP))
    el = jnp.where(in_grp, lg, ninf)
    v1 = jnp.max(el, axis=-1, keepdims=True)
    i1 = jnp.min(jnp.where(el == v1, lanef, float(LANES)), axis=-1, keepdims=True)
    el2 = jnp.where(lanef == i1, ninf, el)
    v2 = jnp.max(el2, axis=-1, keepdims=True)
    i2 = jnp.min(jnp.where(el2 == v2, lanef, float(LANES)), axis=-1, keepdims=True)
    t = jnp.exp(v2 - v1)
    w1 = pg / (1.0 + t)
    w2 = pg * t / (1.0 + t)

    sel1 = lanef == i1
    sel2 = lanef == i2
    member = jnp.where(sel1 | sel2, 1.0, 0.0)
    ii = lax.broadcasted_iota(jnp.int32, (tm, tm), 0)
    jj = lax.broadcasted_iota(jnp.int32, (tm, tm), 1)
    before = jnp.where(jj < ii, 1.0, 0.0).astype(BF16)
    ranks = jnp.dot(before, member.astype(BF16), preferred_element_type=F32) + carry_ref[...]
    rank1 = jnp.sum(jnp.where(sel1, ranks, 0.0), axis=-1, keepdims=True)
    rank2 = jnp.sum(jnp.where(sel2, ranks, 0.0), axis=-1, keepdims=True)
    carry = carry_ref[...] + jnp.sum(member, axis=0, keepdims=True)
    carry_ref[...] = carry
    cnt_ref[...] = carry
    rec = jnp.zeros((tm, LANES), F32)
    for lane_id, val in ((RT_E1, i1 - float(EXPERT_LANE0)), (RT_E2, i2 - float(EXPERT_LANE0)),
                         (RT_RANK1, rank1), (RT_RANK2, rank2), (RT_W1, w1), (RT_W2, w2)):
        rec = jnp.where(lanef == float(lane_id), val, rec)
    route_ref[0] = rec


def _outproj(o_gla, proj, y_ssd, x, mod2, gn_row, sn_row, fn_row, w_out, w_router, b_router, ctx_len):
    bn, seq, _ = x.shape
    tm = OUT_TM
    assert ctx_len % tm == 0 and seq % tm == 0
    j0 = ctx_len // tm
    rb = PC_R // GLA_VAL
    zb = PC_Z // SSD_INNER
    tok = lambda b, j: (b, j, 0)
    const = lambda b, j: (0, 0)
    return pl.pallas_call(
        _outproj_body,
        grid=(bn, seq // tm),
        in_specs=[pl.BlockSpec((1, tm, GLA_VAL), tok),
                  pl.BlockSpec((1, tm, GLA_VAL), lambda b, j: (b, j0 + j, rb)),
                  pl.BlockSpec((1, tm, SSD_INNER), tok),
                  pl.BlockSpec((1, tm, SSD_INNER), lambda b, j: (b, j0 + j, zb)),
                  pl.BlockSpec((1, tm, D_MODEL), tok),
                  pl.BlockSpec((1, 4, D_MODEL), lambda b, j: (b, 0, 0)),
                  pl.BlockSpec((1, GLA_DV), const),
                  pl.BlockSpec((1, SSD_INNER), const),
                  pl.BlockSpec((1, D_MODEL), const),
                  pl.BlockSpec((GLA_VAL + SSD_INNER, D_MODEL), const),
                  pl.BlockSpec((D_MODEL, LANES), const),
                  pl.BlockSpec((1, LANES), const)],
        out_specs=[pl.BlockSpec((1, tm, D_MODEL), tok),
                   pl.BlockSpec((1, tm) + ROW_TILE, lambda b, j: (b, j, 0, 0)),
                   pl.BlockSpec((1, tm, LANES), tok),
                   pl.BlockSpec((1, LANES), const)],
        out_shape=[jax.ShapeDtypeStruct((bn, seq, D_MODEL), F32),
                   jax.ShapeDtypeStruct((bn, seq) + ROW_TILE, F32),
                   jax.ShapeDtypeStruct((bn, seq, LANES), F32),
                   jax.ShapeDtypeStruct((1, LANES), F32)],
        scratch_shapes=[pltpu.VMEM((1, LANES), F32)],
        compiler_params=_cparams(("arbitrary", "arbitrary")),
        name="outproj",
    )(o_gla, proj, y_ssd, proj, x, mod2, gn_row, sn_row, fn_row, w_out, w_router, b_router)


MOE_TILE = 256
PLAN_TM = 1024
DISPATCH_TM = 1024
COMBINE_TM = 256


def _moe_tiles(n_tokens):
    return (2 * n_tokens) // MOE_TILE + N_EXPERTS


def _plan_body(nt_pad, route_ref, cnt_ref, pos_ref, te_ref, nt_ref, pad0_ref, padn_ref, seg_ref, nxt_ref):
    lane = lax.broadcasted_iota(jnp.int32, (1, LANES), 1).astype(F32)
    is_e = (lane >= float(EXPERT_LANE0)) & (lane < float(EXPERT_LANE0 + N_EXPERTS))
    tiles = jnp.where(is_e, jnp.floor((cnt_ref[...] + float(MOE_TILE - 1)) / float(MOE_TILE)), 0.0)
    ii = lax.broadcasted_iota(jnp.int32, (LANES, LANES), 0)
    jj = lax.broadcasted_iota(jnp.int32, (LANES, LANES), 1)
    lower = jnp.where(ii < jj, 1.0, 0.0).astype(BF16)
    first_tile = jnp.dot(jnp.broadcast_to(tiles, (8, LANES)).astype(BF16), lower,
                         preferred_element_type=F32)[0:1]
    off_row = first_tile * float(MOE_TILE)
    for sb in range(PLAN_TM // LANES):
        r = route_ref[sb * LANES:(sb + 1) * LANES, :]
        o1 = jnp.sum(jnp.where(lane == r[:, RT_E1:RT_E1 + 1] + float(EXPERT_LANE0), off_row, 0.0),
                     axis=-1, keepdims=True)
        o2 = jnp.sum(jnp.where(lane == r[:, RT_E2:RT_E2 + 1] + float(EXPERT_LANE0), off_row, 0.0),
                     axis=-1, keepdims=True)
        p = jnp.where(lane == 0.0, o1 + r[:, RT_RANK1:RT_RANK1 + 1],
                      jnp.where(lane == 1.0, o2 + r[:, RT_RANK2:RT_RANK2 + 1], 0.0))
        pos_ref[sb] = p.T[0:8, :].astype(jnp.int32)

    @pl.when(pl.program_id(0) == 0)
    def _():
        ti = lax.broadcasted_iota(jnp.int32, (nt_pad, 1), 0).astype(F32)
        done = jnp.where(is_e & (first_tile + tiles <= ti), 1.0, 0.0)
        te = jnp.minimum(jnp.sum(done, axis=-1, keepdims=True), float(N_EXPERTS - 1))
        te_ref[...] = te.astype(jnp.int32)
        nt_ref[...] = jnp.sum(tiles, axis=-1, keepdims=True).astype(jnp.int32)
        lane_e = lane - float(EXPERT_LANE0)
        nonempty = is_e & (tiles > 0.0)
        seg_ref[...] = jnp.sum(jnp.where(nonempty & (lane_e < te), 1.0, 0.0), axis=-1,
                               keepdims=True).astype(jnp.int32)
        nxt = jnp.min(jnp.where(nonempty & (lane_e > te), lane_e, float(N_EXPERTS)), axis=-1, keepdims=True)
        nxt_ref[...] = jnp.where(nxt < float(N_EXPERTS), nxt, -1.0).astype(jnp.int32)
        used = jnp.where(is_e, cnt_ref[...], 0.0)
        pad0_ref[...] = (off_row + used).astype(jnp.int32)
        padn_ref[...] = (tiles * float(MOE_TILE) - used).astype(jnp.int32)


def _plan(route, cnt):
    n_tok = route.shape[0]
    nt_pad = _moe_tiles(n_tok)
    nsb = PLAN_TM // LANES
    return pl.pallas_call(
        functools.partial(_plan_body, nt_pad),
        grid=(n_tok // PLAN_TM,),
        in_specs=[pl.BlockSpec((PLAN_TM, LANES), lambda i: (i, 0)),
                  pl.BlockSpec((1, LANES), lambda i: (0, 0))],
        out_specs=[pl.BlockSpec((nsb, 8, LANES), lambda i: (i, 0, 0)),
                   pl.BlockSpec((nt_pad, 1), lambda i: (0, 0)),
                   pl.BlockSpec((1, 1), lambda i: (0, 0)),
                   pl.BlockSpec((1, LANES), lambda i: (0, 0)),
                   pl.BlockSpec((1, LANES), lambda i: (0, 0)),
                   pl.BlockSpec((nt_pad, 1), lambda i: (0, 0)),
                   pl.BlockSpec((nt_pad, 1), lambda i: (0, 0))],
        out_shape=[jax.ShapeDtypeStruct((n_tok // LANES, 8, LANES), jnp.int32),
                   jax.ShapeDtypeStruct((nt_pad, 1), jnp.int32),
                   jax.ShapeDtypeStruct((1, 1), jnp.int32),
                   jax.ShapeDtypeStruct((1, LANES), jnp.int32),
                   jax.ShapeDtypeStruct((1, LANES), jnp.int32),
                   jax.ShapeDtypeStruct((nt_pad, 1), jnp.int32),
                   jax.ShapeDtypeStruct((nt_pad, 1), jnp.int32)],
        compiler_params=_cparams(("arbitrary",)),
        name="moe_plan",
    )(route, cnt)


def _row_copy(src, s_row, dst, d_row, sem):
    return pltpu.make_async_copy(src.at[s_row], dst.at[d_row], sem)


def _issue_rows(n_rows, pos_ref, start_pair):
    for blk in range(n_rows // LANES):
        def issue(l, c, blk=blk):
            start_pair(blk * LANES + l, pos_ref[blk, 0, l], pos_ref[blk, 1, l])
            return c
        lax.fori_loop(0, LANES, issue, 0, unroll=8)


def _drain_rows(n_copies, copy):
    def drain(t, c):
        copy.wait()
        return c
    lax.fori_loop(0, n_copies, drain, 0, unroll=8)


def _dispatch_body(pad0_ref, padn_ref, nt_ref, pos_ref, h2_ref, xs_ref, zero_ref, sem):
    def start_pair(t, p1, p2):
        _row_copy(h2_ref, t, xs_ref, p1, sem).start()
        _row_copy(h2_ref, t, xs_ref, p2, sem).start()

    _issue_rows(DISPATCH_TM, pos_ref, start_pair)
    _drain_rows(2 * DISPATCH_TM, _row_copy(h2_ref, 0, xs_ref, 0, sem))

    @pl.when(pl.program_id(0) == 0)
    def _():
        zero_ref[...] = jnp.zeros_like(zero_ref)
        n_tiles = xs_ref.shape[0] // MOE_TILE

        def tile_copy(i):
            return pltpu.make_async_copy(zero_ref, xs_ref.at[pl.ds(i * MOE_TILE, MOE_TILE)], sem)

        def fill_tile(i, c):
            tile_copy(i).start()
            return c

        def drain_tile(i, c):
            tile_copy(i).wait()
            return c

        lax.fori_loop(nt_ref[0], n_tiles, fill_tile, 0)
        lax.fori_loop(nt_ref[0], n_tiles, drain_tile, 0)
        for e in range(N_EXPERTS):
            n_pad = padn_ref[EXPERT_LANE0 + e]
            for wait in (False, True):
                row = pad0_ref[EXPERT_LANE0 + e]
                size = MOE_TILE // 2
                while size >= 1:
                    has = (n_pad & size) != 0
                    copy = pltpu.make_async_copy(zero_ref.at[pl.ds(0, size)], xs_ref.at[pl.ds(row, size)], sem)
                    pl.when(has)(copy.wait if wait else copy.start)
                    row = row + jnp.where(has, size, 0)
                    size //= 2


def _dispatch(pad0, padn, nt, pos, h2t, n_rows):
    n_tok = h2t.shape[0]
    nb = DISPATCH_TM // LANES
    grid_spec = pltpu.PrefetchScalarGridSpec(
        num_scalar_prefetch=3,
        grid=(n_tok // DISPATCH_TM,),
        in_specs=[pl.BlockSpec((nb, 8, LANES), lambda i, *_: (i, 0, 0), memory_space=pltpu.SMEM),
                  pl.BlockSpec((DISPATCH_TM,) + ROW_TILE, lambda i, *_: (i, 0, 0))],
        out_specs=pl.BlockSpec(memory_space=pl.ANY),
        scratch_shapes=[pltpu.VMEM((MOE_TILE,) + ROW_TILE, F32), pltpu.SemaphoreType.DMA(())])
    return pl.pallas_call(
        _dispatch_body,
        grid_spec=grid_spec,
        out_shape=jax.ShapeDtypeStruct((n_rows,) + ROW_TILE, F32),
        compiler_params=_cparams(("arbitrary",)),
        name="moe_dispatch",
    )(pad0, padn, nt, pos, h2t)


def _expert_body(te_ref, nt_ref, seg_ref, nxt_ref, xs_ref, wg_ref, wu_ref, wd_ref, ys_ref,
                 wgf_ref, wuf_ref, wdf_ref, wgb_ref, wub_ref, wdb_ref, sem):
    i = pl.program_id(0)

    def copies(e, s):
        return [pltpu.make_async_copy(src.at[e], dst.at[s], sem.at[s])
                for src, dst in ((wg_ref, wgf_ref), (wu_ref, wuf_ref), (wd_ref, wdf_ref))]

    @pl.when(i < nt_ref[0])
    def _():
        seg = seg_ref[i]
        slot = seg % 2
        first = (i == 0) | (seg != seg_ref[jnp.maximum(i - 1, 0)])

        @pl.when(first)
        def _():
            @pl.when(i == 0)
            def _():
                for c in copies(te_ref[i], slot):
                    c.start()

            for c in copies(te_ref[i], slot):
                c.wait()
            wgb_ref[...] = wgf_ref[slot].astype(BF16)
            wub_ref[...] = wuf_ref[slot].astype(BF16)
            wdb_ref[...] = wdf_ref[slot].astype(BF16)

            @pl.when(nxt_ref[i] >= 0)
            def _():
                for c in copies(nxt_ref[i], 1 - slot):
                    c.start()

        x = _load_row_tiles(xs_ref).astype(BF16)
        gate = jnp.dot(x, wgb_ref[...], preferred_element_type=F32)
        up = jnp.dot(x, wub_ref[...], preferred_element_type=F32)
        y = jnp.dot((_silu(gate) * up).astype(BF16), wdb_ref[...], preferred_element_type=F32)
        _store_row_tiles(ys_ref, y)

    @pl.when(i >= nt_ref[0])
    def _():
        ys_ref[...] = jnp.zeros_like(ys_ref)


def _experts(te, nt, seg, nxt, xs, w_gate, w_up, w_down):
    rows = xs.shape[0]
    n_tiles = rows // MOE_TILE
    row_block = pl.BlockSpec((MOE_TILE,) + ROW_TILE, lambda i, *_: (i, 0, 0))
    in_w = (D_MODEL, D_FF)
    out_w = (D_FF, D_MODEL)
    grid_spec = pltpu.PrefetchScalarGridSpec(
        num_scalar_prefetch=4,
        grid=(n_tiles,),
        in_specs=[row_block] + [pl.BlockSpec(memory_space=pl.ANY)] * 3,
        out_specs=row_block,
        scratch_shapes=[pltpu.VMEM((2,) + in_w, F32), pltpu.VMEM((2,) + in_w, F32), pltpu.VMEM((2,) + out_w, F32),
                        pltpu.VMEM(in_w, BF16), pltpu.VMEM(in_w, BF16), pltpu.VMEM(out_w, BF16),
                        pltpu.SemaphoreType.DMA((2,))])
    return pl.pallas_call(
        _expert_body,
        grid_spec=grid_spec,
        out_shape=jax.ShapeDtypeStruct(xs.shape, F32),
        compiler_params=_cparams(("arbitrary",)),
        name="moe_experts",
    )(te, nt, seg, nxt, xs, w_gate, w_up, w_down)


def _combine_body(pos_ref, nxt_ref, route_ref, hlat_ref, g2_ref, fn_ref, ys_ref, out_ref, buf_ref, sem):
    tm = COMBINE_TM
    i = pl.program_id(0)
    slot = i % 2

    def fetch(p_ref, s):
        def start_pair(t, p1, p2):
            _row_copy(ys_ref, p1, buf_ref.at[s, 0], t, sem.at[s]).start()
            _row_copy(ys_ref, p2, buf_ref.at[s, 1], t, sem.at[s]).start()
        _issue_rows(tm, p_ref, start_pair)

    @pl.when(i == 0)
    def _():
        fetch(pos_ref, 0)

    @pl.when(i + 1 < pl.num_programs(0))
    def _():
        fetch(nxt_ref, 1 - slot)

    _drain_rows(2 * tm, _row_copy(ys_ref, 0, buf_ref.at[slot, 0], 0, sem.at[slot]))

    r = route_ref[0]
    moe = (r[:, RT_W1:RT_W1 + 1] * _load_row_tiles(buf_ref.at[slot, 0])
           + r[:, RT_W2:RT_W2 + 1] * _load_row_tiles(buf_ref.at[slot, 1]))
    out_ref[0] = _rms(hlat_ref[0] + g2_ref[0] * moe, fn_ref[...])


def _combine(pos, route, hlat, g2, fn_row, ys):
    bn, seq, _ = hlat.shape
    tm = COMBINE_TM
    nj = seq // tm
    nb = tm // LANES
    last = bn * nj - 1
    tok = lambda i: (i // nj, i % nj, 0)
    return pl.pallas_call(
        _combine_body,
        grid=(bn * nj,),
        in_specs=[pl.BlockSpec((nb, 8, LANES), lambda i: (i, 0, 0), memory_space=pltpu.SMEM),
                  pl.BlockSpec((nb, 8, LANES), lambda i: (jnp.minimum(i + 1, last), 0, 0),
                               memory_space=pltpu.SMEM),
                  pl.BlockSpec((1, tm, LANES), tok),
                  pl.BlockSpec((1, tm, D_MODEL), tok),
                  pl.BlockSpec((1, 1, D_MODEL), lambda i: (i // nj, 0, 0)),
                  pl.BlockSpec((1, D_MODEL), lambda i: (0, 0)),
                  pl.BlockSpec(memory_space=pl.ANY)],
        out_specs=pl.BlockSpec((1, tm, D_MODEL), tok),
        out_shape=jax.ShapeDtypeStruct((bn, seq, D_MODEL), F32),
        scratch_shapes=[pltpu.VMEM((2, 2, tm) + ROW_TILE, F32), pltpu.SemaphoreType.DMA((2,))],
        compiler_params=_cparams(("arbitrary",)),
        name="moe_combine",
    )(pos, pos, route, hlat, g2, fn_row, ys)


def _prep_weights(w_in, gla_w_up, gla_b, ssd_dt_bias, ssd_a_log, ssd_d, router_group_w, router_group_b,
                  router_expert_w, router_expert_b):
    off_k = GLA_KEY
    off_v = 2 * GLA_KEY
    off_r = off_v + GLA_VAL
    off_g = off_r + GLA_VAL
    off_z = off_g + 2 * GLA_RANK
    off_xbc = off_z + SSD_INNER
    off_dt = off_xbc + SSD_XBC
    w_main = jnp.concatenate([w_in[:, :off_g], w_in[:, off_z:off_dt]], axis=1).astype(BF16)
    w_small = jnp.concatenate([w_in[:, off_g:off_z], w_in[:, off_dt:off_dt + 2 * SSD_HEADS]], axis=1)
    w_small = jnp.pad(w_small, ((0, 0), (0, LANES - w_small.shape[1]))).astype(BF16)
    wup = jnp.zeros((2, LANES, GLA_KEY), F32)
    wup = wup.at[0, 0:GLA_RANK].set(gla_w_up[0]).at[1, GLA_RANK:2 * GLA_RANK].set(gla_w_up[1]).astype(BF16)
    pad_h = lambda t: jnp.pad(t.reshape(2, SSD_GROUPS, 1, SSD_HPG), ((0, 0), (0, 0), (0, 0), (0, LANES - SSD_HPG)))
    w_router = jnp.zeros((D_MODEL, LANES), F32)
    w_router = w_router.at[:, 0:N_GROUPS].set(router_group_w)
    w_router = w_router.at[:, EXPERT_LANE0:EXPERT_LANE0 + N_EXPERTS].set(router_expert_w).astype(BF16)
    b_router = jnp.zeros((1, LANES), F32).at[0, 0:N_GROUPS].set(router_group_b)
    b_router = b_router.at[0, EXPERT_LANE0:EXPERT_LANE0 + N_EXPERTS].set(router_expert_b)
    return (w_main, w_small, wup, gla_b.reshape(2, 1, GLA_KEY), pad_h(ssd_dt_bias), pad_h(ssd_a_log),
            jnp.repeat(ssd_d, SSD_P)[None, :], w_router, b_router)


def kernel(x, c, ctx, c_ctx, w_ada, b_ada, norm_mix, norm_ffn, w_in, gla_w_up, gla_b, gla_norm, ssd_conv_w,
           ssd_conv_b, ssd_dt_bias, ssd_a_log, ssd_d, ssd_norm, w_out, router_group_w, router_group_b,
           router_expert_w, router_expert_b, expert_w_gate, expert_w_up, expert_w_down, final_norm):
    assert w_ada.shape[0] == 1, "single-layer block"
    bn, seq, d = x.shape
    ctx_len = ctx.shape[1]
    assert seq // GRID_W == GRID_W and d == D_MODEL

    c8 = jnp.concatenate([c, c_ctx[None, :], jnp.zeros((8 - bn - 1, d), F32)], axis=0)
    mods = _adaln(c8, w_ada[0], b_ada[0][None, :])
    sh1, sc1, g1, sh2, sc2, g2 = [mods[:, i * d:(i + 1) * d] for i in range(6)]
    ctx_row = lambda t: jnp.broadcast_to(t[bn:bn + 1], (bn, d))
    mod1 = jnp.stack([ctx_row(sh1), ctx_row(sc1), sh1[:bn], sc1[:bn]], axis=1)
    mod2 = jnp.stack([g1[:bn], sh2[:bn], sc2[:bn], g2[:bn]], axis=1)

    (w_main, w_small, wup, gla_b3, dtb4, alog4, dskip_row, w_router, b_router) = _prep_weights(
        w_in[0], gla_w_up[0], gla_b[0], ssd_dt_bias[0], ssd_a_log[0], ssd_d[0], router_group_w[0],
        router_group_b[0], router_expert_w[0], router_expert_b[0])

    proj, small = _inproj(ctx, x, mod1, norm_mix, w_main, w_small)
    xact = _conv(proj, ssd_conv_w[0].reshape(9, SSD_XBC), ssd_conv_b, ctx_len, seq)
    lvm = jnp.asarray(np.stack([_level_matrix(CHUNK, False), _level_matrix(CHUNK, True)]))
    o_gla = _gla(proj, small, wup, gla_b3, lvm, ctx_len, seq)
    y_ssd = _ssd(xact, small, dtb4, alog4, dskip_row, ctx_len, seq)
    hlat, h2p, route, cnt = _outproj(o_gla, proj, y_ssd, x, mod2, gla_norm, ssd_norm, norm_ffn,
                                     w_out[0].astype(BF16), w_router, b_router, ctx_len)
    n_tok = bn * seq
    pos, te, nt, pad0, padn, seg, nxt = _plan(route.reshape(n_tok, LANES), cnt)
    te, nt = te.reshape(-1), nt.reshape(-1)
    xs = _dispatch(pad0.reshape(-1), padn.reshape(-1), nt, pos, h2p.reshape((n_tok,) + ROW_TILE),
                   _moe_tiles(n_tok) * MOE_TILE)
    ys = _experts(te, nt, seg.reshape(-1), nxt.reshape(-1), xs, expert_w_gate[0], expert_w_up[0], expert_w_down[0])
    return _combine(pos, route, hlat, mod2[:, 3:4, :], final_norm[None, :], ys)
```

```python
import functools
import math

import numpy as np
import jax
import jax.numpy as jnp
from jax import lax
from jax.experimental import pallas as pl
from jax.experimental.pallas import tpu as pltpu

F32 = jnp.float32
BF16 = jnp.bfloat16

D_MODEL = 1024
GRID_W = 64
EPS = 1e-6

GLA_HEADS = 4
GLA_DK = 128
GLA_DV = 256
GLA_KEY = GLA_HEADS * GLA_DK
GLA_VAL = GLA_HEADS * GLA_DV
GLA_RANK = 16
GLA_GATE_NORM = 16.0

SSD_HEADS = 16
SSD_P = 64
SSD_INNER = SSD_HEADS * SSD_P
SSD_GROUPS = 2
SSD_HPG = SSD_HEADS // SSD_GROUPS
SSD_N = 128
SSD_XBC = SSD_INNER + 2 * SSD_GROUPS * SSD_N
SSD_GW = SSD_HPG * SSD_P

N_GROUPS = 4
EXPERTS_PER_GROUP = 8
N_EXPERTS = N_GROUPS * EXPERTS_PER_GROUP
D_FF = 512

LANES = 128
VMEM_LIMIT = 56 * 1024 * 1024

PC_Q = 0
PC_K = PC_Q + GLA_KEY
PC_V = PC_K + GLA_KEY
PC_R = PC_V + GLA_VAL
PC_Z = PC_R + GLA_VAL
PC_XBC = PC_Z + SSD_INNER
PROJ_W = PC_XBC + SSD_XBC
SMALL_DT0 = 2 * GLA_RANK
SMALL_ROT0 = 2
SMALL_W = LANES * (SMALL_ROT0 + 2 * SSD_GROUPS)

CHUNK = 128
EXPERT_LANE0 = 32


def _cparams(sem):
    return pltpu.CompilerParams(dimension_semantics=sem, vmem_limit_bytes=VMEM_LIMIT)


def _silu(x):
    return x / (1.0 + jnp.exp(-x))


def _softplus(x):
    return jnp.maximum(x, 0.0) + jnp.log(1.0 + jnp.exp(-jnp.abs(x)))


def _log_sigmoid(x):
    return jnp.minimum(x, 0.0) - jnp.log(1.0 + jnp.exp(-jnp.abs(x)))


def _bdot(a, b):
    return jnp.dot(a.astype(BF16), b.astype(BF16), preferred_element_type=F32)


def _bdot_nt(a, b):
    return lax.dot_general(a.astype(BF16), b.astype(BF16), (((1,), (1,)), ((), ())),
                           preferred_element_type=F32)


def _adaln_body(c_ref, w_ref, b_ref, o_ref):
    o_ref[...] = _bdot(_silu(c_ref[...]), w_ref[...]) + b_ref[...]


def _adaln(c8, w, b):
    n = w.shape[1]
    bn = 1024
    return pl.pallas_call(
        _adaln_body,
        grid=(n // bn,),
        in_specs=[pl.BlockSpec((8, D_MODEL), lambda j: (0, 0)),
                  pl.BlockSpec((D_MODEL, bn), lambda j: (0, j)),
                  pl.BlockSpec((1, bn), lambda j: (0, j))],
        out_specs=pl.BlockSpec((8, bn), lambda j: (0, j)),
        out_shape=jax.ShapeDtypeStruct((8, n), F32),
        compiler_params=_cparams(("arbitrary",)),
        name="adaln",
    )(c8, w, b)


INPROJ_TM = 256
INPROJ_NC = 512


def _inproj_body(ctx_ref, x_ref, mod_ref, nw_ref, w_ref, ws_ref, proj_ref, small_ref):
    is_ctx = pl.program_id(1) == 0
    xin = jnp.where(is_ctx, ctx_ref[0], x_ref[0])
    m = mod_ref[0]
    shift = jnp.where(is_ctx, m[0:1], m[2:3])
    scale = jnp.where(is_ctx, m[1:2], m[3:4])
    ms = jnp.mean(xin * xin, axis=-1, keepdims=True)
    h = xin * lax.rsqrt(ms + EPS) * nw_ref[...]
    hb = (h * (1.0 + scale) + shift).astype(BF16)
    for n in range(PROJ_W // INPROJ_NC):
        sl = slice(n * INPROJ_NC, (n + 1) * INPROJ_NC)
        proj_ref[0, :, sl] = jnp.dot(hb, w_ref[:, sl], preferred_element_type=F32).astype(BF16)
    sm = jnp.dot(hb, ws_ref[...], preferred_element_type=F32)
    small_ref[0, :, 0:LANES] = sm
    small_ref[0, :, LANES:SMALL_ROT0 * LANES] = jnp.zeros((sm.shape[0], (SMALL_ROT0 - 1) * LANES), F32)
    for k in range(2 * SSD_GROUPS):
        lane0 = SMALL_DT0 + k * SSD_HPG
        small_ref[0, :, (SMALL_ROT0 + k) * LANES:(SMALL_ROT0 + k + 1) * LANES] = pltpu.roll(sm, LANES - lane0, 1)


def _inproj(ctx, x, mod1, norm_w, w_main, w_small):
    bn, seq, _ = x.shape
    ctx_len = ctx.shape[1]
    tm = INPROJ_TM
    assert ctx_len == tm and seq % tm == 0
    tt = ctx_len + seq
    nj = tt // tm
    return pl.pallas_call(
        _inproj_body,
        grid=(bn, nj),
        in_specs=[pl.BlockSpec((1, tm, D_MODEL), lambda b, j: (b, 0, 0)),
                  pl.BlockSpec((1, tm, D_MODEL), lambda b, j: (b, jnp.maximum(j - 1, 0), 0)),
                  pl.BlockSpec((1, 4, D_MODEL), lambda b, j: (b, 0, 0)),
                  pl.BlockSpec((1, D_MODEL), lambda b, j: (0, 0)),
                  pl.BlockSpec((D_MODEL, PROJ_W), lambda b, j: (0, 0)),
                  pl.BlockSpec((D_MODEL, LANES), lambda b, j: (0, 0))],
        out_specs=[pl.BlockSpec((1, tm, PROJ_W), lambda b, j: (b, j, 0)),
                   pl.BlockSpec((1, tm, SMALL_W), lambda b, j: (b, j, 0))],
        out_shape=[jax.ShapeDtypeStruct((bn, tt, PROJ_W), BF16),
                   jax.ShapeDtypeStruct((bn, tt, SMALL_W), F32)],
        compiler_params=_cparams(("arbitrary", "arbitrary")),
        name="inproj",
    )(ctx, x, mod1, norm_w, w_main, w_small)


CONV_CB = 256
CONV_TT = 256


def _conv_body(ctx_len, seq, p_ref, w_ref, b_ref, o_ref, s_ref):
    lat0 = ctx_len + GRID_W
    tt = CONV_TT
    n_tiles = seq // tt
    zeros = jnp.zeros((GRID_W, CONV_CB), F32)
    s_ref[0:ctx_len, :] = p_ref[0, 0:ctx_len, :].astype(F32)
    s_ref[ctx_len:lat0, :] = zeros
    s_ref[lat0 + seq:lat0 + seq + GRID_W, :] = zeros

    def copy_tile(t, c):
        src = pl.multiple_of(ctx_len + t * tt, tt)
        dst = pl.multiple_of(lat0 + t * tt, GRID_W)
        s_ref[pl.ds(dst, tt), :] = p_ref[0, pl.ds(src, tt), :].astype(F32)
        return c

    lax.fori_loop(0, n_tiles, copy_tile, 0)

    w = w_ref[...]
    bias = b_ref[...]
    pos = lax.broadcasted_iota(jnp.int32, (tt, 1), 0)

    def conv_rows(rows, first, last):
        left, mid, right = [sum(xr * w[3 * kh + kw:3 * kh + kw + 1] for xr, kh in rows) for kw in range(3)]
        return (bias + mid + jnp.where(first, 0.0, pltpu.roll(left, 1, 0))
                + jnp.where(last, 0.0, pltpu.roll(right, tt - 1, 0)))

    acc = conv_rows([(s_ref[0:ctx_len, :], 1)], pos == 0, pos == ctx_len - 1)
    o_ref[0, 0:ctx_len, :] = _silu(acc).astype(BF16)

    col = pos % GRID_W
    first = col == 0
    last = col == GRID_W - 1

    def tile(t, c):
        base = lat0 + t * tt
        rows = [(s_ref[pl.ds(pl.multiple_of(base + (kh - 1) * GRID_W, GRID_W), tt), :], kh) for kh in range(3)]
        dst = pl.multiple_of(ctx_len + t * tt, tt)
        o_ref[0, pl.ds(dst, tt), :] = _silu(conv_rows(rows, first, last)).astype(BF16)
        return c

    lax.fori_loop(0, n_tiles, tile, 0)


def _conv(proj, conv_w9, conv_b, ctx_len, seq):
    bn, tt, _ = proj.shape
    assert ctx_len == CONV_TT and seq % CONV_TT == 0 and CONV_TT % GRID_W == 0
    ncb = SSD_XBC // CONV_CB
    cb0 = PC_XBC // CONV_CB
    return pl.pallas_call(
        functools.partial(_conv_body, ctx_len, seq),
        grid=(bn, ncb),
        in_specs=[pl.BlockSpec((1, tt, CONV_CB), lambda b, j: (b, 0, cb0 + j)),
                  pl.BlockSpec((9, CONV_CB), lambda b, j: (0, j)),
                  pl.BlockSpec((1, CONV_CB), lambda b, j: (0, j))],
        out_specs=pl.BlockSpec((1, tt, CONV_CB), lambda b, j: (b, 0, j)),
        out_shape=jax.ShapeDtypeStruct((bn, tt, SSD_XBC), BF16),
        scratch_shapes=[pltpu.VMEM((tt + 2 * GRID_W, CONV_CB), F32)],
        compiler_params=_cparams(("arbitrary", "arbitrary")),
        name="conv",
    )(proj, conv_w9, conv_b)


def _bwd_chunk(s, n_ctx, n_steps):
    return jnp.where(s < n_ctx, n_ctx - 1 - s, n_steps + n_ctx - 1 - s)


def _out_row(c, n_ctx, n, seq):
    return pl.multiple_of(jnp.where(c >= n_ctx, (c - n_ctx) * n, seq), n)


def _scan_sum(a, rev):
    n = a.shape[0]
    row = lax.broadcasted_iota(jnp.int32, (n, 1), 0)
    sh = 1
    while sh < n:
        if rev:
            a = a + jnp.where(row < n - sh, pltpu.roll(a, n - sh, 0), 0.0)
        else:
            a = a + jnp.where(row >= sh, pltpu.roll(a, sh, 0), 0.0)
        sh *= 2
    return a


def _level_matrix(n, rev):
    i = np.arange(n)[:, None]
    j = np.arange(n)[None, :]
    x = i ^ j
    lv = np.where(x > 0, np.floor(np.log2(np.maximum(x, 1))).astype(np.int64) + 1, 0)
    earlier = (j > i) if rev else (j < i)
    return np.where(earlier, lv, 0).astype(np.int32)


GLA_HPS = 4


def _gla_dir(rev, dst, q_ref, k_ref, v_ref, sm_ref, wup, gb, lv, o_ref, s_ref):
    n = CHUNK
    q = q_ref[0].astype(F32) * (GLA_DK ** -0.5)
    k = k_ref[0].astype(F32)
    x = jnp.dot(sm_ref[0].astype(BF16), wup, preferred_element_type=F32) + gb
    b = _scan_sum(_log_sigmoid(x) / GLA_GATE_NORM, rev)
    row = lax.broadcasted_iota(jnp.int32, (n, 1), 0)
    b_end = b[0:1] if rev else b[n - 1:n]
    k_end = k * jnp.exp(b_end - b)
    q_in = q * jnp.exp(b)
    dec_end = jnp.exp(b_end)

    att = [jnp.zeros((n, n), F32) for _ in range(GLA_HPS)]
    e_h = b
    h = 1
    level = 1
    while h < n:
        upper = (row & h) != 0
        if rev:
            e = jnp.where(upper, e_h - b, b - pltpu.roll(e_h, n - h, 0))
        else:
            e = jnp.where(upper, b - pltpu.roll(e_h, h, 0), e_h - b)
        w = jnp.exp(e)
        qw = (q * w).astype(BF16)
        kw = (k * w).astype(BF16)
        mask = lv == level
        for hh in range(GLA_HPS):
            ks = slice(hh * GLA_DK, (hh + 1) * GLA_DK)
            att[hh] = att[hh] + jnp.where(mask, _bdot_nt(qw[:, ks], kw[:, ks]), 0.0)
        if 2 * h < n:
            if rev:
                e_h = jnp.where(upper, pltpu.roll(e_h, h, 0), e_h)
            else:
                e_h = jnp.where(upper, e_h, pltpu.roll(e_h, n - h, 0))
        h *= 2
        level += 1

    for hh in range(GLA_HPS):
        ks = slice(hh * GLA_DK, (hh + 1) * GLA_DK)
        vs = slice(hh * GLA_DV, (hh + 1) * GLA_DV)
        v = v_ref[0, :, vs]
        s_old = s_ref[hh]
        o = jnp.dot(att[hh].astype(BF16), v, preferred_element_type=F32)
        o = o + jnp.sum(q[:, ks] * k[:, ks], axis=-1, keepdims=True) * v.astype(F32)
        o = o + _bdot(q_in[:, ks], s_old)
        o_ref[0, pl.ds(dst, n), vs] += o
        dec_col = jnp.broadcast_to(dec_end[:, ks], (n, GLA_DK)).T[:, 0:1]
        s_ref[hh] = dec_col * s_old + jnp.dot(k_end[:, ks].T.astype(BF16), v, preferred_element_type=F32)


def _gla_body(n_ctx, seq, qf, kf, vf, smf, qb, kb, vb, smb, wup_ref, gb_ref, lv_ref, o_ref, s_ref):
    step = pl.program_id(2)
    n = CHUNK

    @pl.when(step == 0)
    def _():
        s_ref[...] = jnp.zeros_like(s_ref)
        o_ref[...] = jnp.zeros_like(o_ref)

    cb = _bwd_chunk(step, n_ctx, pl.num_programs(2))
    _gla_dir(False, _out_row(step, n_ctx, n, seq), qf, kf, vf, smf, wup_ref[0], gb_ref[0], lv_ref[0],
             o_ref, s_ref.at[0])
    _gla_dir(True, _out_row(cb, n_ctx, n, seq), qb, kb, vb, smb, wup_ref[1], gb_ref[1], lv_ref[1],
             o_ref, s_ref.at[1])


def _gla(proj, small, wup_pad, gla_b3, lvm, ctx_len, seq):
    bn, tt, _ = proj.shape
    n = CHUNK
    assert ctx_len % n == 0 and seq % n == 0
    n_ctx = ctx_len // n
    n_steps = tt // n
    hw = GLA_HPS * GLA_DK
    vw = GLA_HPS * GLA_DV
    kb = PC_K // hw
    vb = PC_V // vw
    fwd = lambda s: s
    bwd = lambda s: _bwd_chunk(s, n_ctx, n_steps)

    def chunk_specs(c):
        return [pl.BlockSpec((1, n, hw), lambda b, h, s: (b, c(s), h)),
                pl.BlockSpec((1, n, hw), lambda b, h, s: (b, c(s), kb + h)),
                pl.BlockSpec((1, n, vw), lambda b, h, s: (b, c(s), vb + h)),
                pl.BlockSpec((1, n, LANES), lambda b, h, s: (b, c(s), 0))]

    return pl.pallas_call(
        functools.partial(_gla_body, n_ctx, seq),
        grid=(bn, GLA_HEADS // GLA_HPS, n_steps),
        in_specs=chunk_specs(fwd) + chunk_specs(bwd) + [
            pl.BlockSpec((2, LANES, hw), lambda b, h, s: (0, 0, h)),
            pl.BlockSpec((2, 1, hw), lambda b, h, s: (0, 0, h)),
            pl.BlockSpec((2, n, n), lambda b, h, s: (0, 0, 0))],
        out_specs=pl.BlockSpec((1, seq + n, vw), lambda b, h, s: (b, 0, h)),
        out_shape=jax.ShapeDtypeStruct((bn, seq + n, GLA_VAL), F32),
        scratch_shapes=[pltpu.VMEM((2, GLA_HPS, GLA_DK, GLA_DV), F32)],
        compiler_params=_cparams(("arbitrary",) * 3),
        name="gla",
    )(proj, proj, proj, small, proj, proj, proj, small, wup_pad, gla_b3, lvm)


def _ssd_dir(rev, first, g, dst, x_ref, bm_ref, cm_ref, sm_ref, dtb, alog, dsk_ref, y_ref, s_ref):
    n = CHUNK
    gl = slice(g * LANES, (g + 1) * LANES)
    g0 = g * SSD_GW
    lane = lax.broadcasted_iota(jnp.int32, (1, LANES), 1)
    dt = _softplus(sm_ref[0, :, gl] + dtb)
    a_neg = jnp.where(lane < SSD_HPG, -jnp.exp(alog), 0.0)
    acum = _scan_sum(dt * a_neg, rev)
    acum_t = acum.T
    cm = cm_ref[0, :, gl]
    bm_t = bm_ref[0, :, gl].astype(F32).T.astype(BF16)
    ii = lax.broadcasted_iota(jnp.int32, (n, n), 0)
    jj = lax.broadcasted_iota(jnp.int32, (n, n), 1)
    causal = (jj >= ii) if rev else (jj <= ii)
    cb = _bdot_nt(cm, bm_ref[0, :, gl])
    low = lane < SSD_P
    for p in range(SSD_HPG // 2):
        slab = slice(p * LANES, (p + 1) * LANES)
        out = slice(g0 + p * LANES, g0 + (p + 1) * LANES)
        h0, h1 = 2 * p, 2 * p + 1
        col0 = jnp.broadcast_to(acum[:, h0:h0 + 1], (n, LANES))
        col1 = jnp.broadcast_to(acum[:, h1:h1 + 1], (n, LANES))
        m0 = (cb * jnp.exp(jnp.where(causal, col0 - acum_t[h0:h0 + 1, :], -jnp.inf))).astype(BF16)
        m1 = (cb * jnp.exp(jnp.where(causal, col1 - acum_t[h1:h1 + 1, :], -jnp.inf))).astype(BF16)
        col = jnp.where(low, col0, col1)
        dt2 = jnp.where(low, jnp.broadcast_to(dt[:, h0:h0 + 1], (n, LANES)),
                        jnp.broadcast_to(dt[:, h1:h1 + 1], (n, LANES)))
        end = col[0:1] if rev else col[n - 1:n]
        x = x_ref[0, :, out].astype(F32)
        s_old = s_ref[:, slab]
        xdt = x * dt2
        rhs = jnp.concatenate([jnp.where(low, xdt, 0.0), jnp.where(low, 0.0, xdt)], axis=0).astype(BF16)
        y = jnp.dot(jnp.concatenate([m0, m1], axis=1), rhs, preferred_element_type=F32)
        y = y + jnp.dot(cm, s_old.astype(BF16), preferred_element_type=F32) * jnp.exp(col)
        if first:
            y = y + x * dsk_ref[:, out]
        y_ref[0, pl.ds(dst, n), out] += y
        new = jnp.dot(bm_t, (xdt * jnp.exp(end - col)).astype(BF16), preferred_element_type=F32)
        s_ref[:, slab] = s_old * jnp.exp(end) + new


def _ssd_body(n_ctx, seq, xf, bf, cf, smf, xb, bb, cbk, smb, dtb_ref, alog_ref, dsk_ref, y_ref, s_ref):
    step = pl.program_id(1)
    n = CHUNK

    @pl.when(step == 0)
    def _():
        s_ref[...] = jnp.zeros_like(s_ref)
        y_ref[...] = jnp.zeros_like(y_ref)

    dst_f = _out_row(step, n_ctx, n, seq)
    dst_b = _out_row(_bwd_chunk(step, n_ctx, pl.num_programs(1)), n_ctx, n, seq)
    for g in range(SSD_GROUPS):
        _ssd_dir(False, True, g, dst_f, xf, bf, cf, smf, dtb_ref[0, g], alog_ref[0, g], dsk_ref, y_ref,
                 s_ref.at[0, g])
        _ssd_dir(True, False, g, dst_b, xb, bb, cbk, smb, dtb_ref[1, g], alog_ref[1, g], dsk_ref, y_ref,
                 s_ref.at[1, g])


def _ssd(xact, small, dtb4, alog4, dskip_row, ctx_len, seq):
    bn, tt, _ = xact.shape
    n = CHUNK
    n_ctx = ctx_len // n
    n_steps = tt // n
    gn = SSD_GROUPS * SSD_N
    assert SSD_INNER % gn == 0
    bblk = SSD_INNER // gn
    fwd = lambda s: s
    bwd = lambda s: _bwd_chunk(s, n_ctx, n_steps)

    assert SMALL_ROT0 % SSD_GROUPS == 0
    sblk = SMALL_ROT0 // SSD_GROUPS

    def chunk_specs(c, d):
        return [pl.BlockSpec((1, n, SSD_INNER), lambda b, s: (b, c(s), 0)),
                pl.BlockSpec((1, n, gn), lambda b, s: (b, c(s), bblk)),
                pl.BlockSpec((1, n, gn), lambda b, s: (b, c(s), bblk + 1)),
                pl.BlockSpec((1, n, SSD_GROUPS * LANES), lambda b, s: (b, c(s), sblk + d))]

    whole4 = pl.BlockSpec((2, SSD_GROUPS, 1, LANES), lambda b, s: (0, 0, 0, 0))
    return pl.pallas_call(
        functools.partial(_ssd_body, n_ctx, seq),
        grid=(bn, n_steps),
        in_specs=chunk_specs(fwd, 0) + chunk_specs(bwd, 1) + [
            whole4, whole4, pl.BlockSpec((1, SSD_INNER), lambda b, s: (0, 0))],
        out_specs=pl.BlockSpec((1, seq + n, SSD_INNER), lambda b, s: (b, 0, 0)),
        out_shape=jax.ShapeDtypeStruct((bn, seq + n, SSD_INNER), F32),
        scratch_shapes=[pltpu.VMEM((2, SSD_GROUPS, SSD_N, SSD_GW), F32)],
        compiler_params=_cparams(("arbitrary",) * 2),
        name="ssd",
    )(xact, xact, xact, small, xact, xact, xact, small, dtb4, alog4, dskip_row)


OUT_TM = 256
RT_E1, RT_E2, RT_RANK1, RT_RANK2, RT_W1, RT_W2 = range(6)


def _rms(x, w):
    return x * lax.rsqrt(jnp.mean(x * x, axis=-1, keepdims=True) + EPS) * w


ROW_TILE = (D_MODEL // LANES, LANES)


def _store_row_tiles(ref, x):
    ref[...] = x.reshape((x.shape[0],) + ROW_TILE)


def _load_row_tiles(ref):
    return ref[...].reshape(ref.shape[0], D_MODEL)


def _outproj_body(o_ref, r_ref, y_ref, z_ref, x_ref, mod_ref, gn_ref, sn_ref, fn_ref, wo_ref, wr_ref, br_ref,
                  hlat_ref, h2_ref, route_ref, cnt_ref, carry_ref):
    tm = OUT_TM

    @pl.when((pl.program_id(0) == 0) & (pl.program_id(1) == 0))
    def _():
        carry_ref[...] = jnp.zeros_like(carry_ref)

    parts = []
    for h in range(GLA_HEADS):
        hs = slice(h * GLA_DV, (h + 1) * GLA_DV)
        parts.append((_rms(o_ref[0, :, hs], gn_ref[...]) * _silu(r_ref[0, :, hs].astype(F32))).astype(BF16))
    for g in range(SSD_GROUPS):
        gs = slice(g * SSD_GW, (g + 1) * SSD_GW)
        yg = y_ref[0, :, gs] * _silu(z_ref[0, :, gs].astype(F32))
        parts.append(_rms(yg, sn_ref[:, gs]).astype(BF16))
    mix = jnp.concatenate(parts, axis=-1)
    m = mod_ref[0]
    hlat = x_ref[0] + m[0:1] * jnp.dot(mix, wo_ref[...], preferred_element_type=F32)
    hlat_ref[0] = hlat
    h2f = _rms(hlat, fn_ref[...]) * (1.0 + m[2:3]) + m[1:2]
    _store_row_tiles(h2_ref.at[0], h2f)

    lg = jnp.dot(h2f.astype(BF16), wr_ref[...], preferred_element_type=F32) + br_ref[...]
    lanef = lax.broadcasted_iota(jnp.int32, (tm, LANES), 1).astype(F32)
    ninf = -jnp.inf
    is_g = lanef < float(N_GROUPS)
    gl = jnp.where(is_g, lg, ninf)
    gmax = jnp.max(gl, axis=-1, keepdims=True)
    gsel = jnp.min(jnp.where(gl == gmax, lanef, float(LANES)), axis=-1, keepdims=True)
    pg = 1.0 / jnp.sum(jnp.where(is_g, jnp.exp(lg - gmax), 0.0), axis=-1, keepdims=True)
    lo = float(EXPERT_LANE0) + float(EXPERTS_PER_GROUP) * gsel
    in_grp = (lanef >= lo) & (lanef < lo + float(EXPERTS_PER_GROUP))
    el = jnp.where(in_grp, lg, ninf)
    v1 = jnp.max(el, axis=-1, keepdims=True)
    i1 = jnp.min(jnp.where(el == v1, lanef, float(LANES)), axis=-1, keepdims=True)
    el2 = jnp.where(lanef == i1, ninf, el)
    v2 = jnp.max(el2, axis=-1, keepdims=True)
    i2 = jnp.min(jnp.where(el2 == v2, lanef, float(LANES)), axis=-1, keepdims=True)
    t = jnp.exp(v2 - v1)
    w1 = pg / (1.0 + t)
    w2 = pg * t / (1.0 + t)

    sel1 = lanef == i1
    sel2 = lanef == i2
    member = jnp.where(sel1 | sel2, 1.0, 0.0)
    ii = lax.broadcasted_iota(jnp.int32, (tm, tm), 0)
    jj = lax.broadcasted_iota(jnp.int32, (tm, tm), 1)
    before = jnp.where(jj < ii, 1.0, 0.0).astype(BF16)
    ranks = jnp.dot(before, member.astype(BF16), preferred_element_type=F32) + carry_ref[...]
    rank1 = jnp.sum(jnp.where(sel1, ranks, 0.0), axis=-1, keepdims=True)
    rank2 = jnp.sum(jnp.where(sel2, ranks, 0.0), axis=-1, keepdims=True)
    carry = carry_ref[...] + jnp.sum(member, axis=0, keepdims=True)
    carry_ref[...] = carry
    cnt_ref[...] = carry
    rec = jnp.zeros((tm, LANES), F32)
    for lane_id, val in ((RT_E1, i1 - float(EXPERT_LANE0)), (RT_E2, i2 - float(EXPERT_LANE0)),
                         (RT_RANK1, rank1), (RT_RANK2, rank2), (RT_W1, w1), (RT_W2, w2)):
        rec = jnp.where(lanef == float(lane_id), val, rec)
    route_ref[0] = rec


def _outproj(o_gla, proj, y_ssd, x, mod2, gn_row, sn_row, fn_row, w_out, w_router, b_router, ctx_len):
    bn, seq, _ = x.shape
    tm = OUT_TM
    assert ctx_len % tm == 0 and seq % tm == 0
    j0 = ctx_len // tm
    rb = PC_R // GLA_VAL
    zb = PC_Z // SSD_INNER
    tok = lambda b, j: (b, j, 0)
    const = lambda b, j: (0, 0)
    return pl.pallas_call(
        _outproj_body,
        grid=(bn, seq // tm),
        in_specs=[pl.BlockSpec((1, tm, GLA_VAL), tok),
                  pl.BlockSpec((1, tm, GLA_VAL), lambda b, j: (b, j0 + j, rb)),
                  pl.BlockSpec((1, tm, SSD_INNER), tok),
                  pl.BlockSpec((1, tm, SSD_INNER), lambda b, j: (b, j0 + j, zb)),
                  pl.BlockSpec((1, tm, D_MODEL), tok),
                  pl.BlockSpec((1, 4, D_MODEL), lambda b, j: (b, 0, 0)),
                  pl.BlockSpec((1, GLA_DV), const),
                  pl.BlockSpec((1, SSD_INNER), const),
                  pl.BlockSpec((1, D_MODEL), const),
                  pl.BlockSpec((GLA_VAL + SSD_INNER, D_MODEL), const),
                  pl.BlockSpec((D_MODEL, LANES), const),
                  pl.BlockSpec((1, LANES), const)],
        out_specs=[pl.BlockSpec((1, tm, D_MODEL), tok),
                   pl.BlockSpec((1, tm) + ROW_TILE, lambda b, j: (b, j, 0, 0)),
                   pl.BlockSpec((1, tm, LANES), tok),
                   pl.BlockSpec((1, LANES), const)],
        out_shape=[jax.ShapeDtypeStruct((bn, seq, D_MODEL), F32),
                   jax.ShapeDtypeStruct((bn, seq) + ROW_TILE, F32),
                   jax.ShapeDtypeStruct((bn, seq, LANES), F32),
                   jax.ShapeDtypeStruct((1, LANES), F32)],
        scratch_shapes=[pltpu.VMEM((1, LANES), F32)],
        compiler_params=_cparams(("arbitrary", "arbitrary")),
        name="outproj",
    )(o_gla, proj, y_ssd, proj, x, mod2, gn_row, sn_row, fn_row, w_out, w_router, b_router)


MOE_TILE = 256
PLAN_TM = 1024
DISPATCH_TM = 1024
COMBINE_TM = 256


def _moe_tiles(n_tokens):
    return (2 * n_tokens) // MOE_TILE + N_EXPERTS


def _plan_body(nt_pad, route_ref, cnt_ref, pos_ref, te_ref, nt_ref, pad0_ref, padn_ref, seg_ref, nxt_ref):
    lane = lax.broadcasted_iota(jnp.int32, (1, LANES), 1).astype(F32)
    is_e = (lane >= float(EXPERT_LANE0)) & (lane < float(EXPERT_LANE0 + N_EXPERTS))
    tiles = jnp.where(is_e, jnp.floor((cnt_ref[...] + float(MOE_TILE - 1)) / float(MOE_TILE)), 0.0)
    ii = lax.broadcasted_iota(jnp.int32, (LANES, LANES), 0)
    jj = lax.broadcasted_iota(jnp.int32, (LANES, LANES), 1)
    lower = jnp.where(ii < jj, 1.0, 0.0).astype(BF16)
    first_tile = jnp.dot(jnp.broadcast_to(tiles, (8, LANES)).astype(BF16), lower,
                         preferred_element_type=F32)[0:1]
    off_row = first_tile * float(MOE_TILE)
    for sb in range(PLAN_TM // LANES):
        r = route_ref[sb * LANES:(sb + 1) * LANES, :]
        o1 = jnp.sum(jnp.where(lane == r[:, RT_E1:RT_E1 + 1] + float(EXPERT_LANE0), off_row, 0.0),
                     axis=-1, keepdims=True)
        o2 = jnp.sum(jnp.where(lane == r[:, RT_E2:RT_E2 + 1] + float(EXPERT_LANE0), off_row, 0.0),
                     axis=-1, keepdims=True)
        p = jnp.where(lane == 0.0, o1 + r[:, RT_RANK1:RT_RANK1 + 1],
                      jnp.where(lane == 1.0, o2 + r[:, RT_RANK2:RT_RANK2 + 1], 0.0))
        pos_ref[sb] = p.T[0:8, :].astype(jnp.int32)

    @pl.when(pl.program_id(0) == 0)
    def _():
        ti = lax.broadcasted_iota(jnp.int32, (nt_pad, 1), 0).astype(F32)
        done = jnp.where(is_e & (first_tile + tiles <= ti), 1.0, 0.0)
        te = jnp.minimum(jnp.sum(done, axis=-1, keepdims=True), float(N_EXPERTS - 1))
        te_ref[...] = te.astype(jnp.int32)
        nt_ref[...] = jnp.sum(tiles, axis=-1, keepdims=True).astype(jnp.int32)
        lane_e = lane - float(EXPERT_LANE0)
        nonempty = is_e & (tiles > 0.0)
        seg_ref[...] = jnp.sum(jnp.where(nonempty & (lane_e < te), 1.0, 0.0), axis=-1,
                               keepdims=True).astype(jnp.int32)
        nxt = jnp.min(jnp.where(nonempty & (lane_e > te), lane_e, float(N_EXPERTS)), axis=-1, keepdims=True)
        nxt_ref[...] = jnp.where(nxt < float(N_EXPERTS), nxt, -1.0).astype(jnp.int32)
        used = jnp.where(is_e, cnt_ref[...], 0.0)
        pad0_ref[...] = (off_row + used).astype(jnp.int32)
        padn_ref[...] = (tiles * float(MOE_TILE) - used).astype(jnp.int32)


def _plan(route, cnt):
    n_tok = route.shape[0]
    nt_pad = _moe_tiles(n_tok)
    nsb = PLAN_TM // LANES
    return pl.pallas_call(
        functools.partial(_plan_body, nt_pad),
        grid=(n_tok // PLAN_TM,),
        in_specs=[pl.BlockSpec((PLAN_TM, LANES), lambda i: (i, 0)),
                  pl.BlockSpec((1, LANES), lambda i: (0, 0))],
        out_specs=[pl.BlockSpec((nsb, 8, LANES), lambda i: (i, 0, 0)),
                   pl.BlockSpec((nt_pad, 1), lambda i: (0, 0)),
                   pl.BlockSpec((1, 1), lambda i: (0, 0)),
                   pl.BlockSpec((1, LANES), lambda i: (0, 0)),
                   pl.BlockSpec((1, LANES), lambda i: (0, 0)),
                   pl.BlockSpec((nt_pad, 1), lambda i: (0, 0)),
                   pl.BlockSpec((nt_pad, 1), lambda i: (0, 0))],
        out_shape=[jax.ShapeDtypeStruct((n_tok // LANES, 8, LANES), jnp.int32),
                   jax.ShapeDtypeStruct((nt_pad, 1), jnp.int32),
                   jax.ShapeDtypeStruct((1, 1), jnp.int32),
                   jax.ShapeDtypeStruct((1, LANES), jnp.int32),
                   jax.ShapeDtypeStruct((1, LANES), jnp.int32),
                   jax.ShapeDtypeStruct((nt_pad, 1), jnp.int32),
                   jax.ShapeDtypeStruct((nt_pad, 1), jnp.int32)],
        compiler_params=_cparams(("arbitrary",)),
        name="moe_plan",
    )(route, cnt)


def _row_copy(src, s_row, dst, d_row, sem):
    return pltpu.make_async_copy(src.at[s_row], dst.at[d_row], sem)


def _issue_rows(n_rows, pos_ref, start_pair):
    for blk in range(n_rows // LANES):
        def issue(l, c, blk=blk):
            start_pair(blk * LANES + l, pos_ref[blk, 0, l], pos_ref[blk, 1, l])
            return c
        lax.fori_loop(0, LANES, issue, 0, unroll=8)


def _drain_rows(n_copies, copy):
    def drain(t, c):
        copy.wait()
        return c
    lax.fori_loop(0, n_copies, drain, 0, unroll=8)


def _dispatch_body(pad0_ref, padn_ref, nt_ref, pos_ref, h2_ref, xs_ref, zero_ref, sem):
    def start_pair(t, p1, p2):
        _row_copy(h2_ref, t, xs_ref, p1, sem).start(priority=0)
        _row_copy(h2_ref, t, xs_ref, p2, sem).start(priority=1)

    _issue_rows(DISPATCH_TM, pos_ref, start_pair)
    _drain_rows(2 * DISPATCH_TM, _row_copy(h2_ref, 0, xs_ref, 0, sem))

    @pl.when(pl.program_id(0) == 0)
    def _():
        zero_ref[...] = jnp.zeros_like(zero_ref)
        n_tiles = xs_ref.shape[0] // MOE_TILE

        def tile_copy(i):
            return pltpu.make_async_copy(zero_ref, xs_ref.at[pl.ds(i * MOE_TILE, MOE_TILE)], sem)

        def fill_tile(i, c):
            tile_copy(i).start()
            return c

        def drain_tile(i, c):
            tile_copy(i).wait()
            return c

        lax.fori_loop(nt_ref[0], n_tiles, fill_tile, 0)
        lax.fori_loop(nt_ref[0], n_tiles, drain_tile, 0)
        for e in range(N_EXPERTS):
            n_pad = padn_ref[EXPERT_LANE0 + e]
            for wait in (False, True):
                row = pad0_ref[EXPERT_LANE0 + e]
                size = MOE_TILE // 2
                while size >= 1:
                    has = (n_pad & size) != 0
                    copy = pltpu.make_async_copy(zero_ref.at[pl.ds(0, size)], xs_ref.at[pl.ds(row, size)], sem)
                    pl.when(has)(copy.wait if wait else copy.start)
                    row = row + jnp.where(has, size, 0)
                    size //= 2


def _dispatch(pad0, padn, nt, pos, h2t, n_rows):
    n_tok = h2t.shape[0]
    nb = DISPATCH_TM // LANES
    grid_spec = pltpu.PrefetchScalarGridSpec(
        num_scalar_prefetch=3,
        grid=(n_tok // DISPATCH_TM,),
        in_specs=[pl.BlockSpec((nb, 8, LANES), lambda i, *_: (i, 0, 0), memory_space=pltpu.SMEM),
                  pl.BlockSpec((DISPATCH_TM,) + ROW_TILE, lambda i, *_: (i, 0, 0))],
        out_specs=pl.BlockSpec(memory_space=pl.ANY),
        scratch_shapes=[pltpu.VMEM((MOE_TILE,) + ROW_TILE, F32), pltpu.SemaphoreType.DMA(())])
    return pl.pallas_call(
        _dispatch_body,
        grid_spec=grid_spec,
        out_shape=jax.ShapeDtypeStruct((n_rows,) + ROW_TILE, F32),
        compiler_params=_cparams(("arbitrary",)),
        name="moe_dispatch",
    )(pad0, padn, nt, pos, h2t)


def _expert_body(te_ref, nt_ref, seg_ref, nxt_ref, xs_ref, wg_ref, wu_ref, wd_ref, ys_ref,
                 wgf_ref, wuf_ref, wdf_ref, wgb_ref, wub_ref, wdb_ref, sem):
    i = pl.program_id(0)

    def copies(e, s):
        return [pltpu.make_async_copy(src.at[e], dst.at[s], sem.at[s])
                for src, dst in ((wg_ref, wgf_ref), (wu_ref, wuf_ref), (wd_ref, wdf_ref))]

    @pl.when(i < nt_ref[0])
    def _():
        seg = seg_ref[i]
        slot = seg % 2
        first = (i == 0) | (seg != seg_ref[jnp.maximum(i - 1, 0)])

        @pl.when(first)
        def _():
            @pl.when(i == 0)
            def _():
                for c in copies(te_ref[i], slot):
                    c.start()

            for c in copies(te_ref[i], slot):
                c.wait()
            wgb_ref[...] = wgf_ref[slot].astype(BF16)
            wub_ref[...] = wuf_ref[slot].astype(BF16)
            wdb_ref[...] = wdf_ref[slot].astype(BF16)

            @pl.when(nxt_ref[i] >= 0)
            def _():
                for c in copies(nxt_ref[i], 1 - slot):
                    c.start()

        x = _load_row_tiles(xs_ref).astype(BF16)
        gate = jnp.dot(x, wgb_ref[...], preferred_element_type=F32)
        up = jnp.dot(x, wub_ref[...], preferred_element_type=F32)
        y = jnp.dot((_silu(gate) * up).astype(BF16), wdb_ref[...], preferred_element_type=F32)
        _store_row_tiles(ys_ref, y)

    @pl.when(i >= nt_ref[0])
    def _():
        ys_ref[...] = jnp.zeros_like(ys_ref)


def _experts(te, nt, seg, nxt, xs, w_gate, w_up, w_down):
    rows = xs.shape[0]
    n_tiles = rows // MOE_TILE
    row_block = pl.BlockSpec((MOE_TILE,) + ROW_TILE, lambda i, *_: (i, 0, 0))
    in_w = (D_MODEL, D_FF)
    out_w = (D_FF, D_MODEL)
    grid_spec = pltpu.PrefetchScalarGridSpec(
        num_scalar_prefetch=4,
        grid=(n_tiles,),
        in_specs=[row_block] + [pl.BlockSpec(memory_space=pl.ANY)] * 3,
        out_specs=row_block,
        scratch_shapes=[pltpu.VMEM((2,) + in_w, F32), pltpu.VMEM((2,) + in_w, F32), pltpu.VMEM((2,) + out_w, F32),
                        pltpu.VMEM(in_w, BF16), pltpu.VMEM(in_w, BF16), pltpu.VMEM(out_w, BF16),
                        pltpu.SemaphoreType.DMA((2,))])
    return pl.pallas_call(
        _expert_body,
        grid_spec=grid_spec,
        out_shape=jax.ShapeDtypeStruct(xs.shape, F32),
        compiler_params=_cparams(("arbitrary",)),
        name="moe_experts",
    )(te, nt, seg, nxt, xs, w_gate, w_up, w_down)


def _combine_body(pos_ref, nxt_ref, route_ref, hlat_ref, g2_ref, fn_ref, ys_ref, out_ref, buf_ref, sem):
    tm = COMBINE_TM
    i = pl.program_id(0)
    slot = i % 2

    def fetch(p_ref, s):
        def start_pair(t, p1, p2):
            _row_copy(ys_ref, p1, buf_ref.at[s, 0], t, sem.at[s]).start(priority=0)
            _row_copy(ys_ref, p2, buf_ref.at[s, 1], t, sem.at[s]).start(priority=1)
        _issue_rows(tm, p_ref, start_pair)

    @pl.when(i == 0)
    def _():
        fetch(pos_ref, 0)

    @pl.when(i + 1 < pl.num_programs(0))
    def _():
        fetch(nxt_ref, 1 - slot)

    _drain_rows(2 * tm, _row_copy(ys_ref, 0, buf_ref.at[slot, 0], 0, sem.at[slot]))

    r = route_ref[0]
    moe = (r[:, RT_W1:RT_W1 + 1] * _load_row_tiles(buf_ref.at[slot, 0])
           + r[:, RT_W2:RT_W2 + 1] * _load_row_tiles(buf_ref.at[slot, 1]))
    out_ref[0] = _rms(hlat_ref[0] + g2_ref[0] * moe, fn_ref[...])


def _combine(pos, route, hlat, g2, fn_row, ys):
    bn, seq, _ = hlat.shape
    tm = COMBINE_TM
    nj = seq // tm
    nb = tm // LANES
    last = bn * nj - 1
    tok = lambda i: (i // nj, i % nj, 0)
    return pl.pallas_call(
        _combine_body,
        grid=(bn * nj,),
        in_specs=[pl.BlockSpec((nb, 8, LANES), lambda i: (i, 0, 0), memory_space=pltpu.SMEM),
                  pl.BlockSpec((nb, 8, LANES), lambda i: (jnp.minimum(i + 1, last), 0, 0),
                               memory_space=pltpu.SMEM),
                  pl.BlockSpec((1, tm, LANES), tok),
                  pl.BlockSpec((1, tm, D_MODEL), tok),
                  pl.BlockSpec((1, 1, D_MODEL), lambda i: (i // nj, 0, 0)),
                  pl.BlockSpec((1, D_MODEL), lambda i: (0, 0)),
                  pl.BlockSpec(memory_space=pl.ANY)],
        out_specs=pl.BlockSpec((1, tm, D_MODEL), tok),
        out_shape=jax.ShapeDtypeStruct((bn, seq, D_MODEL), F32),
        scratch_shapes=[pltpu.VMEM((2, 2, tm) + ROW_TILE, F32), pltpu.SemaphoreType.DMA((2,))],
        compiler_params=_cparams(("arbitrary",)),
        name="moe_combine",
    )(pos, pos, route, hlat, g2, fn_row, ys)


def _prep_weights(w_in, gla_w_up, gla_b, ssd_dt_bias, ssd_a_log, ssd_d, router_group_w, router_group_b,
                  router_expert_w, router_expert_b):
    off_k = GLA_KEY
    off_v = 2 * GLA_KEY
    off_r = off_v + GLA_VAL
    off_g = off_r + GLA_VAL
    off_z = off_g + 2 * GLA_RANK
    off_xbc = off_z + SSD_INNER
    off_dt = off_xbc + SSD_XBC
    w_main = jnp.concatenate([w_in[:, :off_g], w_in[:, off_z:off_dt]], axis=1).astype(BF16)
    w_small = jnp.concatenate([w_in[:, off_g:off_z], w_in[:, off_dt:off_dt + 2 * SSD_HEADS]], axis=1)
    w_small = jnp.pad(w_small, ((0, 0), (0, LANES - w_small.shape[1]))).astype(BF16)
    wup = jnp.zeros((2, LANES, GLA_KEY), F32)
    wup = wup.at[0, 0:GLA_RANK].set(gla_w_up[0]).at[1, GLA_RANK:2 * GLA_RANK].set(gla_w_up[1]).astype(BF16)
    pad_h = lambda t: jnp.pad(t.reshape(2, SSD_GROUPS, 1, SSD_HPG), ((0, 0), (0, 0), (0, 0), (0, LANES - SSD_HPG)))
    w_router = jnp.zeros((D_MODEL, LANES), F32)
    w_router = w_router.at[:, 0:N_GROUPS].set(router_group_w)
    w_router = w_router.at[:, EXPERT_LANE0:EXPERT_LANE0 + N_EXPERTS].set(router_expert_w).astype(BF16)
    b_router = jnp.zeros((1, LANES), F32).at[0, 0:N_GROUPS].set(router_group_b)
    b_router = b_router.at[0, EXPERT_LANE0:EXPERT_LANE0 + N_EXPERTS].set(router_expert_b)
    return (w_main, w_small, wup, gla_b.reshape(2, 1, GLA_KEY), pad_h(ssd_dt_bias), pad_h(ssd_a_log),
            jnp.repeat(ssd_d, SSD_P)[None, :], w_router, b_router)


def kernel(x, c, ctx, c_ctx, w_ada, b_ada, norm_mix, norm_ffn, w_in, gla_w_up, gla_b, gla_norm, ssd_conv_w,
           ssd_conv_b, ssd_dt_bias, ssd_a_log, ssd_d, ssd_norm, w_out, router_group_w, router_group_b,
           router_expert_w, router_expert_b, expert_w_gate, expert_w_up, expert_w_down, final_norm):
    assert w_ada.shape[0] == 1, "single-layer block"
    bn, seq, d = x.shape
    ctx_len = ctx.shape[1]
    assert seq // GRID_W == GRID_W and d == D_MODEL

    c8 = jnp.concatenate([c, c_ctx[None, :], jnp.zeros((8 - bn - 1, d), F32)], axis=0)
    mods = _adaln(c8, w_ada[0], b_ada[0][None, :])
    sh1, sc1, g1, sh2, sc2, g2 = [mods[:, i * d:(i + 1) * d] for i in range(6)]
    ctx_row = lambda t: jnp.broadcast_to(t[bn:bn + 1], (bn, d))
    mod1 = jnp.stack([ctx_row(sh1), ctx_row(sc1), sh1[:bn], sc1[:bn]], axis=1)
    mod2 = jnp.stack([g1[:bn], sh2[:bn], sc2[:bn], g2[:bn]], axis=1)

    (w_main, w_small, wup, gla_b3, dtb4, alog4, dskip_row, w_router, b_router) = _prep_weights(
        w_in[0], gla_w_up[0], gla_b[0], ssd_dt_bias[0], ssd_a_log[0], ssd_d[0], router_group_w[0],
        router_group_b[0], router_expert_w[0], router_expert_b[0])

    proj, small = _inproj(ctx, x, mod1, norm_mix, w_main, w_small)
    xact = _conv(proj, ssd_conv_w[0].reshape(9, SSD_XBC), ssd_conv_b, ctx_len, seq)
    lvm = jnp.asarray(np.stack([_level_matrix(CHUNK, False), _level_matrix(CHUNK, True)]))
    o_gla = _gla(proj, small, wup, gla_b3, lvm, ctx_len, seq)
    y_ssd = _ssd(xact, small, dtb4, alog4, dskip_row, ctx_len, seq)
    hlat, h2p, route, cnt = _outproj(o_gla, proj, y_ssd, x, mod2, gla_norm, ssd_norm, norm_ffn,
                                     w_out[0].astype(BF16), w_router, b_router, ctx_len)
    n_tok = bn * seq
    pos, te, nt, pad0, padn, seg, nxt = _plan(route.reshape(n_tok, LANES), cnt)
    te, nt = te.reshape(-1), nt.reshape(-1)
    xs = _dispatch(pad0.reshape(-1), padn.reshape(-1), nt, pos, h2p.reshape((n_tok,) + ROW_TILE),
                   _moe_tiles(n_tok) * MOE_TILE)
    ys = _experts(te, nt, seg.reshape(-1), nxt.reshape(-1), xs, expert_w_gate[0], expert_w_up[0], expert_w_down[0])
    return _combine(pos, route, hlat, mod2[:, 3:4, :], final_norm[None, :], ys)
```

```python
import functools
import math

import numpy as np
import jax
import jax.numpy as jnp
from jax import lax
from jax.experimental import pallas as pl
from jax.experimental.pallas import tpu as pltpu

F32 = jnp.float32
BF16 = jnp.bfloat16

D_MODEL = 1024
GRID_W = 64
EPS = 1e-6

GLA_HEADS = 4
GLA_DK = 128
GLA_DV = 256
GLA_KEY = GLA_HEADS * GLA_DK
GLA_VAL = GLA_HEADS * GLA_DV
GLA_RANK = 16
GLA_GATE_NORM = 16.0

SSD_HEADS = 16
SSD_P = 64
SSD_INNER = SSD_HEADS * SSD_P
SSD_GROUPS = 2
SSD_HPG = SSD_HEADS // SSD_GROUPS
SSD_N = 128
SSD_XBC = SSD_INNER + 2 * SSD_GROUPS * SSD_N
SSD_GW = SSD_HPG * SSD_P

N_GROUPS = 4
EXPERTS_PER_GROUP = 8
N_EXPERTS = N_GROUPS * EXPERTS_PER_GROUP
D_FF = 512

LANES = 128
SUBLANES = 8
VMEM_LIMIT = 56 * 1024 * 1024

PC_Q = 0
PC_K = PC_Q + GLA_KEY
PC_V = PC_K + GLA_KEY
PC_R = PC_V + GLA_VAL
PC_Z = PC_R + GLA_VAL
PC_XBC = PC_Z + SSD_INNER
PROJ_W = PC_XBC + SSD_XBC
SMALL_DT0 = 2 * GLA_RANK
SMALL_ROT0 = 2
SMALL_W = LANES * (SMALL_ROT0 + 2 * SSD_GROUPS)

CHUNK = 128
EXPERT_LANE0 = 32


def _cparams(sem):
    return pltpu.CompilerParams(dimension_semantics=sem, vmem_limit_bytes=VMEM_LIMIT)


def _silu(x):
    return x / (1.0 + jnp.exp(-x))


def _softplus(x):
    return jnp.maximum(x, 0.0) + jnp.log(1.0 + jnp.exp(-jnp.abs(x)))


def _log_sigmoid(x):
    return jnp.minimum(x, 0.0) - jnp.log(1.0 + jnp.exp(-jnp.abs(x)))


def _bdot(a, b):
    return jnp.dot(a.astype(BF16), b.astype(BF16), preferred_element_type=F32)


def _bdot_nt(a, b):
    return lax.dot_general(a.astype(BF16), b.astype(BF16), (((1,), (1,)), ((), ())),
                           preferred_element_type=F32)


def _adaln_body(c_ref, w_ref, b_ref, o_ref):
    o_ref[...] = _bdot(_silu(c_ref[...]), w_ref[...]) + b_ref[...]


def _adaln(c8, w, b):
    n = w.shape[1]
    bn = 1024
    return pl.pallas_call(
        _adaln_body,
        grid=(n // bn,),
        in_specs=[pl.BlockSpec((8, D_MODEL), lambda j: (0, 0)),
                  pl.BlockSpec((D_MODEL, bn), lambda j: (0, j)),
                  pl.BlockSpec((1, bn), lambda j: (0, j))],
        out_specs=pl.BlockSpec((8, bn), lambda j: (0, j)),
        out_shape=jax.ShapeDtypeStruct((8, n), F32),
        compiler_params=_cparams(("arbitrary",)),
        name="adaln",
    )(c8, w, b)


INPROJ_TM = 256
INPROJ_NC = 512


def _inproj_body(ctx_ref, x_ref, mod_ref, nw_ref, w_ref, ws_ref, proj_ref, small_ref):
    is_ctx = pl.program_id(1) == 0
    xin = jnp.where(is_ctx, ctx_ref[0], x_ref[0])
    m = mod_ref[0]
    shift = jnp.where(is_ctx, m[0:1], m[2:3])
    scale = jnp.where(is_ctx, m[1:2], m[3:4])
    ms = jnp.mean(xin * xin, axis=-1, keepdims=True)
    h = xin * lax.rsqrt(ms + EPS) * nw_ref[...]
    hb = (h * (1.0 + scale) + shift).astype(BF16)
    for n in range(PROJ_W // INPROJ_NC):
        sl = slice(n * INPROJ_NC, (n + 1) * INPROJ_NC)
        proj_ref[0, :, sl] = jnp.dot(hb, w_ref[:, sl], preferred_element_type=F32).astype(BF16)
    sm = jnp.dot(hb, ws_ref[...], preferred_element_type=F32)
    small_ref[0, :, 0:LANES] = sm
    small_ref[0, :, LANES:SMALL_ROT0 * LANES] = jnp.zeros((sm.shape[0], (SMALL_ROT0 - 1) * LANES), F32)
    for k in range(2 * SSD_GROUPS):
        lane0 = SMALL_DT0 + k * SSD_HPG
        small_ref[0, :, (SMALL_ROT0 + k) * LANES:(SMALL_ROT0 + k + 1) * LANES] = pltpu.roll(sm, LANES - lane0, 1)


def _inproj(ctx, x, mod1, norm_w, w_main, w_small):
    bn, seq, _ = x.shape
    ctx_len = ctx.shape[1]
    tm = INPROJ_TM
    assert ctx_len == tm and seq % tm == 0
    tt = ctx_len + seq
    nj = tt // tm
    return pl.pallas_call(
        _inproj_body,
        grid=(bn, nj),
        in_specs=[pl.BlockSpec((1, tm, D_MODEL), lambda b, j: (b, 0, 0)),
                  pl.BlockSpec((1, tm, D_MODEL), lambda b, j: (b, jnp.maximum(j - 1, 0), 0)),
                  pl.BlockSpec((1, 4, D_MODEL), lambda b, j: (b, 0, 0)),
                  pl.BlockSpec((1, D_MODEL), lambda b, j: (0, 0)),
                  pl.BlockSpec((D_MODEL, PROJ_W), lambda b, j: (0, 0)),
                  pl.BlockSpec((D_MODEL, LANES), lambda b, j: (0, 0))],
        out_specs=[pl.BlockSpec((1, tm, PROJ_W), lambda b, j: (b, j, 0)),
                   pl.BlockSpec((1, tm, SMALL_W), lambda b, j: (b, j, 0))],
        out_shape=[jax.ShapeDtypeStruct((bn, tt, PROJ_W), BF16),
                   jax.ShapeDtypeStruct((bn, tt, SMALL_W), F32)],
        compiler_params=_cparams(("arbitrary", "arbitrary")),
        name="inproj",
    )(ctx, x, mod1, norm_w, w_main, w_small)


CONV_CB = 256
CONV_TT = 256


def _conv_body(ctx_len, seq, p_ref, w_ref, b_ref, o_ref, s_ref):
    lat0 = ctx_len + GRID_W
    tt = CONV_TT
    n_tiles = seq // tt
    zeros = jnp.zeros((GRID_W, CONV_CB), F32)
    s_ref[0:ctx_len, :] = p_ref[0, 0:ctx_len, :].astype(F32)
    s_ref[ctx_len:lat0, :] = zeros
    s_ref[lat0 + seq:lat0 + seq + GRID_W, :] = zeros

    def copy_tile(t, c):
        src = pl.multiple_of(ctx_len + t * tt, tt)
        dst = pl.multiple_of(lat0 + t * tt, GRID_W)
        s_ref[pl.ds(dst, tt), :] = p_ref[0, pl.ds(src, tt), :].astype(F32)
        return c

    lax.fori_loop(0, n_tiles, copy_tile, 0)

    w = w_ref[...]
    bias = b_ref[...]
    pos = lax.broadcasted_iota(jnp.int32, (tt, 1), 0)

    def conv_rows(rows, first, last):
        left, mid, right = [sum(xr * w[3 * kh + kw:3 * kh + kw + 1] for xr, kh in rows) for kw in range(3)]
        return (bias + mid + jnp.where(first, 0.0, pltpu.roll(left, 1, 0))
                + jnp.where(last, 0.0, pltpu.roll(right, tt - 1, 0)))

    acc = conv_rows([(s_ref[0:ctx_len, :], 1)], pos == 0, pos == ctx_len - 1)
    o_ref[0, 0:ctx_len, :] = _silu(acc).astype(BF16)

    col = pos % GRID_W
    first = col == 0
    last = col == GRID_W - 1

    def tile(t, c):
        base = lat0 + t * tt
        rows = [(s_ref[pl.ds(pl.multiple_of(base + (kh - 1) * GRID_W, GRID_W), tt), :], kh) for kh in range(3)]
        dst = pl.multiple_of(ctx_len + t * tt, tt)
        o_ref[0, pl.ds(dst, tt), :] = _silu(conv_rows(rows, first, last)).astype(BF16)
        return c

    lax.fori_loop(0, n_tiles, tile, 0)


def _conv(proj, conv_w9, conv_b, ctx_len, seq):
    bn, tt, _ = proj.shape
    assert ctx_len == CONV_TT and seq % CONV_TT == 0 and CONV_TT % GRID_W == 0
    ncb = SSD_XBC // CONV_CB
    cb0 = PC_XBC // CONV_CB
    return pl.pallas_call(
        functools.partial(_conv_body, ctx_len, seq),
        grid=(bn, ncb),
        in_specs=[pl.BlockSpec((1, tt, CONV_CB), lambda b, j: (b, 0, cb0 + j)),
                  pl.BlockSpec((9, CONV_CB), lambda b, j: (0, j)),
                  pl.BlockSpec((1, CONV_CB), lambda b, j: (0, j))],
        out_specs=pl.BlockSpec((1, tt, CONV_CB), lambda b, j: (b, 0, j)),
        out_shape=jax.ShapeDtypeStruct((bn, tt, SSD_XBC), BF16),
        scratch_shapes=[pltpu.VMEM((tt + 2 * GRID_W, CONV_CB), F32)],
        compiler_params=_cparams(("arbitrary", "arbitrary")),
        name="conv",
    )(proj, conv_w9, conv_b)


def _bwd_chunk(s, n_ctx, n_steps):
    return jnp.where(s < n_ctx, n_ctx - 1 - s, n_steps + n_ctx - 1 - s)


def _out_row(c, n_ctx, n, seq):
    return pl.multiple_of(jnp.where(c >= n_ctx, (c - n_ctx) * n, seq), n)


def _scan_sum(a, rev):
    n = a.shape[0]
    row = lax.broadcasted_iota(jnp.int32, (n, 1), 0)
    sh = 1
    while sh < n:
        if rev:
            a = a + jnp.where(row < n - sh, pltpu.roll(a, n - sh, 0), 0.0)
        else:
            a = a + jnp.where(row >= sh, pltpu.roll(a, sh, 0), 0.0)
        sh *= 2
    return a


def _level_matrix(n, rev):
    i = np.arange(n)[:, None]
    j = np.arange(n)[None, :]
    x = i ^ j
    lv = np.where(x > 0, np.floor(np.log2(np.maximum(x, 1))).astype(np.int64) + 1, 0)
    earlier = (j > i) if rev else (j < i)
    return np.where(earlier, lv, 0).astype(np.int32)


GLA_HPS = 4


def _gla_dir(rev, dst, qkv_ref, sm_ref, wup, gb, lv, o_ref, s_ref):
    n = CHUNK
    q = qkv_ref[0, :, PC_Q:PC_K].astype(F32) * (GLA_DK ** -0.5)
    k = qkv_ref[0, :, PC_K:PC_V].astype(F32)
    x = jnp.dot(sm_ref[0].astype(BF16), wup, preferred_element_type=F32) + gb
    b = _scan_sum(_log_sigmoid(x) / GLA_GATE_NORM, rev)
    row = lax.broadcasted_iota(jnp.int32, (n, 1), 0)
    b_end = b[0:1] if rev else b[n - 1:n]
    k_end = k * jnp.exp(b_end - b)
    q_in = q * jnp.exp(b)
    dec_end = jnp.exp(b_end)

    att = [jnp.zeros((n, n), F32) for _ in range(GLA_HPS)]
    e_h = b
    h = 1
    level = 1
    while h < n:
        upper = (row & h) != 0
        if 2 * h < SUBLANES:
            if rev:
                e = jnp.where(upper, e_h - b, b - pltpu.roll(e_h, n - h, 0))
            else:
                e = jnp.where(upper, b - pltpu.roll(e_h, h, 0), e_h - b)
        else:
            blocks = b.reshape(n // (2 * h), 2 * h, b.shape[1])
            mid = h if rev else h - 1
            ref = jnp.broadcast_to(blocks[:, mid:mid + 1, :], blocks.shape).reshape(b.shape)
            e = -jnp.abs(b - ref)
        w = jnp.exp(e)
        qw = (q * w).astype(BF16)
        kw = (k * w).astype(BF16)
        mask = lv == level
        for hh in range(GLA_HPS):
            ks = slice(hh * GLA_DK, (hh + 1) * GLA_DK)
            att[hh] = att[hh] + jnp.where(mask, _bdot_nt(qw[:, ks], kw[:, ks]), 0.0)
        if 4 * h < SUBLANES:
            if rev:
                e_h = jnp.where(upper, pltpu.roll(e_h, h, 0), e_h)
            else:
                e_h = jnp.where(upper, e_h, pltpu.roll(e_h, n - h, 0))
        h *= 2
        level += 1

    for hh in range(GLA_HPS):
        ks = slice(hh * GLA_DK, (hh + 1) * GLA_DK)
        vs = slice(hh * GLA_DV, (hh + 1) * GLA_DV)
        v = qkv_ref[0, :, PC_V + hh * GLA_DV:PC_V + (hh + 1) * GLA_DV]
        s_old = s_ref[hh]
        o = jnp.dot(att[hh].astype(BF16), v, preferred_element_type=F32)
        o = o + jnp.sum(q[:, ks] * k[:, ks], axis=-1, keepdims=True) * v.astype(F32)
        o = o + _bdot(q_in[:, ks], s_old)
        o_ref[0, pl.ds(dst, n), vs] += o
        dec_col = jnp.broadcast_to(dec_end[:, ks], (n, GLA_DK)).T[:, 0:1]
        s_ref[hh] = dec_col * s_old + jnp.dot(k_end[:, ks].T.astype(BF16), v, preferred_element_type=F32)


def _gla_body(n_ctx, seq, qkvf, smf, qkvb, smb, wup_ref, gb_ref, lv_ref, o_ref, s_ref):
    step = pl.program_id(2)
    n = CHUNK

    @pl.when(step == 0)
    def _():
        s_ref[...] = jnp.zeros_like(s_ref)
        o_ref[...] = jnp.zeros_like(o_ref)

    cb = _bwd_chunk(step, n_ctx, pl.num_programs(2))
    _gla_dir(False, _out_row(step, n_ctx, n, seq), qkvf, smf, wup_ref[0], gb_ref[0], lv_ref[0],
             o_ref, s_ref.at[0])
    _gla_dir(True, _out_row(cb, n_ctx, n, seq), qkvb, smb, wup_ref[1], gb_ref[1], lv_ref[1],
             o_ref, s_ref.at[1])


def _gla(proj, small, wup_pad, gla_b3, lvm, ctx_len, seq):
    bn, tt, _ = proj.shape
    n = CHUNK
    assert ctx_len % n == 0 and seq % n == 0
    n_ctx = ctx_len // n
    n_steps = tt // n
    assert GLA_HPS == GLA_HEADS and PC_Q == 0 and PC_R % LANES == 0
    hw = GLA_HPS * GLA_DK
    vw = GLA_HPS * GLA_DV
    fwd = lambda s: s
    bwd = lambda s: _bwd_chunk(s, n_ctx, n_steps)

    def chunk_specs(c):
        return [pl.BlockSpec((1, n, PC_R), lambda b, h, s: (b, c(s), 0)),
                pl.BlockSpec((1, n, LANES), lambda b, h, s: (b, c(s), 0))]

    return pl.pallas_call(
        functools.partial(_gla_body, n_ctx, seq),
        grid=(bn, GLA_HEADS // GLA_HPS, n_steps),
        in_specs=chunk_specs(fwd) + chunk_specs(bwd) + [
            pl.BlockSpec((2, LANES, hw), lambda b, h, s: (0, 0, h)),
            pl.BlockSpec((2, 1, hw), lambda b, h, s: (0, 0, h)),
            pl.BlockSpec((2, n, n), lambda b, h, s: (0, 0, 0))],
        out_specs=pl.BlockSpec((1, seq + n, vw), lambda b, h, s: (b, 0, h)),
        out_shape=jax.ShapeDtypeStruct((bn, seq + n, GLA_VAL), F32),
        scratch_shapes=[pltpu.VMEM((2, GLA_HPS, GLA_DK, GLA_DV), F32)],
        compiler_params=_cparams(("arbitrary",) * 3),
        name="gla",
    )(proj, small, proj, small, wup_pad, gla_b3, lvm)


def _ssd_dir(rev, first, g, dst, xbc_ref, sm_ref, dtb, alog, dsk_ref, y_ref, s_ref):
    n = CHUNK
    gl = slice(g * LANES, (g + 1) * LANES)
    g0 = g * SSD_GW
    b_cols = slice(SSD_INNER + g * SSD_N, SSD_INNER + (g + 1) * SSD_N)
    c_cols = slice(SSD_INNER + (SSD_GROUPS + g) * SSD_N, SSD_INNER + (SSD_GROUPS + g + 1) * SSD_N)
    lane = lax.broadcasted_iota(jnp.int32, (1, LANES), 1)
    dt = _softplus(sm_ref[0, :, gl] + dtb)
    a_neg = jnp.where(lane < SSD_HPG, -jnp.exp(alog), 0.0)
    acum = _scan_sum(dt * a_neg, rev)
    acum_t = acum.T
    cm = xbc_ref[0, :, c_cols]
    bm_t = xbc_ref[0, :, b_cols].astype(F32).T.astype(BF16)
    ii = lax.broadcasted_iota(jnp.int32, (n, n), 0)
    jj = lax.broadcasted_iota(jnp.int32, (n, n), 1)
    causal = (jj >= ii) if rev else (jj <= ii)
    cb = _bdot_nt(cm, xbc_ref[0, :, b_cols])
    low = lane < SSD_P
    for p in range(SSD_HPG // 2):
        slab = slice(p * LANES, (p + 1) * LANES)
        out = slice(g0 + p * LANES, g0 + (p + 1) * LANES)
        h0, h1 = 2 * p, 2 * p + 1
        col0 = jnp.broadcast_to(acum[:, h0:h0 + 1], (n, LANES))
        col1 = jnp.broadcast_to(acum[:, h1:h1 + 1], (n, LANES))
        m0 = (cb * jnp.exp(jnp.where(causal, col0 - acum_t[h0:h0 + 1, :], -jnp.inf))).astype(BF16)
        m1 = (cb * jnp.exp(jnp.where(causal, col1 - acum_t[h1:h1 + 1, :], -jnp.inf))).astype(BF16)
        col = jnp.where(low, col0, col1)
        dt2 = jnp.where(low, jnp.broadcast_to(dt[:, h0:h0 + 1], (n, LANES)),
                        jnp.broadcast_to(dt[:, h1:h1 + 1], (n, LANES)))
        end = col[0:1] if rev else col[n - 1:n]
        x = xbc_ref[0, :, out].astype(F32)
        s_old = s_ref[:, slab]
        xdt = x * dt2
        rhs = jnp.concatenate([jnp.where(low, xdt, 0.0), jnp.where(low, 0.0, xdt)], axis=0).astype(BF16)
        y = jnp.dot(jnp.concatenate([m0, m1], axis=1), rhs, preferred_element_type=F32)
        y = y + jnp.dot(cm, s_old.astype(BF16), preferred_element_type=F32) * jnp.exp(col)
        if first:
            y = y + x * dsk_ref[:, out]
        y_ref[0, pl.ds(dst, n), out] += y
        new = jnp.dot(bm_t, (xdt * jnp.exp(end - col)).astype(BF16), preferred_element_type=F32)
        s_ref[:, slab] = s_old * jnp.exp(end) + new


def _ssd_body(n_ctx, seq, xbcf, smf, xbcb, smb, dtb_ref, alog_ref, dsk_ref, y_ref, s_ref):
    step = pl.program_id(1)
    n = CHUNK

    @pl.when(step == 0)
    def _():
        s_ref[...] = jnp.zeros_like(s_ref)
        y_ref[...] = jnp.zeros_like(y_ref)

    dst_f = _out_row(step, n_ctx, n, seq)
    dst_b = _out_row(_bwd_chunk(step, n_ctx, pl.num_programs(1)), n_ctx, n, seq)
    for g in range(SSD_GROUPS):
        _ssd_dir(False, True, g, dst_f, xbcf, smf, dtb_ref[0, g], alog_ref[0, g], dsk_ref, y_ref,
                 s_ref.at[0, g])
        _ssd_dir(True, False, g, dst_b, xbcb, smb, dtb_ref[1, g], alog_ref[1, g], dsk_ref, y_ref,
                 s_ref.at[1, g])


def _ssd(xact, small, dtb4, alog4, dskip_row, ctx_len, seq):
    bn, tt, _ = xact.shape
    n = CHUNK
    n_ctx = ctx_len // n
    n_steps = tt // n
    fwd = lambda s: s
    bwd = lambda s: _bwd_chunk(s, n_ctx, n_steps)

    assert SMALL_ROT0 % SSD_GROUPS == 0
    sblk = SMALL_ROT0 // SSD_GROUPS

    def chunk_specs(c, d):
        return [pl.BlockSpec((1, n, SSD_XBC), lambda b, s: (b, c(s), 0)),
                pl.BlockSpec((1, n, SSD_GROUPS * LANES), lambda b, s: (b, c(s), sblk + d))]

    whole4 = pl.BlockSpec((2, SSD_GROUPS, 1, LANES), lambda b, s: (0, 0, 0, 0))
    return pl.pallas_call(
        functools.partial(_ssd_body, n_ctx, seq),
        grid=(bn, n_steps),
        in_specs=chunk_specs(fwd, 0) + chunk_specs(bwd, 1) + [
            whole4, whole4, pl.BlockSpec((1, SSD_INNER), lambda b, s: (0, 0))],
        out_specs=pl.BlockSpec((1, seq + n, SSD_INNER), lambda b, s: (b, 0, 0)),
        out_shape=jax.ShapeDtypeStruct((bn, seq + n, SSD_INNER), F32),
        scratch_shapes=[pltpu.VMEM((2, SSD_GROUPS, SSD_N, SSD_GW), F32)],
        compiler_params=_cparams(("arbitrary",) * 2),
        name="ssd",
    )(xact, small, xact, small, dtb4, alog4, dskip_row)


OUT_TM = 256
RT_E1, RT_E2, RT_RANK1, RT_RANK2, RT_W1, RT_W2 = range(6)


def _rms(x, w):
    return x * lax.rsqrt(jnp.mean(x * x, axis=-1, keepdims=True) + EPS) * w


ROW_TILE = (D_MODEL // LANES, LANES)


def _store_row_tiles(ref, x):
    ref[...] = x.reshape((x.shape[0],) + ROW_TILE)


def _load_row_tiles(ref):
    return ref[...].reshape(ref.shape[0], D_MODEL)


def _outproj_body(o_ref, r_ref, y_ref, z_ref, x_ref, mod_ref, gn_ref, sn_ref, fn_ref, wo_ref, wr_ref, br_ref,
                  hlat_ref, h2_ref, route_ref, cnt_ref, carry_ref):
    tm = OUT_TM

    @pl.when((pl.program_id(0) == 0) & (pl.program_id(1) == 0))
    def _():
        carry_ref[...] = jnp.zeros_like(carry_ref)

    parts = []
    for h in range(GLA_HEADS):
        hs = slice(h * GLA_DV, (h + 1) * GLA_DV)
        parts.append((_rms(o_ref[0, :, hs], gn_ref[...]) * _silu(r_ref[0, :, hs].astype(F32))).astype(BF16))
    for g in range(SSD_GROUPS):
        gs = slice(g * SSD_GW, (g + 1) * SSD_GW)
        yg = y_ref[0, :, gs] * _silu(z_ref[0, :, gs].astype(F32))
        parts.append(_rms(yg, sn_ref[:, gs]).astype(BF16))
    mix = jnp.concatenate(parts, axis=-1)
    m = mod_ref[0]
    hlat = x_ref[0] + m[0:1] * jnp.dot(mix, wo_ref[...], preferred_element_type=F32)
    hlat_ref[0] = hlat
    h2f = _rms(hlat, fn_ref[...]) * (1.0 + m[2:3]) + m[1:2]
    _store_row_tiles(h2_ref.at[0], h2f)

    lg = jnp.dot(h2f.astype(BF16), wr_ref[...], preferred_element_type=F32) + br_ref[...]
    lanef = lax.broadcasted_iota(jnp.int32, (tm, LANES), 1).astype(F32)
    ninf = -jnp.inf
    is_g = lanef < float(N_GROUPS)
    gl = jnp.where(is_g, lg, ninf)
    gmax = jnp.max(gl, axis=-1, keepdims=True)
    gsel = jnp.min(jnp.where(gl == gmax, lanef, float(LANES)), axis=-1, keepdims=True)
    pg = 1.0 / jnp.sum(jnp.where(is_g, jnp.exp(lg - gmax), 0.0), axis=-1, keepdims=True)
    lo = float(EXPERT_LANE0) + float(EXPERTS_PER_GROUP) * gsel
    in_grp = (lanef >= lo) & (lanef < lo + float(EXPERTS_PER_GROUP))
    el = jnp.where(in_grp, lg, ninf)
    v1 = jnp.max(el, axis=-1, keepdims=True)
    i1 = jnp.min(jnp.where(el == v1, lanef, float(LANES)), axis=-1, keepdims=True)
    el2 = jnp.where(lanef == i1, ninf, el)
    v2 = jnp.max(el2, axis=-1, keepdims=True)
    i2 = jnp.min(jnp.where(el2 == v2, lanef, float(LANES)), axis=-1, keepdims=True)
    t = jnp.exp(v2 - v1)
    w1 = pg / (1.0 + t)
    w2 = pg * t / (1.0 + t)

    sel1 = lanef == i1
    sel2 = lanef == i2
    member = jnp.where(sel1 | sel2, 1.0, 0.0)
    ii = lax.broadcasted_iota(jnp.int32, (tm, tm), 0)
    jj = lax.broadcasted_iota(jnp.int32, (tm, tm), 1)
    before = jnp.where(jj < ii, 1.0, 0.0).astype(BF16)
    ranks = jnp.dot(before, member.astype(BF16), preferred_element_type=F32) + carry_ref[...]
    rank1 = jnp.sum(jnp.where(sel1, ranks, 0.0), axis=-1, keepdims=True)
    rank2 = jnp.sum(jnp.where(sel2, ranks, 0.0), axis=-1, keepdims=True)
    carry = carry_ref[...] + jnp.sum(member, axis=0, keepdims=True)
    carry_ref[...] = carry
    cnt_ref[...] = carry
    rec = jnp.zeros((tm, LANES), F32)
    for lane_id, val in ((RT_E1, i1 - float(EXPERT_LANE0)), (RT_E2, i2 - float(EXPERT_LANE0)),
                         (RT_RANK1, rank1), (RT_RANK2, rank2), (RT_W1, w1), (RT_W2, w2)):
        rec = jnp.where(lanef == float(lane_id), val, rec)
    route_ref[0] = rec


def _outproj(o_gla, proj, y_ssd, x, mod2, gn_row, sn_row, fn_row, w_out, w_router, b_router, ctx_len):
    bn, seq, _ = x.shape
    tm = OUT_TM
    assert ctx_len % tm == 0 and seq % tm == 0
    j0 = ctx_len // tm
    rb = PC_R // GLA_VAL
    zb = PC_Z // SSD_INNER
    tok = lambda b, j: (b, j, 0)
    const = lambda b, j: (0, 0)
    return pl.pallas_call(
        _outproj_body,
        grid=(bn, seq // tm),
        in_specs=[pl.BlockSpec((1, tm, GLA_VAL), tok),
                  pl.BlockSpec((1, tm, GLA_VAL), lambda b, j: (b, j0 + j, rb)),
                  pl.BlockSpec((1, tm, SSD_INNER), tok),
                  pl.BlockSpec((1, tm, SSD_INNER), lambda b, j: (b, j0 + j, zb)),
                  pl.BlockSpec((1, tm, D_MODEL), tok),
                  pl.BlockSpec((1, 4, D_MODEL), lambda b, j: (b, 0, 0)),
                  pl.BlockSpec((1, GLA_DV), const),
                  pl.BlockSpec((1, SSD_INNER), const),
                  pl.BlockSpec((1, D_MODEL), const),
                  pl.BlockSpec((GLA_VAL + SSD_INNER, D_MODEL), const),
                  pl.BlockSpec((D_MODEL, LANES), const),
                  pl.BlockSpec((1, LANES), const)],
        out_specs=[pl.BlockSpec((1, tm, D_MODEL), tok),
                   pl.BlockSpec((1, tm) + ROW_TILE, lambda b, j: (b, j, 0, 0)),
                   pl.BlockSpec((1, tm, LANES), tok),
                   pl.BlockSpec((1, LANES), const)],
        out_shape=[jax.ShapeDtypeStruct((bn, seq, D_MODEL), F32),
                   jax.ShapeDtypeStruct((bn, seq) + ROW_TILE, F32),
                   jax.ShapeDtypeStruct((bn, seq, LANES), F32),
                   jax.ShapeDtypeStruct((1, LANES), F32)],
        scratch_shapes=[pltpu.VMEM((1, LANES), F32)],
        compiler_params=_cparams(("arbitrary", "arbitrary")),
        name="outproj",
    )(o_gla, proj, y_ssd, proj, x, mod2, gn_row, sn_row, fn_row, w_out, w_router, b_router)


MOE_TILE = 256
PLAN_TM = 1024
DISPATCH_TM = 1024
COMBINE_TM = 512


def _moe_tiles(n_tokens):
    return (2 * n_tokens) // MOE_TILE + N_EXPERTS


def _plan_body(nt_pad, route_ref, cnt_ref, pos_ref, te_ref, nt_ref, pad0_ref, padn_ref, seg_ref, nxt_ref):
    lane = lax.broadcasted_iota(jnp.int32, (1, LANES), 1).astype(F32)
    is_e = (lane >= float(EXPERT_LANE0)) & (lane < float(EXPERT_LANE0 + N_EXPERTS))
    tiles = jnp.where(is_e, jnp.floor((cnt_ref[...] + float(MOE_TILE - 1)) / float(MOE_TILE)), 0.0)
    ii = lax.broadcasted_iota(jnp.int32, (LANES, LANES), 0)
    jj = lax.broadcasted_iota(jnp.int32, (LANES, LANES), 1)
    lower = jnp.where(ii < jj, 1.0, 0.0).astype(BF16)
    first_tile = jnp.dot(jnp.broadcast_to(tiles, (8, LANES)).astype(BF16), lower,
                         preferred_element_type=F32)[0:1]
    off_row = first_tile * float(MOE_TILE)
    for sb in range(PLAN_TM // LANES):
        r = route_ref[sb * LANES:(sb + 1) * LANES, :]
        o1 = jnp.sum(jnp.where(lane == r[:, RT_E1:RT_E1 + 1] + float(EXPERT_LANE0), off_row, 0.0),
                     axis=-1, keepdims=True)
        o2 = jnp.sum(jnp.where(lane == r[:, RT_E2:RT_E2 + 1] + float(EXPERT_LANE0), off_row, 0.0),
                     axis=-1, keepdims=True)
        p = jnp.where(lane == 0.0, o1 + r[:, RT_RANK1:RT_RANK1 + 1],
                      jnp.where(lane == 1.0, o2 + r[:, RT_RANK2:RT_RANK2 + 1], 0.0))
        pos_ref[sb] = p.T[0:8, :].astype(jnp.int32)

    @pl.when(pl.program_id(0) == 0)
    def _():
        ti = lax.broadcasted_iota(jnp.int32, (nt_pad, 1), 0).astype(F32)
        done = jnp.where(is_e & (first_tile + tiles <= ti), 1.0, 0.0)
        te = jnp.minimum(jnp.sum(done, axis=-1, keepdims=True), float(N_EXPERTS - 1))
        te_ref[...] = te.astype(jnp.int32)
        nt_ref[...] = jnp.sum(tiles, axis=-1, keepdims=True).astype(jnp.int32)
        lane_e = lane - float(EXPERT_LANE0)
        nonempty = is_e & (tiles > 0.0)
        seg_ref[...] = jnp.sum(jnp.where(nonempty & (lane_e < te), 1.0, 0.0), axis=-1,
                               keepdims=True).astype(jnp.int32)
        nxt = jnp.min(jnp.where(nonempty & (lane_e > te), lane_e, float(N_EXPERTS)), axis=-1, keepdims=True)
        nxt_ref[...] = jnp.where(nxt < float(N_EXPERTS), nxt, -1.0).astype(jnp.int32)
        used = jnp.where(is_e, cnt_ref[...], 0.0)
        pad0_ref[...] = (off_row + used).astype(jnp.int32)
        padn_ref[...] = (tiles * float(MOE_TILE) - used).astype(jnp.int32)


def _plan(route, cnt):
    n_tok = route.shape[0]
    nt_pad = _moe_tiles(n_tok)
    nsb = PLAN_TM // LANES
    return pl.pallas_call(
        functools.partial(_plan_body, nt_pad),
        grid=(n_tok // PLAN_TM,),
        in_specs=[pl.BlockSpec((PLAN_TM, LANES), lambda i: (i, 0)),
                  pl.BlockSpec((1, LANES), lambda i: (0, 0))],
        out_specs=[pl.BlockSpec((nsb, 8, LANES), lambda i: (i, 0, 0)),
                   pl.BlockSpec((nt_pad, 1), lambda i: (0, 0)),
                   pl.BlockSpec((1, 1), lambda i: (0, 0)),
                   pl.BlockSpec((1, LANES), lambda i: (0, 0)),
                   pl.BlockSpec((1, LANES), lambda i: (0, 0)),
                   pl.BlockSpec((nt_pad, 1), lambda i: (0, 0)),
                   pl.BlockSpec((nt_pad, 1), lambda i: (0, 0))],
        out_shape=[jax.ShapeDtypeStruct((n_tok // LANES, 8, LANES), jnp.int32),
                   jax.ShapeDtypeStruct((nt_pad, 1), jnp.int32),
                   jax.ShapeDtypeStruct((1, 1), jnp.int32),
                   jax.ShapeDtypeStruct((1, LANES), jnp.int32),
                   jax.ShapeDtypeStruct((1, LANES), jnp.int32),
                   jax.ShapeDtypeStruct((nt_pad, 1), jnp.int32),
                   jax.ShapeDtypeStruct((nt_pad, 1), jnp.int32)],
        compiler_params=_cparams(("arbitrary",)),
        name="moe_plan",
    )(route, cnt)


def _row_copy(src, s_row, dst, d_row, sem):
    return pltpu.make_async_copy(src.at[s_row], dst.at[d_row], sem)


def _issue_rows(n_rows, pos_ref, start_pair):
    for blk in range(n_rows // LANES):
        def issue(l, c, blk=blk):
            start_pair(blk * LANES + l, pos_ref[blk, 0, l], pos_ref[blk, 1, l])
            return c
        lax.fori_loop(0, LANES, issue, 0, unroll=8)


def _drain_rows(n_copies, copy):
    def drain(t, c):
        copy.wait()
        return c
    lax.fori_loop(0, n_copies, drain, 0, unroll=8)


def _dispatch_body(pad0_ref, padn_ref, nt_ref, pos_ref, h2_ref, xs_ref, zero_ref, sem):
    def start_pair(t, p1, p2):
        _row_copy(h2_ref, t, xs_ref, p1, sem).start(priority=0)
        _row_copy(h2_ref, t, xs_ref, p2, sem).start(priority=1)

    _issue_rows(DISPATCH_TM, pos_ref, start_pair)
    _drain_rows(2 * DISPATCH_TM, _row_copy(h2_ref, 0, xs_ref, 0, sem))

    @pl.when(pl.program_id(0) == 0)
    def _():
        zero_ref[...] = jnp.zeros_like(zero_ref)
        n_tiles = xs_ref.shape[0] // MOE_TILE

        def tile_copy(i):
            return pltpu.make_async_copy(zero_ref, xs_ref.at[pl.ds(i * MOE_TILE, MOE_TILE)], sem)

        def fill_tile(i, c):
            tile_copy(i).start()
            return c

        def drain_tile(i, c):
            tile_copy(i).wait()
            return c

        lax.fori_loop(nt_ref[0], n_tiles, fill_tile, 0)
        lax.fori_loop(nt_ref[0], n_tiles, drain_tile, 0)
        for e in range(N_EXPERTS):
            n_pad = padn_ref[EXPERT_LANE0 + e]
            for wait in (False, True):
                row = pad0_ref[EXPERT_LANE0 + e]
                size = MOE_TILE // 2
                while size >= 1:
                    has = (n_pad & size) != 0
                    copy = pltpu.make_async_copy(zero_ref.at[pl.ds(0, size)], xs_ref.at[pl.ds(row, size)], sem)
                    pl.when(has)(copy.wait if wait else copy.start)
                    row = row + jnp.where(has, size, 0)
                    size //= 2


def _dispatch(pad0, padn, nt, pos, h2t, n_rows):
    n_tok = h2t.shape[0]
    nb = DISPATCH_TM // LANES
    grid_spec = pltpu.PrefetchScalarGridSpec(
        num_scalar_prefetch=3,
        grid=(n_tok // DISPATCH_TM,),
        in_specs=[pl.BlockSpec((nb, 8, LANES), lambda i, *_: (i, 0, 0), memory_space=pltpu.SMEM),
                  pl.BlockSpec((DISPATCH_TM,) + ROW_TILE, lambda i, *_: (i, 0, 0))],
        out_specs=pl.BlockSpec(memory_space=pl.ANY),
        scratch_shapes=[pltpu.VMEM((MOE_TILE,) + ROW_TILE, F32), pltpu.SemaphoreType.DMA(())])
    return pl.pallas_call(
        _dispatch_body,
        grid_spec=grid_spec,
        out_shape=jax.ShapeDtypeStruct((n_rows,) + ROW_TILE, F32),
        compiler_params=_cparams(("arbitrary",)),
        name="moe_dispatch",
    )(pad0, padn, nt, pos, h2t)


def _expert_body(te_ref, nt_ref, seg_ref, nxt_ref, xs_ref, wg_ref, wu_ref, wd_ref, ys_ref,
                 wgf_ref, wuf_ref, wdf_ref, wgb_ref, wub_ref, wdb_ref, sem):
    i = pl.program_id(0)

    def copies(e, s):
        return [pltpu.make_async_copy(src.at[e], dst.at[s], sem.at[s])
                for src, dst in ((wg_ref, wgf_ref), (wu_ref, wuf_ref), (wd_ref, wdf_ref))]

    @pl.when(i < nt_ref[0])
    def _():
        seg = seg_ref[i]
        slot = seg % 2
        first = (i == 0) | (seg != seg_ref[jnp.maximum(i - 1, 0)])

        @pl.when(first)
        def _():
            @pl.when(i == 0)
            def _():
                for c in copies(te_ref[i], slot):
                    c.start()

            for c in copies(te_ref[i], slot):
                c.wait()
            wgb_ref[...] = wgf_ref[slot].astype(BF16)
            wub_ref[...] = wuf_ref[slot].astype(BF16)
            wdb_ref[...] = wdf_ref[slot].astype(BF16)

            @pl.when(nxt_ref[i] >= 0)
            def _():
                for c in copies(nxt_ref[i], 1 - slot):
                    c.start()

        x = _load_row_tiles(xs_ref).astype(BF16)
        gate = jnp.dot(x, wgb_ref[...], preferred_element_type=F32)
        up = jnp.dot(x, wub_ref[...], preferred_element_type=F32)
        y = jnp.dot((_silu(gate) * up).astype(BF16), wdb_ref[...], preferred_element_type=F32)
        _store_row_tiles(ys_ref, y)

    @pl.when(i >= nt_ref[0])
    def _():
        ys_ref[...] = jnp.zeros_like(ys_ref)


def _experts(te, nt, seg, nxt, xs, w_gate, w_up, w_down):
    rows = xs.shape[0]
    n_tiles = rows // MOE_TILE
    row_block = pl.BlockSpec((MOE_TILE,) + ROW_TILE, lambda i, *_: (i, 0, 0))
    in_w = (D_MODEL, D_FF)
    out_w = (D_FF, D_MODEL)
    grid_spec = pltpu.PrefetchScalarGridSpec(
        num_scalar_prefetch=4,
        grid=(n_tiles,),
        in_specs=[row_block] + [pl.BlockSpec(memory_space=pl.ANY)] * 3,
        out_specs=row_block,
        scratch_shapes=[pltpu.VMEM((2,) + in_w, F32), pltpu.VMEM((2,) + in_w, F32), pltpu.VMEM((2,) + out_w, F32),
                        pltpu.VMEM(in_w, BF16), pltpu.VMEM(in_w, BF16), pltpu.VMEM(out_w, BF16),
                        pltpu.SemaphoreType.DMA((2,))])
    return pl.pallas_call(
        _expert_body,
        grid_spec=grid_spec,
        out_shape=jax.ShapeDtypeStruct(xs.shape, F32),
        compiler_params=_cparams(("arbitrary",)),
        name="moe_experts",
    )(te, nt, seg, nxt, xs, w_gate, w_up, w_down)


def _combine_body(pos_ref, nxt_ref, route_ref, hlat_ref, g2_ref, fn_ref, ys_ref, out_ref, buf_ref, sem):
    tm = COMBINE_TM
    i = pl.program_id(0)
    slot = i % 2

    def fetch(p_ref, s):
        def start_pair(t, p1, p2):
            _row_copy(ys_ref, p1, buf_ref.at[s, 0], t, sem.at[s]).start(priority=0)
            _row_copy(ys_ref, p2, buf_ref.at[s, 1], t, sem.at[s]).start(priority=1)
        _issue_rows(tm, p_ref, start_pair)

    @pl.when(i == 0)
    def _():
        fetch(pos_ref, 0)

    @pl.when(i + 1 < pl.num_programs(0))
    def _():
        fetch(nxt_ref, 1 - slot)

    _drain_rows(2 * tm, _row_copy(ys_ref, 0, buf_ref.at[slot, 0], 0, sem.at[slot]))

    r = route_ref[0]
    moe = (r[:, RT_W1:RT_W1 + 1] * _load_row_tiles(buf_ref.at[slot, 0])
           + r[:, RT_W2:RT_W2 + 1] * _load_row_tiles(buf_ref.at[slot, 1]))
    out_ref[0] = _rms(hlat_ref[0] + g2_ref[0] * moe, fn_ref[...])


def _combine(pos, route, hlat, g2, fn_row, ys):
    bn, seq, _ = hlat.shape
    tm = COMBINE_TM
    nj = seq // tm
    nb = tm // LANES
    last = bn * nj - 1
    tok = lambda i: (i // nj, i % nj, 0)
    return pl.pallas_call(
        _combine_body,
        grid=(bn * nj,),
        in_specs=[pl.BlockSpec((nb, 8, LANES), lambda i: (i, 0, 0), memory_space=pltpu.SMEM),
                  pl.BlockSpec((nb, 8, LANES), lambda i: (jnp.minimum(i + 1, last), 0, 0),
                               memory_space=pltpu.SMEM),
                  pl.BlockSpec((1, tm, LANES), tok),
                  pl.BlockSpec((1, tm, D_MODEL), tok),
                  pl.BlockSpec((1, 1, D_MODEL), lambda i: (i // nj, 0, 0)),
                  pl.BlockSpec((1, D_MODEL), lambda i: (0, 0)),
                  pl.BlockSpec(memory_space=pl.ANY)],
        out_specs=pl.BlockSpec((1, tm, D_MODEL), tok),
        out_shape=jax.ShapeDtypeStruct((bn, seq, D_MODEL), F32),
        scratch_shapes=[pltpu.VMEM((2, 2, tm) + ROW_TILE, F32), pltpu.SemaphoreType.DMA((2,))],
        compiler_params=_cparams(("arbitrary",)),
        name="moe_combine",
    )(pos, pos, route, hlat, g2, fn_row, ys)


def _prep_weights(w_in, gla_w_up, gla_b, ssd_dt_bias, ssd_a_log, ssd_d, router_group_w, router_group_b,
                  router_expert_w, router_expert_b):
    off_k = GLA_KEY
    off_v = 2 * GLA_KEY
    off_r = off_v + GLA_VAL
    off_g = off_r + GLA_VAL
    off_z = off_g + 2 * GLA_RANK
    off_xbc = off_z + SSD_INNER
    off_dt = off_xbc + SSD_XBC
    w_main = jnp.concatenate([w_in[:, :off_g], w_in[:, off_z:off_dt]], axis=1).astype(BF16)
    w_small = jnp.concatenate([w_in[:, off_g:off_z], w_in[:, off_dt:off_dt + 2 * SSD_HEADS]], axis=1)
    w_small = jnp.pad(w_small, ((0, 0), (0, LANES - w_small.shape[1]))).astype(BF16)
    wup = jnp.zeros((2, LANES, GLA_KEY), F32)
    wup = wup.at[0, 0:GLA_RANK].set(gla_w_up[0]).at[1, GLA_RANK:2 * GLA_RANK].set(gla_w_up[1]).astype(BF16)
    pad_h = lambda t: jnp.pad(t.reshape(2, SSD_GROUPS, 1, SSD_HPG), ((0, 0), (0, 0), (0, 0), (0, LANES - SSD_HPG)))
    w_router = jnp.zeros((D_MODEL, LANES), F32)
    w_router = w_router.at[:, 0:N_GROUPS].set(router_group_w)
    w_router = w_router.at[:, EXPERT_LANE0:EXPERT_LANE0 + N_EXPERTS].set(router_expert_w).astype(BF16)
    b_router = jnp.zeros((1, LANES), F32).at[0, 0:N_GROUPS].set(router_group_b)
    b_router = b_router.at[0, EXPERT_LANE0:EXPERT_LANE0 + N_EXPERTS].set(router_expert_b)
    return (w_main, w_small, wup, gla_b.reshape(2, 1, GLA_KEY), pad_h(ssd_dt_bias), pad_h(ssd_a_log),
            jnp.repeat(ssd_d, SSD_P)[None, :], w_router, b_router)


def kernel(x, c, ctx, c_ctx, w_ada, b_ada, norm_mix, norm_ffn, w_in, gla_w_up, gla_b, gla_norm, ssd_conv_w,
           ssd_conv_b, ssd_dt_bias, ssd_a_log, ssd_d, ssd_norm, w_out, router_group_w, router_group_b,
           router_expert_w, router_expert_b, expert_w_gate, expert_w_up, expert_w_down, final_norm):
    assert w_ada.shape[0] == 1, "single-layer block"
    bn, seq, d = x.shape
    ctx_len = ctx.shape[1]
    assert seq // GRID_W == GRID_W and d == D_MODEL

    c8 = jnp.concatenate([c, c_ctx[None, :], jnp.zeros((8 - bn - 1, d), F32)], axis=0)
    mods = _adaln(c8, w_ada[0], b_ada[0][None, :])
    sh1, sc1, g1, sh2, sc2, g2 = [mods[:, i * d:(i + 1) * d] for i in range(6)]
    ctx_row = lambda t: jnp.broadcast_to(t[bn:bn + 1], (bn, d))
    mod1 = jnp.stack([ctx_row(sh1), ctx_row(sc1), sh1[:bn], sc1[:bn]], axis=1)
    mod2 = jnp.stack([g1[:bn], sh2[:bn], sc2[:bn], g2[:bn]], axis=1)

    (w_main, w_small, wup, gla_b3, dtb4, alog4, dskip_row, w_router, b_router) = _prep_weights(
        w_in[0], gla_w_up[0], gla_b[0], ssd_dt_bias[0], ssd_a_log[0], ssd_d[0], router_group_w[0],
        router_group_b[0], router_expert_w[0], router_expert_b[0])

    proj, small = _inproj(ctx, x, mod1, norm_mix, w_main, w_small)
    xact = _conv(proj, ssd_conv_w[0].reshape(9, SSD_XBC), ssd_conv_b, ctx_len, seq)
    lvm = jnp.asarray(np.stack([_level_matrix(CHUNK, False), _level_matrix(CHUNK, True)]))
    o_gla = _gla(proj, small, wup, gla_b3, lvm, ctx_len, seq)
    y_ssd = _ssd(xact, small, dtb4, alog4, dskip_row, ctx_len, seq)
    hlat, h2p, route, cnt = _outproj(o_gla, proj, y_ssd, x, mod2, gla_norm, ssd_norm, norm_ffn,
                                     w_out[0].astype(BF16), w_router, b_router, ctx_len)
    n_tok = bn * seq
    pos, te, nt, pad0, padn, seg, nxt = _plan(route.reshape(n_tok, LANES), cnt)
    te, nt = te.reshape(-1), nt.reshape(-1)
    xs = _dispatch(pad0.reshape(-1), padn.reshape(-1), nt, pos, h2p.reshape((n_tok,) + ROW_TILE),
                   _moe_tiles(n_tok) * MOE_TILE)
    ys = _experts(te, nt, seg.reshape(-1), nxt.reshape(-1), xs, expert_w_gate[0], expert_w_up[0], expert_w_down[0])
    return _combine(pos, route, hlat, mod2[:, 3:4, :], final_norm[None, :], ys)
```

```python
import functools
import math

import numpy as np
import jax
import jax.numpy as jnp
from jax import lax
from jax.experimental import pallas as pl
from jax.experimental.pallas import tpu as pltpu

F32 = jnp.float32
BF16 = jnp.bfloat16

D_MODEL = 1024
GRID_W = 64
EPS = 1e-6

GLA_HEADS = 4
GLA_DK = 128
GLA_DV = 256
GLA_KEY = GLA_HEADS * GLA_DK
GLA_VAL = GLA_HEADS * GLA_DV
GLA_RANK = 16
GLA_GATE_NORM = 16.0

SSD_HEADS = 16
SSD_P = 64
SSD_INNER = SSD_HEADS * SSD_P
SSD_GROUPS = 2
SSD_HPG = SSD_HEADS // SSD_GROUPS
SSD_N = 128
SSD_XBC = SSD_INNER + 2 * SSD_GROUPS * SSD_N
SSD_GW = SSD_HPG * SSD_P

N_GROUPS = 4
EXPERTS_PER_GROUP = 8
N_EXPERTS = N_GROUPS * EXPERTS_PER_GROUP
D_FF = 512

LANES = 128
SUBLANES = 8
VMEM_LIMIT = 56 * 1024 * 1024

PC_Q = 0
PC_K = PC_Q + GLA_KEY
PC_V = PC_K + GLA_KEY
PC_R = PC_V + GLA_VAL
PC_Z = PC_R + GLA_VAL
PC_XBC = PC_Z + SSD_INNER
PROJ_W = PC_XBC + SSD_XBC
SMALL_DT0 = 2 * GLA_RANK
SMALL_ROT0 = 2
SMALL_W = LANES * (SMALL_ROT0 + 2 * SSD_GROUPS)

CHUNK = 128
EXPERT_LANE0 = 32


def _cparams(sem):
    return pltpu.CompilerParams(dimension_semantics=sem, vmem_limit_bytes=VMEM_LIMIT)


def _silu(x):
    return x / (1.0 + jnp.exp(-x))


def _softplus(x):
    return jnp.maximum(x, 0.0) + jnp.log(1.0 + jnp.exp(-jnp.abs(x)))


def _log_sigmoid(x):
    return jnp.minimum(x, 0.0) - jnp.log(1.0 + jnp.exp(-jnp.abs(x)))


def _bdot(a, b):
    return jnp.dot(a.astype(BF16), b.astype(BF16), preferred_element_type=F32)


def _bdot_nt(a, b):
    return lax.dot_general(a.astype(BF16), b.astype(BF16), (((1,), (1,)), ((), ())),
                           preferred_element_type=F32)


def _adaln_body(c_ref, w_ref, b_ref, o_ref):
    o_ref[...] = _bdot(_silu(c_ref[...]), w_ref[...]) + b_ref[...]


def _adaln(c8, w, b):
    n = w.shape[1]
    bn = 1024
    return pl.pallas_call(
        _adaln_body,
        grid=(n // bn,),
        in_specs=[pl.BlockSpec((8, D_MODEL), lambda j: (0, 0)),
                  pl.BlockSpec((D_MODEL, bn), lambda j: (0, j)),
                  pl.BlockSpec((1, bn), lambda j: (0, j))],
        out_specs=pl.BlockSpec((8, bn), lambda j: (0, j)),
        out_shape=jax.ShapeDtypeStruct((8, n), F32),
        compiler_params=_cparams(("arbitrary",)),
        name="adaln",
    )(c8, w, b)


INPROJ_TM = 256
INPROJ_NC = 512


def _inproj_body(ctx_ref, x_ref, mod_ref, nw_ref, w_ref, ws_ref, proj_ref, small_ref):
    is_ctx = pl.program_id(1) == 0
    xin = jnp.where(is_ctx, ctx_ref[0], x_ref[0])
    m = mod_ref[0]
    shift = jnp.where(is_ctx, m[0:1], m[2:3])
    scale = jnp.where(is_ctx, m[1:2], m[3:4])
    ms = jnp.mean(xin * xin, axis=-1, keepdims=True)
    h = xin * lax.rsqrt(ms + EPS) * nw_ref[...]
    hb = (h * (1.0 + scale) + shift).astype(BF16)
    for n in range(PROJ_W // INPROJ_NC):
        sl = slice(n * INPROJ_NC, (n + 1) * INPROJ_NC)
        proj_ref[0, :, sl] = jnp.dot(hb, w_ref[:, sl], preferred_element_type=F32).astype(BF16)
    sm = jnp.dot(hb, ws_ref[...], preferred_element_type=F32)
    small_ref[0, :, 0:LANES] = sm
    small_ref[0, :, LANES:SMALL_ROT0 * LANES] = jnp.zeros((sm.shape[0], (SMALL_ROT0 - 1) * LANES), F32)
    for k in range(2 * SSD_GROUPS):
        lane0 = SMALL_DT0 + k * SSD_HPG
        small_ref[0, :, (SMALL_ROT0 + k) * LANES:(SMALL_ROT0 + k + 1) * LANES] = pltpu.roll(sm, LANES - lane0, 1)


def _inproj(ctx, x, mod1, norm_w, w_main, w_small):
    bn, seq, _ = x.shape
    ctx_len = ctx.shape[1]
    tm = INPROJ_TM
    assert ctx_len == tm and seq % tm == 0
    tt = ctx_len + seq
    nj = tt // tm
    return pl.pallas_call(
        _inproj_body,
        grid=(bn, nj),
        in_specs=[pl.BlockSpec((1, tm, D_MODEL), lambda b, j: (b, 0, 0)),
                  pl.BlockSpec((1, tm, D_MODEL), lambda b, j: (b, jnp.maximum(j - 1, 0), 0)),
                  pl.BlockSpec((1, 4, D_MODEL), lambda b, j: (b, 0, 0)),
                  pl.BlockSpec((1, D_MODEL), lambda b, j: (0, 0)),
                  pl.BlockSpec((D_MODEL, PROJ_W), lambda b, j: (0, 0)),
                  pl.BlockSpec((D_MODEL, LANES), lambda b, j: (0, 0))],
        out_specs=[pl.BlockSpec((1, tm, PROJ_W), lambda b, j: (b, j, 0)),
                   pl.BlockSpec((1, tm, SMALL_W), lambda b, j: (b, j, 0))],
        out_shape=[jax.ShapeDtypeStruct((bn, tt, PROJ_W), BF16),
                   jax.ShapeDtypeStruct((bn, tt, SMALL_W), F32)],
        compiler_params=_cparams(("arbitrary", "arbitrary")),
        name="inproj",
    )(ctx, x, mod1, norm_w, w_main, w_small)


CONV_CB = 256
CONV_TT = 256


def _conv_body(ctx_len, seq, p_ref, w_ref, b_ref, o_ref, s_ref):
    lat0 = ctx_len + GRID_W
    tt = CONV_TT
    n_tiles = seq // tt
    zeros = jnp.zeros((GRID_W, CONV_CB), F32)
    s_ref[0:ctx_len, :] = p_ref[0, 0:ctx_len, :].astype(F32)
    s_ref[ctx_len:lat0, :] = zeros
    s_ref[lat0 + seq:lat0 + seq + GRID_W, :] = zeros

    def copy_tile(t, c):
        src = pl.multiple_of(ctx_len + t * tt, tt)
        dst = pl.multiple_of(lat0 + t * tt, GRID_W)
        s_ref[pl.ds(dst, tt), :] = p_ref[0, pl.ds(src, tt), :].astype(F32)
        return c

    lax.fori_loop(0, n_tiles, copy_tile, 0)

    w = w_ref[...]
    bias = b_ref[...]
    pos = lax.broadcasted_iota(jnp.int32, (tt, 1), 0)

    def conv_rows(rows, first, last):
        left, mid, right = [sum(xr * w[3 * kh + kw:3 * kh + kw + 1] for xr, kh in rows) for kw in range(3)]
        return (bias + mid + jnp.where(first, 0.0, pltpu.roll(left, 1, 0))
                + jnp.where(last, 0.0, pltpu.roll(right, tt - 1, 0)))

    acc = conv_rows([(s_ref[0:ctx_len, :], 1)], pos == 0, pos == ctx_len - 1)
    o_ref[0, 0:ctx_len, :] = _silu(acc).astype(BF16)

    col = pos % GRID_W
    first = col == 0
    last = col == GRID_W - 1

    def tile(t, c):
        base = lat0 + t * tt
        rows = [(s_ref[pl.ds(pl.multiple_of(base + (kh - 1) * GRID_W, GRID_W), tt), :], kh) for kh in range(3)]
        dst = pl.multiple_of(ctx_len + t * tt, tt)
        o_ref[0, pl.ds(dst, tt), :] = _silu(conv_rows(rows, first, last)).astype(BF16)
        return c

    lax.fori_loop(0, n_tiles, tile, 0)


def _conv(proj, conv_w9, conv_b, ctx_len, seq):
    bn, tt, _ = proj.shape
    assert ctx_len == CONV_TT and seq % CONV_TT == 0 and CONV_TT % GRID_W == 0
    ncb = SSD_XBC // CONV_CB
    cb0 = PC_XBC // CONV_CB
    return pl.pallas_call(
        functools.partial(_conv_body, ctx_len, seq),
        grid=(bn, ncb),
        in_specs=[pl.BlockSpec((1, tt, CONV_CB), lambda b, j: (b, 0, cb0 + j)),
                  pl.BlockSpec((9, CONV_CB), lambda b, j: (0, j)),
                  pl.BlockSpec((1, CONV_CB), lambda b, j: (0, j))],
        out_specs=pl.BlockSpec((1, tt, CONV_CB), lambda b, j: (b, 0, j)),
        out_shape=jax.ShapeDtypeStruct((bn, tt, SSD_XBC), BF16),
        scratch_shapes=[pltpu.VMEM((tt + 2 * GRID_W, CONV_CB), F32)],
        compiler_params=_cparams(("arbitrary", "arbitrary")),
        name="conv",
    )(proj, conv_w9, conv_b)


def _bwd_chunk(s, n_ctx, n_steps):
    return jnp.where(s < n_ctx, n_ctx - 1 - s, n_steps + n_ctx - 1 - s)


def _out_row(c, n_ctx, n):
    return pl.multiple_of(jnp.maximum(c - n_ctx, 0) * n, n)


def _scan_sum(a, rev):
    n = a.shape[0]
    row = lax.broadcasted_iota(jnp.int32, (n, 1), 0)
    sh = 1
    while sh < n:
        if rev:
            a = a + jnp.where(row < n - sh, pltpu.roll(a, n - sh, 0), 0.0)
        else:
            a = a + jnp.where(row >= sh, pltpu.roll(a, sh, 0), 0.0)
        sh *= 2
    return a


def _level_matrix(n, rev):
    i = np.arange(n)[:, None]
    j = np.arange(n)[None, :]
    x = i ^ j
    lv = np.where(x > 0, np.floor(np.log2(np.maximum(x, 1))).astype(np.int64) + 1, 0)
    earlier = (j > i) if rev else (j < i)
    return np.where(earlier, lv, 0).astype(np.int32)


GLA_HPS = 4


def _gla_dir(rev, emit, dst, qkv_ref, sm_ref, wup, gb, lv, o_ref, s_ref):
    n = CHUNK
    q = qkv_ref[0, :, PC_Q:PC_K].astype(F32) * (GLA_DK ** -0.5)
    k = qkv_ref[0, :, PC_K:PC_V].astype(F32)
    x = jnp.dot(sm_ref[0].astype(BF16), wup, preferred_element_type=F32) + gb
    b = _scan_sum(_log_sigmoid(x) / GLA_GATE_NORM, rev)
    row = lax.broadcasted_iota(jnp.int32, (n, 1), 0)
    b_end = b[0:1] if rev else b[n - 1:n]
    k_end = k * jnp.exp(b_end - b)
    dec_end = jnp.exp(b_end)
    if emit:
        q_in = q * jnp.exp(b)

    att = [jnp.zeros((n, n), F32) for _ in range(GLA_HPS)]
    e_h = b
    h = 1
    level = 1
    while emit and h < n:
        upper = (row & h) != 0
        if 2 * h < SUBLANES:
            if rev:
                e = jnp.where(upper, e_h - b, b - pltpu.roll(e_h, n - h, 0))
            else:
                e = jnp.where(upper, b - pltpu.roll(e_h, h, 0), e_h - b)
        else:
            blocks = b.reshape(n // (2 * h), 2 * h, b.shape[1])
            mid = h if rev else h - 1
            ref = jnp.broadcast_to(blocks[:, mid:mid + 1, :], blocks.shape).reshape(b.shape)
            e = -jnp.abs(b - ref)
        w = jnp.exp(e)
        qw = (q * w).astype(BF16)
        kw = (k * w).astype(BF16)
        mask = lv == level
        for hh in range(GLA_HPS):
            ks = slice(hh * GLA_DK, (hh + 1) * GLA_DK)
            att[hh] = att[hh] + jnp.where(mask, _bdot_nt(qw[:, ks], kw[:, ks]), 0.0)
        if 4 * h < SUBLANES:
            if rev:
                e_h = jnp.where(upper, pltpu.roll(e_h, h, 0), e_h)
            else:
                e_h = jnp.where(upper, e_h, pltpu.roll(e_h, n - h, 0))
        h *= 2
        level += 1

    for hh in range(GLA_HPS):
        ks = slice(hh * GLA_DK, (hh + 1) * GLA_DK)
        vs = slice(hh * GLA_DV, (hh + 1) * GLA_DV)
        v = qkv_ref[0, :, PC_V + hh * GLA_DV:PC_V + (hh + 1) * GLA_DV]
        s_old = s_ref[hh]
        if emit:
            o = jnp.dot(att[hh].astype(BF16), v, preferred_element_type=F32)
            o = o + jnp.sum(q[:, ks] * k[:, ks], axis=-1, keepdims=True) * v.astype(F32)
            o = o + _bdot(q_in[:, ks], s_old)
            o_ref[0, pl.ds(dst, n), vs] += o
        dec_col = jnp.broadcast_to(dec_end[:, ks], (n, GLA_DK)).T[:, 0:1]
        s_ref[hh] = dec_col * s_old + jnp.dot(k_end[:, ks].T.astype(BF16), v, preferred_element_type=F32)


def _gla_body(n_ctx, seq, qkvf, smf, qkvb, smb, wup_ref, gb_ref, lv_ref, o_ref, s_ref):
    step = pl.program_id(2)
    n = CHUNK

    @pl.when(step == 0)
    def _():
        s_ref[...] = jnp.zeros_like(s_ref)
        o_ref[...] = jnp.zeros_like(o_ref)

    def both_directions(emit):
        cb = _bwd_chunk(step, n_ctx, pl.num_programs(2))
        _gla_dir(False, emit, _out_row(step, n_ctx, n), qkvf, smf, wup_ref[0], gb_ref[0], lv_ref[0],
                 o_ref, s_ref.at[0])
        _gla_dir(True, emit, _out_row(cb, n_ctx, n), qkvb, smb, wup_ref[1], gb_ref[1], lv_ref[1],
                 o_ref, s_ref.at[1])

    pl.when(step < n_ctx)(functools.partial(both_directions, False))
    pl.when(step >= n_ctx)(functools.partial(both_directions, True))


def _gla(proj, small, wup_pad, gla_b3, lvm, ctx_len, seq):
    bn, tt, _ = proj.shape
    n = CHUNK
    assert ctx_len % n == 0 and seq % n == 0
    n_ctx = ctx_len // n
    n_steps = tt // n
    assert GLA_HPS == GLA_HEADS and PC_Q == 0 and PC_R % LANES == 0
    hw = GLA_HPS * GLA_DK
    vw = GLA_HPS * GLA_DV
    fwd = lambda s: s
    bwd = lambda s: _bwd_chunk(s, n_ctx, n_steps)

    def chunk_specs(c):
        return [pl.BlockSpec((1, n, PC_R), lambda b, h, s: (b, c(s), 0)),
                pl.BlockSpec((1, n, LANES), lambda b, h, s: (b, c(s), 0))]

    return pl.pallas_call(
        functools.partial(_gla_body, n_ctx, seq),
        grid=(bn, GLA_HEADS // GLA_HPS, n_steps),
        in_specs=chunk_specs(fwd) + chunk_specs(bwd) + [
            pl.BlockSpec((2, LANES, hw), lambda b, h, s: (0, 0, h)),
            pl.BlockSpec((2, 1, hw), lambda b, h, s: (0, 0, h)),
            pl.BlockSpec((2, n, n), lambda b, h, s: (0, 0, 0))],
        out_specs=pl.BlockSpec((1, seq, vw), lambda b, h, s: (b, 0, h)),
        out_shape=jax.ShapeDtypeStruct((bn, seq, GLA_VAL), F32),
        scratch_shapes=[pltpu.VMEM((2, GLA_HPS, GLA_DK, GLA_DV), F32)],
        compiler_params=_cparams(("arbitrary",) * 3),
        name="gla",
    )(proj, small, proj, small, wup_pad, gla_b3, lvm)


def _ssd_dir(rev, first, emit, g, dst, xbc_ref, sm_ref, dtb, alog, dsk_ref, y_ref, s_ref):
    n = CHUNK
    gl = slice(g * LANES, (g + 1) * LANES)
    g0 = g * SSD_GW
    b_cols = slice(SSD_INNER + g * SSD_N, SSD_INNER + (g + 1) * SSD_N)
    c_cols = slice(SSD_INNER + (SSD_GROUPS + g) * SSD_N, SSD_INNER + (SSD_GROUPS + g + 1) * SSD_N)
    lane = lax.broadcasted_iota(jnp.int32, (1, LANES), 1)
    dt = _softplus(sm_ref[0, :, gl] + dtb)
    a_neg = jnp.where(lane < SSD_HPG, -jnp.exp(alog), 0.0)
    acum = _scan_sum(dt * a_neg, rev)
    bm_t = xbc_ref[0, :, b_cols].astype(F32).T.astype(BF16)
    if emit:
        acum_t = acum.T
        cm = xbc_ref[0, :, c_cols]
        ii = lax.broadcasted_iota(jnp.int32, (n, n), 0)
        jj = lax.broadcasted_iota(jnp.int32, (n, n), 1)
        causal = (jj >= ii) if rev else (jj <= ii)
        cb = _bdot_nt(cm, xbc_ref[0, :, b_cols])
    low = lane < SSD_P
    for p in range(SSD_HPG // 2):
        slab = slice(p * LANES, (p + 1) * LANES)
        out = slice(g0 + p * LANES, g0 + (p + 1) * LANES)
        h0, h1 = 2 * p, 2 * p + 1
        col0 = jnp.broadcast_to(acum[:, h0:h0 + 1], (n, LANES))
        col1 = jnp.broadcast_to(acum[:, h1:h1 + 1], (n, LANES))
        col = jnp.where(low, col0, col1)
        dt2 = jnp.where(low, jnp.broadcast_to(dt[:, h0:h0 + 1], (n, LANES)),
                        jnp.broadcast_to(dt[:, h1:h1 + 1], (n, LANES)))
        end = col[0:1] if rev else col[n - 1:n]
        x = xbc_ref[0, :, out].astype(F32)
        s_old = s_ref[:, slab]
        xdt = x * dt2
        if emit:
            m0 = (cb * jnp.exp(jnp.where(causal, col0 - acum_t[h0:h0 + 1, :], -jnp.inf))).astype(BF16)
            m1 = (cb * jnp.exp(jnp.where(causal, col1 - acum_t[h1:h1 + 1, :], -jnp.inf))).astype(BF16)
            rhs = jnp.concatenate([jnp.where(low, xdt, 0.0), jnp.where(low, 0.0, xdt)], axis=0).astype(BF16)
            y = jnp.dot(jnp.concatenate([m0, m1], axis=1), rhs, preferred_element_type=F32)
            y = y + jnp.dot(cm, s_old.astype(BF16), preferred_element_type=F32) * jnp.exp(col)
            if first:
                y = y + x * dsk_ref[:, out]
            y_ref[0, pl.ds(dst, n), out] += y
        new = jnp.dot(bm_t, (xdt * jnp.exp(end - col)).astype(BF16), preferred_element_type=F32)
        s_ref[:, slab] = s_old * jnp.exp(end) + new


def _ssd_body(n_ctx, seq, xbcf, smf, xbcb, smb, dtb_ref, alog_ref, dsk_ref, y_ref, s_ref):
    step = pl.program_id(1)
    n = CHUNK

    @pl.when(step == 0)
    def _():
        s_ref[...] = jnp.zeros_like(s_ref)
        y_ref[...] = jnp.zeros_like(y_ref)

    def both_directions(emit):
        dst_f = _out_row(step, n_ctx, n)
        dst_b = _out_row(_bwd_chunk(step, n_ctx, pl.num_programs(1)), n_ctx, n)
        for g in range(SSD_GROUPS):
            _ssd_dir(False, True, emit, g, dst_f, xbcf, smf, dtb_ref[0, g], alog_ref[0, g], dsk_ref, y_ref,
                     s_ref.at[0, g])
            _ssd_dir(True, False, emit, g, dst_b, xbcb, smb, dtb_ref[1, g], alog_ref[1, g], dsk_ref, y_ref,
                     s_ref.at[1, g])

    pl.when(step < n_ctx)(functools.partial(both_directions, False))
    pl.when(step >= n_ctx)(functools.partial(both_directions, True))


def _ssd(xact, small, dtb4, alog4, dskip_row, ctx_len, seq):
    bn, tt, _ = xact.shape
    n = CHUNK
    n_ctx = ctx_len // n
    n_steps = tt // n
    fwd = lambda s: s
    bwd = lambda s: _bwd_chunk(s, n_ctx, n_steps)

    assert SMALL_ROT0 % SSD_GROUPS == 0
    sblk = SMALL_ROT0 // SSD_GROUPS

    def chunk_specs(c, d):
        return [pl.BlockSpec((1, n, SSD_XBC), lambda b, s: (b, c(s), 0)),
                pl.BlockSpec((1, n, SSD_GROUPS * LANES), lambda b, s: (b, c(s), sblk + d))]

    whole4 = pl.BlockSpec((2, SSD_GROUPS, 1, LANES), lambda b, s: (0, 0, 0, 0))
    return pl.pallas_call(
        functools.partial(_ssd_body, n_ctx, seq),
        grid=(bn, n_steps),
        in_specs=chunk_specs(fwd, 0) + chunk_specs(bwd, 1) + [
            whole4, whole4, pl.BlockSpec((1, SSD_INNER), lambda b, s: (0, 0))],
        out_specs=pl.BlockSpec((1, seq, SSD_INNER), lambda b, s: (b, 0, 0)),
        out_shape=jax.ShapeDtypeStruct((bn, seq, SSD_INNER), F32),
        scratch_shapes=[pltpu.VMEM((2, SSD_GROUPS, SSD_N, SSD_GW), F32)],
        compiler_params=_cparams(("arbitrary",) * 2),
        name="ssd",
    )(xact, small, xact, small, dtb4, alog4, dskip_row)


OUT_TM = 256
RT_E1, RT_E2, RT_RANK1, RT_RANK2, RT_W1, RT_W2 = range(6)


def _rms(x, w):
    return x * lax.rsqrt(jnp.mean(x * x, axis=-1, keepdims=True) + EPS) * w


ROW_TILE = (D_MODEL // LANES, LANES)


def _store_row_tiles(ref, x):
    ref[...] = x.reshape((x.shape[0],) + ROW_TILE)


def _load_row_tiles(ref):
    return ref[...].reshape(ref.shape[0], D_MODEL)


def _outproj_body(o_ref, r_ref, y_ref, z_ref, x_ref, mod_ref, gn_ref, sn_ref, fn_ref, wo_ref, wr_ref, br_ref,
                  hlat_ref, h2_ref, route_ref, cnt_ref, carry_ref):
    tm = OUT_TM

    @pl.when((pl.program_id(0) == 0) & (pl.program_id(1) == 0))
    def _():
        carry_ref[...] = jnp.zeros_like(carry_ref)

    parts = []
    for h in range(GLA_HEADS):
        hs = slice(h * GLA_DV, (h + 1) * GLA_DV)
        parts.append((_rms(o_ref[0, :, hs], gn_ref[...]) * _silu(r_ref[0, :, hs].astype(F32))).astype(BF16))
    for g in range(SSD_GROUPS):
        gs = slice(g * SSD_GW, (g + 1) * SSD_GW)
        yg = y_ref[0, :, gs] * _silu(z_ref[0, :, gs].astype(F32))
        parts.append(_rms(yg, sn_ref[:, gs]).astype(BF16))
    mix = jnp.concatenate(parts, axis=-1)
    m = mod_ref[0]
    hlat = x_ref[0] + m[0:1] * jnp.dot(mix, wo_ref[...], preferred_element_type=F32)
    hlat_ref[0] = hlat
    h2f = _rms(hlat, fn_ref[...]) * (1.0 + m[2:3]) + m[1:2]
    _store_row_tiles(h2_ref.at[0], h2f)

    lg = jnp.dot(h2f.astype(BF16), wr_ref[...], preferred_element_type=F32) + br_ref[...]
    lanef = lax.broadcasted_iota(jnp.int32, (tm, LANES), 1).astype(F32)
    ninf = -jnp.inf
    is_g = lanef < float(N_GROUPS)
    gl = jnp.where(is_g, lg, ninf)
    gmax = jnp.max(gl, axis=-1, keepdims=True)
    gsel = jnp.min(jnp.where(gl == gmax, lanef, float(LANES)), axis=-1, keepdims=True)
    pg = 1.0 / jnp.sum(jnp.where(is_g, jnp.exp(lg - gmax), 0.0), axis=-1, keepdims=True)
    lo = float(EXPERT_LANE0) + float(EXPERTS_PER_GROUP) * gsel
    in_grp = (lanef >= lo) & (lanef < lo + float(EXPERTS_PER_GROUP))
    el = jnp.where(in_grp, lg, ninf)
    v1 = jnp.max(el, axis=-1, keepdims=True)
    i1 = jnp.min(jnp.where(el == v1, lanef, float(LANES)), axis=-1, keepdims=True)
    el2 = jnp.where(lanef == i1, ninf, el)
    v2 = jnp.max(el2, axis=-1, keepdims=True)
    i2 = jnp.min(jnp.where(el2 == v2, lanef, float(LANES)), axis=-1, keepdims=True)
    t = jnp.exp(v2 - v1)
    w1 = pg / (1.0 + t)
    w2 = pg * t / (1.0 + t)

    sel1 = lanef == i1
    sel2 = lanef == i2
    member = jnp.where(sel1 | sel2, 1.0, 0.0)
    ii = lax.broadcasted_iota(jnp.int32, (tm, tm), 0)
    jj = lax.broadcasted_iota(jnp.int32, (tm, tm), 1)
    before = jnp.where(jj < ii, 1.0, 0.0).astype(BF16)
    ranks = jnp.dot(before, member.astype(BF16), preferred_element_type=F32) + carry_ref[...]
    rank1 = jnp.sum(jnp.where(sel1, ranks, 0.0), axis=-1, keepdims=True)
    rank2 = jnp.sum(jnp.where(sel2, ranks, 0.0), axis=-1, keepdims=True)
    carry = carry_ref[...] + jnp.sum(member, axis=0, keepdims=True)
    carry_ref[...] = carry
    cnt_ref[...] = carry
    rec = jnp.zeros((tm, LANES), F32)
    for lane_id, val in ((RT_E1, i1 - float(EXPERT_LANE0)), (RT_E2, i2 - float(EXPERT_LANE0)),
                         (RT_RANK1, rank1), (RT_RANK2, rank2), (RT_W1, w1), (RT_W2, w2)):
        rec = jnp.where(lanef == float(lane_id), val, rec)
    route_ref[0] = rec


def _outproj(o_gla, proj, y_ssd, x, mod2, gn_row, sn_row, fn_row, w_out, w_router, b_router, ctx_len):
    bn, seq, _ = x.shape
    tm = OUT_TM
    assert ctx_len % tm == 0 and seq % tm == 0
    j0 = ctx_len // tm
    rb = PC_R // GLA_VAL
    zb = PC_Z // SSD_INNER
    tok = lambda b, j: (b, j, 0)
    const = lambda b, j: (0, 0)
    return pl.pallas_call(
        _outproj_body,
        grid=(bn, seq // tm),
        in_specs=[pl.BlockSpec((1, tm, GLA_VAL), tok),
                  pl.BlockSpec((1, tm, GLA_VAL), lambda b, j: (b, j0 + j, rb)),
                  pl.BlockSpec((1, tm, SSD_INNER), tok),
                  pl.BlockSpec((1, tm, SSD_INNER), lambda b, j: (b, j0 + j, zb)),
                  pl.BlockSpec((1, tm, D_MODEL), tok),
                  pl.BlockSpec((1, 4, D_MODEL), lambda b, j: (b, 0, 0)),
                  pl.BlockSpec((1, GLA_DV), const),
                  pl.BlockSpec((1, SSD_INNER), const),
                  pl.BlockSpec((1, D_MODEL), const),
                  pl.BlockSpec((GLA_VAL + SSD_INNER, D_MODEL), const),
                  pl.BlockSpec((D_MODEL, LANES), const),
                  pl.BlockSpec((1, LANES), const)],
        out_specs=[pl.BlockSpec((1, tm, D_MODEL), tok),
                   pl.BlockSpec((1, tm) + ROW_TILE, lambda b, j: (b, j, 0, 0)),
                   pl.BlockSpec((1, tm, LANES), tok),
                   pl.BlockSpec((1, LANES), const)],
        out_shape=[jax.ShapeDtypeStruct((bn, seq, D_MODEL), F32),
                   jax.ShapeDtypeStruct((bn, seq) + ROW_TILE, F32),
                   jax.ShapeDtypeStruct((bn, seq, LANES), F32),
                   jax.ShapeDtypeStruct((1, LANES), F32)],
        scratch_shapes=[pltpu.VMEM((1, LANES), F32)],
        compiler_params=_cparams(("arbitrary", "arbitrary")),
        name="outproj",
    )(o_gla, proj, y_ssd, proj, x, mod2, gn_row, sn_row, fn_row, w_out, w_router, b_router)


MOE_TILE = 256
PLAN_TM = 1024
DISPATCH_TM = 1024
COMBINE_TM = 512


def _moe_tiles(n_tokens):
    return (2 * n_tokens) // MOE_TILE + N_EXPERTS


def _plan_body(nt_pad, route_ref, cnt_ref, pos_ref, te_ref, nt_ref, pad0_ref, padn_ref, seg_ref, nxt_ref):
    lane = lax.broadcasted_iota(jnp.int32, (1, LANES), 1).astype(F32)
    is_e = (lane >= float(EXPERT_LANE0)) & (lane < float(EXPERT_LANE0 + N_EXPERTS))
    tiles = jnp.where(is_e, jnp.floor((cnt_ref[...] + float(MOE_TILE - 1)) / float(MOE_TILE)), 0.0)
    ii = lax.broadcasted_iota(jnp.int32, (LANES, LANES), 0)
    jj = lax.broadcasted_iota(jnp.int32, (LANES, LANES), 1)
    lower = jnp.where(ii < jj, 1.0, 0.0).astype(BF16)
    first_tile = jnp.dot(jnp.broadcast_to(tiles, (8, LANES)).astype(BF16), lower,
                         preferred_element_type=F32)[0:1]
    off_row = first_tile * float(MOE_TILE)
    for sb in range(PLAN_TM // LANES):
        r = route_ref[sb * LANES:(sb + 1) * LANES, :]
        o1 = jnp.sum(jnp.where(lane == r[:, RT_E1:RT_E1 + 1] + float(EXPERT_LANE0), off_row, 0.0),
                     axis=-1, keepdims=True)
        o2 = jnp.sum(jnp.where(lane == r[:, RT_E2:RT_E2 + 1] + float(EXPERT_LANE0), off_row, 0.0),
                     axis=-1, keepdims=True)
        p = jnp.where(lane == 0.0, o1 + r[:, RT_RANK1:RT_RANK1 + 1],
                      jnp.where(lane == 1.0, o2 + r[:, RT_RANK2:RT_RANK2 + 1], 0.0))
        pos_ref[sb] = p.T[0:8, :].astype(jnp.int32)

    @pl.when(pl.program_id(0) == 0)
    def _():
        ti = lax.broadcasted_iota(jnp.int32, (nt_pad, 1), 0).astype(F32)
        done = jnp.where(is_e & (first_tile + tiles <= ti), 1.0, 0.0)
        te = jnp.minimum(jnp.sum(done, axis=-1, keepdims=True), float(N_EXPERTS - 1))
        te_ref[...] = te.astype(jnp.int32)
        nt_ref[...] = jnp.sum(tiles, axis=-1, keepdims=True).astype(jnp.int32)
        lane_e = lane - float(EXPERT_LANE0)
        nonempty = is_e & (tiles > 0.0)
        seg_ref[...] = jnp.sum(jnp.where(nonempty & (lane_e < te), 1.0, 0.0), axis=-1,
                               keepdims=True).astype(jnp.int32)
        nxt = jnp.min(jnp.where(nonempty & (lane_e > te), lane_e, float(N_EXPERTS)), axis=-1, keepdims=True)
        nxt_ref[...] = jnp.where(nxt < float(N_EXPERTS), nxt, -1.0).astype(jnp.int32)
        used = jnp.where(is_e, cnt_ref[...], 0.0)
        pad0_ref[...] = (off_row + used).astype(jnp.int32)
        padn_ref[...] = (tiles * float(MOE_TILE) - used).astype(jnp.int32)


def _plan(route, cnt):
    n_tok = route.shape[0]
    nt_pad = _moe_tiles(n_tok)
    nsb = PLAN_TM // LANES
    return pl.pallas_call(
        functools.partial(_plan_body, nt_pad),
        grid=(n_tok // PLAN_TM,),
        in_specs=[pl.BlockSpec((PLAN_TM, LANES), lambda i: (i, 0)),
                  pl.BlockSpec((1, LANES), lambda i: (0, 0))],
        out_specs=[pl.BlockSpec((nsb, 8, LANES), lambda i: (i, 0, 0)),
                   pl.BlockSpec((nt_pad, 1), lambda i: (0, 0)),
                   pl.BlockSpec((1, 1), lambda i: (0, 0)),
                   pl.BlockSpec((1, LANES), lambda i: (0, 0)),
                   pl.BlockSpec((1, LANES), lambda i: (0, 0)),
                   pl.BlockSpec((nt_pad, 1), lambda i: (0, 0)),
                   pl.BlockSpec((nt_pad, 1), lambda i: (0, 0))],
        out_shape=[jax.ShapeDtypeStruct((n_tok // LANES, 8, LANES), jnp.int32),
                   jax.ShapeDtypeStruct((nt_pad, 1), jnp.int32),
                   jax.ShapeDtypeStruct((1, 1), jnp.int32),
                   jax.ShapeDtypeStruct((1, LANES), jnp.int32),
                   jax.ShapeDtypeStruct((1, LANES), jnp.int32),
                   jax.ShapeDtypeStruct((nt_pad, 1), jnp.int32),
                   jax.ShapeDtypeStruct((nt_pad, 1), jnp.int32)],
        compiler_params=_cparams(("arbitrary",)),
        name="moe_plan",
    )(route, cnt)


def _row_copy(src, s_row, dst, d_row, sem):
    return pltpu.make_async_copy(src.at[s_row], dst.at[d_row], sem)


def _issue_rows(n_rows, pos_ref, start_pair):
    for blk in range(n_rows // LANES):
        def issue(l, c, blk=blk):
            start_pair(blk * LANES + l, pos_ref[blk, 0, l], pos_ref[blk, 1, l])
            return c
        lax.fori_loop(0, LANES, issue, 0, unroll=8)


def _drain_rows(n_copies, copy):
    def drain(t, c):
        copy.wait()
        return c
    lax.fori_loop(0, n_copies, drain, 0, unroll=8)


def _dispatch_body(pad0_ref, padn_ref, nt_ref, pos_ref, h2_ref, xs_ref, zero_ref, sem):
    def start_pair(t, p1, p2):
        _row_copy(h2_ref, t, xs_ref, p1, sem).start(priority=0)
        _row_copy(h2_ref, t, xs_ref, p2, sem).start(priority=1)

    _issue_rows(DISPATCH_TM, pos_ref, start_pair)
    _drain_rows(2 * DISPATCH_TM, _row_copy(h2_ref, 0, xs_ref, 0, sem))

    @pl.when(pl.program_id(0) == 0)
    def _():
        zero_ref[...] = jnp.zeros_like(zero_ref)
        n_tiles = xs_ref.shape[0] // MOE_TILE

        def tile_copy(i):
            return pltpu.make_async_copy(zero_ref, xs_ref.at[pl.ds(i * MOE_TILE, MOE_TILE)], sem)

        def fill_tile(i, c):
            tile_copy(i).start()
            return c

        def drain_tile(i, c):
            tile_copy(i).wait()
            return c

        lax.fori_loop(nt_ref[0], n_tiles, fill_tile, 0)
        lax.fori_loop(nt_ref[0], n_tiles, drain_tile, 0)
        for e in range(N_EXPERTS):
            n_pad = padn_ref[EXPERT_LANE0 + e]
            for wait in (False, True):
                row = pad0_ref[EXPERT_LANE0 + e]
                size = MOE_TILE // 2
                while size >= 1:
                    has = (n_pad & size) != 0
                    copy = pltpu.make_async_copy(zero_ref.at[pl.ds(0, size)], xs_ref.at[pl.ds(row, size)], sem)
                    pl.when(has)(copy.wait if wait else copy.start)
                    row = row + jnp.where(has, size, 0)
                    size //= 2


def _dispatch(pad0, padn, nt, pos, h2t, n_rows):
    n_tok = h2t.shape[0]
    nb = DISPATCH_TM // LANES
    grid_spec = pltpu.PrefetchScalarGridSpec(
        num_scalar_prefetch=3,
        grid=(n_tok // DISPATCH_TM,),
        in_specs=[pl.BlockSpec((nb, 8, LANES), lambda i, *_: (i, 0, 0), memory_space=pltpu.SMEM),
                  pl.BlockSpec((DISPATCH_TM,) + ROW_TILE, lambda i, *_: (i, 0, 0))],
        out_specs=pl.BlockSpec(memory_space=pl.ANY),
        scratch_shapes=[pltpu.VMEM((MOE_TILE,) + ROW_TILE, F32), pltpu.SemaphoreType.DMA(())])
    return pl.pallas_call(
        _dispatch_body,
        grid_spec=grid_spec,
        out_shape=jax.ShapeDtypeStruct((n_rows,) + ROW_TILE, F32),
        compiler_params=_cparams(("arbitrary",)),
        name="moe_dispatch",
    )(pad0, padn, nt, pos, h2t)


def _expert_body(te_ref, nt_ref, seg_ref, nxt_ref, xs_ref, wg_ref, wu_ref, wd_ref, ys_ref,
                 wgf_ref, wuf_ref, wdf_ref, wgb_ref, wub_ref, wdb_ref, sem):
    i = pl.program_id(0)

    def copies(e, s):
        return [pltpu.make_async_copy(src.at[e], dst.at[s], sem.at[s])
                for src, dst in ((wg_ref, wgf_ref), (wu_ref, wuf_ref), (wd_ref, wdf_ref))]

    @pl.when(i < nt_ref[0])
    def _():
        seg = seg_ref[i]
        slot = seg % 2
        first = (i == 0) | (seg != seg_ref[jnp.maximum(i - 1, 0)])

        @pl.when(first)
        def _():
            @pl.when(i == 0)
            def _():
                for c in copies(te_ref[i], slot):
                    c.start()

            for c in copies(te_ref[i], slot):
                c.wait()
            wgb_ref[...] = wgf_ref[slot].astype(BF16)
            wub_ref[...] = wuf_ref[slot].astype(BF16)
            wdb_ref[...] = wdf_ref[slot].astype(BF16)

            @pl.when(nxt_ref[i] >= 0)
            def _():
                for c in copies(nxt_ref[i], 1 - slot):
                    c.start()

        x = _load_row_tiles(xs_ref).astype(BF16)
        gate = jnp.dot(x, wgb_ref[...], preferred_element_type=F32)
        up = jnp.dot(x, wub_ref[...], preferred_element_type=F32)
        y = jnp.dot((_silu(gate) * up).astype(BF16), wdb_ref[...], preferred_element_type=F32)
        _store_row_tiles(ys_ref, y)

    @pl.when(i >= nt_ref[0])
    def _():
        ys_ref[...] = jnp.zeros_like(ys_ref)


def _experts(te, nt, seg, nxt, xs, w_gate, w_up, w_down):
    rows = xs.shape[0]
    n_tiles = rows // MOE_TILE
    row_block = pl.BlockSpec((MOE_TILE,) + ROW_TILE, lambda i, *_: (i, 0, 0))
    in_w = (D_MODEL, D_FF)
    out_w = (D_FF, D_MODEL)
    grid_spec = pltpu.PrefetchScalarGridSpec(
        num_scalar_prefetch=4,
        grid=(n_tiles,),
        in_specs=[row_block] + [pl.BlockSpec(memory_space=pl.ANY)] * 3,
        out_specs=row_block,
        scratch_shapes=[pltpu.VMEM((2,) + in_w, F32), pltpu.VMEM((2,) + in_w, F32), pltpu.VMEM((2,) + out_w, F32),
                        pltpu.VMEM(in_w, BF16), pltpu.VMEM(in_w, BF16), pltpu.VMEM(out_w, BF16),
                        pltpu.SemaphoreType.DMA((2,))])
    return pl.pallas_call(
        _expert_body,
        grid_spec=grid_spec,
        out_shape=jax.ShapeDtypeStruct(xs.shape, F32),
        compiler_params=_cparams(("arbitrary",)),
        name="moe_experts",
    )(te, nt, seg, nxt, xs, w_gate, w_up, w_down)


def _combine_body(pos_ref, nxt_ref, route_ref, hlat_ref, g2_ref, fn_ref, ys_ref, out_ref, buf_ref, sem):
    tm = COMBINE_TM
    i = pl.program_id(0)
    slot = i % 2

    def fetch(p_ref, s):
        def start_pair(t, p1, p2):
            _row_copy(ys_ref, p1, buf_ref.at[s, 0], t, sem.at[s]).start(priority=0)
            _row_copy(ys_ref, p2, buf_ref.at[s, 1], t, sem.at[s]).start(priority=1)
        _issue_rows(tm, p_ref, start_pair)

    @pl.when(i == 0)
    def _():
        fetch(pos_ref, 0)

    @pl.when(i + 1 < pl.num_programs(0))
    def _():
        fetch(nxt_ref, 1 - slot)

    _drain_rows(2 * tm, _row_copy(ys_ref, 0, buf_ref.at[slot, 0], 0, sem.at[slot]))

    r = route_ref[0]
    moe = (r[:, RT_W1:RT_W1 + 1] * _load_row_tiles(buf_ref.at[slot, 0])
           + r[:, RT_W2:RT_W2 + 1] * _load_row_tiles(buf_ref.at[slot, 1]))
    out_ref[0] = _rms(hlat_ref[0] + g2_ref[0] * moe, fn_ref[...])


def _combine(pos, route, hlat, g2, fn_row, ys):
    bn, seq, _ = hlat.shape
    tm = COMBINE_TM
    nj = seq // tm
    nb = tm // LANES
    last = bn * nj - 1
    tok = lambda i: (i // nj, i % nj, 0)
    return pl.pallas_call(
        _combine_body,
        grid=(bn * nj,),
        in_specs=[pl.BlockSpec((nb, 8, LANES), lambda i: (i, 0, 0), memory_space=pltpu.SMEM),
                  pl.BlockSpec((nb, 8, LANES), lambda i: (jnp.minimum(i + 1, last), 0, 0),
                               memory_space=pltpu.SMEM),
                  pl.BlockSpec((1, tm, LANES), tok),
                  pl.BlockSpec((1, tm, D_MODEL), tok),
                  pl.BlockSpec((1, 1, D_MODEL), lambda i: (i // nj, 0, 0)),
                  pl.BlockSpec((1, D_MODEL), lambda i: (0, 0)),
                  pl.BlockSpec(memory_space=pl.ANY)],
        out_specs=pl.BlockSpec((1, tm, D_MODEL), tok),
        out_shape=jax.ShapeDtypeStruct((bn, seq, D_MODEL), F32),
        scratch_shapes=[pltpu.VMEM((2, 2, tm) + ROW_TILE, F32), pltpu.SemaphoreType.DMA((2,))],
        compiler_params=_cparams(("arbitrary",)),
        name="moe_combine",
    )(pos, pos, route, hlat, g2, fn_row, ys)


def _prep_weights(w_in, gla_w_up, gla_b, ssd_dt_bias, ssd_a_log, ssd_d, router_group_w, router_group_b,
                  router_expert_w, router_expert_b):
    off_k = GLA_KEY
    off_v = 2 * GLA_KEY
    off_r = off_v + GLA_VAL
    off_g = off_r + GLA_VAL
    off_z = off_g + 2 * GLA_RANK
    off_xbc = off_z + SSD_INNER
    off_dt = off_xbc + SSD_XBC
    w_main = jnp.concatenate([w_in[:, :off_g], w_in[:, off_z:off_dt]], axis=1).astype(BF16)
    w_small = jnp.concatenate([w_in[:, off_g:off_z], w_in[:, off_dt:off_dt + 2 * SSD_HEADS]], axis=1)
    w_small = jnp.pad(w_small, ((0, 0), (0, LANES - w_small.shape[1]))).astype(BF16)
    wup = jnp.zeros((2, LANES, GLA_KEY), F32)
    wup = wup.at[0, 0:GLA_RANK].set(gla_w_up[0]).at[1, GLA_RANK:2 * GLA_RANK].set(gla_w_up[1]).astype(BF16)
    pad_h = lambda t: jnp.pad(t.reshape(2, SSD_GROUPS, 1, SSD_HPG), ((0, 0), (0, 0), (0, 0), (0, LANES - SSD_HPG)))
    w_router = jnp.zeros((D_MODEL, LANES), F32)
    w_router = w_router.at[:, 0:N_GROUPS].set(router_group_w)
    w_router = w_router.at[:, EXPERT_LANE0:EXPERT_LANE0 + N_EXPERTS].set(router_expert_w).astype(BF16)
    b_router = jnp.zeros((1, LANES), F32).at[0, 0:N_GROUPS].set(router_group_b)
    b_router = b_router.at[0, EXPERT_LANE0:EXPERT_LANE0 + N_EXPERTS].set(router_expert_b)
    return (w_main, w_small, wup, gla_b.reshape(2, 1, GLA_KEY), pad_h(ssd_dt_bias), pad_h(ssd_a_log),
            jnp.repeat(ssd_d, SSD_P)[None, :], w_router, b_router)


def kernel(x, c, ctx, c_ctx, w_ada, b_ada, norm_mix, norm_ffn, w_in, gla_w_up, gla_b, gla_norm, ssd_conv_w,
           ssd_conv_b, ssd_dt_bias, ssd_a_log, ssd_d, ssd_norm, w_out, router_group_w, router_group_b,
           router_expert_w, router_expert_b, expert_w_gate, expert_w_up, expert_w_down, final_norm):
    assert w_ada.shape[0] == 1, "single-layer block"
    bn, seq, d = x.shape
    ctx_len = ctx.shape[1]
    assert seq // GRID_W == GRID_W and d == D_MODEL

    c8 = jnp.concatenate([c, c_ctx[None, :], jnp.zeros((8 - bn - 1, d), F32)], axis=0)
    mods = _adaln(c8, w_ada[0], b_ada[0][None, :])
    sh1, sc1, g1, sh2, sc2, g2 = [mods[:, i * d:(i + 1) * d] for i in range(6)]
    ctx_row = lambda t: jnp.broadcast_to(t[bn:bn + 1], (bn, d))
    mod1 = jnp.stack([ctx_row(sh1), ctx_row(sc1), sh1[:bn], sc1[:bn]], axis=1)
    mod2 = jnp.stack([g1[:bn], sh2[:bn], sc2[:bn], g2[:bn]], axis=1)

    (w_main, w_small, wup, gla_b3, dtb4, alog4, dskip_row, w_router, b_router) = _prep_weights(
        w_in[0], gla_w_up[0], gla_b[0], ssd_dt_bias[0], ssd_a_log[0], ssd_d[0], router_group_w[0],
        router_group_b[0], router_expert_w[0], router_expert_b[0])

    proj, small = _inproj(ctx, x, mod1, norm_mix, w_main, w_small)
    xact = _conv(proj, ssd_conv_w[0].reshape(9, SSD_XBC), ssd_conv_b, ctx_len, seq)
    lvm = jnp.asarray(np.stack([_level_matrix(CHUNK, False), _level_matrix(CHUNK, True)]))
    o_gla = _gla(proj, small, wup, gla_b3, lvm, ctx_len, seq)
    y_ssd = _ssd(xact, small, dtb4, alog4, dskip_row, ctx_len, seq)
    hlat, h2p, route, cnt = _outproj(o_gla, proj, y_ssd, x, mod2, gla_norm, ssd_norm, norm_ffn,
                                     w_out[0].astype(BF16), w_router, b_router, ctx_len)
    n_tok = bn * seq
    pos, te, nt, pad0, padn, seg, nxt = _plan(route.reshape(n_tok, LANES), cnt)
    te, nt = te.reshape(-1), nt.reshape(-1)
    xs = _dispatch(pad0.reshape(-1), padn.reshape(-1), nt, pos, h2p.reshape((n_tok,) + ROW_TILE),
                   _moe_tiles(n_tok) * MOE_TILE)
    ys = _experts(te, nt, seg.reshape(-1), nxt.reshape(-1), xs, expert_w_gate[0], expert_w_up[0], expert_w_down[0])
    return _combine(pos, route, hlat, mod2[:, 3:4, :], final_norm[None, :], ys)
```

```python
import functools
import math

import numpy as np
import jax
import jax.numpy as jnp
from jax import lax
from jax.experimental import pallas as pl
from jax.experimental.pallas import tpu as pltpu

F32 = jnp.float32
BF16 = jnp.bfloat16

D_MODEL = 1024
GRID_W = 64
EPS = 1e-6

GLA_HEADS = 4
GLA_DK = 128
GLA_DV = 256
GLA_KEY = GLA_HEADS * GLA_DK
GLA_VAL = GLA_HEADS * GLA_DV
GLA_RANK = 16
GLA_GATE_NORM = 16.0

SSD_HEADS = 16
SSD_P = 64
SSD_INNER = SSD_HEADS * SSD_P
SSD_GROUPS = 2
SSD_HPG = SSD_HEADS // SSD_GROUPS
SSD_N = 128
SSD_XBC = SSD_INNER + 2 * SSD_GROUPS * SSD_N
SSD_GW = SSD_HPG * SSD_P

N_GROUPS = 4
EXPERTS_PER_GROUP = 8
N_EXPERTS = N_GROUPS * EXPERTS_PER_GROUP
D_FF = 512

LANES = 128
SUBLANES = 8
VMEM_LIMIT = 56 * 1024 * 1024

PC_Q = 0
PC_K = PC_Q + GLA_KEY
PC_V = PC_K + GLA_KEY
PC_R = PC_V + GLA_VAL
PC_Z = PC_R + GLA_VAL
PC_XBC = PC_Z + SSD_INNER
PROJ_W = PC_XBC + SSD_XBC
SMALL_DT0 = 2 * GLA_RANK
SMALL_ROT0 = 2
SMALL_W = LANES * (SMALL_ROT0 + 2 * SSD_GROUPS)

CHUNK = 128
EXPERT_LANE0 = 32


def _cparams(sem):
    return pltpu.CompilerParams(dimension_semantics=sem, vmem_limit_bytes=VMEM_LIMIT)


def _silu(x):
    return x / (1.0 + jnp.exp(-x))


def _softplus(x):
    return jnp.maximum(x, 0.0) + jnp.log(1.0 + jnp.exp(-jnp.abs(x)))


def _log_sigmoid(x):
    return jnp.minimum(x, 0.0) - jnp.log(1.0 + jnp.exp(-jnp.abs(x)))


def _bdot(a, b):
    return jnp.dot(a.astype(BF16), b.astype(BF16), preferred_element_type=F32)


def _bdot_nt(a, b):
    return lax.dot_general(a.astype(BF16), b.astype(BF16), (((1,), (1,)), ((), ())),
                           preferred_element_type=F32)


def _adaln_body(c_ref, w_ref, b_ref, o_ref):
    o_ref[...] = _bdot(_silu(c_ref[...]), w_ref[...]) + b_ref[...]


def _adaln(c8, w, b):
    n = w.shape[1]
    bn = 1024
    return pl.pallas_call(
        _adaln_body,
        grid=(n // bn,),
        in_specs=[pl.BlockSpec((8, D_MODEL), lambda j: (0, 0)),
                  pl.BlockSpec((D_MODEL, bn), lambda j: (0, j)),
                  pl.BlockSpec((1, bn), lambda j: (0, j))],
        out_specs=pl.BlockSpec((8, bn), lambda j: (0, j)),
        out_shape=jax.ShapeDtypeStruct((8, n), F32),
        compiler_params=_cparams(("arbitrary",)),
        name="adaln",
    )(c8, w, b)


INPROJ_TM = 256
INPROJ_NC = 512


def _inproj_body(ctx_ref, x_ref, mod_ref, nw_ref, w_ref, ws_ref, proj_ref, small_ref):
    is_ctx = pl.program_id(1) == 0
    xin = jnp.where(is_ctx, ctx_ref[0], x_ref[0])
    m = mod_ref[0]
    shift = jnp.where(is_ctx, m[0:1], m[2:3])
    scale = jnp.where(is_ctx, m[1:2], m[3:4])
    ms = jnp.mean(xin * xin, axis=-1, keepdims=True)
    h = xin * lax.rsqrt(ms + EPS) * nw_ref[...]
    hb = (h * (1.0 + scale) + shift).astype(BF16)
    for n in range(PROJ_W // INPROJ_NC):
        sl = slice(n * INPROJ_NC, (n + 1) * INPROJ_NC)
        proj_ref[0, :, sl] = jnp.dot(hb, w_ref[:, sl], preferred_element_type=F32).astype(BF16)
    sm = jnp.dot(hb, ws_ref[...], preferred_element_type=F32)
    small_ref[0, :, 0:LANES] = sm
    small_ref[0, :, LANES:SMALL_ROT0 * LANES] = jnp.zeros((sm.shape[0], (SMALL_ROT0 - 1) * LANES), F32)
    for k in range(2 * SSD_GROUPS):
        lane0 = SMALL_DT0 + k * SSD_HPG
        small_ref[0, :, (SMALL_ROT0 + k) * LANES:(SMALL_ROT0 + k + 1) * LANES] = pltpu.roll(sm, LANES - lane0, 1)


def _inproj(ctx, x, mod1, norm_w, w_main, w_small):
    bn, seq, _ = x.shape
    ctx_len = ctx.shape[1]
    tm = INPROJ_TM
    assert ctx_len == tm and seq % tm == 0
    tt = ctx_len + seq
    nj = tt // tm
    return pl.pallas_call(
        _inproj_body,
        grid=(bn, nj),
        in_specs=[pl.BlockSpec((1, tm, D_MODEL), lambda b, j: (b, 0, 0)),
                  pl.BlockSpec((1, tm, D_MODEL), lambda b, j: (b, jnp.maximum(j - 1, 0), 0)),
                  pl.BlockSpec((1, 4, D_MODEL), lambda b, j: (b, 0, 0)),
                  pl.BlockSpec((1, D_MODEL), lambda b, j: (0, 0)),
                  pl.BlockSpec((D_MODEL, PROJ_W), lambda b, j: (0, 0)),
                  pl.BlockSpec((D_MODEL, LANES), lambda b, j: (0, 0))],
        out_specs=[pl.BlockSpec((1, tm, PROJ_W), lambda b, j: (b, j, 0)),
                   pl.BlockSpec((1, tm, SMALL_W), lambda b, j: (b, j, 0))],
        out_shape=[jax.ShapeDtypeStruct((bn, tt, PROJ_W), BF16),
                   jax.ShapeDtypeStruct((bn, tt, SMALL_W), F32)],
        compiler_params=_cparams(("arbitrary", "arbitrary")),
        name="inproj",
    )(ctx, x, mod1, norm_w, w_main, w_small)


CONV_CB = 256
CONV_TT = 256


def _conv_body(ctx_len, seq, p_ref, w_ref, b_ref, o_ref, s_ref):
    lat0 = ctx_len + GRID_W
    tt = CONV_TT
    n_tiles = seq // tt
    zeros = jnp.zeros((GRID_W, CONV_CB), F32)
    s_ref[0:ctx_len, :] = p_ref[0, 0:ctx_len, :].astype(F32)
    s_ref[ctx_len:lat0, :] = zeros
    s_ref[lat0 + seq:lat0 + seq + GRID_W, :] = zeros

    def copy_tile(t, c):
        src = pl.multiple_of(ctx_len + t * tt, tt)
        dst = pl.multiple_of(lat0 + t * tt, GRID_W)
        s_ref[pl.ds(dst, tt), :] = p_ref[0, pl.ds(src, tt), :].astype(F32)
        return c

    lax.fori_loop(0, n_tiles, copy_tile, 0)

    w = w_ref[...]
    bias = b_ref[...]
    pos = lax.broadcasted_iota(jnp.int32, (tt, 1), 0)

    def conv_rows(rows, first, last):
        left, mid, right = [sum(xr * w[3 * kh + kw:3 * kh + kw + 1] for xr, kh in rows) for kw in range(3)]
        return (bias + mid + jnp.where(first, 0.0, pltpu.roll(left, 1, 0))
                + jnp.where(last, 0.0, pltpu.roll(right, tt - 1, 0)))

    acc = conv_rows([(s_ref[0:ctx_len, :], 1)], pos == 0, pos == ctx_len - 1)
    o_ref[0, 0:ctx_len, :] = _silu(acc).astype(BF16)

    col = pos % GRID_W
    first = col == 0
    last = col == GRID_W - 1

    def tile(t, c):
        base = lat0 + t * tt
        rows = [(s_ref[pl.ds(pl.multiple_of(base + (kh - 1) * GRID_W, GRID_W), tt), :], kh) for kh in range(3)]
        dst = pl.multiple_of(ctx_len + t * tt, tt)
        o_ref[0, pl.ds(dst, tt), :] = _silu(conv_rows(rows, first, last)).astype(BF16)
        return c

    lax.fori_loop(0, n_tiles, tile, 0)


def _conv(proj, conv_w9, conv_b, ctx_len, seq):
    bn, tt, _ = proj.shape
    assert ctx_len == CONV_TT and seq % CONV_TT == 0 and CONV_TT % GRID_W == 0
    ncb = SSD_XBC // CONV_CB
    cb0 = PC_XBC // CONV_CB
    return pl.pallas_call(
        functools.partial(_conv_body, ctx_len, seq),
        grid=(bn, ncb),
        in_specs=[pl.BlockSpec((1, tt, CONV_CB), lambda b, j: (b, 0, cb0 + j)),
                  pl.BlockSpec((9, CONV_CB), lambda b, j: (0, j)),
                  pl.BlockSpec((1, CONV_CB), lambda b, j: (0, j))],
        out_specs=pl.BlockSpec((1, tt, CONV_CB), lambda b, j: (b, 0, j)),
        out_shape=jax.ShapeDtypeStruct((bn, tt, SSD_XBC), BF16),
        scratch_shapes=[pltpu.VMEM((tt + 2 * GRID_W, CONV_CB), F32)],
        compiler_params=_cparams(("arbitrary", "arbitrary")),
        name="conv",
    )(proj, conv_w9, conv_b)


def _bwd_chunk(s, n_ctx, n_steps):
    return jnp.where(s < n_ctx, n_ctx - 1 - s, n_steps + n_ctx - 1 - s)


def _out_row(c, n_ctx, n):
    return pl.multiple_of(jnp.maximum(c - n_ctx, 0) * n, n)


def _scan_sum(a, rev):
    n = a.shape[0]
    row = lax.broadcasted_iota(jnp.int32, (n, 1), 0)
    sh = 1
    while sh < n:
        if rev:
            a = a + jnp.where(row < n - sh, pltpu.roll(a, n - sh, 0), 0.0)
        else:
            a = a + jnp.where(row >= sh, pltpu.roll(a, sh, 0), 0.0)
        sh *= 2
    return a


def _level_matrix(n, rev):
    i = np.arange(n)[:, None]
    j = np.arange(n)[None, :]
    x = i ^ j
    lv = np.where(x > 0, np.floor(np.log2(np.maximum(x, 1))).astype(np.int64) + 1, 0)
    earlier = (j > i) if rev else (j < i)
    return np.where(earlier, lv, 0).astype(np.int32)


GLA_HPS = 4


def _gla_dir(rev, emit, dst, qkv_ref, sm_ref, wup, gb, lv, o_ref, s_ref):
    n = CHUNK
    q = qkv_ref[0, :, PC_Q:PC_K].astype(F32) * (GLA_DK ** -0.5)
    k = qkv_ref[0, :, PC_K:PC_V].astype(F32)
    x = jnp.dot(sm_ref[0].astype(BF16), wup, preferred_element_type=F32) + gb
    b = _scan_sum(_log_sigmoid(x) / GLA_GATE_NORM, rev)
    row = lax.broadcasted_iota(jnp.int32, (n, 1), 0)
    b_end = b[0:1] if rev else b[n - 1:n]
    k_end = k * jnp.exp(b_end - b)
    dec_end = jnp.exp(b_end)
    if emit:
        q_in = q * jnp.exp(b)

    att = [jnp.zeros((n, n), F32) for _ in range(GLA_HPS)]
    e_h = b
    h = 1
    level = 1
    while emit and h < n:
        upper = (row & h) != 0
        if 2 * h < SUBLANES:
            if rev:
                e = jnp.where(upper, e_h - b, b - pltpu.roll(e_h, n - h, 0))
            else:
                e = jnp.where(upper, b - pltpu.roll(e_h, h, 0), e_h - b)
        else:
            blocks = b.reshape(n // (2 * h), 2 * h, b.shape[1])
            mid = h if rev else h - 1
            ref = jnp.broadcast_to(blocks[:, mid:mid + 1, :], blocks.shape).reshape(b.shape)
            e = -jnp.abs(b - ref)
        w = jnp.exp(e)
        qw = (q * w).astype(BF16)
        kw = (k * w).astype(BF16)
        mask = lv == level
        for hh in range(GLA_HPS):
            ks = slice(hh * GLA_DK, (hh + 1) * GLA_DK)
            att[hh] = att[hh] + jnp.where(mask, _bdot_nt(qw[:, ks], kw[:, ks]), 0.0)
        if 4 * h < SUBLANES:
            if rev:
                e_h = jnp.where(upper, pltpu.roll(e_h, h, 0), e_h)
            else:
                e_h = jnp.where(upper, e_h, pltpu.roll(e_h, n - h, 0))
        h *= 2
        level += 1

    for hh in range(GLA_HPS):
        ks = slice(hh * GLA_DK, (hh + 1) * GLA_DK)
        vs = slice(hh * GLA_DV, (hh + 1) * GLA_DV)
        v = qkv_ref[0, :, PC_V + hh * GLA_DV:PC_V + (hh + 1) * GLA_DV]
        s_old = s_ref[hh]
        if emit:
            o = jnp.dot(att[hh].astype(BF16), v, preferred_element_type=F32)
            o = o + jnp.sum(q[:, ks] * k[:, ks], axis=-1, keepdims=True) * v.astype(F32)
            o = o + _bdot(q_in[:, ks], s_old)
            o_ref[0, pl.ds(dst, n), vs] += o
        dec_col = jnp.broadcast_to(dec_end[:, ks], (n, GLA_DK)).T[:, 0:1]
        s_ref[hh] = dec_col * s_old + jnp.dot(k_end[:, ks].T.astype(BF16), v, preferred_element_type=F32)


def _gla_body(n_ctx, seq, qkvf, smf, qkvb, smb, wup_ref, gb_ref, lv_ref, o_ref, s_ref):
    step = pl.program_id(2)
    n = CHUNK

    @pl.when(step == 0)
    def _():
        s_ref[...] = jnp.zeros_like(s_ref)
        o_ref[...] = jnp.zeros_like(o_ref)

    def both_directions(emit):
        cb = _bwd_chunk(step, n_ctx, pl.num_programs(2))
        _gla_dir(False, emit, _out_row(step, n_ctx, n), qkvf, smf, wup_ref[0], gb_ref[0], lv_ref[0],
                 o_ref, s_ref.at[0])
        _gla_dir(True, emit, _out_row(cb, n_ctx, n), qkvb, smb, wup_ref[1], gb_ref[1], lv_ref[1],
                 o_ref, s_ref.at[1])

    pl.when(step < n_ctx)(functools.partial(both_directions, False))
    pl.when(step >= n_ctx)(functools.partial(both_directions, True))


def _gla(proj, small, wup_pad, gla_b3, lvm, ctx_len, seq):
    bn, tt, _ = proj.shape
    n = CHUNK
    assert ctx_len % n == 0 and seq % n == 0
    n_ctx = ctx_len // n
    n_steps = tt // n
    assert GLA_HPS == GLA_HEADS and PC_Q == 0 and PC_R % LANES == 0
    hw = GLA_HPS * GLA_DK
    vw = GLA_HPS * GLA_DV
    fwd = lambda s: s
    bwd = lambda s: _bwd_chunk(s, n_ctx, n_steps)

    def chunk_specs(c):
        return [pl.BlockSpec((1, n, PC_R), lambda b, h, s: (b, c(s), 0)),
                pl.BlockSpec((1, n, LANES), lambda b, h, s: (b, c(s), 0))]

    return pl.pallas_call(
        functools.partial(_gla_body, n_ctx, seq),
        grid=(bn, GLA_HEADS // GLA_HPS, n_steps),
        in_specs=chunk_specs(fwd) + chunk_specs(bwd) + [
            pl.BlockSpec((2, LANES, hw), lambda b, h, s: (0, 0, h)),
            pl.BlockSpec((2, 1, hw), lambda b, h, s: (0, 0, h)),
            pl.BlockSpec((2, n, n), lambda b, h, s: (0, 0, 0))],
        out_specs=pl.BlockSpec((1, seq, vw), lambda b, h, s: (b, 0, h)),
        out_shape=jax.ShapeDtypeStruct((bn, seq, GLA_VAL), F32),
        scratch_shapes=[pltpu.VMEM((2, GLA_HPS, GLA_DK, GLA_DV), F32)],
        compiler_params=_cparams(("arbitrary",) * 3),
        name="gla",
    )(proj, small, proj, small, wup_pad, gla_b3, lvm)


def _ssd_dir(rev, first, emit, g, dst, xbc_ref, sm_ref, dtb, alog, expand, dsk_ref, y_ref, s_ref):
    n = CHUNK
    gl = slice(g * LANES, (g + 1) * LANES)
    g0 = g * SSD_GW
    b_cols = slice(SSD_INNER + g * SSD_N, SSD_INNER + (g + 1) * SSD_N)
    c_cols = slice(SSD_INNER + (SSD_GROUPS + g) * SSD_N, SSD_INNER + (SSD_GROUPS + g + 1) * SSD_N)
    lane = lax.broadcasted_iota(jnp.int32, (1, LANES), 1)
    dt = _softplus(sm_ref[0, :, gl] + dtb)
    a_neg = jnp.where(lane < SSD_HPG, -jnp.exp(alog), 0.0)
    acum = _scan_sum(dt * a_neg, rev)
    dt_hi = dt.astype(BF16)
    dt_lo = (dt - dt_hi.astype(F32)).astype(BF16)
    dt_wide = jnp.dot(jnp.concatenate([dt_hi, dt_lo], axis=0), expand, preferred_element_type=F32)
    dt_wide = dt_wide[:n] + dt_wide[n:]
    bm_t = xbc_ref[0, :, b_cols].astype(F32).T.astype(BF16)
    if emit:
        acum_t = acum.T
        cm = xbc_ref[0, :, c_cols]
        ii = lax.broadcasted_iota(jnp.int32, (n, n), 0)
        jj = lax.broadcasted_iota(jnp.int32, (n, n), 1)
        causal = (jj >= ii) if rev else (jj <= ii)
        cb = _bdot_nt(cm, xbc_ref[0, :, b_cols])
    low = lane < SSD_P
    for p in range(SSD_HPG // 2):
        slab = slice(p * LANES, (p + 1) * LANES)
        out = slice(g0 + p * LANES, g0 + (p + 1) * LANES)
        h0, h1 = 2 * p, 2 * p + 1
        col0 = jnp.broadcast_to(acum[:, h0:h0 + 1], (n, LANES))
        col1 = jnp.broadcast_to(acum[:, h1:h1 + 1], (n, LANES))
        col = jnp.where(low, col0, col1)
        dt2 = dt_wide[:, slab]
        end = col[0:1] if rev else col[n - 1:n]
        x = xbc_ref[0, :, out].astype(F32)
        s_old = s_ref[:, slab]
        xdt = x * dt2
        if emit:
            m0 = (cb * jnp.exp(jnp.where(causal, col0 - acum_t[h0:h0 + 1, :], -jnp.inf))).astype(BF16)
            m1 = (cb * jnp.exp(jnp.where(causal, col1 - acum_t[h1:h1 + 1, :], -jnp.inf))).astype(BF16)
            rhs = jnp.concatenate([jnp.where(low, xdt, 0.0), jnp.where(low, 0.0, xdt)], axis=0).astype(BF16)
            y = jnp.dot(jnp.concatenate([m0, m1], axis=1), rhs, preferred_element_type=F32)
            y = y + jnp.dot(cm, s_old.astype(BF16), preferred_element_type=F32) * jnp.exp(col)
            if first:
                y = y + x * dsk_ref[:, out]
            y_ref[0, pl.ds(dst, n), out] += y
        new = jnp.dot(bm_t, (xdt * jnp.exp(end - col)).astype(BF16), preferred_element_type=F32)
        s_ref[:, slab] = s_old * jnp.exp(end) + new


def _ssd_body(n_ctx, seq, xbcf, smf, xbcb, smb, dtb_ref, alog_ref, exp_ref, dsk_ref, y_ref, s_ref):
    step = pl.program_id(1)
    n = CHUNK

    @pl.when(step == 0)
    def _():
        s_ref[...] = jnp.zeros_like(s_ref)
        y_ref[...] = jnp.zeros_like(y_ref)

    def both_directions(emit):
        dst_f = _out_row(step, n_ctx, n)
        dst_b = _out_row(_bwd_chunk(step, n_ctx, pl.num_programs(1)), n_ctx, n)
        for g in range(SSD_GROUPS):
            _ssd_dir(False, True, emit, g, dst_f, xbcf, smf, dtb_ref[0, g], alog_ref[0, g], exp_ref[...], dsk_ref, y_ref,
                     s_ref.at[0, g])
            _ssd_dir(True, False, emit, g, dst_b, xbcb, smb, dtb_ref[1, g], alog_ref[1, g], exp_ref[...], dsk_ref, y_ref,
                     s_ref.at[1, g])

    pl.when(step < n_ctx)(functools.partial(both_directions, False))
    pl.when(step >= n_ctx)(functools.partial(both_directions, True))


def _ssd(xact, small, dtb4, alog4, dskip_row, ctx_len, seq):
    bn, tt, _ = xact.shape
    n = CHUNK
    n_ctx = ctx_len // n
    n_steps = tt // n
    fwd = lambda s: s
    bwd = lambda s: _bwd_chunk(s, n_ctx, n_steps)

    assert SMALL_ROT0 % SSD_GROUPS == 0
    sblk = SMALL_ROT0 // SSD_GROUPS

    def chunk_specs(c, d):
        return [pl.BlockSpec((1, n, SSD_XBC), lambda b, s: (b, c(s), 0)),
                pl.BlockSpec((1, n, SSD_GROUPS * LANES), lambda b, s: (b, c(s), sblk + d))]

    whole4 = pl.BlockSpec((2, SSD_GROUPS, 1, LANES), lambda b, s: (0, 0, 0, 0))
    expand = jnp.asarray(np.arange(LANES)[:, None] == np.arange(SSD_GW)[None, :] // SSD_P, dtype=BF16)
    return pl.pallas_call(
        functools.partial(_ssd_body, n_ctx, seq),
        grid=(bn, n_steps),
        in_specs=chunk_specs(fwd, 0) + chunk_specs(bwd, 1) + [
            whole4, whole4, pl.BlockSpec((LANES, SSD_GW), lambda b, s: (0, 0)),
            pl.BlockSpec((1, SSD_INNER), lambda b, s: (0, 0))],
        out_specs=pl.BlockSpec((1, seq, SSD_INNER), lambda b, s: (b, 0, 0)),
        out_shape=jax.ShapeDtypeStruct((bn, seq, SSD_INNER), F32),
        scratch_shapes=[pltpu.VMEM((2, SSD_GROUPS, SSD_N, SSD_GW), F32)],
        compiler_params=_cparams(("arbitrary",) * 2),
        name="ssd",
    )(xact, small, xact, small, dtb4, alog4, expand, dskip_row)


OUT_TM = 256
RT_E1, RT_E2, RT_RANK1, RT_RANK2, RT_W1, RT_W2 = range(6)


def _rms(x, w):
    return x * lax.rsqrt(jnp.mean(x * x, axis=-1, keepdims=True) + EPS) * w


ROW_TILE = (D_MODEL // LANES, LANES)


def _store_row_tiles(ref, x):
    ref[...] = x.reshape((x.shape[0],) + ROW_TILE)


def _load_row_tiles(ref):
    return ref[...].reshape(ref.shape[0], D_MODEL)


def _outproj_body(o_ref, r_ref, y_ref, z_ref, x_ref, mod_ref, gn_ref, sn_ref, fn_ref, wo_ref, wr_ref, br_ref,
                  hlat_ref, h2_ref, route_ref, cnt_ref, carry_ref):
    tm = OUT_TM

    @pl.when((pl.program_id(0) == 0) & (pl.program_id(1) == 0))
    def _():
        carry_ref[...] = jnp.zeros_like(carry_ref)

    parts = []
    for h in range(GLA_HEADS):
        hs = slice(h * GLA_DV, (h + 1) * GLA_DV)
        parts.append((_rms(o_ref[0, :, hs], gn_ref[...]) * _silu(r_ref[0, :, hs].astype(F32))).astype(BF16))
    for g in range(SSD_GROUPS):
        gs = slice(g * SSD_GW, (g + 1) * SSD_GW)
        yg = y_ref[0, :, gs] * _silu(z_ref[0, :, gs].astype(F32))
        parts.append(_rms(yg, sn_ref[:, gs]).astype(BF16))
    mix = jnp.concatenate(parts, axis=-1)
    m = mod_ref[0]
    hlat = x_ref[0] + m[0:1] * jnp.dot(mix, wo_ref[...], preferred_element_type=F32)
    hlat_ref[0] = hlat
    h2f = _rms(hlat, fn_ref[...]) * (1.0 + m[2:3]) + m[1:2]
    _store_row_tiles(h2_ref.at[0], h2f)

    lg = jnp.dot(h2f.astype(BF16), wr_ref[...], preferred_element_type=F32) + br_ref[...]
    lanef = lax.broadcasted_iota(jnp.int32, (tm, LANES), 1).astype(F32)
    ninf = -jnp.inf
    is_g = lanef < float(N_GROUPS)
    gl = jnp.where(is_g, lg, ninf)
    gmax = jnp.max(gl, axis=-1, keepdims=True)
    gsel = jnp.min(jnp.where(gl == gmax, lanef, float(LANES)), axis=-1, keepdims=True)
    pg = 1.0 / jnp.sum(jnp.where(is_g, jnp.exp(lg - gmax), 0.0), axis=-1, keepdims=True)
    lo = float(EXPERT_LANE0) + float(EXPERTS_PER_GROUP) * gsel
    in_grp = (lanef >= lo) & (lanef < lo + float(EXPERTS_PER_GROUP))
    el = jnp.where(in_grp, lg, ninf)
    v1 = jnp.max(el, axis=-1, keepdims=True)
    i1 = jnp.min(jnp.where(el == v1, lanef, float(LANES)), axis=-1, keepdims=True)
    el2 = jnp.where(lanef == i1, ninf, el)
    v2 = jnp.max(el2, axis=-1, keepdims=True)
    i2 = jnp.min(jnp.where(el2 == v2, lanef, float(LANES)), axis=-1, keepdims=True)
    t = jnp.exp(v2 - v1)
    w1 = pg / (1.0 + t)
    w2 = pg * t / (1.0 + t)

    sel1 = lanef == i1
    sel2 = lanef == i2
    member = jnp.where(sel1 | sel2, 1.0, 0.0)
    ii = lax.broadcasted_iota(jnp.int32, (tm, tm), 0)
    jj = lax.broadcasted_iota(jnp.int32, (tm, tm), 1)
    before = jnp.where(jj < ii, 1.0, 0.0).astype(BF16)
    ranks = jnp.dot(before, member.astype(BF16), preferred_element_type=F32) + carry_ref[...]
    rank1 = jnp.sum(jnp.where(sel1, ranks, 0.0), axis=-1, keepdims=True)
    rank2 = jnp.sum(jnp.where(sel2, ranks, 0.0), axis=-1, keepdims=True)
    carry = carry_ref[...] + jnp.sum(member, axis=0, keepdims=True)
    carry_ref[...] = carry
    cnt_ref[...] = carry
    rec = jnp.zeros((tm, LANES), F32)
    for lane_id, val in ((RT_E1, i1 - float(EXPERT_LANE0)), (RT_E2, i2 - float(EXPERT_LANE0)),
                         (RT_RANK1, rank1), (RT_RANK2, rank2), (RT_W1, w1), (RT_W2, w2)):
        rec = jnp.where(lanef == float(lane_id), val, rec)
    route_ref[0] = rec


def _outproj(o_gla, proj, y_ssd, x, mod2, gn_row, sn_row, fn_row, w_out, w_router, b_router, ctx_len):
    bn, seq, _ = x.shape
    tm = OUT_TM
    assert ctx_len % tm == 0 and seq % tm == 0
    j0 = ctx_len // tm
    rb = PC_R // GLA_VAL
    zb = PC_Z // SSD_INNER
    tok = lambda b, j: (b, j, 0)
    const = lambda b, j: (0, 0)
    return pl.pallas_call(
        _outproj_body,
        grid=(bn, seq // tm),
        in_specs=[pl.BlockSpec((1, tm, GLA_VAL), tok),
                  pl.BlockSpec((1, tm, GLA_VAL), lambda b, j: (b, j0 + j, rb)),
                  pl.BlockSpec((1, tm, SSD_INNER), tok),
                  pl.BlockSpec((1, tm, SSD_INNER), lambda b, j: (b, j0 + j, zb)),
                  pl.BlockSpec((1, tm, D_MODEL), tok),
                  pl.BlockSpec((1, 4, D_MODEL), lambda b, j: (b, 0, 0)),
                  pl.BlockSpec((1, GLA_DV), const),
                  pl.BlockSpec((1, SSD_INNER), const),
                  pl.BlockSpec((1, D_MODEL), const),
                  pl.BlockSpec((GLA_VAL + SSD_INNER, D_MODEL), const),
                  pl.BlockSpec((D_MODEL, LANES), const),
                  pl.BlockSpec((1, LANES), const)],
        out_specs=[pl.BlockSpec((1, tm, D_MODEL), tok),
                   pl.BlockSpec((1, tm) + ROW_TILE, lambda b, j: (b, j, 0, 0)),
                   pl.BlockSpec((1, tm, LANES), tok),
                   pl.BlockSpec((1, LANES), const)],
        out_shape=[jax.ShapeDtypeStruct((bn, seq, D_MODEL), F32),
                   jax.ShapeDtypeStruct((bn, seq) + ROW_TILE, F32),
                   jax.ShapeDtypeStruct((bn, seq, LANES), F32),
                   jax.ShapeDtypeStruct((1, LANES), F32)],
        scratch_shapes=[pltpu.VMEM((1, LANES), F32)],
        compiler_params=_cparams(("arbitrary", "arbitrary")),
        name="outproj",
    )(o_gla, proj, y_ssd, proj, x, mod2, gn_row, sn_row, fn_row, w_out, w_router, b_router)


MOE_TILE = 256
PLAN_TM = 1024
DISPATCH_TM = 1024
COMBINE_TM = 512


def _moe_tiles(n_tokens):
    return (2 * n_tokens) // MOE_TILE + N_EXPERTS


def _plan_body(nt_pad, route_ref, cnt_ref, pos_ref, te_ref, nt_ref, pad0_ref, padn_ref, seg_ref, nxt_ref):
    lane = lax.broadcasted_iota(jnp.int32, (1, LANES), 1).astype(F32)
    is_e = (lane >= float(EXPERT_LANE0)) & (lane < float(EXPERT_LANE0 + N_EXPERTS))
    tiles = jnp.where(is_e, jnp.floor((cnt_ref[...] + float(MOE_TILE - 1)) / float(MOE_TILE)), 0.0)
    ii = lax.broadcasted_iota(jnp.int32, (LANES, LANES), 0)
    jj = lax.broadcasted_iota(jnp.int32, (LANES, LANES), 1)
    lower = jnp.where(ii < jj, 1.0, 0.0).astype(BF16)
    first_tile = jnp.dot(jnp.broadcast_to(tiles, (8, LANES)).astype(BF16), lower,
                         preferred_element_type=F32)[0:1]
    off_row = first_tile * float(MOE_TILE)
    for sb in range(PLAN_TM // LANES):
        r = route_ref[sb * LANES:(sb + 1) * LANES, :]
        o1 = jnp.sum(jnp.where(lane == r[:, RT_E1:RT_E1 + 1] + float(EXPERT_LANE0), off_row, 0.0),
                     axis=-1, keepdims=True)
        o2 = jnp.sum(jnp.where(lane == r[:, RT_E2:RT_E2 + 1] + float(EXPERT_LANE0), off_row, 0.0),
                     axis=-1, keepdims=True)
        p = jnp.where(lane == 0.0, o1 + r[:, RT_RANK1:RT_RANK1 + 1],
                      jnp.where(lane == 1.0, o2 + r[:, RT_RANK2:RT_RANK2 + 1], 0.0))
        pos_ref[sb] = p.T[0:8, :].astype(jnp.int32)

    @pl.when(pl.program_id(0) == 0)
    def _():
        ti = lax.broadcasted_iota(jnp.int32, (nt_pad, 1), 0).astype(F32)
        done = jnp.where(is_e & (first_tile + tiles <= ti), 1.0, 0.0)
        te = jnp.minimum(jnp.sum(done, axis=-1, keepdims=True), float(N_EXPERTS - 1))
        te_ref[...] = te.astype(jnp.int32)
        nt_ref[...] = jnp.sum(tiles, axis=-1, keepdims=True).astype(jnp.int32)
        lane_e = lane - float(EXPERT_LANE0)
        nonempty = is_e & (tiles > 0.0)
        seg_ref[...] = jnp.sum(jnp.where(nonempty & (lane_e < te), 1.0, 0.0), axis=-1,
                               keepdims=True).astype(jnp.int32)
        nxt = jnp.min(jnp.where(nonempty & (lane_e > te), lane_e, float(N_EXPERTS)), axis=-1, keepdims=True)
        nxt_ref[...] = jnp.where(nxt < float(N_EXPERTS), nxt, -1.0).astype(jnp.int32)
        used = jnp.where(is_e, cnt_ref[...], 0.0)
        pad0_ref[...] = (off_row + used).astype(jnp.int32)
        padn_ref[...] = (tiles * float(MOE_TILE) - used).astype(jnp.int32)


def _plan(route, cnt):
    n_tok = route.shape[0]
    nt_pad = _moe_tiles(n_tok)
    nsb = PLAN_TM // LANES
    return pl.pallas_call(
        functools.partial(_plan_body, nt_pad),
        grid=(n_tok // PLAN_TM,),
        in_specs=[pl.BlockSpec((PLAN_TM, LANES), lambda i: (i, 0)),
                  pl.BlockSpec((1, LANES), lambda i: (0, 0))],
        out_specs=[pl.BlockSpec((nsb, 8, LANES), lambda i: (i, 0, 0)),
                   pl.BlockSpec((nt_pad, 1), lambda i: (0, 0)),
                   pl.BlockSpec((1, 1), lambda i: (0, 0)),
                   pl.BlockSpec((1, LANES), lambda i: (0, 0)),
                   pl.BlockSpec((1, LANES), lambda i: (0, 0)),
                   pl.BlockSpec((nt_pad, 1), lambda i: (0, 0)),
                   pl.BlockSpec((nt_pad, 1), lambda i: (0, 0))],
        out_shape=[jax.ShapeDtypeStruct((n_tok // LANES, 8, LANES), jnp.int32),
                   jax.ShapeDtypeStruct((nt_pad, 1), jnp.int32),
                   jax.ShapeDtypeStruct((1, 1), jnp.int32),
                   jax.ShapeDtypeStruct((1, LANES), jnp.int32),
                   jax.ShapeDtypeStruct((1, LANES), jnp.int32),
                   jax.ShapeDtypeStruct((nt_pad, 1), jnp.int32),
                   jax.ShapeDtypeStruct((nt_pad, 1), jnp.int32)],
        compiler_params=_cparams(("arbitrary",)),
        name="moe_plan",
    )(route, cnt)


def _row_copy(src, s_row, dst, d_row, sem):
    return pltpu.make_async_copy(src.at[s_row], dst.at[d_row], sem)


def _issue_rows(n_rows, pos_ref, start_pair):
    for blk in range(n_rows // LANES):
        def issue(l, c, blk=blk):
            start_pair(blk * LANES + l, pos_ref[blk, 0, l], pos_ref[blk, 1, l])
            return c
        lax.fori_loop(0, LANES, issue, 0, unroll=8)


def _drain_rows(n_copies, copy):
    def drain(t, c):
        copy.wait()
        return c
    lax.fori_loop(0, n_copies, drain, 0, unroll=8)


def _dispatch_body(pad0_ref, padn_ref, nt_ref, pos_ref, h2_ref, xs_ref, zero_ref, sem):
    def start_pair(t, p1, p2):
        _row_copy(h2_ref, t, xs_ref, p1, sem).start(priority=0)
        _row_copy(h2_ref, t, xs_ref, p2, sem).start(priority=1)

    _issue_rows(DISPATCH_TM, pos_ref, start_pair)
    _drain_rows(2 * DISPATCH_TM, _row_copy(h2_ref, 0, xs_ref, 0, sem))

    @pl.when(pl.program_id(0) == 0)
    def _():
        zero_ref[...] = jnp.zeros_like(zero_ref)
        n_tiles = xs_ref.shape[0] // MOE_TILE

        def tile_copy(i):
            return pltpu.make_async_copy(zero_ref, xs_ref.at[pl.ds(i * MOE_TILE, MOE_TILE)], sem)

        def fill_tile(i, c):
            tile_copy(i).start()
            return c

        def drain_tile(i, c):
            tile_copy(i).wait()
            return c

        lax.fori_loop(nt_ref[0], n_tiles, fill_tile, 0)
        lax.fori_loop(nt_ref[0], n_tiles, drain_tile, 0)
        for e in range(N_EXPERTS):
            n_pad = padn_ref[EXPERT_LANE0 + e]
            for wait in (False, True):
                row = pad0_ref[EXPERT_LANE0 + e]
                size = MOE_TILE // 2
                while size >= 1:
                    has = (n_pad & size) != 0
                    copy = pltpu.make_async_copy(zero_ref.at[pl.ds(0, size)], xs_ref.at[pl.ds(row, size)], sem)
                    pl.when(has)(copy.wait if wait else copy.start)
                    row = row + jnp.where(has, size, 0)
                    size //= 2


def _dispatch(pad0, padn, nt, pos, h2t, n_rows):
    n_tok = h2t.shape[0]
    nb = DISPATCH_TM // LANES
    grid_spec = pltpu.PrefetchScalarGridSpec(
        num_scalar_prefetch=3,
        grid=(n_tok // DISPATCH_TM,),
        in_specs=[pl.BlockSpec((nb, 8, LANES), lambda i, *_: (i, 0, 0), memory_space=pltpu.SMEM),
                  pl.BlockSpec((DISPATCH_TM,) + ROW_TILE, lambda i, *_: (i, 0, 0))],
        out_specs=pl.BlockSpec(memory_space=pl.ANY),
        scratch_shapes=[pltpu.VMEM((MOE_TILE,) + ROW_TILE, F32), pltpu.SemaphoreType.DMA(())])
    return pl.pallas_call(
        _dispatch_body,
        grid_spec=grid_spec,
        out_shape=jax.ShapeDtypeStruct((n_rows,) + ROW_TILE, F32),
        compiler_params=_cparams(("arbitrary",)),
        name="moe_dispatch",
    )(pad0, padn, nt, pos, h2t)


def _expert_body(te_ref, nt_ref, seg_ref, nxt_ref, xs_ref, wg_ref, wu_ref, wd_ref, ys_ref,
                 wgf_ref, wuf_ref, wdf_ref, wgb_ref, wub_ref, wdb_ref, sem):
    i = pl.program_id(0)

    def copies(e, s):
        return [pltpu.make_async_copy(src.at[e], dst.at[s], sem.at[s])
                for src, dst in ((wg_ref, wgf_ref), (wu_ref, wuf_ref), (wd_ref, wdf_ref))]

    @pl.when(i < nt_ref[0])
    def _():
        seg = seg_ref[i]
        slot = seg % 2
        first = (i == 0) | (seg != seg_ref[jnp.maximum(i - 1, 0)])

        @pl.when(first)
        def _():
            @pl.when(i == 0)
            def _():
                for c in copies(te_ref[i], slot):
                    c.start()

            for c in copies(te_ref[i], slot):
                c.wait()
            wgb_ref[...] = wgf_ref[slot].astype(BF16)
            wub_ref[...] = wuf_ref[slot].astype(BF16)
            wdb_ref[...] = wdf_ref[slot].astype(BF16)

            @pl.when(nxt_ref[i] >= 0)
            def _():
                for c in copies(nxt_ref[i], 1 - slot):
                    c.start()

        x = _load_row_tiles(xs_ref).astype(BF16)
        gate = jnp.dot(x, wgb_ref[...], preferred_element_type=F32)
        up = jnp.dot(x, wub_ref[...], preferred_element_type=F32)
        y = jnp.dot((_silu(gate) * up).astype(BF16), wdb_ref[...], preferred_element_type=F32)
        _store_row_tiles(ys_ref, y)

    @pl.when(i >= nt_ref[0])
    def _():
        ys_ref[...] = jnp.zeros_like(ys_ref)


def _experts(te, nt, seg, nxt, xs, w_gate, w_up, w_down):
    rows = xs.shape[0]
    n_tiles = rows // MOE_TILE
    row_block = pl.BlockSpec((MOE_TILE,) + ROW_TILE, lambda i, *_: (i, 0, 0))
    in_w = (D_MODEL, D_FF)
    out_w = (D_FF, D_MODEL)
    grid_spec = pltpu.PrefetchScalarGridSpec(
        num_scalar_prefetch=4,
        grid=(n_tiles,),
        in_specs=[row_block] + [pl.BlockSpec(memory_space=pl.ANY)] * 3,
        out_specs=row_block,
        scratch_shapes=[pltpu.VMEM((2,) + in_w, F32), pltpu.VMEM((2,) + in_w, F32), pltpu.VMEM((2,) + out_w, F32),
                        pltpu.VMEM(in_w, BF16), pltpu.VMEM(in_w, BF16), pltpu.VMEM(out_w, BF16),
                        pltpu.SemaphoreType.DMA((2,))])
    return pl.pallas_call(
        _expert_body,
        grid_spec=grid_spec,
        out_shape=jax.ShapeDtypeStruct(xs.shape, F32),
        compiler_params=_cparams(("arbitrary",)),
        name="moe_experts",
    )(te, nt, seg, nxt, xs, w_gate, w_up, w_down)


def _combine_body(pos_ref, nxt_ref, route_ref, hlat_ref, g2_ref, fn_ref, ys_ref, out_ref, buf_ref, sem):
    tm = COMBINE_TM
    i = pl.program_id(0)
    slot = i % 2

    def fetch(p_ref, s):
        def start_pair(t, p1, p2):
            _row_copy(ys_ref, p1, buf_ref.at[s, 0], t, sem.at[s]).start(priority=0)
            _row_copy(ys_ref, p2, buf_ref.at[s, 1], t, sem.at[s]).start(priority=1)
        _issue_rows(tm, p_ref, start_pair)

    @pl.when(i == 0)
    def _():
        fetch(pos_ref, 0)

    @pl.when(i + 1 < pl.num_programs(0))
    def _():
        fetch(nxt_ref, 1 - slot)

    _drain_rows(2 * tm, _row_copy(ys_ref, 0, buf_ref.at[slot, 0], 0, sem.at[slot]))

    r = route_ref[0]
    moe = (r[:, RT_W1:RT_W1 + 1] * _load_row_tiles(buf_ref.at[slot, 0])
           + r[:, RT_W2:RT_W2 + 1] * _load_row_tiles(buf_ref.at[slot, 1]))
    out_ref[0] = _rms(hlat_ref[0] + g2_ref[0] * moe, fn_ref[...])


def _combine(pos, route, hlat, g2, fn_row, ys):
    bn, seq, _ = hlat.shape
    tm = COMBINE_TM
    nj = seq // tm
    nb = tm // LANES
    last = bn * nj - 1
    tok = lambda i: (i // nj, i % nj, 0)
    return pl.pallas_call(
        _combine_body,
        grid=(bn * nj,),
        in_specs=[pl.BlockSpec((nb, 8, LANES), lambda i: (i, 0, 0), memory_space=pltpu.SMEM),
                  pl.BlockSpec((nb, 8, LANES), lambda i: (jnp.minimum(i + 1, last), 0, 0),
                               memory_space=pltpu.SMEM),
                  pl.BlockSpec((1, tm, LANES), tok),
                  pl.BlockSpec((1, tm, D_MODEL), tok),
                  pl.BlockSpec((1, 1, D_MODEL), lambda i: (i // nj, 0, 0)),
                  pl.BlockSpec((1, D_MODEL), lambda i: (0, 0)),
                  pl.BlockSpec(memory_space=pl.ANY)],
        out_specs=pl.BlockSpec((1, tm, D_MODEL), tok),
        out_shape=jax.ShapeDtypeStruct((bn, seq, D_MODEL), F32),
        scratch_shapes=[pltpu.VMEM((2, 2, tm) + ROW_TILE, F32), pltpu.SemaphoreType.DMA((2,))],
        compiler_params=_cparams(("arbitrary",)),
        name="moe_combine",
    )(pos, pos, route, hlat, g2, fn_row, ys)


def _prep_weights(w_in, gla_w_up, gla_b, ssd_dt_bias, ssd_a_log, ssd_d, router_group_w, router_group_b,
                  router_expert_w, router_expert_b):
    off_k = GLA_KEY
    off_v = 2 * GLA_KEY
    off_r = off_v + GLA_VAL
    off_g = off_r + GLA_VAL
    off_z = off_g + 2 * GLA_RANK
    off_xbc = off_z + SSD_INNER
    off_dt = off_xbc + SSD_XBC
    w_main = jnp.concatenate([w_in[:, :off_g], w_in[:, off_z:off_dt]], axis=1).astype(BF16)
    w_small = jnp.concatenate([w_in[:, off_g:off_z], w_in[:, off_dt:off_dt + 2 * SSD_HEADS]], axis=1)
    w_small = jnp.pad(w_small, ((0, 0), (0, LANES - w_small.shape[1]))).astype(BF16)
    wup = jnp.zeros((2, LANES, GLA_KEY), F32)
    wup = wup.at[0, 0:GLA_RANK].set(gla_w_up[0]).at[1, GLA_RANK:2 * GLA_RANK].set(gla_w_up[1]).astype(BF16)
    pad_h = lambda t: jnp.pad(t.reshape(2, SSD_GROUPS, 1, SSD_HPG), ((0, 0), (0, 0), (0, 0), (0, LANES - SSD_HPG)))
    w_router = jnp.zeros((D_MODEL, LANES), F32)
    w_router = w_router.at[:, 0:N_GROUPS].set(router_group_w)
    w_router = w_router.at[:, EXPERT_LANE0:EXPERT_LANE0 + N_EXPERTS].set(router_expert_w).astype(BF16)
    b_router = jnp.zeros((1, LANES), F32).at[0, 0:N_GROUPS].set(router_group_b)
    b_router = b_router.at[0, EXPERT_LANE0:EXPERT_LANE0 + N_EXPERTS].set(router_expert_b)
    return (w_main, w_small, wup, gla_b.reshape(2, 1, GLA_KEY), pad_h(ssd_dt_bias), pad_h(ssd_a_log),
            jnp.repeat(ssd_d, SSD_P)[None, :], w_router, b_router)


def kernel(x, c, ctx, c_ctx, w_ada, b_ada, norm_mix, norm_ffn, w_in, gla_w_up, gla_b, gla_norm, ssd_conv_w,
           ssd_conv_b, ssd_dt_bias, ssd_a_log, ssd_d, ssd_norm, w_out, router_group_w, router_group_b,
           router_expert_w, router_expert_b, expert_w_gate, expert_w_up, expert_w_down, final_norm):
    assert w_ada.shape[0] == 1, "single-layer block"
    bn, seq, d = x.shape
    ctx_len = ctx.shape[1]
    assert seq // GRID_W == GRID_W and d == D_MODEL

    c8 = jnp.concatenate([c, c_ctx[None, :], jnp.zeros((8 - bn - 1, d), F32)], axis=0)
    mods = _adaln(c8, w_ada[0], b_ada[0][None, :])
    sh1, sc1, g1, sh2, sc2, g2 = [mods[:, i * d:(i + 1) * d] for i in range(6)]
    ctx_row = lambda t: jnp.broadcast_to(t[bn:bn + 1], (bn, d))
    mod1 = jnp.stack([ctx_row(sh1), ctx_row(sc1), sh1[:bn], sc1[:bn]], axis=1)
    mod2 = jnp.stack([g1[:bn], sh2[:bn], sc2[:bn], g2[:bn]], axis=1)

    (w_main, w_small, wup, gla_b3, dtb4, alog4, dskip_row, w_router, b_router) = _prep_weights(
        w_in[0], gla_w_up[0], gla_b[0], ssd_dt_bias[0], ssd_a_log[0], ssd_d[0], router_group_w[0],
        router_group_b[0], router_expert_w[0], router_expert_b[0])

    proj, small = _inproj(ctx, x, mod1, norm_mix, w_main, w_small)
    xact = _conv(proj, ssd_conv_w[0].reshape(9, SSD_XBC), ssd_conv_b, ctx_len, seq)
    lvm = jnp.asarray(np.stack([_level_matrix(CHUNK, False), _level_matrix(CHUNK, True)]))
    o_gla = _gla(proj, small, wup, gla_b3, lvm, ctx_len, seq)
    y_ssd = _ssd(xact, small, dtb4, alog4, dskip_row, ctx_len, seq)
    hlat, h2p, route, cnt = _outproj(o_gla, proj, y_ssd, x, mod2, gla_norm, ssd_norm, norm_ffn,
                                     w_out[0].astype(BF16), w_router, b_router, ctx_len)
    n_tok = bn * seq
    pos, te, nt, pad0, padn, seg, nxt = _plan(route.reshape(n_tok, LANES), cnt)
    te, nt = te.reshape(-1), nt.reshape(-1)
    xs = _dispatch(pad0.reshape(-1), padn.reshape(-1), nt, pos, h2p.reshape((n_tok,) + ROW_TILE),
                   _moe_tiles(n_tok) * MOE_TILE)
    ys = _experts(te, nt, seg.reshape(-1), nxt.reshape(-1), xs, expert_w_gate[0], expert_w_up[0], expert_w_down[0])
    return _combine(pos, route, hlat, mod2[:, 3:4, :], final_norm[None, :], ys)
```

```python
import functools
import math

import numpy as np
import jax
import jax.numpy as jnp
from jax import lax
from jax.experimental import pallas as pl
from jax.experimental.pallas import tpu as pltpu

F32 = jnp.float32
BF16 = jnp.bfloat16

D_MODEL = 1024
GRID_W = 64
EPS = 1e-6

GLA_HEADS = 4
GLA_DK = 128
GLA_DV = 256
GLA_KEY = GLA_HEADS * GLA_DK
GLA_VAL = GLA_HEADS * GLA_DV
GLA_RANK = 16
GLA_GATE_NORM = 16.0

SSD_HEADS = 16
SSD_P = 64
SSD_INNER = SSD_HEADS * SSD_P
SSD_GROUPS = 2
SSD_HPG = SSD_HEADS // SSD_GROUPS
SSD_N = 128
SSD_XBC = SSD_INNER + 2 * SSD_GROUPS * SSD_N
SSD_GW = SSD_HPG * SSD_P

N_GROUPS = 4
EXPERTS_PER_GROUP = 8
N_EXPERTS = N_GROUPS * EXPERTS_PER_GROUP
D_FF = 512

LANES = 128
SUBLANES = 8
VMEM_LIMIT = 56 * 1024 * 1024

PC_Q = 0
PC_K = PC_Q + GLA_KEY
PC_V = PC_K + GLA_KEY
PC_R = PC_V + GLA_VAL
PC_Z = PC_R + GLA_VAL
PC_XBC = PC_Z + SSD_INNER
PROJ_W = PC_XBC + SSD_XBC
SMALL_DT0 = 2 * GLA_RANK
SMALL_ROT0 = 2
SMALL_W = LANES * (SMALL_ROT0 + 2 * SSD_GROUPS)

CHUNK = 128
EXPERT_LANE0 = 32


def _cparams(sem):
    return pltpu.CompilerParams(dimension_semantics=sem, vmem_limit_bytes=VMEM_LIMIT)


def _silu(x):
    return x / (1.0 + jnp.exp(-x))


def _softplus(x):
    return jnp.maximum(x, 0.0) + jnp.log(1.0 + jnp.exp(-jnp.abs(x)))


def _log_sigmoid(x):
    return jnp.minimum(x, 0.0) - jnp.log(1.0 + jnp.exp(-jnp.abs(x)))


def _bdot(a, b):
    return jnp.dot(a.astype(BF16), b.astype(BF16), preferred_element_type=F32)


def _bdot_nt(a, b):
    return lax.dot_general(a.astype(BF16), b.astype(BF16), (((1,), (1,)), ((), ())),
                           preferred_element_type=F32)


def _adaln_body(c_ref, w_ref, b_ref, o_ref):
    o_ref[...] = _bdot(_silu(c_ref[...]), w_ref[...]) + b_ref[...]


def _adaln(c8, w, b):
    n = w.shape[1]
    bn = 1024
    return pl.pallas_call(
        _adaln_body,
        grid=(n // bn,),
        in_specs=[pl.BlockSpec((8, D_MODEL), lambda j: (0, 0)),
                  pl.BlockSpec((D_MODEL, bn), lambda j: (0, j)),
                  pl.BlockSpec((1, bn), lambda j: (0, j))],
        out_specs=pl.BlockSpec((8, bn), lambda j: (0, j)),
        out_shape=jax.ShapeDtypeStruct((8, n), F32),
        compiler_params=_cparams(("arbitrary",)),
        name="adaln",
    )(c8, w, b)


INPROJ_TM = 256
INPROJ_NC = 512


def _inproj_body(ctx_ref, x_ref, mod_ref, nw_ref, w_ref, ws_ref, proj_ref, small_ref):
    is_ctx = pl.program_id(1) == 0
    xin = jnp.where(is_ctx, ctx_ref[0], x_ref[0])
    m = mod_ref[0]
    shift = jnp.where(is_ctx, m[0:1], m[2:3])
    scale = jnp.where(is_ctx, m[1:2], m[3:4])
    ms = jnp.mean(xin * xin, axis=-1, keepdims=True)
    h = xin * lax.rsqrt(ms + EPS) * nw_ref[...]
    hb = (h * (1.0 + scale) + shift).astype(BF16)
    for n in range(PROJ_W // INPROJ_NC):
        sl = slice(n * INPROJ_NC, (n + 1) * INPROJ_NC)
        proj_ref[0, :, sl] = jnp.dot(hb, w_ref[:, sl], preferred_element_type=F32).astype(BF16)
    sm = jnp.dot(hb, ws_ref[...], preferred_element_type=F32)
    small_ref[0, :, 0:LANES] = sm
    small_ref[0, :, LANES:SMALL_ROT0 * LANES] = jnp.zeros((sm.shape[0], (SMALL_ROT0 - 1) * LANES), F32)
    for k in range(2 * SSD_GROUPS):
        lane0 = SMALL_DT0 + k * SSD_HPG
        small_ref[0, :, (SMALL_ROT0 + k) * LANES:(SMALL_ROT0 + k + 1) * LANES] = pltpu.roll(sm, LANES - lane0, 1)


def _inproj(ctx, x, mod1, norm_w, w_main, w_small):
    bn, seq, _ = x.shape
    ctx_len = ctx.shape[1]
    tm = INPROJ_TM
    assert ctx_len == tm and seq % tm == 0
    tt = ctx_len + seq
    nj = tt // tm
    return pl.pallas_call(
        _inproj_body,
        grid=(bn, nj),
        in_specs=[pl.BlockSpec((1, tm, D_MODEL), lambda b, j: (b, 0, 0)),
                  pl.BlockSpec((1, tm, D_MODEL), lambda b, j: (b, jnp.maximum(j - 1, 0), 0)),
                  pl.BlockSpec((1, 4, D_MODEL), lambda b, j: (b, 0, 0)),
                  pl.BlockSpec((1, D_MODEL), lambda b, j: (0, 0)),
                  pl.BlockSpec((D_MODEL, PROJ_W), lambda b, j: (0, 0)),
                  pl.BlockSpec((D_MODEL, LANES), lambda b, j: (0, 0))],
        out_specs=[pl.BlockSpec((1, tm, PROJ_W), lambda b, j: (b, j, 0)),
                   pl.BlockSpec((1, tm, SMALL_W), lambda b, j: (b, j, 0))],
        out_shape=[jax.ShapeDtypeStruct((bn, tt, PROJ_W), BF16),
                   jax.ShapeDtypeStruct((bn, tt, SMALL_W), F32)],
        compiler_params=_cparams(("arbitrary", "arbitrary")),
        name="inproj",
    )(ctx, x, mod1, norm_w, w_main, w_small)


CONV_CB = 256
CONV_TT = 256


def _conv_body(ctx_len, seq, p_ref, w_ref, b_ref, o_ref, s_ref):
    lat0 = ctx_len + GRID_W
    tt = CONV_TT
    n_tiles = seq // tt
    zeros = jnp.zeros((GRID_W, CONV_CB), F32)
    s_ref[0:ctx_len, :] = p_ref[0, 0:ctx_len, :].astype(F32)
    s_ref[ctx_len:lat0, :] = zeros
    s_ref[lat0 + seq:lat0 + seq + GRID_W, :] = zeros

    def copy_tile(t, c):
        src = pl.multiple_of(ctx_len + t * tt, tt)
        dst = pl.multiple_of(lat0 + t * tt, GRID_W)
        s_ref[pl.ds(dst, tt), :] = p_ref[0, pl.ds(src, tt), :].astype(F32)
        return c

    lax.fori_loop(0, n_tiles, copy_tile, 0)

    w = w_ref[...]
    bias = b_ref[...]
    pos = lax.broadcasted_iota(jnp.int32, (tt, 1), 0)

    def conv_rows(rows, first, last):
        left, mid, right = [sum(xr * w[3 * kh + kw:3 * kh + kw + 1] for xr, kh in rows) for kw in range(3)]
        return (bias + mid + jnp.where(first, 0.0, pltpu.roll(left, 1, 0))
                + jnp.where(last, 0.0, pltpu.roll(right, tt - 1, 0)))

    acc = conv_rows([(s_ref[0:ctx_len, :], 1)], pos == 0, pos == ctx_len - 1)
    o_ref[0, 0:ctx_len, :] = _silu(acc).astype(BF16)

    col = pos % GRID_W
    first = col == 0
    last = col == GRID_W - 1

    def tile(t, c):
        base = lat0 + t * tt
        rows = [(s_ref[pl.ds(pl.multiple_of(base + (kh - 1) * GRID_W, GRID_W), tt), :], kh) for kh in range(3)]
        dst = pl.multiple_of(ctx_len + t * tt, tt)
        o_ref[0, pl.ds(dst, tt), :] = _silu(conv_rows(rows, first, last)).astype(BF16)
        return c

    lax.fori_loop(0, n_tiles, tile, 0)


def _conv(proj, conv_w9, conv_b, ctx_len, seq):
    bn, tt, _ = proj.shape
    assert ctx_len == CONV_TT and seq % CONV_TT == 0 and CONV_TT % GRID_W == 0
    ncb = SSD_XBC // CONV_CB
    cb0 = PC_XBC // CONV_CB
    return pl.pallas_call(
        functools.partial(_conv_body, ctx_len, seq),
        grid=(bn, ncb),
        in_specs=[pl.BlockSpec((1, tt, CONV_CB), lambda b, j: (b, 0, cb0 + j)),
                  pl.BlockSpec((9, CONV_CB), lambda b, j: (0, j)),
                  pl.BlockSpec((1, CONV_CB), lambda b, j: (0, j))],
        out_specs=pl.BlockSpec((1, tt, CONV_CB), lambda b, j: (b, 0, j)),
        out_shape=jax.ShapeDtypeStruct((bn, tt, SSD_XBC), BF16),
        scratch_shapes=[pltpu.VMEM((tt + 2 * GRID_W, CONV_CB), F32)],
        compiler_params=_cparams(("arbitrary", "arbitrary")),
        name="conv",
    )(proj, conv_w9, conv_b)


def _bwd_chunk(s, n_ctx, n_steps):
    return jnp.where(s < n_ctx, n_ctx - 1 - s, n_steps + n_ctx - 1 - s)


def _out_row(c, n_ctx, n):
    return pl.multiple_of(jnp.maximum(c - n_ctx, 0) * n, n)


def _scan_sum(a, rev):
    n = a.shape[0]
    row = lax.broadcasted_iota(jnp.int32, (n, 1), 0)
    sh = 1
    while sh < n:
        if rev:
            a = a + jnp.where(row < n - sh, pltpu.roll(a, n - sh, 0), 0.0)
        else:
            a = a + jnp.where(row >= sh, pltpu.roll(a, sh, 0), 0.0)
        sh *= 2
    return a


def _level_matrix(n, rev):
    i = np.arange(n)[:, None]
    j = np.arange(n)[None, :]
    x = i ^ j
    lv = np.where(x > 0, np.floor(np.log2(np.maximum(x, 1))).astype(np.int64) + 1, 0)
    earlier = (j > i) if rev else (j < i)
    return np.where(earlier, lv, 0).astype(np.int32)


GLA_HPS = 4


def _gla_dir(rev, emit, dst, qkv_ref, sm_ref, wup, gb, lv, o_ref, s_ref):
    n = CHUNK
    q = qkv_ref[0, :, PC_Q:PC_K].astype(F32) * (GLA_DK ** -0.5)
    k = qkv_ref[0, :, PC_K:PC_V].astype(F32)
    x = jnp.dot(sm_ref[0].astype(BF16), wup, preferred_element_type=F32) + gb
    b = _scan_sum(_log_sigmoid(x) / GLA_GATE_NORM, rev)
    row = lax.broadcasted_iota(jnp.int32, (n, 1), 0)
    b_end = b[0:1] if rev else b[n - 1:n]
    k_end = k * jnp.exp(b_end - b)
    dec_end = jnp.exp(b_end)
    if emit:
        q_in = q * jnp.exp(b)

    att = [jnp.zeros((n, n), F32) for _ in range(GLA_HPS)]
    e_h = b
    h = 1
    level = 1
    while emit and h < n:
        upper = (row & h) != 0
        if 2 * h < SUBLANES:
            if rev:
                e = jnp.where(upper, e_h - b, b - pltpu.roll(e_h, n - h, 0))
            else:
                e = jnp.where(upper, b - pltpu.roll(e_h, h, 0), e_h - b)
        else:
            blocks = b.reshape(n // (2 * h), 2 * h, b.shape[1])
            mid = h if rev else h - 1
            ref = jnp.broadcast_to(blocks[:, mid:mid + 1, :], blocks.shape).reshape(b.shape)
            e = -jnp.abs(b - ref)
        later = jnp.logical_not(upper) if rev else upper
        z = (jnp.where(later, q, k) * jnp.exp(e)).astype(BF16)
        mask = lv == level
        for hh in range(GLA_HPS):
            ks = slice(hh * GLA_DK, (hh + 1) * GLA_DK)
            att[hh] = att[hh] + jnp.where(mask, _bdot_nt(z[:, ks], z[:, ks]), 0.0)
        if 4 * h < SUBLANES:
            if rev:
                e_h = jnp.where(upper, pltpu.roll(e_h, h, 0), e_h)
            else:
                e_h = jnp.where(upper, e_h, pltpu.roll(e_h, n - h, 0))
        h *= 2
        level += 1

    for hh in range(GLA_HPS):
        ks = slice(hh * GLA_DK, (hh + 1) * GLA_DK)
        vs = slice(hh * GLA_DV, (hh + 1) * GLA_DV)
        v = qkv_ref[0, :, PC_V + hh * GLA_DV:PC_V + (hh + 1) * GLA_DV]
        s_old = s_ref[hh]
        if emit:
            o = jnp.dot(att[hh].astype(BF16), v, preferred_element_type=F32)
            o = o + jnp.sum(q[:, ks] * k[:, ks], axis=-1, keepdims=True) * v.astype(F32)
            o = o + _bdot(q_in[:, ks], s_old)
            o_ref[0, pl.ds(dst, n), vs] += o
        dec_col = jnp.broadcast_to(dec_end[:, ks], (n, GLA_DK)).T[:, 0:1]
        s_ref[hh] = dec_col * s_old + jnp.dot(k_end[:, ks].T.astype(BF16), v, preferred_element_type=F32)


def _gla_body(n_ctx, seq, qkvf, smf, qkvb, smb, wup_ref, gb_ref, lv_ref, o_ref, s_ref):
    step = pl.program_id(2)
    n = CHUNK

    @pl.when(step == 0)
    def _():
        s_ref[...] = jnp.zeros_like(s_ref)
        o_ref[...] = jnp.zeros_like(o_ref)

    def both_directions(emit):
        cb = _bwd_chunk(step, n_ctx, pl.num_programs(2))
        _gla_dir(False, emit, _out_row(step, n_ctx, n), qkvf, smf, wup_ref[0], gb_ref[0], lv_ref[0],
                 o_ref, s_ref.at[0])
        _gla_dir(True, emit, _out_row(cb, n_ctx, n), qkvb, smb, wup_ref[1], gb_ref[1], lv_ref[1],
                 o_ref, s_ref.at[1])

    pl.when(step < n_ctx)(functools.partial(both_directions, False))
    pl.when(step >= n_ctx)(functools.partial(both_directions, True))


def _gla(proj, small, wup_pad, gla_b3, lvm, ctx_len, seq):
    bn, tt, _ = proj.shape
    n = CHUNK
    assert ctx_len % n == 0 and seq % n == 0
    n_ctx = ctx_len // n
    n_steps = tt // n
    assert GLA_HPS == GLA_HEADS and PC_Q == 0 and PC_R % LANES == 0
    hw = GLA_HPS * GLA_DK
    vw = GLA_HPS * GLA_DV
    fwd = lambda s: s
    bwd = lambda s: _bwd_chunk(s, n_ctx, n_steps)

    def chunk_specs(c):
        return [pl.BlockSpec((1, n, PC_R), lambda b, h, s: (b, c(s), 0)),
                pl.BlockSpec((1, n, LANES), lambda b, h, s: (b, c(s), 0))]

    return pl.pallas_call(
        functools.partial(_gla_body, n_ctx, seq),
        grid=(bn, GLA_HEADS // GLA_HPS, n_steps),
        in_specs=chunk_specs(fwd) + chunk_specs(bwd) + [
            pl.BlockSpec((2, LANES, hw), lambda b, h, s: (0, 0, h)),
            pl.BlockSpec((2, 1, hw), lambda b, h, s: (0, 0, h)),
            pl.BlockSpec((2, n, n), lambda b, h, s: (0, 0, 0))],
        out_specs=pl.BlockSpec((1, seq, vw), lambda b, h, s: (b, 0, h)),
        out_shape=jax.ShapeDtypeStruct((bn, seq, GLA_VAL), F32),
        scratch_shapes=[pltpu.VMEM((2, GLA_HPS, GLA_DK, GLA_DV), F32)],
        compiler_params=_cparams(("arbitrary",) * 3),
        name="gla",
    )(proj, small, proj, small, wup_pad, gla_b3, lvm)


def _ssd_dir(rev, first, emit, g, dst, xbc_ref, sm_ref, dtb, alog, expand, dsk_ref, y_ref, s_ref):
    n = CHUNK
    gl = slice(g * LANES, (g + 1) * LANES)
    g0 = g * SSD_GW
    b_cols = slice(SSD_INNER + g * SSD_N, SSD_INNER + (g + 1) * SSD_N)
    c_cols = slice(SSD_INNER + (SSD_GROUPS + g) * SSD_N, SSD_INNER + (SSD_GROUPS + g + 1) * SSD_N)
    lane = lax.broadcasted_iota(jnp.int32, (1, LANES), 1)
    dt = _softplus(sm_ref[0, :, gl] + dtb)
    a_neg = jnp.where(lane < SSD_HPG, -jnp.exp(alog), 0.0)
    acum = _scan_sum(dt * a_neg, rev)
    dt_hi = dt.astype(BF16)
    dt_lo = (dt - dt_hi.astype(F32)).astype(BF16)
    dt_wide = jnp.dot(jnp.concatenate([dt_hi, dt_lo], axis=0), expand, preferred_element_type=F32)
    dt_wide = dt_wide[:n] + dt_wide[n:]
    bm_t = xbc_ref[0, :, b_cols].astype(F32).T.astype(BF16)
    if emit:
        acum_t = acum.T
        cm = xbc_ref[0, :, c_cols]
        ii = lax.broadcasted_iota(jnp.int32, (n, n), 0)
        jj = lax.broadcasted_iota(jnp.int32, (n, n), 1)
        causal = (jj >= ii) if rev else (jj <= ii)
        cb = _bdot_nt(cm, xbc_ref[0, :, b_cols])
    low = lane < SSD_P
    for p in range(SSD_HPG // 2):
        slab = slice(p * LANES, (p + 1) * LANES)
        out = slice(g0 + p * LANES, g0 + (p + 1) * LANES)
        h0, h1 = 2 * p, 2 * p + 1
        col0 = jnp.broadcast_to(acum[:, h0:h0 + 1], (n, LANES))
        col1 = jnp.broadcast_to(acum[:, h1:h1 + 1], (n, LANES))
        col = jnp.where(low, col0, col1)
        dt2 = dt_wide[:, slab]
        end = col[0:1] if rev else col[n - 1:n]
        x = xbc_ref[0, :, out].astype(F32)
        s_old = s_ref[:, slab]
        xdt = x * dt2
        if emit:
            m0 = (cb * jnp.exp(jnp.where(causal, col0 - acum_t[h0:h0 + 1, :], -jnp.inf))).astype(BF16)
            m1 = (cb * jnp.exp(jnp.where(causal, col1 - acum_t[h1:h1 + 1, :], -jnp.inf))).astype(BF16)
            rhs = jnp.concatenate([jnp.where(low, xdt, 0.0), jnp.where(low, 0.0, xdt)], axis=0).astype(BF16)
            y = jnp.dot(jnp.concatenate([m0, m1], axis=1), rhs, preferred_element_type=F32)
            y = y + jnp.dot(cm, s_old.astype(BF16), preferred_element_type=F32) * jnp.exp(col)
            if first:
                y = y + x * dsk_ref[:, out]
            y_ref[0, pl.ds(dst, n), out] += y
        new = jnp.dot(bm_t, (xdt * jnp.exp(end - col)).astype(BF16), preferred_element_type=F32)
        s_ref[:, slab] = s_old * jnp.exp(end) + new


def _ssd_body(n_ctx, seq, xbcf, smf, xbcb, smb, dtb_ref, alog_ref, exp_ref, dsk_ref, y_ref, s_ref):
    step = pl.program_id(1)
    n = CHUNK

    @pl.when(step == 0)
    def _():
        s_ref[...] = jnp.zeros_like(s_ref)
        y_ref[...] = jnp.zeros_like(y_ref)

    def both_directions(emit):
        dst_f = _out_row(step, n_ctx, n)
        dst_b = _out_row(_bwd_chunk(step, n_ctx, pl.num_programs(1)), n_ctx, n)
        for g in range(SSD_GROUPS):
            _ssd_dir(False, True, emit, g, dst_f, xbcf, smf, dtb_ref[0, g], alog_ref[0, g], exp_ref[...], dsk_ref, y_ref,
                     s_ref.at[0, g])
            _ssd_dir(True, False, emit, g, dst_b, xbcb, smb, dtb_ref[1, g], alog_ref[1, g], exp_ref[...], dsk_ref, y_ref,
                     s_ref.at[1, g])

    pl.when(step < n_ctx)(functools.partial(both_directions, False))
    pl.when(step >= n_ctx)(functools.partial(both_directions, True))


def _ssd(xact, small, dtb4, alog4, dskip_row, ctx_len, seq):
    bn, tt, _ = xact.shape
    n = CHUNK
    n_ctx = ctx_len // n
    n_steps = tt // n
    fwd = lambda s: s
    bwd = lambda s: _bwd_chunk(s, n_ctx, n_steps)

    assert SMALL_ROT0 % SSD_GROUPS == 0
    sblk = SMALL_ROT0 // SSD_GROUPS

    def chunk_specs(c, d):
        return [pl.BlockSpec((1, n, SSD_XBC), lambda b, s: (b, c(s), 0)),
                pl.BlockSpec((1, n, SSD_GROUPS * LANES), lambda b, s: (b, c(s), sblk + d))]

    whole4 = pl.BlockSpec((2, SSD_GROUPS, 1, LANES), lambda b, s: (0, 0, 0, 0))
    expand = jnp.asarray(np.arange(LANES)[:, None] == np.arange(SSD_GW)[None, :] // SSD_P, dtype=BF16)
    return pl.pallas_call(
        functools.partial(_ssd_body, n_ctx, seq),
        grid=(bn, n_steps),
        in_specs=chunk_specs(fwd, 0) + chunk_specs(bwd, 1) + [
            whole4, whole4, pl.BlockSpec((LANES, SSD_GW), lambda b, s: (0, 0)),
            pl.BlockSpec((1, SSD_INNER), lambda b, s: (0, 0))],
        out_specs=pl.BlockSpec((1, seq, SSD_INNER), lambda b, s: (b, 0, 0)),
        out_shape=jax.ShapeDtypeStruct((bn, seq, SSD_INNER), F32),
        scratch_shapes=[pltpu.VMEM((2, SSD_GROUPS, SSD_N, SSD_GW), F32)],
        compiler_params=_cparams(("arbitrary",) * 2),
        name="ssd",
    )(xact, small, xact, small, dtb4, alog4, expand, dskip_row)


OUT_TM = 256
RT_E1, RT_E2, RT_RANK1, RT_RANK2, RT_W1, RT_W2 = range(6)


def _rms(x, w):
    return x * lax.rsqrt(jnp.mean(x * x, axis=-1, keepdims=True) + EPS) * w


ROW_TILE = (D_MODEL // LANES, LANES)


def _store_row_tiles(ref, x):
    ref[...] = x.reshape((x.shape[0],) + ROW_TILE)


def _load_row_tiles(ref):
    return ref[...].reshape(ref.shape[0], D_MODEL)


def _outproj_body(o_ref, r_ref, y_ref, z_ref, x_ref, mod_ref, gn_ref, sn_ref, fn_ref, wo_ref, wr_ref, br_ref,
                  hlat_ref, h2_ref, route_ref, cnt_ref, carry_ref):
    tm = OUT_TM

    @pl.when((pl.program_id(0) == 0) & (pl.program_id(1) == 0))
    def _():
        carry_ref[...] = jnp.zeros_like(carry_ref)

    parts = []
    for h in range(GLA_HEADS):
        hs = slice(h * GLA_DV, (h + 1) * GLA_DV)
        parts.append((_rms(o_ref[0, :, hs], gn_ref[...]) * _silu(r_ref[0, :, hs].astype(F32))).astype(BF16))
    for g in range(SSD_GROUPS):
        gs = slice(g * SSD_GW, (g + 1) * SSD_GW)
        yg = y_ref[0, :, gs] * _silu(z_ref[0, :, gs].astype(F32))
        parts.append(_rms(yg, sn_ref[:, gs]).astype(BF16))
    mix = jnp.concatenate(parts, axis=-1)
    m = mod_ref[0]
    hlat = x_ref[0] + m[0:1] * jnp.dot(mix, wo_ref[...], preferred_element_type=F32)
    hlat_ref[0] = hlat
    h2f = _rms(hlat, fn_ref[...]) * (1.0 + m[2:3]) + m[1:2]
    _store_row_tiles(h2_ref.at[0], h2f)

    lg = jnp.dot(h2f.astype(BF16), wr_ref[...], preferred_element_type=F32) + br_ref[...]
    lanef = lax.broadcasted_iota(jnp.int32, (tm, LANES), 1).astype(F32)
    ninf = -jnp.inf
    is_g = lanef < float(N_GROUPS)
    gl = jnp.where(is_g, lg, ninf)
    gmax = jnp.max(gl, axis=-1, keepdims=True)
    gsel = jnp.min(jnp.where(gl == gmax, lanef, float(LANES)), axis=-1, keepdims=True)
    pg = 1.0 / jnp.sum(jnp.where(is_g, jnp.exp(lg - gmax), 0.0), axis=-1, keepdims=True)
    lo = float(EXPERT_LANE0) + float(EXPERTS_PER_GROUP) * gsel
    in_grp = (lanef >= lo) & (lanef < lo + float(EXPERTS_PER_GROUP))
    el = jnp.where(in_grp, lg, ninf)
    v1 = jnp.max(el, axis=-1, keepdims=True)
    i1 = jnp.min(jnp.where(el == v1, lanef, float(LANES)), axis=-1, keepdims=True)
    el2 = jnp.where(lanef == i1, ninf, el)
    v2 = jnp.max(el2, axis=-1, keepdims=True)
    i2 = jnp.min(jnp.where(el2 == v2, lanef, float(LANES)), axis=-1, keepdims=True)
    t = jnp.exp(v2 - v1)
    w1 = pg / (1.0 + t)
    w2 = pg * t / (1.0 + t)

    sel1 = lanef == i1
    sel2 = lanef == i2
    member = jnp.where(sel1 | sel2, 1.0, 0.0)
    ii = lax.broadcasted_iota(jnp.int32, (tm, tm), 0)
    jj = lax.broadcasted_iota(jnp.int32, (tm, tm), 1)
    before = jnp.where(jj < ii, 1.0, 0.0).astype(BF16)
    ranks = jnp.dot(before, member.astype(BF16), preferred_element_type=F32) + carry_ref[...]
    rank1 = jnp.sum(jnp.where(sel1, ranks, 0.0), axis=-1, keepdims=True)
    rank2 = jnp.sum(jnp.where(sel2, ranks, 0.0), axis=-1, keepdims=True)
    carry = carry_ref[...] + jnp.sum(member, axis=0, keepdims=True)
    carry_ref[...] = carry
    cnt_ref[...] = carry
    rec = jnp.zeros((tm, LANES), F32)
    for lane_id, val in ((RT_E1, i1 - float(EXPERT_LANE0)), (RT_E2, i2 - float(EXPERT_LANE0)),
                         (RT_RANK1, rank1), (RT_RANK2, rank2), (RT_W1, w1), (RT_W2, w2)):
        rec = jnp.where(lanef == float(lane_id), val, rec)
    route_ref[0] = rec


def _outproj(o_gla, proj, y_ssd, x, mod2, gn_row, sn_row, fn_row, w_out, w_router, b_router, ctx_len):
    bn, seq, _ = x.shape
    tm = OUT_TM
    assert ctx_len % tm == 0 and seq % tm == 0
    j0 = ctx_len // tm
    rb = PC_R // GLA_VAL
    zb = PC_Z // SSD_INNER
    tok = lambda b, j: (b, j, 0)
    const = lambda b, j: (0, 0)
    return pl.pallas_call(
        _outproj_body,
        grid=(bn, seq // tm),
        in_specs=[pl.BlockSpec((1, tm, GLA_VAL), tok),
                  pl.BlockSpec((1, tm, GLA_VAL), lambda b, j: (b, j0 + j, rb)),
                  pl.BlockSpec((1, tm, SSD_INNER), tok),
                  pl.BlockSpec((1, tm, SSD_INNER), lambda b, j: (b, j0 + j, zb)),
                  pl.BlockSpec((1, tm, D_MODEL), tok),
                  pl.BlockSpec((1, 4, D_MODEL), lambda b, j: (b, 0, 0)),
                  pl.BlockSpec((1, GLA_DV), const),
                  pl.BlockSpec((1, SSD_INNER), const),
                  pl.BlockSpec((1, D_MODEL), const),
                  pl.BlockSpec((GLA_VAL + SSD_INNER, D_MODEL), const),
                  pl.BlockSpec((D_MODEL, LANES), const),
                  pl.BlockSpec((1, LANES), const)],
        out_specs=[pl.BlockSpec((1, tm, D_MODEL), tok),
                   pl.BlockSpec((1, tm) + ROW_TILE, lambda b, j: (b, j, 0, 0)),
                   pl.BlockSpec((1, tm, LANES), tok),
                   pl.BlockSpec((1, LANES), const)],
        out_shape=[jax.ShapeDtypeStruct((bn, seq, D_MODEL), F32),
                   jax.ShapeDtypeStruct((bn, seq) + ROW_TILE, F32),
                   jax.ShapeDtypeStruct((bn, seq, LANES), F32),
                   jax.ShapeDtypeStruct((1, LANES), F32)],
        scratch_shapes=[pltpu.VMEM((1, LANES), F32)],
        compiler_params=_cparams(("arbitrary", "arbitrary")),
        name="outproj",
    )(o_gla, proj, y_ssd, proj, x, mod2, gn_row, sn_row, fn_row, w_out, w_router, b_router)


MOE_TILE = 256
PLAN_TM = 1024
DISPATCH_TM = 1024
COMBINE_TM = 512


def _moe_tiles(n_tokens):
    return (2 * n_tokens) // MOE_TILE + N_EXPERTS


def _plan_body(nt_pad, route_ref, cnt_ref, pos_ref, te_ref, nt_ref, pad0_ref, padn_ref, seg_ref, nxt_ref):
    lane = lax.broadcasted_iota(jnp.int32, (1, LANES), 1).astype(F32)
    is_e = (lane >= float(EXPERT_LANE0)) & (lane < float(EXPERT_LANE0 + N_EXPERTS))
    tiles = jnp.where(is_e, jnp.floor((cnt_ref[...] + float(MOE_TILE - 1)) / float(MOE_TILE)), 0.0)
    ii = lax.broadcasted_iota(jnp.int32, (LANES, LANES), 0)
    jj = lax.broadcasted_iota(jnp.int32, (LANES, LANES), 1)
    lower = jnp.where(ii < jj, 1.0, 0.0).astype(BF16)
    first_tile = jnp.dot(jnp.broadcast_to(tiles, (8, LANES)).astype(BF16), lower,
                         preferred_element_type=F32)[0:1]
    off_row = first_tile * float(MOE_TILE)
    for sb in range(PLAN_TM // LANES):
        r = route_ref[sb * LANES:(sb + 1) * LANES, :]
        o1 = jnp.sum(jnp.where(lane == r[:, RT_E1:RT_E1 + 1] + float(EXPERT_LANE0), off_row, 0.0),
                     axis=-1, keepdims=True)
        o2 = jnp.sum(jnp.where(lane == r[:, RT_E2:RT_E2 + 1] + float(EXPERT_LANE0), off_row, 0.0),
                     axis=-1, keepdims=True)
        p = jnp.where(lane == 0.0, o1 + r[:, RT_RANK1:RT_RANK1 + 1],
                      jnp.where(lane == 1.0, o2 + r[:, RT_RANK2:RT_RANK2 + 1], 0.0))
        pos_ref[sb] = p.T[0:8, :].astype(jnp.int32)

    @pl.when(pl.program_id(0) == 0)
    def _():
        ti = lax.broadcasted_iota(jnp.int32, (nt_pad, 1), 0).astype(F32)
        done = jnp.where(is_e & (first_tile + tiles <= ti), 1.0, 0.0)
        te = jnp.minimum(jnp.sum(done, axis=-1, keepdims=True), float(N_EXPERTS - 1))
        te_ref[...] = te.astype(jnp.int32)
        nt_ref[...] = jnp.sum(tiles, axis=-1, keepdims=True).astype(jnp.int32)
        lane_e = lane - float(EXPERT_LANE0)
        nonempty = is_e & (tiles > 0.0)
        seg_ref[...] = jnp.sum(jnp.where(nonempty & (lane_e < te), 1.0, 0.0), axis=-1,
                               keepdims=True).astype(jnp.int32)
        nxt = jnp.min(jnp.where(nonempty & (lane_e > te), lane_e, float(N_EXPERTS)), axis=-1, keepdims=True)
        nxt_ref[...] = jnp.where(nxt < float(N_EXPERTS), nxt, -1.0).astype(jnp.int32)
        used = jnp.where(is_e, cnt_ref[...], 0.0)
        pad0_ref[...] = (off_row + used).astype(jnp.int32)
        padn_ref[...] = (tiles * float(MOE_TILE) - used).astype(jnp.int32)


def _plan(route, cnt):
    n_tok = route.shape[0]
    nt_pad = _moe_tiles(n_tok)
    nsb = PLAN_TM // LANES
    return pl.pallas_call(
        functools.partial(_plan_body, nt_pad),
        grid=(n_tok // PLAN_TM,),
        in_specs=[pl.BlockSpec((PLAN_TM, LANES), lambda i: (i, 0)),
                  pl.BlockSpec((1, LANES), lambda i: (0, 0))],
        out_specs=[pl.BlockSpec((nsb, 8, LANES), lambda i: (i, 0, 0)),
                   pl.BlockSpec((nt_pad, 1), lambda i: (0, 0)),
                   pl.BlockSpec((1, 1), lambda i: (0, 0)),
                   pl.BlockSpec((1, LANES), lambda i: (0, 0)),
                   pl.BlockSpec((1, LANES), lambda i: (0, 0)),
                   pl.BlockSpec((nt_pad, 1), lambda i: (0, 0)),
                   pl.BlockSpec((nt_pad, 1), lambda i: (0, 0))],
        out_shape=[jax.ShapeDtypeStruct((n_tok // LANES, 8, LANES), jnp.int32),
                   jax.ShapeDtypeStruct((nt_pad, 1), jnp.int32),
                   jax.ShapeDtypeStruct((1, 1), jnp.int32),
                   jax.ShapeDtypeStruct((1, LANES), jnp.int32),
                   jax.ShapeDtypeStruct((1, LANES), jnp.int32),
                   jax.ShapeDtypeStruct((nt_pad, 1), jnp.int32),
                   jax.ShapeDtypeStruct((nt_pad, 1), jnp.int32)],
        compiler_params=_cparams(("arbitrary",)),
        name="moe_plan",
    )(route, cnt)


def _row_copy(src, s_row, dst, d_row, sem):
    return pltpu.make_async_copy(src.at[s_row], dst.at[d_row], sem)


def _issue_rows(n_rows, pos_ref, start_pair):
    for blk in range(n_rows // LANES):
        def issue(l, c, blk=blk):
            start_pair(blk * LANES + l, pos_ref[blk, 0, l], pos_ref[blk, 1, l])
            return c
        lax.fori_loop(0, LANES, issue, 0, unroll=8)


def _drain_rows(n_copies, copy):
    def drain(t, c):
        copy.wait()
        return c
    lax.fori_loop(0, n_copies, drain, 0, unroll=8)


def _dispatch_body(pad0_ref, padn_ref, nt_ref, pos_ref, h2_ref, xs_ref, zero_ref, sem):
    def start_pair(t, p1, p2):
        _row_copy(h2_ref, t, xs_ref, p1, sem).start(priority=0)
        _row_copy(h2_ref, t, xs_ref, p2, sem).start(priority=1)

    _issue_rows(DISPATCH_TM, pos_ref, start_pair)
    _drain_rows(2 * DISPATCH_TM, _row_copy(h2_ref, 0, xs_ref, 0, sem))

    @pl.when(pl.program_id(0) == 0)
    def _():
        zero_ref[...] = jnp.zeros_like(zero_ref)
        n_tiles = xs_ref.shape[0] // MOE_TILE

        def tile_copy(i):
            return pltpu.make_async_copy(zero_ref, xs_ref.at[pl.ds(i * MOE_TILE, MOE_TILE)], sem)

        def fill_tile(i, c):
            tile_copy(i).start()
            return c

        def drain_tile(i, c):
            tile_copy(i).wait()
            return c

        lax.fori_loop(nt_ref[0], n_tiles, fill_tile, 0)
        lax.fori_loop(nt_ref[0], n_tiles, drain_tile, 0)
        for e in range(N_EXPERTS):
            n_pad = padn_ref[EXPERT_LANE0 + e]
            for wait in (False, True):
                row = pad0_ref[EXPERT_LANE0 + e]
                size = MOE_TILE // 2
                while size >= 1:
                    has = (n_pad & size) != 0
                    copy = pltpu.make_async_copy(zero_ref.at[pl.ds(0, size)], xs_ref.at[pl.ds(row, size)], sem)
                    pl.when(has)(copy.wait if wait else copy.start)
                    row = row + jnp.where(has, size, 0)
                    size //= 2


def _dispatch(pad0, padn, nt, pos, h2t, n_rows):
    n_tok = h2t.shape[0]
    nb = DISPATCH_TM // LANES
    grid_spec = pltpu.PrefetchScalarGridSpec(
        num_scalar_prefetch=3,
        grid=(n_tok // DISPATCH_TM,),
        in_specs=[pl.BlockSpec((nb, 8, LANES), lambda i, *_: (i, 0, 0), memory_space=pltpu.SMEM),
                  pl.BlockSpec((DISPATCH_TM,) + ROW_TILE, lambda i, *_: (i, 0, 0))],
        out_specs=pl.BlockSpec(memory_space=pl.ANY),
        scratch_shapes=[pltpu.VMEM((MOE_TILE,) + ROW_TILE, F32), pltpu.SemaphoreType.DMA(())])
    return pl.pallas_call(
        _dispatch_body,
        grid_spec=grid_spec,
        out_shape=jax.ShapeDtypeStruct((n_rows,) + ROW_TILE, F32),
        compiler_params=_cparams(("arbitrary",)),
        name="moe_dispatch",
    )(pad0, padn, nt, pos, h2t)


def _expert_body(te_ref, nt_ref, seg_ref, nxt_ref, xs_ref, wg_ref, wu_ref, wd_ref, ys_ref,
                 wgf_ref, wuf_ref, wdf_ref, wgb_ref, wub_ref, wdb_ref, sem):
    i = pl.program_id(0)

    def copies(e, s):
        return [pltpu.make_async_copy(src.at[e], dst.at[s], sem.at[s])
                for src, dst in ((wg_ref, wgf_ref), (wu_ref, wuf_ref), (wd_ref, wdf_ref))]

    @pl.when(i < nt_ref[0])
    def _():
        seg = seg_ref[i]
        slot = seg % 2
        first = (i == 0) | (seg != seg_ref[jnp.maximum(i - 1, 0)])

        @pl.when(first)
        def _():
            @pl.when(i == 0)
            def _():
                for c in copies(te_ref[i], slot):
                    c.start()

            for c in copies(te_ref[i], slot):
                c.wait()
            wgb_ref[...] = wgf_ref[slot].astype(BF16)
            wub_ref[...] = wuf_ref[slot].astype(BF16)
            wdb_ref[...] = wdf_ref[slot].astype(BF16)

            @pl.when(nxt_ref[i] >= 0)
            def _():
                for c in copies(nxt_ref[i], 1 - slot):
                    c.start()

        x = _load_row_tiles(xs_ref).astype(BF16)
        gate = jnp.dot(x, wgb_ref[...], preferred_element_type=F32)
        up = jnp.dot(x, wub_ref[...], preferred_element_type=F32)
        y = jnp.dot((_silu(gate) * up).astype(BF16), wdb_ref[...], preferred_element_type=F32)
        _store_row_tiles(ys_ref, y)

    @pl.when(i >= nt_ref[0])
    def _():
        ys_ref[...] = jnp.zeros_like(ys_ref)


def _experts(te, nt, seg, nxt, xs, w_gate, w_up, w_down):
    rows = xs.shape[0]
    n_tiles = rows // MOE_TILE
    row_block = pl.BlockSpec((MOE_TILE,) + ROW_TILE, lambda i, *_: (i, 0, 0))
    in_w = (D_MODEL, D_FF)
    out_w = (D_FF, D_MODEL)
    grid_spec = pltpu.PrefetchScalarGridSpec(
        num_scalar_prefetch=4,
        grid=(n_tiles,),
        in_specs=[row_block] + [pl.BlockSpec(memory_space=pl.ANY)] * 3,
        out_specs=row_block,
        scratch_shapes=[pltpu.VMEM((2,) + in_w, F32), pltpu.VMEM((2,) + in_w, F32), pltpu.VMEM((2,) + out_w, F32),
                        pltpu.VMEM(in_w, BF16), pltpu.VMEM(in_w, BF16), pltpu.VMEM(out_w, BF16),
                        pltpu.SemaphoreType.DMA((2,))])
    return pl.pallas_call(
        _expert_body,
        grid_spec=grid_spec,
        out_shape=jax.ShapeDtypeStruct(xs.shape, F32),
        compiler_params=_cparams(("arbitrary",)),
        name="moe_experts",
    )(te, nt, seg, nxt, xs, w_gate, w_up, w_down)


def _combine_body(pos_ref, nxt_ref, route_ref, hlat_ref, g2_ref, fn_ref, ys_ref, out_ref, buf_ref, sem):
    tm = COMBINE_TM
    i = pl.program_id(0)
    slot = i % 2

    def fetch(p_ref, s):
        def start_pair(t, p1, p2):
            _row_copy(ys_ref, p1, buf_ref.at[s, 0], t, sem.at[s]).start(priority=0)
            _row_copy(ys_ref, p2, buf_ref.at[s, 1], t, sem.at[s]).start(priority=1)
        _issue_rows(tm, p_ref, start_pair)

    @pl.when(i == 0)
    def _():
        fetch(pos_ref, 0)

    @pl.when(i + 1 < pl.num_programs(0))
    def _():
        fetch(nxt_ref, 1 - slot)

    _drain_rows(2 * tm, _row_copy(ys_ref, 0, buf_ref.at[slot, 0], 0, sem.at[slot]))

    r = route_ref[0]
    moe = (r[:, RT_W1:RT_W1 + 1] * _load_row_tiles(buf_ref.at[slot, 0])
           + r[:, RT_W2:RT_W2 + 1] * _load_row_tiles(buf_ref.at[slot, 1]))
    out_ref[0] = _rms(hlat_ref[0] + g2_ref[0] * moe, fn_ref[...])


def _combine(pos, route, hlat, g2, fn_row, ys):
    bn, seq, _ = hlat.shape
    tm = COMBINE_TM
    nj = seq // tm
    nb = tm // LANES
    last = bn * nj - 1
    tok = lambda i: (i // nj, i % nj, 0)
    return pl.pallas_call(
        _combine_body,
        grid=(bn * nj,),
        in_specs=[pl.BlockSpec((nb, 8, LANES), lambda i: (i, 0, 0), memory_space=pltpu.SMEM),
                  pl.BlockSpec((nb, 8, LANES), lambda i: (jnp.minimum(i + 1, last), 0, 0),
                               memory_space=pltpu.SMEM),
                  pl.BlockSpec((1, tm, LANES), tok),
                  pl.BlockSpec((1, tm, D_MODEL), tok),
                  pl.BlockSpec((1, 1, D_MODEL), lambda i: (i // nj, 0, 0)),
                  pl.BlockSpec((1, D_MODEL), lambda i: (0, 0)),
                  pl.BlockSpec(memory_space=pl.ANY)],
        out_specs=pl.BlockSpec((1, tm, D_MODEL), tok),
        out_shape=jax.ShapeDtypeStruct((bn, seq, D_MODEL), F32),
        scratch_shapes=[pltpu.VMEM((2, 2, tm) + ROW_TILE, F32), pltpu.SemaphoreType.DMA((2,))],
        compiler_params=_cparams(("arbitrary",)),
        name="moe_combine",
    )(pos, pos, route, hlat, g2, fn_row, ys)


def _prep_weights(w_in, gla_w_up, gla_b, ssd_dt_bias, ssd_a_log, ssd_d, router_group_w, router_group_b,
                  router_expert_w, router_expert_b):
    off_k = GLA_KEY
    off_v = 2 * GLA_KEY
    off_r = off_v + GLA_VAL
    off_g = off_r + GLA_VAL
    off_z = off_g + 2 * GLA_RANK
    off_xbc = off_z + SSD_INNER
    off_dt = off_xbc + SSD_XBC
    w_main = jnp.concatenate([w_in[:, :off_g], w_in[:, off_z:off_dt]], axis=1).astype(BF16)
    w_small = jnp.concatenate([w_in[:, off_g:off_z], w_in[:, off_dt:off_dt + 2 * SSD_HEADS]], axis=1)
    w_small = jnp.pad(w_small, ((0, 0), (0, LANES - w_small.shape[1]))).astype(BF16)
    wup = jnp.zeros((2, LANES, GLA_KEY), F32)
    wup = wup.at[0, 0:GLA_RANK].set(gla_w_up[0]).at[1, GLA_RANK:2 * GLA_RANK].set(gla_w_up[1]).astype(BF16)
    pad_h = lambda t: jnp.pad(t.reshape(2, SSD_GROUPS, 1, SSD_HPG), ((0, 0), (0, 0), (0, 0), (0, LANES - SSD_HPG)))
    w_router = jnp.zeros((D_MODEL, LANES), F32)
    w_router = w_router.at[:, 0:N_GROUPS].set(router_group_w)
    w_router = w_router.at[:, EXPERT_LANE0:EXPERT_LANE0 + N_EXPERTS].set(router_expert_w).astype(BF16)
    b_router = jnp.zeros((1, LANES), F32).at[0, 0:N_GROUPS].set(router_group_b)
    b_router = b_router.at[0, EXPERT_LANE0:EXPERT_LANE0 + N_EXPERTS].set(router_expert_b)
    return (w_main, w_small, wup, gla_b.reshape(2, 1, GLA_KEY), pad_h(ssd_dt_bias), pad_h(ssd_a_log),
            jnp.repeat(ssd_d, SSD_P)[None, :], w_router, b_router)


def kernel(x, c, ctx, c_ctx, w_ada, b_ada, norm_mix, norm_ffn, w_in, gla_w_up, gla_b, gla_norm, ssd_conv_w,
           ssd_conv_b, ssd_dt_bias, ssd_a_log, ssd_d, ssd_norm, w_out, router_group_w, router_group_b,
           router_expert_w, router_expert_b, expert_w_gate, expert_w_up, expert_w_down, final_norm):
    assert w_ada.shape[0] == 1, "single-layer block"
    bn, seq, d = x.shape
    ctx_len = ctx.shape[1]
    assert seq // GRID_W == GRID_W and d == D_MODEL

    c8 = jnp.concatenate([c, c_ctx[None, :], jnp.zeros((8 - bn - 1, d), F32)], axis=0)
    mods = _adaln(c8, w_ada[0], b_ada[0][None, :])
    sh1, sc1, g1, sh2, sc2, g2 = [mods[:, i * d:(i + 1) * d] for i in range(6)]
    ctx_row = lambda t: jnp.broadcast_to(t[bn:bn + 1], (bn, d))
    mod1 = jnp.stack([ctx_row(sh1), ctx_row(sc1), sh1[:bn], sc1[:bn]], axis=1)
    mod2 = jnp.stack([g1[:bn], sh2[:bn], sc2[:bn], g2[:bn]], axis=1)

    (w_main, w_small, wup, gla_b3, dtb4, alog4, dskip_row, w_router, b_router) = _prep_weights(
        w_in[0], gla_w_up[0], gla_b[0], ssd_dt_bias[0], ssd_a_log[0], ssd_d[0], router_group_w[0],
        router_group_b[0], router_expert_w[0], router_expert_b[0])

    proj, small = _inproj(ctx, x, mod1, norm_mix, w_main, w_small)
    xact = _conv(proj, ssd_conv_w[0].reshape(9, SSD_XBC), ssd_conv_b, ctx_len, seq)
    lvm = jnp.asarray(np.stack([_level_matrix(CHUNK, False), _level_matrix(CHUNK, True)]))
    o_gla = _gla(proj, small, wup, gla_b3, lvm, ctx_len, seq)
    y_ssd = _ssd(xact, small, dtb4, alog4, dskip_row, ctx_len, seq)
    hlat, h2p, route, cnt = _outproj(o_gla, proj, y_ssd, x, mod2, gla_norm, ssd_norm, norm_ffn,
                                     w_out[0].astype(BF16), w_router, b_router, ctx_len)
    n_tok = bn * seq
    pos, te, nt, pad0, padn, seg, nxt = _plan(route.reshape(n_tok, LANES), cnt)
    te, nt = te.reshape(-1), nt.reshape(-1)
    xs = _dispatch(pad0.reshape(-1), padn.reshape(-1), nt, pos, h2p.reshape((n_tok,) + ROW_TILE),
                   _moe_tiles(n_tok) * MOE_TILE)
    ys = _experts(te, nt, seg.reshape(-1), nxt.reshape(-1), xs, expert_w_gate[0], expert_w_up[0], expert_w_down[0])
    return _combine(pos, route, hlat, mod2[:, 3:4, :], final_norm[None, :], ys)
```

```python
import functools
import math

import numpy as np
import jax
import jax.numpy as jnp
from jax import lax
from jax.experimental import pallas as pl
from jax.experimental.pallas import tpu as pltpu

F32 = jnp.float32
BF16 = jnp.bfloat16

D_MODEL = 1024
GRID_W = 64
EPS = 1e-6

GLA_HEADS = 4
GLA_DK = 128
GLA_DV = 256
GLA_KEY = GLA_HEADS * GLA_DK
GLA_VAL = GLA_HEADS * GLA_DV
GLA_RANK = 16
GLA_GATE_NORM = 16.0

SSD_HEADS = 16
SSD_P = 64
SSD_INNER = SSD_HEADS * SSD_P
SSD_GROUPS = 2
SSD_HPG = SSD_HEADS // SSD_GROUPS
SSD_N = 128
SSD_XBC = SSD_INNER + 2 * SSD_GROUPS * SSD_N
SSD_GW = SSD_HPG * SSD_P

N_GROUPS = 4
EXPERTS_PER_GROUP = 8
N_EXPERTS = N_GROUPS * EXPERTS_PER_GROUP
D_FF = 512

LANES = 128
SUBLANES = 8
VMEM_LIMIT = 56 * 1024 * 1024

PC_Q = 0
PC_K = PC_Q + GLA_KEY
PC_V = PC_K + GLA_KEY
PC_R = PC_V + GLA_VAL
PC_Z = PC_R + GLA_VAL
PC_XBC = PC_Z + SSD_INNER
PROJ_W = PC_XBC + SSD_XBC
SMALL_DT0 = 2 * GLA_RANK
SMALL_ROT0 = 2
SMALL_W = LANES * (SMALL_ROT0 + 2 * SSD_GROUPS)

CHUNK = 128
EXPERT_LANE0 = 32


def _cparams(sem):
    return pltpu.CompilerParams(dimension_semantics=sem, vmem_limit_bytes=VMEM_LIMIT)


def _silu(x):
    return x / (1.0 + jnp.exp(-x))


def _softplus(x):
    return jnp.maximum(x, 0.0) + jnp.log(1.0 + jnp.exp(-jnp.abs(x)))


def _log_sigmoid(x):
    return jnp.minimum(x, 0.0) - jnp.log(1.0 + jnp.exp(-jnp.abs(x)))


def _bdot(a, b):
    return jnp.dot(a.astype(BF16), b.astype(BF16), preferred_element_type=F32)


def _bdot_nt(a, b):
    return lax.dot_general(a.astype(BF16), b.astype(BF16), (((1,), (1,)), ((), ())),
                           preferred_element_type=F32)


def _adaln_body(c_ref, w_ref, b_ref, o_ref):
    o_ref[...] = _bdot(_silu(c_ref[...]), w_ref[...]) + b_ref[...]


def _adaln(c8, w, b):
    n = w.shape[1]
    bn = 1024
    return pl.pallas_call(
        _adaln_body,
        grid=(n // bn,),
        in_specs=[pl.BlockSpec((8, D_MODEL), lambda j: (0, 0)),
                  pl.BlockSpec((D_MODEL, bn), lambda j: (0, j)),
                  pl.BlockSpec((1, bn), lambda j: (0, j))],
        out_specs=pl.BlockSpec((8, bn), lambda j: (0, j)),
        out_shape=jax.ShapeDtypeStruct((8, n), F32),
        compiler_params=_cparams(("arbitrary",)),
        name="adaln",
    )(c8, w, b)


INPROJ_TM = 256
INPROJ_NC = 512


def _inproj_body(ctx_ref, x_ref, mod_ref, nw_ref, w_ref, ws_ref, proj_ref, small_ref):
    is_ctx = pl.program_id(1) == 0
    xin = jnp.where(is_ctx, ctx_ref[0], x_ref[0])
    m = mod_ref[0]
    shift = jnp.where(is_ctx, m[0:1], m[2:3])
    scale = jnp.where(is_ctx, m[1:2], m[3:4])
    ms = jnp.mean(xin * xin, axis=-1, keepdims=True)
    h = xin * lax.rsqrt(ms + EPS) * nw_ref[...]
    hb = (h * (1.0 + scale) + shift).astype(BF16)
    for n in range(PROJ_W // INPROJ_NC):
        sl = slice(n * INPROJ_NC, (n + 1) * INPROJ_NC)
        proj_ref[0, :, sl] = jnp.dot(hb, w_ref[:, sl], preferred_element_type=F32).astype(BF16)
    sm = jnp.dot(hb, ws_ref[...], preferred_element_type=F32)
    small_ref[0, :, 0:LANES] = sm
    small_ref[0, :, LANES:SMALL_ROT0 * LANES] = jnp.zeros((sm.shape[0], (SMALL_ROT0 - 1) * LANES), F32)
    for k in range(2 * SSD_GROUPS):
        lane0 = SMALL_DT0 + k * SSD_HPG
        small_ref[0, :, (SMALL_ROT0 + k) * LANES:(SMALL_ROT0 + k + 1) * LANES] = pltpu.roll(sm, LANES - lane0, 1)


def _inproj(ctx, x, mod1, norm_w, w_main, w_small):
    bn, seq, _ = x.shape
    ctx_len = ctx.shape[1]
    tm = INPROJ_TM
    assert ctx_len == tm and seq % tm == 0
    tt = ctx_len + seq
    nj = tt // tm
    return pl.pallas_call(
        _inproj_body,
        grid=(bn, nj),
        in_specs=[pl.BlockSpec((1, tm, D_MODEL), lambda b, j: (b, 0, 0)),
                  pl.BlockSpec((1, tm, D_MODEL), lambda b, j: (b, jnp.maximum(j - 1, 0), 0)),
                  pl.BlockSpec((1, 4, D_MODEL), lambda b, j: (b, 0, 0)),
                  pl.BlockSpec((1, D_MODEL), lambda b, j: (0, 0)),
                  pl.BlockSpec((D_MODEL, PROJ_W), lambda b, j: (0, 0)),
                  pl.BlockSpec((D_MODEL, LANES), lambda b, j: (0, 0))],
        out_specs=[pl.BlockSpec((1, tm, PROJ_W), lambda b, j: (b, j, 0)),
                   pl.BlockSpec((1, tm, SMALL_W), lambda b, j: (b, j, 0))],
        out_shape=[jax.ShapeDtypeStruct((bn, tt, PROJ_W), BF16),
                   jax.ShapeDtypeStruct((bn, tt, SMALL_W), F32)],
        compiler_params=_cparams(("arbitrary", "arbitrary")),
        name="inproj",
    )(ctx, x, mod1, norm_w, w_main, w_small)


CONV_CB = 256
CONV_TT = 256


def _conv_body(ctx_len, seq, p_ref, w_ref, b_ref, o_ref, s_ref):
    lat0 = ctx_len + GRID_W
    tt = CONV_TT
    n_tiles = seq // tt
    zeros = jnp.zeros((GRID_W, CONV_CB), F32)
    s_ref[0:ctx_len, :] = p_ref[0, 0:ctx_len, :].astype(F32)
    s_ref[ctx_len:lat0, :] = zeros
    s_ref[lat0 + seq:lat0 + seq + GRID_W, :] = zeros

    def copy_tile(t, c):
        src = pl.multiple_of(ctx_len + t * tt, tt)
        dst = pl.multiple_of(lat0 + t * tt, GRID_W)
        s_ref[pl.ds(dst, tt), :] = p_ref[0, pl.ds(src, tt), :].astype(F32)
        return c

    lax.fori_loop(0, n_tiles, copy_tile, 0)

    w = w_ref[...]
    bias = b_ref[...]
    pos = lax.broadcasted_iota(jnp.int32, (tt, 1), 0)

    def conv_rows(rows, first, last):
        left, mid, right = [sum(xr * w[3 * kh + kw:3 * kh + kw + 1] for xr, kh in rows) for kw in range(3)]
        return (bias + mid + jnp.where(first, 0.0, pltpu.roll(left, 1, 0))
                + jnp.where(last, 0.0, pltpu.roll(right, tt - 1, 0)))

    acc = conv_rows([(s_ref[0:ctx_len, :], 1)], pos == 0, pos == ctx_len - 1)
    o_ref[0, 0:ctx_len, :] = _silu(acc).astype(BF16)

    col = pos % GRID_W
    first = col == 0
    last = col == GRID_W - 1

    def tile(t, c):
        base = lat0 + t * tt
        rows = [(s_ref[pl.ds(pl.multiple_of(base + (kh - 1) * GRID_W, GRID_W), tt), :], kh) for kh in range(3)]
        dst = pl.multiple_of(ctx_len + t * tt, tt)
        o_ref[0, pl.ds(dst, tt), :] = _silu(conv_rows(rows, first, last)).astype(BF16)
        return c

    lax.fori_loop(0, n_tiles, tile, 0)


def _conv(proj, conv_w9, conv_b, ctx_len, seq):
    bn, tt, _ = proj.shape
    assert ctx_len == CONV_TT and seq % CONV_TT == 0 and CONV_TT % GRID_W == 0
    ncb = SSD_XBC // CONV_CB
    cb0 = PC_XBC // CONV_CB
    return pl.pallas_call(
        functools.partial(_conv_body, ctx_len, seq),
        grid=(bn, ncb),
        in_specs=[pl.BlockSpec((1, tt, CONV_CB), lambda b, j: (b, 0, cb0 + j)),
                  pl.BlockSpec((9, CONV_CB), lambda b, j: (0, j)),
                  pl.BlockSpec((1, CONV_CB), lambda b, j: (0, j))],
        out_specs=pl.BlockSpec((1, tt, CONV_CB), lambda b, j: (b, 0, j)),
        out_shape=jax.ShapeDtypeStruct((bn, tt, SSD_XBC), BF16),
        scratch_shapes=[pltpu.VMEM((tt + 2 * GRID_W, CONV_CB), F32)],
        compiler_params=_cparams(("arbitrary", "arbitrary")),
        name="conv",
    )(proj, conv_w9, conv_b)


def _bwd_chunk(s, n_ctx, n_steps):
    return jnp.where(s < n_ctx, n_ctx - 1 - s, n_steps + n_ctx - 1 - s)


def _out_row(c, n_ctx, n):
    return pl.multiple_of(jnp.maximum(c - n_ctx, 0) * n, n)


def _scan_sum(a, rev):
    n = a.shape[0]
    row = lax.broadcasted_iota(jnp.int32, (n, 1), 0)
    sh = 1
    while sh < n:
        if rev:
            a = a + jnp.where(row < n - sh, pltpu.roll(a, n - sh, 0), 0.0)
        else:
            a = a + jnp.where(row >= sh, pltpu.roll(a, sh, 0), 0.0)
        sh *= 2
    return a


def _level_matrix(n, rev):
    i = np.arange(n)[:, None]
    j = np.arange(n)[None, :]
    x = i ^ j
    lv = np.where(x > 0, np.floor(np.log2(np.maximum(x, 1))).astype(np.int64) + 1, 0)
    earlier = (j > i) if rev else (j < i)
    return np.where(earlier, lv, 0).astype(np.int32)


GLA_HPS = 4


def _gla_dir(rev, emit, dst, qkv_ref, sm_ref, wup, gb, lv, o_ref, s_ref):
    n = CHUNK
    q = qkv_ref[0, :, PC_Q:PC_K].astype(F32) * (GLA_DK ** -0.5)
    k = qkv_ref[0, :, PC_K:PC_V].astype(F32)
    x = jnp.dot(sm_ref[0].astype(BF16), wup, preferred_element_type=F32) + gb
    b = _scan_sum(_log_sigmoid(x) / GLA_GATE_NORM, rev)
    row = lax.broadcasted_iota(jnp.int32, (n, 1), 0)
    b_end = b[0:1] if rev else b[n - 1:n]
    k_end = k * jnp.exp(b_end - b)
    dec_end = jnp.exp(b_end)
    if emit:
        q_in = q * jnp.exp(b)

    att = [jnp.zeros((n, n), F32) for _ in range(GLA_HPS)]
    e_h = b
    h = 1
    level = 1
    while emit and h < n:
        upper = (row & h) != 0
        if 2 * h < SUBLANES:
            if rev:
                e = jnp.where(upper, e_h - b, b - pltpu.roll(e_h, n - h, 0))
            else:
                e = jnp.where(upper, b - pltpu.roll(e_h, h, 0), e_h - b)
        else:
            blocks = b.reshape(n // (2 * h), 2 * h, b.shape[1])
            mid = h if rev else h - 1
            ref = jnp.broadcast_to(blocks[:, mid:mid + 1, :], blocks.shape).reshape(b.shape)
            e = -jnp.abs(b - ref)
        later = jnp.logical_not(upper) if rev else upper
        z = (jnp.where(later, q, k) * jnp.exp(e)).astype(BF16)
        mask = lv == level
        for hh in range(GLA_HPS):
            ks = slice(hh * GLA_DK, (hh + 1) * GLA_DK)
            att[hh] = att[hh] + jnp.where(mask, _bdot_nt(z[:, ks], z[:, ks]), 0.0)
        if 4 * h < SUBLANES:
            if rev:
                e_h = jnp.where(upper, pltpu.roll(e_h, h, 0), e_h)
            else:
                e_h = jnp.where(upper, e_h, pltpu.roll(e_h, n - h, 0))
        h *= 2
        level += 1

    for hh in range(GLA_HPS):
        ks = slice(hh * GLA_DK, (hh + 1) * GLA_DK)
        vs = slice(hh * GLA_DV, (hh + 1) * GLA_DV)
        v = qkv_ref[0, :, PC_V + hh * GLA_DV:PC_V + (hh + 1) * GLA_DV]
        s_old = s_ref[hh]
        if emit:
            o = jnp.dot(att[hh].astype(BF16), v, preferred_element_type=F32)
            o = o + jnp.sum(q[:, ks] * k[:, ks], axis=-1, keepdims=True) * v.astype(F32)
            o = o + _bdot(q_in[:, ks], s_old)
            o_ref[0, pl.ds(dst, n), vs] += o
        dec_col = jnp.broadcast_to(dec_end[:, ks], (n, GLA_DK)).T[:, 0:1]
        s_ref[hh] = dec_col * s_old + jnp.dot(k_end[:, ks].T.astype(BF16), v, preferred_element_type=F32)


def _gla_body(n_ctx, seq, qkvf, smf, qkvb, smb, wup_ref, gb_ref, lv_ref, o_ref, s_ref):
    step = pl.program_id(2)
    n = CHUNK

    @pl.when(step == 0)
    def _():
        s_ref[...] = jnp.zeros_like(s_ref)
        o_ref[...] = jnp.zeros_like(o_ref)

    def both_directions(emit):
        cb = _bwd_chunk(step, n_ctx, pl.num_programs(2))
        _gla_dir(False, emit, _out_row(step, n_ctx, n), qkvf, smf, wup_ref[0], gb_ref[0], lv_ref[0],
                 o_ref, s_ref.at[0])
        _gla_dir(True, emit, _out_row(cb, n_ctx, n), qkvb, smb, wup_ref[1], gb_ref[1], lv_ref[1],
                 o_ref, s_ref.at[1])

    pl.when(step < n_ctx)(functools.partial(both_directions, False))
    pl.when(step >= n_ctx)(functools.partial(both_directions, True))


def _gla(proj, small, wup_pad, gla_b3, lvm, ctx_len, seq):
    bn, tt, _ = proj.shape
    n = CHUNK
    assert ctx_len % n == 0 and seq % n == 0
    n_ctx = ctx_len // n
    n_steps = tt // n
    assert GLA_HPS == GLA_HEADS and PC_Q == 0 and PC_R % LANES == 0
    hw = GLA_HPS * GLA_DK
    vw = GLA_HPS * GLA_DV
    fwd = lambda s: s
    bwd = lambda s: _bwd_chunk(s, n_ctx, n_steps)

    def chunk_specs(c):
        return [pl.BlockSpec((1, n, PC_R), lambda b, h, s: (b, c(s), 0)),
                pl.BlockSpec((1, n, LANES), lambda b, h, s: (b, c(s), 0))]

    return pl.pallas_call(
        functools.partial(_gla_body, n_ctx, seq),
        grid=(bn, GLA_HEADS // GLA_HPS, n_steps),
        in_specs=chunk_specs(fwd) + chunk_specs(bwd) + [
            pl.BlockSpec((2, LANES, hw), lambda b, h, s: (0, 0, h)),
            pl.BlockSpec((2, 1, hw), lambda b, h, s: (0, 0, h)),
            pl.BlockSpec((2, n, n), lambda b, h, s: (0, 0, 0))],
        out_specs=pl.BlockSpec((1, seq, vw), lambda b, h, s: (b, 0, h)),
        out_shape=jax.ShapeDtypeStruct((bn, seq, GLA_VAL), F32),
        scratch_shapes=[pltpu.VMEM((2, GLA_HPS, GLA_DK, GLA_DV), F32)],
        compiler_params=_cparams(("arbitrary",) * 3),
        name="gla",
    )(proj, small, proj, small, wup_pad, gla_b3, lvm)


def _ssd_dir(rev, first, emit, g, dst, xbc_ref, sm_ref, dtb, alog, expand, dsk_ref, y_ref, s_ref):
    n = CHUNK
    gl = slice(g * LANES, (g + 1) * LANES)
    g0 = g * SSD_GW
    b_cols = slice(SSD_INNER + g * SSD_N, SSD_INNER + (g + 1) * SSD_N)
    c_cols = slice(SSD_INNER + (SSD_GROUPS + g) * SSD_N, SSD_INNER + (SSD_GROUPS + g + 1) * SSD_N)
    lane = lax.broadcasted_iota(jnp.int32, (1, LANES), 1)
    dt = _softplus(sm_ref[0, :, gl] + dtb)
    a_neg = jnp.where(lane < SSD_HPG, -jnp.exp(alog), 0.0)
    acum = _scan_sum(dt * a_neg, rev)
    dt_hi = dt.astype(BF16)
    dt_lo = (dt - dt_hi.astype(F32)).astype(BF16)
    dt_wide = jnp.dot(jnp.concatenate([dt_hi, dt_lo], axis=0), expand, preferred_element_type=F32)
    dt_wide = dt_wide[:n] + dt_wide[n:]
    bm_t = xbc_ref[0, :, b_cols].astype(F32).T.astype(BF16)
    if emit:
        acum_t = acum.T
        cm = xbc_ref[0, :, c_cols]
        ii = lax.broadcasted_iota(jnp.int32, (n, n), 0)
        jj = lax.broadcasted_iota(jnp.int32, (n, n), 1)
        causal = (jj >= ii) if rev else (jj <= ii)
        cb = _bdot_nt(cm, xbc_ref[0, :, b_cols])
    low = lane < SSD_P
    for p in range(SSD_HPG // 2):
        slab = slice(p * LANES, (p + 1) * LANES)
        out = slice(g0 + p * LANES, g0 + (p + 1) * LANES)
        h0, h1 = 2 * p, 2 * p + 1
        col0 = jnp.broadcast_to(acum[:, h0:h0 + 1], (n, LANES))
        col1 = jnp.broadcast_to(acum[:, h1:h1 + 1], (n, LANES))
        col = jnp.where(low, col0, col1)
        dt2 = dt_wide[:, slab]
        end = col[0:1] if rev else col[n - 1:n]
        x = xbc_ref[0, :, out].astype(F32)
        s_old = s_ref[:, slab]
        xdt = x * dt2
        if emit:
            m0 = (cb * jnp.exp(jnp.where(causal, col0 - acum_t[h0:h0 + 1, :], -jnp.inf))).astype(BF16)
            m1 = (cb * jnp.exp(jnp.where(causal, col1 - acum_t[h1:h1 + 1, :], -jnp.inf))).astype(BF16)
            rhs = jnp.concatenate([jnp.where(low, xdt, 0.0), jnp.where(low, 0.0, xdt)], axis=0).astype(BF16)
            y = jnp.dot(jnp.concatenate([m0, m1], axis=1), rhs, preferred_element_type=F32)
            y = y + jnp.dot(cm, s_old.astype(BF16), preferred_element_type=F32) * jnp.exp(col)
            if first:
                y = y + x * dsk_ref[:, out]
            y_ref[0, pl.ds(dst, n), out] += y
        new = jnp.dot(bm_t, (xdt * jnp.exp(end - col)).astype(BF16), preferred_element_type=F32)
        s_ref[:, slab] = s_old * jnp.exp(end) + new


def _ssd_body(n_ctx, seq, xbcf, smf, xbcb, smb, dtb_ref, alog_ref, exp_ref, dsk_ref, y_ref, s_ref):
    step = pl.program_id(1)
    n = CHUNK

    @pl.when(step == 0)
    def _():
        s_ref[...] = jnp.zeros_like(s_ref)
        y_ref[...] = jnp.zeros_like(y_ref)

    def both_directions(emit):
        dst_f = _out_row(step, n_ctx, n)
        dst_b = _out_row(_bwd_chunk(step, n_ctx, pl.num_programs(1)), n_ctx, n)
        for g in range(SSD_GROUPS):
            _ssd_dir(False, True, emit, g, dst_f, xbcf, smf, dtb_ref[0, g], alog_ref[0, g], exp_ref[...], dsk_ref, y_ref,
                     s_ref.at[0, g])
            _ssd_dir(True, False, emit, g, dst_b, xbcb, smb, dtb_ref[1, g], alog_ref[1, g], exp_ref[...], dsk_ref, y_ref,
                     s_ref.at[1, g])

    pl.when(step < n_ctx)(functools.partial(both_directions, False))
    pl.when(step >= n_ctx)(functools.partial(both_directions, True))


def _ssd(xact, small, dtb4, alog4, dskip_row, ctx_len, seq):
    bn, tt, _ = xact.shape
    n = CHUNK
    n_ctx = ctx_len // n
    n_steps = tt // n
    fwd = lambda s: s
    bwd = lambda s: _bwd_chunk(s, n_ctx, n_steps)

    assert SMALL_ROT0 % SSD_GROUPS == 0
    sblk = SMALL_ROT0 // SSD_GROUPS

    def chunk_specs(c, d):
        return [pl.BlockSpec((1, n, SSD_XBC), lambda b, s: (b, c(s), 0)),
                pl.BlockSpec((1, n, SSD_GROUPS * LANES), lambda b, s: (b, c(s), sblk + d))]

    whole4 = pl.BlockSpec((2, SSD_GROUPS, 1, LANES), lambda b, s: (0, 0, 0, 0))
    expand = jnp.asarray(np.arange(LANES)[:, None] == np.arange(SSD_GW)[None, :] // SSD_P, dtype=BF16)
    return pl.pallas_call(
        functools.partial(_ssd_body, n_ctx, seq),
        grid=(bn, n_steps),
        in_specs=chunk_specs(fwd, 0) + chunk_specs(bwd, 1) + [
            whole4, whole4, pl.BlockSpec((LANES, SSD_GW), lambda b, s: (0, 0)),
            pl.BlockSpec((1, SSD_INNER), lambda b, s: (0, 0))],
        out_specs=pl.BlockSpec((1, seq, SSD_INNER), lambda b, s: (b, 0, 0)),
        out_shape=jax.ShapeDtypeStruct((bn, seq, SSD_INNER), F32),
        scratch_shapes=[pltpu.VMEM((2, SSD_GROUPS, SSD_N, SSD_GW), F32)],
        compiler_params=_cparams(("arbitrary",) * 2),
        name="ssd",
    )(xact, small, xact, small, dtb4, alog4, expand, dskip_row)


OUT_TM = 256
RT_E1, RT_E2, RT_RANK1, RT_RANK2, RT_W1, RT_W2 = range(6)


def _rms(x, w):
    return x * lax.rsqrt(jnp.mean(x * x, axis=-1, keepdims=True) + EPS) * w


ROW_TILE = (D_MODEL // LANES, LANES)


def _store_row_tiles(ref, x):
    ref[...] = x.reshape((x.shape[0],) + ROW_TILE)


def _load_row_tiles(ref):
    return ref[...].reshape(ref.shape[0], D_MODEL)


def _outproj_body(o_ref, r_ref, y_ref, z_ref, x_ref, mod_ref, gn_ref, sn_ref, fn_ref, wo_ref, wr_ref, br_ref,
                  hlat_ref, h2_ref, route_ref, cnt_ref, carry_ref):
    tm = OUT_TM

    @pl.when((pl.program_id(0) == 0) & (pl.program_id(1) == 0))
    def _():
        carry_ref[...] = jnp.zeros_like(carry_ref)

    parts = []
    for h in range(GLA_HEADS):
        hs = slice(h * GLA_DV, (h + 1) * GLA_DV)
        parts.append((_rms(o_ref[0, :, hs], gn_ref[...]) * _silu(r_ref[0, :, hs].astype(F32))).astype(BF16))
    for g in range(SSD_GROUPS):
        gs = slice(g * SSD_GW, (g + 1) * SSD_GW)
        yg = y_ref[0, :, gs] * _silu(z_ref[0, :, gs].astype(F32))
        parts.append(_rms(yg, sn_ref[:, gs]).astype(BF16))
    mix = jnp.concatenate(parts, axis=-1)
    m = mod_ref[0]
    hlat = x_ref[0] + m[0:1] * jnp.dot(mix, wo_ref[...], preferred_element_type=F32)
    hlat_ref[0] = hlat
    h2f = _rms(hlat, fn_ref[...]) * (1.0 + m[2:3]) + m[1:2]
    _store_row_tiles(h2_ref.at[0], h2f)

    lg = jnp.dot(h2f.astype(BF16), wr_ref[...], preferred_element_type=F32) + br_ref[...]
    lanef = lax.broadcasted_iota(jnp.int32, (tm, LANES), 1).astype(F32)
    ninf = -jnp.inf
    is_g = lanef < float(N_GROUPS)
    gl = jnp.where(is_g, lg, ninf)
    gmax = jnp.max(gl, axis=-1, keepdims=True)
    gsel = jnp.min(jnp.where(gl == gmax, lanef, float(LANES)), axis=-1, keepdims=True)
    pg = 1.0 / jnp.sum(jnp.where(is_g, jnp.exp(lg - gmax), 0.0), axis=-1, keepdims=True)
    lo = float(EXPERT_LANE0) + float(EXPERTS_PER_GROUP) * gsel
    in_grp = (lanef >= lo) & (lanef < lo + float(EXPERTS_PER_GROUP))
    el = jnp.where(in_grp, lg, ninf)
    v1 = jnp.max(el, axis=-1, keepdims=True)
    i1 = jnp.min(jnp.where(el == v1, lanef, float(LANES)), axis=-1, keepdims=True)
    el2 = jnp.where(lanef == i1, ninf, el)
    v2 = jnp.max(el2, axis=-1, keepdims=True)
    i2 = jnp.min(jnp.where(el2 == v2, lanef, float(LANES)), axis=-1, keepdims=True)
    t = jnp.exp(v2 - v1)
    w1 = pg / (1.0 + t)
    w2 = pg * t / (1.0 + t)

    sel1 = lanef == i1
    sel2 = lanef == i2
    member = jnp.where(sel1 | sel2, 1.0, 0.0)
    ii = lax.broadcasted_iota(jnp.int32, (tm, tm), 0)
    jj = lax.broadcasted_iota(jnp.int32, (tm, tm), 1)
    before = jnp.where(jj < ii, 1.0, 0.0).astype(BF16)
    ranks = jnp.dot(before, member.astype(BF16), preferred_element_type=F32) + carry_ref[...]
    rank1 = jnp.sum(jnp.where(sel1, ranks, 0.0), axis=-1, keepdims=True)
    rank2 = jnp.sum(jnp.where(sel2, ranks, 0.0), axis=-1, keepdims=True)
    carry = carry_ref[...] + jnp.sum(member, axis=0, keepdims=True)
    carry_ref[...] = carry
    cnt_ref[...] = carry
    rec = jnp.zeros((tm, LANES), F32)
    for lane_id, val in ((RT_E1, i1 - float(EXPERT_LANE0)), (RT_E2, i2 - float(EXPERT_LANE0)),
                         (RT_RANK1, rank1), (RT_RANK2, rank2), (RT_W1, w1), (RT_W2, w2)):
        rec = jnp.where(lanef == float(lane_id), val, rec)
    route_ref[0] = rec


def _outproj(o_gla, proj, y_ssd, x, mod2, gn_row, sn_row, fn_row, w_out, w_router, b_router, ctx_len):
    bn, seq, _ = x.shape
    tm = OUT_TM
    assert ctx_len % tm == 0 and seq % tm == 0
    j0 = ctx_len // tm
    rb = PC_R // GLA_VAL
    zb = PC_Z // SSD_INNER
    tok = lambda b, j: (b, j, 0)
    const = lambda b, j: (0, 0)
    return pl.pallas_call(
        _outproj_body,
        grid=(bn, seq // tm),
        in_specs=[pl.BlockSpec((1, tm, GLA_VAL), tok),
                  pl.BlockSpec((1, tm, GLA_VAL), lambda b, j: (b, j0 + j, rb)),
                  pl.BlockSpec((1, tm, SSD_INNER), tok),
                  pl.BlockSpec((1, tm, SSD_INNER), lambda b, j: (b, j0 + j, zb)),
                  pl.BlockSpec((1, tm, D_MODEL), tok),
                  pl.BlockSpec((1, 4, D_MODEL), lambda b, j: (b, 0, 0)),
                  pl.BlockSpec((1, GLA_DV), const),
                  pl.BlockSpec((1, SSD_INNER), const),
                  pl.BlockSpec((1, D_MODEL), const),
                  pl.BlockSpec((GLA_VAL + SSD_INNER, D_MODEL), const),
                  pl.BlockSpec((D_MODEL, LANES), const),
                  pl.BlockSpec((1, LANES), const)],
        out_specs=[pl.BlockSpec((1, tm, D_MODEL), tok),
                   pl.BlockSpec((1, tm) + ROW_TILE, lambda b, j: (b, j, 0, 0)),
                   pl.BlockSpec((1, tm, LANES), tok),
                   pl.BlockSpec((1, LANES), const)],
        out_shape=[jax.ShapeDtypeStruct((bn, seq, D_MODEL), F32),
                   jax.ShapeDtypeStruct((bn, seq) + ROW_TILE, F32),
                   jax.ShapeDtypeStruct((bn, seq, LANES), F32),
                   jax.ShapeDtypeStruct((1, LANES), F32)],
        scratch_shapes=[pltpu.VMEM((1, LANES), F32)],
        compiler_params=_cparams(("arbitrary", "arbitrary")),
        name="outproj",
    )(o_gla, proj, y_ssd, proj, x, mod2, gn_row, sn_row, fn_row, w_out, w_router, b_router)


MOE_TILE = 256
PLAN_TM = 1024
DISPATCH_TM = 1024
COMBINE_TM = 512


def _moe_tiles(n_tokens):
    return (2 * n_tokens) // MOE_TILE + N_EXPERTS


def _plan_body(nt_pad, route_ref, cnt_ref, pos_ref, te_ref, nt_ref, pad0_ref, padn_ref, seg_ref, nxt_ref):
    lane = lax.broadcasted_iota(jnp.int32, (1, LANES), 1).astype(F32)
    is_e = (lane >= float(EXPERT_LANE0)) & (lane < float(EXPERT_LANE0 + N_EXPERTS))
    tiles = jnp.where(is_e, jnp.floor((cnt_ref[...] + float(MOE_TILE - 1)) / float(MOE_TILE)), 0.0)
    ii = lax.broadcasted_iota(jnp.int32, (LANES, LANES), 0)
    jj = lax.broadcasted_iota(jnp.int32, (LANES, LANES), 1)
    lower = jnp.where(ii < jj, 1.0, 0.0).astype(BF16)
    first_tile = jnp.dot(jnp.broadcast_to(tiles, (8, LANES)).astype(BF16), lower,
                         preferred_element_type=F32)[0:1]
    off_row = first_tile * float(MOE_TILE)
    for sb in range(PLAN_TM // LANES):
        r = route_ref[sb * LANES:(sb + 1) * LANES, :]
        ft_cols = jnp.broadcast_to(first_tile, (LANES, LANES)).T.astype(BF16)
        hot1 = jnp.where(lane == r[:, RT_E1:RT_E1 + 1] + float(EXPERT_LANE0), 1.0, 0.0).astype(BF16)
        hot2 = jnp.where(lane == r[:, RT_E2:RT_E2 + 1] + float(EXPERT_LANE0), 1.0, 0.0).astype(BF16)
        o1 = jnp.dot(hot1, ft_cols, preferred_element_type=F32) * float(MOE_TILE)
        o2 = jnp.dot(hot2, ft_cols, preferred_element_type=F32) * float(MOE_TILE)
        p = jnp.where(lane == 0.0, o1 + r[:, RT_RANK1:RT_RANK1 + 1],
                      jnp.where(lane == 1.0, o2 + r[:, RT_RANK2:RT_RANK2 + 1], 0.0))
        pos_ref[sb] = p.T[0:8, :].astype(jnp.int32)

    @pl.when(pl.program_id(0) == 0)
    def _():
        ti = lax.broadcasted_iota(jnp.int32, (nt_pad, 1), 0).astype(F32)
        done = jnp.where(is_e & (first_tile + tiles <= ti), 1.0, 0.0)
        te = jnp.minimum(jnp.sum(done, axis=-1, keepdims=True), float(N_EXPERTS - 1))
        te_ref[...] = te.astype(jnp.int32)
        nt_ref[...] = jnp.sum(tiles, axis=-1, keepdims=True).astype(jnp.int32)
        lane_e = lane - float(EXPERT_LANE0)
        nonempty = is_e & (tiles > 0.0)
        seg_ref[...] = jnp.sum(jnp.where(nonempty & (lane_e < te), 1.0, 0.0), axis=-1,
                               keepdims=True).astype(jnp.int32)
        nxt = jnp.min(jnp.where(nonempty & (lane_e > te), lane_e, float(N_EXPERTS)), axis=-1, keepdims=True)
        nxt_ref[...] = jnp.where(nxt < float(N_EXPERTS), nxt, -1.0).astype(jnp.int32)
        used = jnp.where(is_e, cnt_ref[...], 0.0)
        pad0_ref[...] = (off_row + used).astype(jnp.int32)
        padn_ref[...] = (tiles * float(MOE_TILE) - used).astype(jnp.int32)


def _plan(route, cnt):
    n_tok = route.shape[0]
    nt_pad = _moe_tiles(n_tok)
    nsb = PLAN_TM // LANES
    return pl.pallas_call(
        functools.partial(_plan_body, nt_pad),
        grid=(n_tok // PLAN_TM,),
        in_specs=[pl.BlockSpec((PLAN_TM, LANES), lambda i: (i, 0)),
                  pl.BlockSpec((1, LANES), lambda i: (0, 0))],
        out_specs=[pl.BlockSpec((nsb, 8, LANES), lambda i: (i, 0, 0)),
                   pl.BlockSpec((nt_pad, 1), lambda i: (0, 0)),
                   pl.BlockSpec((1, 1), lambda i: (0, 0)),
                   pl.BlockSpec((1, LANES), lambda i: (0, 0)),
                   pl.BlockSpec((1, LANES), lambda i: (0, 0)),
                   pl.BlockSpec((nt_pad, 1), lambda i: (0, 0)),
                   pl.BlockSpec((nt_pad, 1), lambda i: (0, 0))],
        out_shape=[jax.ShapeDtypeStruct((n_tok // LANES, 8, LANES), jnp.int32),
                   jax.ShapeDtypeStruct((nt_pad, 1), jnp.int32),
                   jax.ShapeDtypeStruct((1, 1), jnp.int32),
                   jax.ShapeDtypeStruct((1, LANES), jnp.int32),
                   jax.ShapeDtypeStruct((1, LANES), jnp.int32),
                   jax.ShapeDtypeStruct((nt_pad, 1), jnp.int32),
                   jax.ShapeDtypeStruct((nt_pad, 1), jnp.int32)],
        compiler_params=_cparams(("arbitrary",)),
        name="moe_plan",
    )(route, cnt)


def _row_copy(src, s_row, dst, d_row, sem):
    return pltpu.make_async_copy(src.at[s_row], dst.at[d_row], sem)


def _issue_rows(n_rows, pos_ref, start_pair):
    for blk in range(n_rows // LANES):
        def issue(l, c, blk=blk):
            start_pair(blk * LANES + l, pos_ref[blk, 0, l], pos_ref[blk, 1, l])
            return c
        lax.fori_loop(0, LANES, issue, 0, unroll=8)


def _drain_rows(n_copies, copy):
    def drain(t, c):
        copy.wait()
        return c
    lax.fori_loop(0, n_copies, drain, 0, unroll=8)


def _dispatch_body(pad0_ref, padn_ref, nt_ref, pos_ref, h2_ref, xs_ref, zero_ref, sem):
    def start_pair(t, p1, p2):
        _row_copy(h2_ref, t, xs_ref, p1, sem).start(priority=0)
        _row_copy(h2_ref, t, xs_ref, p2, sem).start(priority=1)

    _issue_rows(DISPATCH_TM, pos_ref, start_pair)
    _drain_rows(2 * DISPATCH_TM, _row_copy(h2_ref, 0, xs_ref, 0, sem))

    @pl.when(pl.program_id(0) == 0)
    def _():
        zero_ref[...] = jnp.zeros_like(zero_ref)
        n_tiles = xs_ref.shape[0] // MOE_TILE

        def tile_copy(i):
            return pltpu.make_async_copy(zero_ref, xs_ref.at[pl.ds(i * MOE_TILE, MOE_TILE)], sem)

        def fill_tile(i, c):
            tile_copy(i).start()
            return c

        def drain_tile(i, c):
            tile_copy(i).wait()
            return c

        lax.fori_loop(nt_ref[0], n_tiles, fill_tile, 0)
        lax.fori_loop(nt_ref[0], n_tiles, drain_tile, 0)
        for e in range(N_EXPERTS):
            n_pad = padn_ref[EXPERT_LANE0 + e]
            for wait in (False, True):
                row = pad0_ref[EXPERT_LANE0 + e]
                size = MOE_TILE // 2
                while size >= 1:
                    has = (n_pad & size) != 0
                    copy = pltpu.make_async_copy(zero_ref.at[pl.ds(0, size)], xs_ref.at[pl.ds(row, size)], sem)
                    pl.when(has)(copy.wait if wait else copy.start)
                    row = row + jnp.where(has, size, 0)
                    size //= 2


def _dispatch(pad0, padn, nt, pos, h2t, n_rows):
    n_tok = h2t.shape[0]
    nb = DISPATCH_TM // LANES
    grid_spec = pltpu.PrefetchScalarGridSpec(
        num_scalar_prefetch=3,
        grid=(n_tok // DISPATCH_TM,),
        in_specs=[pl.BlockSpec((nb, 8, LANES), lambda i, *_: (i, 0, 0), memory_space=pltpu.SMEM),
                  pl.BlockSpec((DISPATCH_TM,) + ROW_TILE, lambda i, *_: (i, 0, 0))],
        out_specs=pl.BlockSpec(memory_space=pl.ANY),
        scratch_shapes=[pltpu.VMEM((MOE_TILE,) + ROW_TILE, F32), pltpu.SemaphoreType.DMA(())])
    return pl.pallas_call(
        _dispatch_body,
        grid_spec=grid_spec,
        out_shape=jax.ShapeDtypeStruct((n_rows,) + ROW_TILE, F32),
        compiler_params=_cparams(("arbitrary",)),
        name="moe_dispatch",
    )(pad0, padn, nt, pos, h2t)


def _expert_body(te_ref, nt_ref, seg_ref, nxt_ref, xs_ref, wg_ref, wu_ref, wd_ref, ys_ref,
                 wgf_ref, wuf_ref, wdf_ref, wgb_ref, wub_ref, wdb_ref, sem):
    i = pl.program_id(0)

    def copies(e, s):
        return [pltpu.make_async_copy(src.at[e], dst.at[s], sem.at[s])
                for src, dst in ((wg_ref, wgf_ref), (wu_ref, wuf_ref), (wd_ref, wdf_ref))]

    @pl.when(i < nt_ref[0])
    def _():
        seg = seg_ref[i]
        slot = seg % 2
        first = (i == 0) | (seg != seg_ref[jnp.maximum(i - 1, 0)])

        @pl.when(first)
        def _():
            @pl.when(i == 0)
            def _():
                for c in copies(te_ref[i], slot):
                    c.start()

            for c in copies(te_ref[i], slot):
                c.wait()
            wgb_ref[...] = wgf_ref[slot].astype(BF16)
            wub_ref[...] = wuf_ref[slot].astype(BF16)
            wdb_ref[...] = wdf_ref[slot].astype(BF16)

            @pl.when(nxt_ref[i] >= 0)
            def _():
                for c in copies(nxt_ref[i], 1 - slot):
                    c.start()

        x = _load_row_tiles(xs_ref).astype(BF16)
        gate = jnp.dot(x, wgb_ref[...], preferred_element_type=F32)
        up = jnp.dot(x, wub_ref[...], preferred_element_type=F32)
        y = jnp.dot((_silu(gate) * up).astype(BF16), wdb_ref[...], preferred_element_type=F32)
        _store_row_tiles(ys_ref, y)

    @pl.when(i >= nt_ref[0])
    def _():
        ys_ref[...] = jnp.zeros_like(ys_ref)


def _experts(te, nt, seg, nxt, xs, w_gate, w_up, w_down):
    rows = xs.shape[0]
    n_tiles = rows // MOE_TILE
    row_block = pl.BlockSpec((MOE_TILE,) + ROW_TILE, lambda i, *_: (i, 0, 0))
    in_w = (D_MODEL, D_FF)
    out_w = (D_FF, D_MODEL)
    grid_spec = pltpu.PrefetchScalarGridSpec(
        num_scalar_prefetch=4,
        grid=(n_tiles,),
        in_specs=[row_block] + [pl.BlockSpec(memory_space=pl.ANY)] * 3,
        out_specs=row_block,
        scratch_shapes=[pltpu.VMEM((2,) + in_w, F32), pltpu.VMEM((2,) + in_w, F32), pltpu.VMEM((2,) + out_w, F32),
                        pltpu.VMEM(in_w, BF16), pltpu.VMEM(in_w, BF16), pltpu.VMEM(out_w, BF16),
                        pltpu.SemaphoreType.DMA((2,))])
    return pl.pallas_call(
        _expert_body,
        grid_spec=grid_spec,
        out_shape=jax.ShapeDtypeStruct(xs.shape, F32),
        compiler_params=_cparams(("arbitrary",)),
        name="moe_experts",
    )(te, nt, seg, nxt, xs, w_gate, w_up, w_down)


def _combine_body(pos_ref, nxt_ref, route_ref, hlat_ref, g2_ref, fn_ref, ys_ref, out_ref, buf_ref, sem):
    tm = COMBINE_TM
    i = pl.program_id(0)
    slot = i % 2

    def fetch(p_ref, s):
        def start_pair(t, p1, p2):
            _row_copy(ys_ref, p1, buf_ref.at[s, 0], t, sem.at[s]).start(priority=0)
            _row_copy(ys_ref, p2, buf_ref.at[s, 1], t, sem.at[s]).start(priority=1)
        _issue_rows(tm, p_ref, start_pair)

    @pl.when(i == 0)
    def _():
        fetch(pos_ref, 0)

    @pl.when(i + 1 < pl.num_programs(0))
    def _():
        fetch(nxt_ref, 1 - slot)

    _drain_rows(2 * tm, _row_copy(ys_ref, 0, buf_ref.at[slot, 0], 0, sem.at[slot]))

    r = route_ref[0]
    moe = (r[:, RT_W1:RT_W1 + 1] * _load_row_tiles(buf_ref.at[slot, 0])
           + r[:, RT_W2:RT_W2 + 1] * _load_row_tiles(buf_ref.at[slot, 1]))
    out_ref[0] = _rms(hlat_ref[0] + g2_ref[0] * moe, fn_ref[...])


def _combine(pos, route, hlat, g2, fn_row, ys):
    bn, seq, _ = hlat.shape
    tm = COMBINE_TM
    nj = seq // tm
    nb = tm // LANES
    last = bn * nj - 1
    tok = lambda i: (i // nj, i % nj, 0)
    return pl.pallas_call(
        _combine_body,
        grid=(bn * nj,),
        in_specs=[pl.BlockSpec((nb, 8, LANES), lambda i: (i, 0, 0), memory_space=pltpu.SMEM),
                  pl.BlockSpec((nb, 8, LANES), lambda i: (jnp.minimum(i + 1, last), 0, 0),
                               memory_space=pltpu.SMEM),
                  pl.BlockSpec((1, tm, LANES), tok),
                  pl.BlockSpec((1, tm, D_MODEL), tok),
                  pl.BlockSpec((1, 1, D_MODEL), lambda i: (i // nj, 0, 0)),
                  pl.BlockSpec((1, D_MODEL), lambda i: (0, 0)),
                  pl.BlockSpec(memory_space=pl.ANY)],
        out_specs=pl.BlockSpec((1, tm, D_MODEL), tok),
        out_shape=jax.ShapeDtypeStruct((bn, seq, D_MODEL), F32),
        scratch_shapes=[pltpu.VMEM((2, 2, tm) + ROW_TILE, F32), pltpu.SemaphoreType.DMA((2,))],
        compiler_params=_cparams(("arbitrary",)),
        name="moe_combine",
    )(pos, pos, route, hlat, g2, fn_row, ys)


def _prep_weights(w_in, gla_w_up, gla_b, ssd_dt_bias, ssd_a_log, ssd_d, router_group_w, router_group_b,
                  router_expert_w, router_expert_b):
    off_k = GLA_KEY
    off_v = 2 * GLA_KEY
    off_r = off_v + GLA_VAL
    off_g = off_r + GLA_VAL
    off_z = off_g + 2 * GLA_RANK
    off_xbc = off_z + SSD_INNER
    off_dt = off_xbc + SSD_XBC
    w_main = jnp.concatenate([w_in[:, :off_g], w_in[:, off_z:off_dt]], axis=1).astype(BF16)
    w_small = jnp.concatenate([w_in[:, off_g:off_z], w_in[:, off_dt:off_dt + 2 * SSD_HEADS]], axis=1)
    w_small = jnp.pad(w_small, ((0, 0), (0, LANES - w_small.shape[1]))).astype(BF16)
    wup = jnp.zeros((2, LANES, GLA_KEY), F32)
    wup = wup.at[0, 0:GLA_RANK].set(gla_w_up[0]).at[1, GLA_RANK:2 * GLA_RANK].set(gla_w_up[1]).astype(BF16)
    pad_h = lambda t: jnp.pad(t.reshape(2, SSD_GROUPS, 1, SSD_HPG), ((0, 0), (0, 0), (0, 0), (0, LANES - SSD_HPG)))
    w_router = jnp.zeros((D_MODEL, LANES), F32)
    w_router = w_router.at[:, 0:N_GROUPS].set(router_group_w)
    w_router = w_router.at[:, EXPERT_LANE0:EXPERT_LANE0 + N_EXPERTS].set(router_expert_w).astype(BF16)
    b_router = jnp.zeros((1, LANES), F32).at[0, 0:N_GROUPS].set(router_group_b)
    b_router = b_router.at[0, EXPERT_LANE0:EXPERT_LANE0 + N_EXPERTS].set(router_expert_b)
    return (w_main, w_small, wup, gla_b.reshape(2, 1, GLA_KEY), pad_h(ssd_dt_bias), pad_h(ssd_a_log),
            jnp.repeat(ssd_d, SSD_P)[None, :], w_router, b_router)


def kernel(x, c, ctx, c_ctx, w_ada, b_ada, norm_mix, norm_ffn, w_in, gla_w_up, gla_b, gla_norm, ssd_conv_w,
           ssd_conv_b, ssd_dt_bias, ssd_a_log, ssd_d, ssd_norm, w_out, router_group_w, router_group_b,
           router_expert_w, router_expert_b, expert_w_gate, expert_w_up, expert_w_down, final_norm):
    assert w_ada.shape[0] == 1, "single-layer block"
    bn, seq, d = x.shape
    ctx_len = ctx.shape[1]
    assert seq // GRID_W == GRID_W and d == D_MODEL

    c8 = jnp.concatenate([c, c_ctx[None, :], jnp.zeros((8 - bn - 1, d), F32)], axis=0)
    mods = _adaln(c8, w_ada[0], b_ada[0][None, :])
    sh1, sc1, g1, sh2, sc2, g2 = [mods[:, i * d:(i + 1) * d] for i in range(6)]
    ctx_row = lambda t: jnp.broadcast_to(t[bn:bn + 1], (bn, d))
    mod1 = jnp.stack([ctx_row(sh1), ctx_row(sc1), sh1[:bn], sc1[:bn]], axis=1)
    mod2 = jnp.stack([g1[:bn], sh2[:bn], sc2[:bn], g2[:bn]], axis=1)

    (w_main, w_small, wup, gla_b3, dtb4, alog4, dskip_row, w_router, b_router) = _prep_weights(
        w_in[0], gla_w_up[0], gla_b[0], ssd_dt_bias[0], ssd_a_log[0], ssd_d[0], router_group_w[0],
        router_group_b[0], router_expert_w[0], router_expert_b[0])

    proj, small = _inproj(ctx, x, mod1, norm_mix, w_main, w_small)
    xact = _conv(proj, ssd_conv_w[0].reshape(9, SSD_XBC), ssd_conv_b, ctx_len, seq)
    lvm = jnp.asarray(np.stack([_level_matrix(CHUNK, False), _level_matrix(CHUNK, True)]))
    o_gla = _gla(proj, small, wup, gla_b3, lvm, ctx_len, seq)
    y_ssd = _ssd(xact, small, dtb4, alog4, dskip_row, ctx_len, seq)
    hlat, h2p, route, cnt = _outproj(o_gla, proj, y_ssd, x, mod2, gla_norm, ssd_norm, norm_ffn,
                                     w_out[0].astype(BF16), w_router, b_router, ctx_len)
    n_tok = bn * seq
    pos, te, nt, pad0, padn, seg, nxt = _plan(route.reshape(n_tok, LANES), cnt)
    te, nt = te.reshape(-1), nt.reshape(-1)
    xs = _dispatch(pad0.reshape(-1), padn.reshape(-1), nt, pos, h2p.reshape((n_tok,) + ROW_TILE),
                   _moe_tiles(n_tok) * MOE_TILE)
    ys = _experts(te, nt, seg.reshape(-1), nxt.reshape(-1), xs, expert_w_gate[0], expert_w_up[0], expert_w_down[0])
    return _combine(pos, route, hlat, mod2[:, 3:4, :], final_norm[None, :], ys)
```
